```python
import jax, jax.numpy as jnp
from jax import lax
import numpy as np

D_MODEL = 1024
BATCH = 16
SEQ = 256
DEPTH = 1
DEC_BATCH = 8
DEC_SEQ = 2048
PAST_LEN = 256

GRID_W = 64
D_MIX = D_MODEL
D_A = D_MIX // 2
HEAD_A = 64
H_A = D_A // HEAD_A
LORA_W = 64
LORA_A = 64
D_B = D_MIX - D_A
HG_EXPAND = 128
H_B = D_B // HG_EXPAND
DV_B = D_B // H_B
CHUNK = 64
N_DIR = 2
EPS = 1e-6
GN_EPS = 64e-5
A_COLS = 4 * D_A + N_DIR * (LORA_W + LORA_A)
B_COLS = 5 * D_B
IN_COLS = A_COLS + B_COLS

kernel_name = 'hybrid_rwkv7_hgrn2_diffusion_step'


def rmsnorm(x, g):
    x32 = x.astype(jnp.float32)
    y = x32 * lax.rsqrt(jnp.mean(x32 * x32, axis=-1, keepdims=True) + EPS)
    return y.astype(x.dtype) * g


def modulation(cvec, w_ada, b_ada):
    m = (jax.nn.silu(cvec) @ w_ada + b_ada)[:, None, :]
    shift, scale, gate = jnp.split(m, 3, axis=-1)
    return shift, scale, gate


def _neighbour_mean(p, axis):
    n = p.shape[axis]
    pad = [(0, 0)] * p.ndim
    pad[axis] = (1, 1)
    pp = jnp.pad(p, pad)
    prev = lax.slice_in_dim(pp, 0, n, axis=axis)
    nxt = lax.slice_in_dim(pp, 2, n + 2, axis=axis)
    return 0.5 * (prev + nxt)


def token_shift(p, mu_h, mu_v, grid):
    if grid:
        B, T, C = p.shape
        rows = T // GRID_W
        pg = p.reshape(B, rows, GRID_W, C)
        h = _neighbour_mean(pg, 2).reshape(B, T, C)
        v = _neighbour_mean(pg, 1).reshape(B, T, C)
        return p + mu_h * (h - p) + mu_v * (v - p)
    h = _neighbour_mean(p, 1)
    return p + mu_h * (h - p)


def _flip(z):
    return jnp.flip(z, axis=1)


def rwkv7_scan(r, w, k, v, kk, a, s0):
    def step(S, inp):
        r_t, w_t, k_t, v_t, kk_t, a_t = inp
        sa = jnp.einsum('bhvk,bhk->bhv', S, kk_t)
        S = (S * w_t[:, :, None, :]
             - sa[..., None] * (kk_t * a_t)[:, :, None, :]
             + v_t[..., None] * k_t[:, :, None, :])
        y = jnp.einsum('bhvk,bhk->bhv', S, r_t)
        return S, y
    xs = tuple(jnp.moveaxis(z, 1, 0) for z in (r, w, k, v, kk, a))
    S, ys = lax.scan(step, s0, xs)
    return jnp.moveaxis(ys, 0, 1), S


def hgrn2_chunk(q, k, v, logf, s0):
    B, T, H, K = q.shape
    V = v.shape[-1]
    nc = T // CHUNK

    def to_chunks(z):
        return jnp.moveaxis(z.reshape(B, nc, CHUNK, H, z.shape[-1]), 1, 0)

    causal = jnp.tril(jnp.ones((CHUNK, CHUNK), dtype=bool))[None, :, :, None, None]

    def step(S, inp):
        qc, kc, vc, gc = inp
        G = jnp.cumsum(gc, axis=1)
        o_inter = jnp.einsum('bthk,bhkv->bthv', qc * jnp.exp(G), S)
        diff = G[:, :, None] - G[:, None, :]
        decay = jnp.exp(jnp.where(causal, diff, -jnp.inf))
        A = jnp.einsum('bthk,btshk->bhts', qc, decay * kc[:, None])
        o = o_inter + jnp.einsum('bhts,bshv->bthv', A, vc)
        G_last = G[:, -1]
        k_dec = kc * jnp.exp(G_last[:, None] - G)
        S = S * jnp.exp(G_last)[..., None] + jnp.einsum('bchk,bchv->bhkv', k_dec, vc)
        return S, o

    S, o = lax.scan(step, s0, tuple(to_chunks(z) for z in (q, k, v, logf)))
    o = jnp.moveaxis(o, 0, 1).reshape(B, T, H, V)
    return o, S


def layer_mix(xm, s_a0, s_b0, grid, w_in, mu_h, mu_v, w0, w2, a0, a2, k_k, k_a, r_k,
              lnx_w, lnx_b, lb, onorm_g, w_out):
    B, T, _ = xm.shape
    dt = xm.dtype
    f32 = jnp.float32
    p = xm @ w_in
    pa = token_shift(p[..., :A_COLS], mu_h, mu_v, grid)
    pb = p[..., A_COLS:]

    r, k, v, za = jnp.split(pa[..., :4 * D_A], 4, axis=-1)
    lora = pa[..., 4 * D_A:].reshape(B, T, N_DIR, LORA_W + LORA_A)
    w_log = (w0 + jnp.einsum('btdr,drc->btdc', jnp.tanh(lora[..., :LORA_W]), w2)).astype(f32)
    decay = jnp.exp(-jnp.exp(-jax.nn.softplus(-w_log) - 0.5))
    a = jax.nn.sigmoid((a0 + jnp.einsum('btdr,drc->btdc', lora[..., LORA_W:], a2)).astype(f32))
    hA = lambda z: z.reshape(B, T, H_A, HEAD_A)
    kk = hA((k * k_k).astype(f32))
    kk = kk / jnp.maximum(jnp.linalg.norm(kk, axis=-1, keepdims=True), 1e-12)
    k_eff = k.astype(f32)[:, :, None] * (1.0 + (a - 1.0) * k_a.astype(f32))
    rf, vf = hA(r.astype(f32)), hA(v.astype(f32))
    y_f, sa_f = rwkv7_scan(rf, hA(decay[:, :, 0]), hA(k_eff[:, :, 0]), vf, kk, hA(a[:, :, 0]),
                           s_a0[:, 0].astype(f32))
    y_b, sa_b = rwkv7_scan(_flip(rf), _flip(hA(decay[:, :, 1])), _flip(hA(k_eff[:, :, 1])), _flip(vf),
                           _flip(kk), _flip(hA(a[:, :, 1])), s_a0[:, 1].astype(f32))
    y = y_f + _flip(y_b)
    mu = jnp.mean(y, axis=-1, keepdims=True)
    var = jnp.mean(jnp.square(y - mu), axis=-1, keepdims=True)
    y = ((y - mu) * lax.rsqrt(var + GN_EPS)).reshape(B, T, D_A) * lnx_w + lnx_b
    bonus = jnp.sum(rf * hA(jnp.mean(k_eff, axis=2)) * r_k, axis=-1, keepdims=True) * vf
    out_a = (y + bonus.reshape(B, T, D_A)).astype(dt) * jax.nn.silu(za)

    hB = lambda z: z.reshape(B, T, H_B, HG_EXPAND)
    q = hB(jax.nn.silu(pb[..., :D_B]).astype(f32))
    vb = pb[..., D_B:2 * D_B].astype(f32).reshape(B, T, H_B, DV_B)
    f = lb + (1.0 - lb) * jax.nn.sigmoid(pb[..., 2 * D_B:4 * D_B].reshape(B, T, N_DIR, D_B).astype(f32))
    zb = pb[..., 4 * D_B:]
    kf, gf = 1.0 - f, jnp.log(f)
    o_f, sb_f = hgrn2_chunk(q, hB(kf[:, :, 0]), vb, hB(gf[:, :, 0]), s_b0[:, 0].astype(f32))
    o_b, sb_b = hgrn2_chunk(_flip(q), _flip(hB(kf[:, :, 1])), _flip(vb), _flip(hB(gf[:, :, 1])),
                            s_b0[:, 1].astype(f32))
    o = o_f + _flip(o_b)
    o = (o * lax.rsqrt(jnp.mean(o * o, axis=-1, keepdims=True) + EPS)).reshape(B, T, D_B) * onorm_g
    out_b = o.astype(dt) * jax.nn.silu(zb)

    out = jnp.concatenate([out_a, out_b], axis=-1) @ w_out
    s_a = jnp.stack([sa_f, sa_b], axis=1)
    s_b = jnp.stack([sb_f, sb_b], axis=1)
    return out, s_a, s_b


def setup_inputs(seed: int = 0) -> dict:
    key = jax.random.key(seed)
    ks = jax.random.split(key, 28)
    n = lambda i, shape: jax.random.normal(ks[i], shape, jnp.float32)
    u = lambda i, shape, lo, hi: jax.random.uniform(ks[i], shape, jnp.float32, lo, hi)
    return {
        'x_prompt': n(0, (BATCH, SEQ, D_MODEL)),
        'x_sample': n(1, (DEC_BATCH, DEC_SEQ, D_MODEL)),
        'state_rwkv': 0.3 * n(2, (DEC_BATCH, DEPTH, N_DIR, H_A, HEAD_A, HEAD_A)),
        'state_hgrn': 0.3 * n(3, (DEC_BATCH, DEPTH, N_DIR, H_B, HG_EXPAND, DV_B)),
        'c': n(4, (DEC_BATCH, D_MODEL)),
        'c_ctx': n(5, (D_MODEL,)),
        'norm_g': 1.0 + 0.02 * n(6, (DEPTH, D_MODEL)),
        'w_ada': 0.5 * D_MODEL ** -0.5 * n(7, (DEPTH, D_MODEL, 3 * D_MODEL)),
        'b_ada': 0.01 * n(8, (DEPTH, 3 * D_MODEL)),
        'w_in': D_MODEL ** -0.5 * n(9, (DEPTH, D_MODEL, IN_COLS)),
        'mu_h': u(10, (DEPTH, A_COLS), 0.0, 0.6),
        'mu_v': u(11, (DEPTH, A_COLS), 0.0, 0.6),
        'w0': u(12, (DEPTH, N_DIR, D_A), -4.0, 1.0),
        'w2': 0.5 * LORA_W ** -0.5 * n(13, (DEPTH, N_DIR, LORA_W, D_A)),
        'a0': 0.5 * n(14, (DEPTH, N_DIR, D_A)),
        'a2': 0.5 * LORA_A ** -0.5 * n(15, (DEPTH, N_DIR, LORA_A, D_A)),
        'k_k': 0.85 + 0.05 * n(16, (DEPTH, D_A)),
        'k_a': 1.0 + 0.05 * n(17, (DEPTH, D_A)),
        'r_k': 0.1 * n(18, (DEPTH, H_A, HEAD_A)),
        'lnx_w': 1.0 + 0.02 * n(19, (DEPTH, D_A)),
        'lnx_b': 0.01 * n(20, (DEPTH, D_A)),
        'lb_logits': 0.1 * n(21, (DEPTH + 1, N_DIR, D_B)),
        'onorm_g': 1.0 + 0.02 * n(22, (DEPTH, D_B)),
        'w_out': D_MIX ** -0.5 * n(23, (DEPTH, D_MIX, D_MODEL)),
        'final_g': 1.0 + 0.02 * n(24, (D_MODEL,)),
    }


def reference(x_prompt, x_sample, state_rwkv, state_hgrn, c, c_ctx, norm_g, w_ada, b_ada, w_in,
              mu_h, mu_v, w0, w2, a0, a2, k_k, k_a, r_k, lnx_w, lnx_b, lb_logits, onorm_g,
              w_out, final_g):
    lb_all = jnp.cumsum(jax.nn.softmax(lb_logits.astype(jnp.float32), axis=0), axis=0)
    hp, hs = x_prompt, x_sample
    Bp = x_prompt.shape[0]
    new_a, new_b = [], []
    for l in range(DEPTH):
        params = (w_in[l], mu_h[l], mu_v[l], w0[l], w2[l], a0[l], a2[l], k_k[l], k_a[l], r_k[l],
                  lnx_w[l], lnx_b[l], lb_all[l], onorm_g[l], w_out[l])
        zero_a = jnp.zeros((Bp, N_DIR, H_A, HEAD_A, HEAD_A), jnp.float32)
        zero_b = jnp.zeros((Bp, N_DIR, H_B, HG_EXPAND, DV_B), jnp.float32)
        sh, sc, g = modulation(c_ctx[None, :], w_ada[l], b_ada[l])
        xm = rmsnorm(hp, norm_g[l]) * (1.0 + sc) + sh
        out, s_a, s_b = layer_mix(xm, zero_a, zero_b, False, *params)
        hp = hp + g * out
        new_a.append(s_a.astype(x_prompt.dtype))
        new_b.append(s_b.astype(x_prompt.dtype))
        sh, sc, g = modulation(c, w_ada[l], b_ada[l])
        xm = rmsnorm(hs, norm_g[l]) * (1.0 + sc) + sh
        out, _, _ = layer_mix(xm, state_rwkv[:, l], state_hgrn[:, l], True, *params)
        hs = hs + g * out
    y_prompt = rmsnorm(hp, final_g)
    y_sample = rmsnorm(hs, final_g)
    new_state_rwkv = jnp.stack(new_a, axis=1)
    new_state_hgrn = jnp.stack(new_b, axis=1)
    return (y_prompt, y_sample, new_state_rwkv, new_state_hgrn)
```

```python
import functools

import jax
import jax.numpy as jnp
from jax import lax
from jax.experimental import pallas as pl
from jax.experimental.pallas import tpu as pltpu

F32 = jnp.float32
BF16 = jnp.bfloat16

D_MODEL = 1024
D_A = 512
D_B = 512
HEAD_A = 64
HEAD_B = 128
LORA = 128
A_COLS = 4 * D_A + 2 * LORA
B_COLS = 5 * D_B
GRID_W = 64
CHUNK = 64
EPS = 1e-6
GN_EPS = 64e-5
PACK = 256
TN = 256
NA = A_COLS // TN
NB = B_COLS // TN
VMEM_LIMIT = 56 * 1024 * 1024

NN = ((1,), (0,))
NT = ((1,), (1,))
TN_DIMS = ((0,), (0,))


def _dot(a, b, dims=NN):
    return lax.dot_general(a, b, (dims, ((), ())), preferred_element_type=F32)


def _split(x):
    if isinstance(x, tuple):
        return x
    hi = x.astype(BF16)
    lo = (x - hi.astype(F32)).astype(BF16)
    return hi, lo


def _mm(a, b, dims=NN, passes=3):
    if passes == 1:
        a = a[0] if isinstance(a, tuple) else a.astype(BF16)
        b = b[0] if isinstance(b, tuple) else b.astype(BF16)
        return _dot(a, b, dims)
    ah, al = _split(a)
    bh, bl = _split(b)
    return _dot(ah, bh, dims) + (_dot(ah, bl, dims) + _dot(al, bh, dims))


def _split3(x):
    h1 = x.astype(BF16)
    r1 = x - h1.astype(F32)
    h2 = r1.astype(BF16)
    h3 = (r1 - h2.astype(F32)).astype(BF16)
    return h1, h2, h3


def _mm_exact_lhs(a_bf16, b):
    b1, b2, b3 = _split3(b)
    return _dot(a_bf16, b1) + (_dot(a_bf16, b2) + _dot(a_bf16, b3))


def _mm_exact_rhs(a, b_bf16):
    a1, a2, a3 = _split3(a)
    return _dot(a1, b_bf16) + (_dot(a2, b_bf16) + _dot(a3, b_bf16))


def _sigmoid(x):
    return 1.0 / (1.0 + jnp.exp(-x))


def _silu(x):
    return x * _sigmoid(x)


def _softplus(x):
    return jnp.maximum(x, 0.0) + jnp.log(1.0 + jnp.exp(-jnp.abs(x)))


def _iota(shape, dim):
    return lax.broadcasted_iota(jnp.int32, shape, dim)


def _block_mask(rows, cols, row_shift, col_shift):
    same = (_iota((rows, cols), 0) >> row_shift) == (_iota((rows, cols), 1) >> col_shift)
    return jnp.where(same, 1.0, 0.0).astype(F32)


def _tile_rows(x, n):
    if isinstance(x, tuple):
        return tuple(_tile_rows(t, n) for t in x)
    return jnp.concatenate([x] * n, axis=0)


def _masked(x, mask_bf16):
    if isinstance(x, tuple):
        return tuple(_masked(t, mask_bf16) for t in x)
    return x * mask_bf16


def _ada_kernel(c_ref, w_ref, b_ref, m_ref):
    m_ref[...] = _mm(_silu(c_ref[...]), w_ref[...]) + b_ref[...]


def _ada_call(cc, w_ada, b_ada):
    rows = cc.shape[0]
    return pl.pallas_call(
        _ada_kernel,
        grid=(3,),
        in_specs=[pl.BlockSpec((rows, D_MODEL), lambda j: (0, 0)),
                  pl.BlockSpec((D_MODEL, D_MODEL), lambda j: (0, j)),
                  pl.BlockSpec((1, D_MODEL), lambda j: (0, j))],
        out_specs=pl.BlockSpec((rows, D_MODEL), lambda j: (0, j)),
        out_shape=jax.ShapeDtypeStruct((rows, 3 * D_MODEL), F32),
        compiler_params=pltpu.CompilerParams(dimension_semantics=("arbitrary",),
                                             vmem_limit_bytes=VMEM_LIMIT),
        name="ada",
    )(cc, w_ada, b_ada)


def _proj_kernel(x_ref, sh_ref, sc_ref, g_ref, w_ref, muh_ref, muv_ref, pa_ref, pb_ref, xm_ref, *,
                 seq, grid_shift):
    j = pl.program_id(1)

    @pl.when(j == 0)
    def _():
        x = x_ref[0]
        xn = x * lax.rsqrt(jnp.mean(x * x, axis=-1, keepdims=True) + EPS) * g_ref[...]
        xm_ref[...] = (xn * (1.0 + sc_ref[0]) + sh_ref[0]).astype(BF16)

    p = _dot(xm_ref[...], w_ref[...])

    @pl.when(j < NA)
    def _():
        row = _iota((seq, TN), 0)
        prev = pltpu.roll(p, 1, axis=0)
        nxt = pltpu.roll(p, seq - 1, axis=0)
        if grid_shift:
            col = row & (GRID_W - 1)
            prev = jnp.where(col == 0, 0.0, prev)
            nxt = jnp.where(col == GRID_W - 1, 0.0, nxt)
            up = jnp.where(row < GRID_W, 0.0, pltpu.roll(p, GRID_W, axis=0))
            dn = jnp.where(row >= seq - GRID_W, 0.0, pltpu.roll(p, seq - GRID_W, axis=0))
            h = 0.5 * (prev + nxt)
            v = 0.5 * (up + dn)
            pa_ref[0] = p + muh_ref[...] * (h - p) + muv_ref[...] * (v - p)
        else:
            prev = jnp.where(row == 0, 0.0, prev)
            nxt = jnp.where(row == seq - 1, 0.0, nxt)
            h = 0.5 * (prev + nxt)
            pa_ref[0] = p + muh_ref[...] * (h - p)

    @pl.when(j >= NA)
    def _():
        pb_ref[0] = p


def _proj_call(x, shift, scale, norm_g, w_in_bf16, mu_h, mu_v, grid_shift):
    bsz, seq, _ = x.shape
    kern = functools.partial(_proj_kernel, seq=seq, grid_shift=grid_shift)
    a_idx = lambda b, j: (b, 0, jnp.minimum(j, NA - 1))
    b_idx = lambda b, j: (b, 0, jnp.maximum(j - NA, 0))
    mu_idx = lambda b, j: (0, jnp.minimum(j, NA - 1))
    return pl.pallas_call(
        kern,
        grid=(bsz, NA + NB),
        in_specs=[pl.BlockSpec((1, seq, D_MODEL), lambda b, j: (b, 0, 0)),
                  pl.BlockSpec((1, 1, D_MODEL), lambda b, j: (b, 0, 0)),
                  pl.BlockSpec((1, 1, D_MODEL), lambda b, j: (b, 0, 0)),
                  pl.BlockSpec((1, D_MODEL), lambda b, j: (0, 0)),
                  pl.BlockSpec((D_MODEL, TN), lambda b, j: (0, j)),
                  pl.BlockSpec((1, TN), mu_idx),
                  pl.BlockSpec((1, TN), mu_idx)],
        out_specs=[pl.BlockSpec((1, seq, TN), a_idx),
                   pl.BlockSpec((1, seq, TN), b_idx)],
        out_shape=[jax.ShapeDtypeStruct((bsz, seq, A_COLS), F32),
                   jax.ShapeDtypeStruct((bsz, seq, B_COLS), F32)],
        scratch_shapes=[pltpu.VMEM((seq, D_MODEL), BF16)],
        compiler_params=pltpu.CompilerParams(dimension_semantics=("arbitrary", "arbitrary"),
                                             vmem_limit_bytes=VMEM_LIMIT),
        name="proj_grid" if grid_shift else "proj_seq",
    )(x, shift, scale, norm_g, w_in_bf16, mu_h, mu_v)


def _scan_masks():
    c = CHUNK
    t = _iota((c, PACK), 0)
    s = _iota((c, PACK), 1) & (c - 1)
    tt = _iota((c, c), 0)
    ss = _iota((c, c), 1)
    f01 = lambda cond: jnp.where(cond, 1.0, 0.0).astype(F32)
    bm64 = _block_mask(PACK, PACK, 6, 6)
    bm128 = _block_mask(PACK, PACK, 7, 7)
    return dict(
        bm64=bm64, bm64_bf=bm64.astype(BF16), bm128=bm128,
        bm_hg_bf=_block_mask(4 * c, D_B, 6, 7).astype(BF16),
        eye=f01(t == s),
        incl=(f01(t >= s), f01(t <= s)),
        strict=(f01(t > s), f01(t < s)),
        tri=(f01(tt >= ss).astype(BF16), f01(tt <= ss).astype(BF16)),
    )


def _rwkv_group(rt, kt, kh, bh, kb, bb, v, s_bd, egc, mk, d):
    c = CHUNK
    bd = lambda x: _masked(_tile_rows(_split(x), PACK // c), mk["bm64_bf"])
    kr = _split(jnp.concatenate([kt, rt], axis=0))
    aa_k = _mm(kr, bd(kh), NT)
    aa_b = _mm(kr, bd(bh), NT)
    a_kk = aa_k[:c] * mk["strict"][d]
    a_rk = aa_k[c:] * mk["incl"][d]
    a_kb = aa_b[:c] * mk["strict"][d]
    a_rb = aa_b[c:] * mk["incl"][d]
    krs = _mm(kr, s_bd, NT)
    bdv = bd(v)
    rhs = krs[:c] + _mm(a_kk, bdv)
    n = -a_kb
    x = mk["eye"] + n
    p = _mm(n, bd(n))
    for i in range(5):
        if i < 4:
            xp = _mm(jnp.concatenate([x, p], axis=0), bd(p))
            x = x + xp[:c]
            p = xp[c:]
        else:
            x = x + _mm(x, bd(p))
    u = _mm(x, bd(rhs))
    y = krs[c:] + _mm(a_rk, bdv) - _mm(a_rb, bd(u))
    vu = jnp.concatenate([v, -u], axis=0)
    kbb = jnp.concatenate([kb, bb], axis=0)
    s_new = (s_bd * egc + _mm(vu, kbb, TN_DIMS)) * mk["bm64"]
    return y, s_new


def _scan_dir(pa, pb, d, prm, sa_scr, sb_scr, mk):
    c = CHUNK
    w0, a0, wa2, k_k, k_a, lb = prm
    r = pa[:, 0:D_A]
    k = pa[:, D_A:2 * D_A]
    v = pa[:, 2 * D_A:3 * D_A]
    lo = pa[:, 4 * D_A + LORA * d:4 * D_A + LORA * (d + 1)]
    lo = jnp.where(_iota((c, LORA), 1) < LORA // 2, jnp.tanh(lo), lo)
    wa = _mm(lo, wa2[d])
    w_log = w0[d:d + 1] + wa[:, :D_A]
    lw = -jnp.exp(-_softplus(-w_log) - 0.5)
    a = _sigmoid(a0[d:d + 1] + wa[:, D_A:])
    kk = k * k_k
    bm64_bf = mk["bm64_bf"]
    sq = kk * kk
    n2 = jnp.concatenate([_mm_exact_rhs(sq[:, :PACK], bm64_bf), _mm_exact_rhs(sq[:, PACK:], bm64_bf)], axis=1)
    kap = kk / jnp.maximum(jnp.sqrt(n2), 1e-12)
    keff = k * (1.0 + (a - 1.0) * k_a)
    b = kap * a

    q = _silu(pb[:, 0:D_B])
    vb = pb[:, D_B:2 * D_B]
    f = lb[d:d + 1] + (1.0 - lb[d:d + 1]) * _sigmoid(pb[:, (2 + d) * D_B:(3 + d) * D_B])
    kf = 1.0 - f
    gf = jnp.log(f)

    gcum = _mm_exact_lhs(mk["tri"][d], jnp.concatenate([lw, gf], axis=1))
    last = c - 1 if d == 0 else 0
    g = gcum[:, :D_A]
    gc = g[last:last + 1]
    eg = jnp.exp(g)
    eng = jnp.exp(-g)
    ed = jnp.exp(gc - g)
    egc = jnp.exp(gc)
    rt = r * eg
    kt = kap * jnp.exp(g - lw)
    kh = keff * eng
    bh = b * eng
    kb = keff * ed
    bb = b * ed
    ys = []
    for grp in range(D_A // PACK):
        sl = slice(PACK * grp, PACK * (grp + 1))
        y, s_new = _rwkv_group(rt[:, sl], kt[:, sl], kh[:, sl], bh[:, sl], kb[:, sl], bb[:, sl], v[:, sl],
                               sa_scr[d, grp], egc[:, sl], mk, d)
        sa_scr[d, grp] = s_new
        ys.append(y)
    y = jnp.concatenate(ys, axis=1)

    gh = gcum[:, D_A:]
    ghc = gh[last:last + 1]
    ghn = gh - gh[c // 2:c // 2 + 1]
    qn = q * jnp.exp(ghn)
    kn = kf * jnp.exp(-ghn)
    qt = q * jnp.exp(gh)
    kbh = kf * jnp.exp(ghc - gh)
    eghc = jnp.exp(ghc)
    bm_hg = mk["bm_hg_bf"]
    a_p = _mm(qn, _masked(_tile_rows(_split(kn), 4), bm_hg), NT) * mk["incl"][d]
    o = _mm(a_p, _masked(_tile_rows(_split(vb), 4), bm_hg))
    os_ = []
    for grp in range(D_B // PACK):
        sl = slice(PACK * grp, PACK * (grp + 1))
        st = sb_scr[d, grp]
        os_.append(o[:, sl] + _mm(qt[:, sl], st, NT))
        sb_scr[d, grp] = (st * eghc[:, sl] + _mm(vb[:, sl], kbh[:, sl], TN_DIMS)) * mk["bm128"]
    return y, jnp.concatenate(os_, axis=1)


def _scan_kernel(paf_ref, pbf_ref, pab_ref, pbb_ref, sa0_ref, sb0_ref, w0_ref, a0_ref, wa2_ref, kk_ref,
                 ka_ref, lb_ref, yf_ref, yb_ref, of_ref, ob_ref, *rest, want_state):
    if want_state:
        sa_out, sb_out, sa_scr, sb_scr = rest
    else:
        sa_scr, sb_scr = rest
    ci = pl.program_id(1)

    @pl.when(ci == 0)
    def _():
        sa_scr[...] = sa0_ref[0]
        sb_scr[...] = sb0_ref[0]

    mk = _scan_masks()
    prm = (w0_ref[...], a0_ref[...], wa2_ref, kk_ref[...], ka_ref[...], lb_ref[...])
    y, o = _scan_dir(paf_ref[0], pbf_ref[0], 0, prm, sa_scr, sb_scr, mk)
    yf_ref[0] = y
    of_ref[0] = o
    y, o = _scan_dir(pab_ref[0], pbb_ref[0], 1, prm, sa_scr, sb_scr, mk)
    yb_ref[0] = y
    ob_ref[0] = o

    if want_state:
        @pl.when(ci == pl.num_programs(1) - 1)
        def _():
            sa_out[0] = sa_scr[...]
            sb_out[0] = sb_scr[...]


def _scan_call(pa, pb, sa0_bd, sb0_bd, w0, a0, wa2, k_k, k_a, lb, want_state):
    bsz, seq, _ = pa.shape
    nc = seq // CHUNK
    fwd = lambda b, c: (b, c, 0)
    bwd = lambda b, c: (b, nc - 1 - c, 0)
    st = lambda b, c: (b, 0, 0, 0, 0)
    full = lambda shape: pl.BlockSpec(shape, lambda b, c: (0,) * len(shape))
    y_shape = jax.ShapeDtypeStruct((bsz, seq, D_A), F32)
    out_specs = [pl.BlockSpec((1, CHUNK, D_A), fwd), pl.BlockSpec((1, CHUNK, D_A), bwd),
                 pl.BlockSpec((1, CHUNK, D_B), fwd), pl.BlockSpec((1, CHUNK, D_B), bwd)]
    out_shape = [y_shape, y_shape, y_shape, y_shape]
    st_block = (1, 2, 2, PACK, PACK)
    if want_state:
        out_specs += [pl.BlockSpec(st_block, st), pl.BlockSpec(st_block, st)]
        out_shape += [jax.ShapeDtypeStruct(sa0_bd.shape, F32), jax.ShapeDtypeStruct(sb0_bd.shape, F32)]
    return pl.pallas_call(
        functools.partial(_scan_kernel, want_state=want_state),
        grid=(bsz, nc),
        in_specs=[pl.BlockSpec((1, CHUNK, A_COLS), fwd), pl.BlockSpec((1, CHUNK, B_COLS), fwd),
                  pl.BlockSpec((1, CHUNK, A_COLS), bwd), pl.BlockSpec((1, CHUNK, B_COLS), bwd),
                  pl.BlockSpec(st_block, st), pl.BlockSpec(st_block, st),
                  full((2, D_A)), full((2, D_A)), full((2, LORA, 2 * D_A)),
                  full((1, D_A)), full((1, D_A)), full((2, D_B))],
        out_specs=out_specs,
        out_shape=out_shape,
        scratch_shapes=[pltpu.VMEM((2, 2, PACK, PACK), F32), pltpu.VMEM((2, 2, PACK, PACK), F32)],
        compiler_params=pltpu.CompilerParams(dimension_semantics=("arbitrary", "arbitrary"),
                                             vmem_limit_bytes=VMEM_LIMIT),
        name="scan_state" if want_state else "scan",
    )(pa, pb, pa, pb, sa0_bd, sb0_bd, w0, a0, wa2, k_k, k_a, lb)


def _out_kernel(x_ref, pa_ref, zb_ref, yf_ref, yb_ref, of_ref, ob_ref, gate_ref, a0_ref, a2p_ref, ka_ref,
                rk_ref, lnw_ref, lnb_ref, og_ref, wout_ref, fg_ref, out_ref):
    pa = pa_ref[0]
    tm = pa.shape[0]
    r = pa[:, 0:D_A]
    k = pa[:, D_A:2 * D_A]
    v = pa[:, 2 * D_A:3 * D_A]
    za = pa[:, 3 * D_A:4 * D_A]
    bm64_bf = _block_mask(PACK, PACK, 6, 6).astype(BF16)

    def seg_sum(z):
        return jnp.concatenate([_mm_exact_rhs(z[:, :PACK], bm64_bf), _mm_exact_rhs(z[:, PACK:], bm64_bf)],
                               axis=1)

    y = yf_ref[0] + yb_ref[0]
    mu = seg_sum(y) * (1.0 / HEAD_A)
    dlt = y - mu
    var = seg_sum(dlt * dlt) * (1.0 / HEAD_A)
    yn = dlt * lax.rsqrt(var + GN_EPS) * lnw_ref[...] + lnb_ref[...]
    keffs = []
    for d in range(2):
        lo = pa[:, 4 * D_A + LORA * d:4 * D_A + LORA * (d + 1)]
        a = _sigmoid(a0_ref[d:d + 1] + _mm(lo, a2p_ref[d]))
        keffs.append(k * (1.0 + (a - 1.0) * ka_ref[...]))
    kmean = 0.5 * (keffs[0] + keffs[1])
    bonus = seg_sum(r * kmean * rk_ref[...]) * v
    out_a = (yn + bonus) * _silu(za)

    o = of_ref[0] + ob_ref[0]
    og = og_ref[...]
    zb = zb_ref[0]
    outs = [out_a]
    for h in range(D_B // HEAD_B):
        sl = slice(HEAD_B * h, HEAD_B * (h + 1))
        oh = o[:, sl]
        oh = oh * lax.rsqrt(jnp.mean(oh * oh, axis=-1, keepdims=True) + EPS) * og[:, sl]
        outs.append(oh * _silu(zb[:, sl]))
    mix = jnp.concatenate(outs, axis=1)
    proj = _dot(mix.astype(BF16), wout_ref[...])
    hs = x_ref[0] + gate_ref[0] * proj
    out_ref[0] = hs * lax.rsqrt(jnp.mean(hs * hs, axis=-1, keepdims=True) + EPS) * fg_ref[...]


def _out_call(x, pa, pb, yf, yb, of, ob, gate, a0, a2p, k_a, r_k, lnx_w, lnx_b, onorm_g, w_out_bf16, final_g):
    bsz, seq, _ = x.shape
    tm = 256
    tok = lambda b, i: (b, i, 0)
    full = lambda shape: pl.BlockSpec(shape, lambda b, i: (0,) * len(shape))
    return pl.pallas_call(
        _out_kernel,
        grid=(bsz, seq // tm),
        in_specs=[pl.BlockSpec((1, tm, D_MODEL), tok),
                  pl.BlockSpec((1, tm, A_COLS), tok),
                  pl.BlockSpec((1, tm, D_B), lambda b, i: (b, i, 4)),
                  pl.BlockSpec((1, tm, D_A), tok), pl.BlockSpec((1, tm, D_A), tok),
                  pl.BlockSpec((1, tm, D_B), tok), pl.BlockSpec((1, tm, D_B), tok),
                  pl.BlockSpec((1, 1, D_MODEL), lambda b, i: (b, 0, 0)),
                  full((2, D_A)), full((2, LORA, D_A)), full((1, D_A)), full((1, D_A)),
                  full((1, D_A)), full((1, D_A)), full((1, D_B)),
                  full((D_MODEL, D_MODEL)), full((1, D_MODEL))],
        out_specs=pl.BlockSpec((1, tm, D_MODEL), tok),
        out_shape=jax.ShapeDtypeStruct((bsz, seq, D_MODEL), F32),
        compiler_params=pltpu.CompilerParams(dimension_semantics=("arbitrary", "arbitrary"),
                                             vmem_limit_bytes=VMEM_LIMIT),
        name="out",
    )(x, pa, pb, yf, yb, of, ob, gate, a0, a2p, k_a, r_k, lnx_w, lnx_b, onorm_g, w_out_bf16, final_g)


def _to_block_diag(s, nb):
    *lead, h, n, m = s.shape
    s = s.reshape(*lead, h // nb, nb, n, m)
    eye = jnp.eye(nb, dtype=s.dtype)
    out = s[..., :, :, None, :] * eye[:, None, :, None]
    return out.reshape(*lead, h // nb, nb * n, nb * m)


def _from_block_diag(sbd, nb):
    *lead, g, nn, mm = sbd.shape
    n, m = nn // nb, mm // nb
    s = sbd.reshape(*lead, g, nb, n, nb, m)
    s = jnp.stack([s[..., i, :, i, :] for i in range(nb)], axis=-3)
    return s.reshape(*lead, g * nb, n, m)


def kernel(x_prompt, x_sample, state_rwkv, state_hgrn, c, c_ctx, norm_g, w_ada, b_ada, w_in, mu_h, mu_v, w0, w2,
           a0, a2, k_k, k_a, r_k, lnx_w, lnx_b, lb_logits, onorm_g, w_out, final_g):
    l = 0
    bp = x_prompt.shape[0]
    bs = x_sample.shape[0]
    lb_all = jnp.cumsum(jax.nn.softmax(lb_logits.astype(F32), axis=0), axis=0)
    lb = lb_all[l]

    w_in_bf = w_in[l].astype(BF16)
    w_out_bf = w_out[l].astype(BF16)
    zeros = jnp.zeros((2, LORA // 2, D_A), F32)
    wa2 = jnp.concatenate([jnp.concatenate([w2[l], zeros], axis=2),
                           jnp.concatenate([zeros, a2[l]], axis=2)], axis=1)
    a2p = jnp.concatenate([zeros, a2[l]], axis=1)
    row = lambda z: z.reshape(1, -1)

    cc = jnp.concatenate([c_ctx[None, :], c, jnp.zeros((16 - 1 - bs, D_MODEL), F32)], axis=0)
    m = _ada_call(cc, w_ada[l], row(b_ada[l]))
    shift, scale, gate = m[:, :D_MODEL], m[:, D_MODEL:2 * D_MODEL], m[:, 2 * D_MODEL:]
    ctx = lambda z: jnp.broadcast_to(z[0:1, None, :], (bp, 1, D_MODEL))
    lat = lambda z: z[1:1 + bs, None, :]

    def path(x, sh, sc, gt, sa0, sb0, grid_shift, want_state):
        pa, pb = _proj_call(x, sh, sc, row(norm_g[l]), w_in_bf, row(mu_h[l]), row(mu_v[l]), grid_shift)
        sa0_bd = _to_block_diag(sa0, PACK // HEAD_A)
        sb0_bd = _to_block_diag(jnp.swapaxes(sb0, -1, -2), PACK // HEAD_B)
        res = _scan_call(pa, pb, sa0_bd, sb0_bd, w0[l], a0[l], wa2, row(k_k[l]), row(k_a[l]), lb, want_state)
        yf, yb, of, ob = res[:4]
        y = _out_call(x, pa, pb, yf, yb, of, ob, gt, a0[l], a2p, row(k_a[l]), row(r_k[l]), row(lnx_w[l]),
                      row(lnx_b[l]), row(onorm_g[l]), w_out_bf, row(final_g))
        if not want_state:
            return y, None, None
        sa = _from_block_diag(res[4], PACK // HEAD_A)
        sb = jnp.swapaxes(_from_block_diag(res[5], PACK // HEAD_B), -1, -2)
        return y, sa, sb

    zero_a = jnp.zeros((bp, 2, D_A // HEAD_A, HEAD_A, HEAD_A), F32)
    zero_b = jnp.zeros((bp, 2, D_B // HEAD_B, HEAD_B, HEAD_B), F32)
    y_prompt, s_a, s_b = path(x_prompt, ctx(shift), ctx(scale), ctx(gate), zero_a, zero_b, False, True)
    y_sample, _, _ = path(x_sample, lat(shift), lat(scale), lat(gate), state_rwkv[:, l], state_hgrn[:, l],
                          True, False)
    return y_prompt, y_sample, s_a[:, None], s_b[:, None]
```

```python
import functools

import jax
import jax.numpy as jnp
from jax import lax
from jax.experimental import pallas as pl
from jax.experimental.pallas import tpu as pltpu

F32 = jnp.float32
BF16 = jnp.bfloat16

D_MODEL = 1024
D_A = 512
D_B = 512
HEAD_A = 64
HEAD_B = 128
LORA = 128
A_COLS = 4 * D_A + 2 * LORA
B_COLS = 5 * D_B
GRID_W = 64
CHUNK = 64
EPS = 1e-6
GN_EPS = 64e-5
PACK = 256
TN = 256
NA = A_COLS // TN
NB = B_COLS // TN
VMEM_LIMIT = 56 * 1024 * 1024

NN = ((1,), (0,))
NT = ((1,), (1,))
TN_DIMS = ((0,), (0,))


def _dot(a, b, dims=NN):
    return lax.dot_general(a, b, (dims, ((), ())), preferred_element_type=F32)


def _split(x, passes=1):
    if isinstance(x, tuple):
        return x
    hi = x.astype(BF16)
    if passes == 1:
        return (hi,)
    return hi, (x - hi.astype(F32)).astype(BF16)


def _mm(a, b, dims=NN, passes=1):
    a = _split(a, passes)
    b = _split(b, passes)
    if len(a) == 1 or len(b) == 1:
        return _dot(a[0], b[0], dims)
    return _dot(a[0], b[0], dims) + (_dot(a[0], b[1], dims) + _dot(a[1], b[0], dims))


def _split3(x):
    h1 = x.astype(BF16)
    r1 = x - h1.astype(F32)
    h2 = r1.astype(BF16)
    h3 = (r1 - h2.astype(F32)).astype(BF16)
    return h1, h2, h3


def _mm_exact_lhs(a_bf16, b):
    b1, b2, b3 = _split3(b)
    return _dot(a_bf16, b1) + (_dot(a_bf16, b2) + _dot(a_bf16, b3))


def _mm_exact_rhs(a, b_bf16):
    a1, a2, a3 = _split3(a)
    return _dot(a1, b_bf16) + (_dot(a2, b_bf16) + _dot(a3, b_bf16))


def _sigmoid(x):
    return 1.0 / (1.0 + jnp.exp(-x))


def _silu(x):
    return x * _sigmoid(x)


def _softplus(x):
    return jnp.maximum(x, 0.0) + jnp.log(1.0 + jnp.exp(-jnp.abs(x)))


def _iota(shape, dim):
    return lax.broadcasted_iota(jnp.int32, shape, dim)


def _block_mask(rows, cols, row_shift, col_shift):
    same = (_iota((rows, cols), 0) >> row_shift) == (_iota((rows, cols), 1) >> col_shift)
    return jnp.where(same, 1.0, 0.0).astype(F32)


def _tile_rows(x, n):
    if isinstance(x, tuple):
        return tuple(_tile_rows(t, n) for t in x)
    return jnp.concatenate([x] * n, axis=0)


def _masked(x, mask_bf16):
    if isinstance(x, tuple):
        return tuple(_masked(t, mask_bf16) for t in x)
    return x * mask_bf16


def _ada_kernel(c_ref, w_ref, b_ref, m_ref):
    m_ref[...] = _mm(_silu(c_ref[...]), w_ref[...], passes=3) + b_ref[...]


def _ada_call(cc, w_ada, b_ada):
    rows = cc.shape[0]
    return pl.pallas_call(
        _ada_kernel,
        grid=(3,),
        in_specs=[pl.BlockSpec((rows, D_MODEL), lambda j: (0, 0)),
                  pl.BlockSpec((D_MODEL, D_MODEL), lambda j: (0, j)),
                  pl.BlockSpec((1, D_MODEL), lambda j: (0, j))],
        out_specs=pl.BlockSpec((rows, D_MODEL), lambda j: (0, j)),
        out_shape=jax.ShapeDtypeStruct((rows, 3 * D_MODEL), F32),
        compiler_params=pltpu.CompilerParams(dimension_semantics=("arbitrary",),
                                             vmem_limit_bytes=VMEM_LIMIT),
        name="ada",
    )(cc, w_ada, b_ada)


def _proj_kernel(x_ref, sh_ref, sc_ref, g_ref, w_ref, muh_ref, muv_ref, pa_ref, pb_ref, xm_ref, *,
                 seq, grid_shift):
    j = pl.program_id(1)

    @pl.when(j == 0)
    def _():
        x = x_ref[0]
        xn = x * lax.rsqrt(jnp.mean(x * x, axis=-1, keepdims=True) + EPS) * g_ref[...]
        xm_ref[...] = (xn * (1.0 + sc_ref[0]) + sh_ref[0]).astype(BF16)

    p = _dot(xm_ref[...], w_ref[...])

    @pl.when(j < NA)
    def _():
        row = _iota((seq, TN), 0)
        prev = pltpu.roll(p, 1, axis=0)
        nxt = pltpu.roll(p, seq - 1, axis=0)
        if grid_shift:
            col = row & (GRID_W - 1)
            prev = jnp.where(col == 0, 0.0, prev)
            nxt = jnp.where(col == GRID_W - 1, 0.0, nxt)
            up = jnp.where(row < GRID_W, 0.0, pltpu.roll(p, GRID_W, axis=0))
            dn = jnp.where(row >= seq - GRID_W, 0.0, pltpu.roll(p, seq - GRID_W, axis=0))
            h = 0.5 * (prev + nxt)
            v = 0.5 * (up + dn)
            pa_ref[0] = p + muh_ref[...] * (h - p) + muv_ref[...] * (v - p)
        else:
            prev = jnp.where(row == 0, 0.0, prev)
            nxt = jnp.where(row == seq - 1, 0.0, nxt)
            h = 0.5 * (prev + nxt)
            pa_ref[0] = p + muh_ref[...] * (h - p)

    @pl.when(j >= NA)
    def _():
        pb_ref[0] = p


def _proj_call(x, shift, scale, norm_g, w_in_bf16, mu_h, mu_v, grid_shift):
    bsz, seq, _ = x.shape
    kern = functools.partial(_proj_kernel, seq=seq, grid_shift=grid_shift)
    a_idx = lambda b, j: (b, 0, jnp.minimum(j, NA - 1))
    b_idx = lambda b, j: (b, 0, jnp.maximum(j - NA, 0))
    mu_idx = lambda b, j: (0, jnp.minimum(j, NA - 1))
    return pl.pallas_call(
        kern,
        grid=(bsz, NA + NB),
        in_specs=[pl.BlockSpec((1, seq, D_MODEL), lambda b, j: (b, 0, 0)),
                  pl.BlockSpec((1, 1, D_MODEL), lambda b, j: (b, 0, 0)),
                  pl.BlockSpec((1, 1, D_MODEL), lambda b, j: (b, 0, 0)),
                  pl.BlockSpec((1, D_MODEL), lambda b, j: (0, 0)),
                  pl.BlockSpec((D_MODEL, TN), lambda b, j: (0, j)),
                  pl.BlockSpec((1, TN), mu_idx),
                  pl.BlockSpec((1, TN), mu_idx)],
        out_specs=[pl.BlockSpec((1, seq, TN), a_idx),
                   pl.BlockSpec((1, seq, TN), b_idx)],
        out_shape=[jax.ShapeDtypeStruct((bsz, seq, A_COLS), F32),
                   jax.ShapeDtypeStruct((bsz, seq, B_COLS), F32)],
        scratch_shapes=[pltpu.VMEM((seq, D_MODEL), BF16)],
        compiler_params=pltpu.CompilerParams(dimension_semantics=("arbitrary", "arbitrary"),
                                             vmem_limit_bytes=VMEM_LIMIT),
        name="proj_grid" if grid_shift else "proj_seq",
    )(x, shift, scale, norm_g, w_in_bf16, mu_h, mu_v)


def _scan_masks():
    c = CHUNK
    t = _iota((c, PACK), 0)
    s = _iota((c, PACK), 1) & (c - 1)
    tt = _iota((c, c), 0)
    ss = _iota((c, c), 1)
    f01 = lambda cond: jnp.where(cond, 1.0, 0.0).astype(F32)
    bm64 = _block_mask(PACK, PACK, 6, 6)
    bm128 = _block_mask(PACK, PACK, 7, 7)
    return dict(
        bm64=bm64, bm64_bf=bm64.astype(BF16), bm128=bm128,
        bm_hg_bf=_block_mask(4 * c, D_B, 6, 7).astype(BF16),
        eye=f01(t == s),
        incl=(f01(t >= s), f01(t <= s)),
        strict=(f01(t > s), f01(t < s)),
        tri=(f01(tt >= ss).astype(BF16), f01(tt <= ss).astype(BF16)),
    )


def _rwkv_group(rt, kt, kh, bh, kb, bb, v, s_bd, egc, mk, d):
    c = CHUNK
    bd = lambda x: _masked(_tile_rows(_split(x), PACK // c), mk["bm64_bf"])
    kr = _split(jnp.concatenate([kt, rt], axis=0))
    aa_k = _mm(kr, bd(kh), NT)
    aa_b = _mm(kr, bd(bh), NT)
    a_kk = aa_k[:c] * mk["strict"][d]
    a_rk = aa_k[c:] * mk["incl"][d]
    a_kb = aa_b[:c] * mk["strict"][d]
    a_rb = aa_b[c:] * mk["incl"][d]
    krs = _mm(kr, s_bd, NT)
    bdv = bd(v)
    rhs = krs[:c] + _mm(a_kk, bdv)
    n = -a_kb
    x = mk["eye"] + n
    p = _mm(n, bd(n))
    for i in range(5):
        if i < 4:
            xp = _mm(jnp.concatenate([x, p], axis=0), bd(p))
            x = x + xp[:c]
            p = xp[c:]
        else:
            x = x + _mm(x, bd(p))
    u = _mm(x, bd(rhs))
    y = krs[c:] + _mm(a_rk, bdv) - _mm(a_rb, bd(u))
    vu = jnp.concatenate([v, -u], axis=0)
    kbb = jnp.concatenate([kb, bb], axis=0)
    s_new = (s_bd * egc + _mm(vu, kbb, TN_DIMS)) * mk["bm64"]
    return y, s_new


def _scan_dir(pa, pb, d, prm, sa_scr, sb_scr, mk):
    c = CHUNK
    w0, a0, wa2, k_k, k_a, lb = prm
    r = pa[:, 0:D_A]
    k = pa[:, D_A:2 * D_A]
    v = pa[:, 2 * D_A:3 * D_A]
    lo = pa[:, 4 * D_A + LORA * d:4 * D_A + LORA * (d + 1)]
    lo = jnp.where(_iota((c, LORA), 1) < LORA // 2, jnp.tanh(lo), lo)
    wa = _mm(lo, wa2[d])
    w_log = w0[d:d + 1] + wa[:, :D_A]
    lw = -jnp.exp(-_softplus(-w_log) - 0.5)
    a = _sigmoid(a0[d:d + 1] + wa[:, D_A:])
    kk = k * k_k
    bm64_bf = mk["bm64_bf"]
    sq = kk * kk
    n2 = jnp.concatenate([_mm_exact_rhs(sq[:, :PACK], bm64_bf), _mm_exact_rhs(sq[:, PACK:], bm64_bf)], axis=1)
    kap = kk / jnp.maximum(jnp.sqrt(n2), 1e-12)
    keff = k * (1.0 + (a - 1.0) * k_a)
    b = kap * a

    q = _silu(pb[:, 0:D_B])
    vb = pb[:, D_B:2 * D_B]
    f = lb[d:d + 1] + (1.0 - lb[d:d + 1]) * _sigmoid(pb[:, (2 + d) * D_B:(3 + d) * D_B])
    kf = 1.0 - f
    gf = jnp.log(f)

    gcum = _mm_exact_lhs(mk["tri"][d], jnp.concatenate([lw, gf], axis=1))
    last = c - 1 if d == 0 else 0
    g = gcum[:, :D_A]
    gc = g[last:last + 1]
    eg = jnp.exp(g)
    eng = jnp.exp(-g)
    ed = jnp.exp(gc - g)
    egc = jnp.exp(gc)
    rt = r * eg
    kt = kap * jnp.exp(g - lw)
    kh = keff * eng
    bh = b * eng
    kb = keff * ed
    bb = b * ed
    ys = []
    for grp in range(D_A // PACK):
        sl = slice(PACK * grp, PACK * (grp + 1))
        y, s_new = _rwkv_group(rt[:, sl], kt[:, sl], kh[:, sl], bh[:, sl], kb[:, sl], bb[:, sl], v[:, sl],
                               sa_scr[d, grp], egc[:, sl], mk, d)
        sa_scr[d, grp] = s_new
        ys.append(y)
    y = jnp.concatenate(ys, axis=1)

    gh = gcum[:, D_A:]
    ghc = gh[last:last + 1]
    ghn = gh - gh[c // 2:c // 2 + 1]
    qn = q * jnp.exp(ghn)
    kn = kf * jnp.exp(-ghn)
    qt = q * jnp.exp(gh)
    kbh = kf * jnp.exp(ghc - gh)
    eghc = jnp.exp(ghc)
    bm_hg = mk["bm_hg_bf"]
    a_p = _mm(qn, _masked(_tile_rows(_split(kn), 4), bm_hg), NT) * mk["incl"][d]
    o = _mm(a_p, _masked(_tile_rows(_split(vb), 4), bm_hg))
    os_ = []
    for grp in range(D_B // PACK):
        sl = slice(PACK * grp, PACK * (grp + 1))
        st = sb_scr[d, grp]
        os_.append(o[:, sl] + _mm(qt[:, sl], st, NT))
        sb_scr[d, grp] = (st * eghc[:, sl] + _mm(vb[:, sl], kbh[:, sl], TN_DIMS)) * mk["bm128"]
    return y, jnp.concatenate(os_, axis=1)


def _scan_kernel(paf_ref, pbf_ref, pab_ref, pbb_ref, sa0_ref, sb0_ref, w0_ref, a0_ref, wa2_ref, kk_ref,
                 ka_ref, lb_ref, yf_ref, yb_ref, of_ref, ob_ref, *rest, want_state):
    if want_state:
        sa_out, sb_out, sa_scr, sb_scr = rest
    else:
        sa_scr, sb_scr = rest
    ci = pl.program_id(1)

    @pl.when(ci == 0)
    def _():
        sa_scr[...] = sa0_ref[0]
        sb_scr[...] = sb0_ref[0]

    mk = _scan_masks()
    prm = (w0_ref[...], a0_ref[...], wa2_ref, kk_ref[...], ka_ref[...], lb_ref[...])
    y, o = _scan_dir(paf_ref[0], pbf_ref[0], 0, prm, sa_scr, sb_scr, mk)
    yf_ref[0] = y
    of_ref[0] = o
    y, o = _scan_dir(pab_ref[0], pbb_ref[0], 1, prm, sa_scr, sb_scr, mk)
    yb_ref[0] = y
    ob_ref[0] = o

    if want_state:
        @pl.when(ci == pl.num_programs(1) - 1)
        def _():
            sa_out[0] = sa_scr[...]
            sb_out[0] = sb_scr[...]


def _scan_call(pa, pb, sa0_bd, sb0_bd, w0, a0, wa2, k_k, k_a, lb, want_state):
    bsz, seq, _ = pa.shape
    nc = seq // CHUNK
    fwd = lambda b, c: (b, c, 0)
    bwd = lambda b, c: (b, nc - 1 - c, 0)
    st = lambda b, c: (b, 0, 0, 0, 0)
    full = lambda shape: pl.BlockSpec(shape, lambda b, c: (0,) * len(shape))
    y_shape = jax.ShapeDtypeStruct((bsz, seq, D_A), F32)
    out_specs = [pl.BlockSpec((1, CHUNK, D_A), fwd), pl.BlockSpec((1, CHUNK, D_A), bwd),
                 pl.BlockSpec((1, CHUNK, D_B), fwd), pl.BlockSpec((1, CHUNK, D_B), bwd)]
    out_shape = [y_shape, y_shape, y_shape, y_shape]
    st_block = (1, 2, 2, PACK, PACK)
    if want_state:
        out_specs += [pl.BlockSpec(st_block, st), pl.BlockSpec(st_block, st)]
        out_shape += [jax.ShapeDtypeStruct(sa0_bd.shape, F32), jax.ShapeDtypeStruct(sb0_bd.shape, F32)]
    return pl.pallas_call(
        functools.partial(_scan_kernel, want_state=want_state),
        grid=(bsz, nc),
        in_specs=[pl.BlockSpec((1, CHUNK, A_COLS), fwd), pl.BlockSpec((1, CHUNK, B_COLS), fwd),
                  pl.BlockSpec((1, CHUNK, A_COLS), bwd), pl.BlockSpec((1, CHUNK, B_COLS), bwd),
                  pl.BlockSpec(st_block, st), pl.BlockSpec(st_block, st),
                  full((2, D_A)), full((2, D_A)), full((2, LORA, 2 * D_A)),
                  full((1, D_A)), full((1, D_A)), full((2, D_B))],
        out_specs=out_specs,
        out_shape=out_shape,
        scratch_shapes=[pltpu.VMEM((2, 2, PACK, PACK), F32), pltpu.VMEM((2, 2, PACK, PACK), F32)],
        compiler_params=pltpu.CompilerParams(dimension_semantics=("arbitrary", "arbitrary"),
                                             vmem_limit_bytes=VMEM_LIMIT),
        name="scan_state" if want_state else "scan",
    )(pa, pb, pa, pb, sa0_bd, sb0_bd, w0, a0, wa2, k_k, k_a, lb)


def _out_kernel(x_ref, pa_ref, zb_ref, yf_ref, yb_ref, of_ref, ob_ref, gate_ref, a0_ref, a2p_ref, ka_ref,
                rk_ref, lnw_ref, lnb_ref, og_ref, wout_ref, fg_ref, out_ref):
    pa = pa_ref[0]
    tm = pa.shape[0]
    r = pa[:, 0:D_A]
    k = pa[:, D_A:2 * D_A]
    v = pa[:, 2 * D_A:3 * D_A]
    za = pa[:, 3 * D_A:4 * D_A]
    bm64_bf = _block_mask(PACK, PACK, 6, 6).astype(BF16)

    def seg_sum(z):
        return jnp.concatenate([_mm_exact_rhs(z[:, :PACK], bm64_bf), _mm_exact_rhs(z[:, PACK:], bm64_bf)],
                               axis=1)

    y = yf_ref[0] + yb_ref[0]
    mu = seg_sum(y) * (1.0 / HEAD_A)
    dlt = y - mu
    var = seg_sum(dlt * dlt) * (1.0 / HEAD_A)
    yn = dlt * lax.rsqrt(var + GN_EPS) * lnw_ref[...] + lnb_ref[...]
    keffs = []
    for d in range(2):
        lo = pa[:, 4 * D_A + LORA * d:4 * D_A + LORA * (d + 1)]
        a = _sigmoid(a0_ref[d:d + 1] + _mm(lo, a2p_ref[d]))
        keffs.append(k * (1.0 + (a - 1.0) * ka_ref[...]))
    kmean = 0.5 * (keffs[0] + keffs[1])
    bonus = seg_sum(r * kmean * rk_ref[...]) * v
    out_a = (yn + bonus) * _silu(za)

    o = of_ref[0] + ob_ref[0]
    og = og_ref[...]
    zb = zb_ref[0]
    outs = [out_a]
    for h in range(D_B // HEAD_B):
        sl = slice(HEAD_B * h, HEAD_B * (h + 1))
        oh = o[:, sl]
        oh = oh * lax.rsqrt(jnp.mean(oh * oh, axis=-1, keepdims=True) + EPS) * og[:, sl]
        outs.append(oh * _silu(zb[:, sl]))
    mix = jnp.concatenate(outs, axis=1)
    proj = _dot(mix.astype(BF16), wout_ref[...])
    hs = x_ref[0] + gate_ref[0] * proj
    out_ref[0] = hs * lax.rsqrt(jnp.mean(hs * hs, axis=-1, keepdims=True) + EPS) * fg_ref[...]


def _out_call(x, pa, pb, yf, yb, of, ob, gate, a0, a2p, k_a, r_k, lnx_w, lnx_b, onorm_g, w_out_bf16, final_g):
    bsz, seq, _ = x.shape
    tm = 256
    tok = lambda b, i: (b, i, 0)
    full = lambda shape: pl.BlockSpec(shape, lambda b, i: (0,) * len(shape))
    return pl.pallas_call(
        _out_kernel,
        grid=(bsz, seq // tm),
        in_specs=[pl.BlockSpec((1, tm, D_MODEL), tok),
                  pl.BlockSpec((1, tm, A_COLS), tok),
                  pl.BlockSpec((1, tm, D_B), lambda b, i: (b, i, 4)),
                  pl.BlockSpec((1, tm, D_A), tok), pl.BlockSpec((1, tm, D_A), tok),
                  pl.BlockSpec((1, tm, D_B), tok), pl.BlockSpec((1, tm, D_B), tok),
                  pl.BlockSpec((1, 1, D_MODEL), lambda b, i: (b, 0, 0)),
                  full((2, D_A)), full((2, LORA, D_A)), full((1, D_A)), full((1, D_A)),
                  full((1, D_A)), full((1, D_A)), full((1, D_B)),
                  full((D_MODEL, D_MODEL)), full((1, D_MODEL))],
        out_specs=pl.BlockSpec((1, tm, D_MODEL), tok),
        out_shape=jax.ShapeDtypeStruct((bsz, seq, D_MODEL), F32),
        compiler_params=pltpu.CompilerParams(dimension_semantics=("arbitrary", "arbitrary"),
                                             vmem_limit_bytes=VMEM_LIMIT),
        name="out",
    )(x, pa, pb, yf, yb, of, ob, gate, a0, a2p, k_a, r_k, lnx_w, lnx_b, onorm_g, w_out_bf16, final_g)


def _to_block_diag(s, nb):
    *lead, h, n, m = s.shape
    s = s.reshape(*lead, h // nb, nb, n, m)
    eye = jnp.eye(nb, dtype=s.dtype)
    out = s[..., :, :, None, :] * eye[:, None, :, None]
    return out.reshape(*lead, h // nb, nb * n, nb * m)


def _from_block_diag(sbd, nb):
    *lead, g, nn, mm = sbd.shape
    n, m = nn // nb, mm // nb
    s = sbd.reshape(*lead, g, nb, n, nb, m)
    s = jnp.stack([s[..., i, :, i, :] for i in range(nb)], axis=-3)
    return s.reshape(*lead, g * nb, n, m)


def kernel(x_prompt, x_sample, state_rwkv, state_hgrn, c, c_ctx, norm_g, w_ada, b_ada, w_in, mu_h, mu_v, w0, w2,
           a0, a2, k_k, k_a, r_k, lnx_w, lnx_b, lb_logits, onorm_g, w_out, final_g):
    l = 0
    bp = x_prompt.shape[0]
    bs = x_sample.shape[0]
    lb_all = jnp.cumsum(jax.nn.softmax(lb_logits.astype(F32), axis=0), axis=0)
    lb = lb_all[l]

    w_in_bf = w_in[l].astype(BF16)
    w_out_bf = w_out[l].astype(BF16)
    zeros = jnp.zeros((2, LORA // 2, D_A), F32)
    wa2 = jnp.concatenate([jnp.concatenate([w2[l], zeros], axis=2),
                           jnp.concatenate([zeros, a2[l]], axis=2)], axis=1)
    a2p = jnp.concatenate([zeros, a2[l]], axis=1)
    row = lambda z: z.reshape(1, -1)

    cc = jnp.concatenate([c_ctx[None, :], c, jnp.zeros((16 - 1 - bs, D_MODEL), F32)], axis=0)
    m = _ada_call(cc, w_ada[l], row(b_ada[l]))
    shift, scale, gate = m[:, :D_MODEL], m[:, D_MODEL:2 * D_MODEL], m[:, 2 * D_MODEL:]
    ctx = lambda z: jnp.broadcast_to(z[0:1, None, :], (bp, 1, D_MODEL))
    lat = lambda z: z[1:1 + bs, None, :]

    def path(x, sh, sc, gt, sa0, sb0, grid_shift, want_state):
        pa, pb = _proj_call(x, sh, sc, row(norm_g[l]), w_in_bf, row(mu_h[l]), row(mu_v[l]), grid_shift)
        sa0_bd = _to_block_diag(sa0, PACK // HEAD_A)
        sb0_bd = _to_block_diag(jnp.swapaxes(sb0, -1, -2), PACK // HEAD_B)
        res = _scan_call(pa, pb, sa0_bd, sb0_bd, w0[l], a0[l], wa2, row(k_k[l]), row(k_a[l]), lb, want_state)
        yf, yb, of, ob = res[:4]
        y = _out_call(x, pa, pb, yf, yb, of, ob, gt, a0[l], a2p, row(k_a[l]), row(r_k[l]), row(lnx_w[l]),
                      row(lnx_b[l]), row(onorm_g[l]), w_out_bf, row(final_g))
        if not want_state:
            return y, None, None
        sa = _from_block_diag(res[4], PACK // HEAD_A)
        sb = jnp.swapaxes(_from_block_diag(res[5], PACK // HEAD_B), -1, -2)
        return y, sa, sb

    zero_a = jnp.zeros((bp, 2, D_A // HEAD_A, HEAD_A, HEAD_A), F32)
    zero_b = jnp.zeros((bp, 2, D_B // HEAD_B, HEAD_B, HEAD_B), F32)
    y_prompt, s_a, s_b = path(x_prompt, ctx(shift), ctx(scale), ctx(gate), zero_a, zero_b, False, True)
    y_sample, _, _ = path(x_sample, lat(shift), lat(scale), lat(gate), state_rwkv[:, l], state_hgrn[:, l],
                          True, False)
    return y_prompt, y_sample, s_a[:, None], s_b[:, None]
```

```python
import functools

import jax
import jax.numpy as jnp
from jax import lax
from jax.experimental import pallas as pl
from jax.experimental.pallas import tpu as pltpu

F32 = jnp.float32
BF16 = jnp.bfloat16

D_MODEL = 1024
D_A = 512
D_B = 512
HEAD_A = 64
HEAD_B = 128
LORA = 128
A_COLS = 4 * D_A + 2 * LORA
B_COLS = 5 * D_B
GRID_W = 64
CHUNK = 64
SCAN_BATCH = 2
EPS = 1e-6
GN_EPS = 64e-5
PACK = 256
TN = 256
NA = A_COLS // TN
NB = B_COLS // TN
VMEM_LIMIT = 56 * 1024 * 1024

NN = ((1,), (0,))
NT = ((1,), (1,))
TN_DIMS = ((0,), (0,))


def _dot(a, b, dims=NN):
    return lax.dot_general(a, b, (dims, ((), ())), preferred_element_type=F32)


def _split(x, passes=1):
    if isinstance(x, tuple):
        return x
    hi = x.astype(BF16)
    if passes == 1:
        return (hi,)
    return hi, (x - hi.astype(F32)).astype(BF16)


def _mm(a, b, dims=NN, passes=1):
    a = _split(a, passes)
    b = _split(b, passes)
    if len(a) == 1 or len(b) == 1:
        return _dot(a[0], b[0], dims)
    return _dot(a[0], b[0], dims) + (_dot(a[0], b[1], dims) + _dot(a[1], b[0], dims))


def _split3(x):
    h1 = x.astype(BF16)
    r1 = x - h1.astype(F32)
    h2 = r1.astype(BF16)
    h3 = (r1 - h2.astype(F32)).astype(BF16)
    return h1, h2, h3


def _mm_exact_lhs(a_bf16, b):
    b1, b2, b3 = _split3(b)
    return _dot(a_bf16, b1) + (_dot(a_bf16, b2) + _dot(a_bf16, b3))


def _mm_exact_rhs(a, b_bf16):
    a1, a2, a3 = _split3(a)
    return _dot(a1, b_bf16) + (_dot(a2, b_bf16) + _dot(a3, b_bf16))


def _sigmoid(x):
    return 1.0 / (1.0 + jnp.exp(-x))


def _silu(x):
    return x * _sigmoid(x)


def _softplus(x):
    return jnp.maximum(x, 0.0) + jnp.log(1.0 + jnp.exp(-jnp.abs(x)))


def _iota(shape, dim):
    return lax.broadcasted_iota(jnp.int32, shape, dim)


def _block_mask(rows, cols, row_shift, col_shift):
    same = (_iota((rows, cols), 0) >> row_shift) == (_iota((rows, cols), 1) >> col_shift)
    return jnp.where(same, 1.0, 0.0).astype(F32)


def _tile_rows(x, n):
    if isinstance(x, tuple):
        return tuple(_tile_rows(t, n) for t in x)
    return jnp.concatenate([x] * n, axis=0)


def _masked(x, mask_bf16):
    if isinstance(x, tuple):
        return tuple(_masked(t, mask_bf16) for t in x)
    return x * mask_bf16


def _ada_kernel(c_ref, w_ref, b_ref, m_ref):
    m_ref[...] = _mm(_silu(c_ref[...]), w_ref[...], passes=3) + b_ref[...]


def _ada_call(cc, w_ada, b_ada):
    rows = cc.shape[0]
    return pl.pallas_call(
        _ada_kernel,
        grid=(3,),
        in_specs=[pl.BlockSpec((rows, D_MODEL), lambda j: (0, 0)),
                  pl.BlockSpec((D_MODEL, D_MODEL), lambda j: (0, j)),
                  pl.BlockSpec((1, D_MODEL), lambda j: (0, j))],
        out_specs=pl.BlockSpec((rows, D_MODEL), lambda j: (0, j)),
        out_shape=jax.ShapeDtypeStruct((rows, 3 * D_MODEL), F32),
        compiler_params=pltpu.CompilerParams(dimension_semantics=("arbitrary",),
                                             vmem_limit_bytes=VMEM_LIMIT),
        name="ada",
    )(cc, w_ada, b_ada)


def _proj_kernel(x_ref, sh_ref, sc_ref, g_ref, w_ref, muh_ref, muv_ref, pa_ref, pb_ref, xm_ref, *,
                 seq, grid_shift):
    j = pl.program_id(1)

    @pl.when(j == 0)
    def _():
        x = x_ref[0]
        xn = x * lax.rsqrt(jnp.mean(x * x, axis=-1, keepdims=True) + EPS) * g_ref[...]
        xm_ref[...] = (xn * (1.0 + sc_ref[0]) + sh_ref[0]).astype(BF16)

    p = _dot(xm_ref[...], w_ref[...])

    @pl.when(j < NA)
    def _():
        row = _iota((seq, TN), 0)
        prev = pltpu.roll(p, 1, axis=0)
        nxt = pltpu.roll(p, seq - 1, axis=0)
        if grid_shift:
            col = row & (GRID_W - 1)
            prev = jnp.where(col == 0, 0.0, prev)
            nxt = jnp.where(col == GRID_W - 1, 0.0, nxt)
            up = jnp.where(row < GRID_W, 0.0, pltpu.roll(p, GRID_W, axis=0))
            dn = jnp.where(row >= seq - GRID_W, 0.0, pltpu.roll(p, seq - GRID_W, axis=0))
            h = 0.5 * (prev + nxt)
            v = 0.5 * (up + dn)
            pa_ref[0] = p + muh_ref[...] * (h - p) + muv_ref[...] * (v - p)
        else:
            prev = jnp.where(row == 0, 0.0, prev)
            nxt = jnp.where(row == seq - 1, 0.0, nxt)
            h = 0.5 * (prev + nxt)
            pa_ref[0] = p + muh_ref[...] * (h - p)

    @pl.when(j >= NA)
    def _():
        pb_ref[0] = p


def _proj_call(x, shift, scale, norm_g, w_in_bf16, mu_h, mu_v, grid_shift):
    bsz, seq, _ = x.shape
    kern = functools.partial(_proj_kernel, seq=seq, grid_shift=grid_shift)
    a_idx = lambda b, j: (b, 0, jnp.minimum(j, NA - 1))
    b_idx = lambda b, j: (b, 0, jnp.maximum(j - NA, 0))
    mu_idx = lambda b, j: (0, jnp.minimum(j, NA - 1))
    return pl.pallas_call(
        kern,
        grid=(bsz, NA + NB),
        in_specs=[pl.BlockSpec((1, seq, D_MODEL), lambda b, j: (b, 0, 0)),
                  pl.BlockSpec((1, 1, D_MODEL), lambda b, j: (b, 0, 0)),
                  pl.BlockSpec((1, 1, D_MODEL), lambda b, j: (b, 0, 0)),
                  pl.BlockSpec((1, D_MODEL), lambda b, j: (0, 0)),
                  pl.BlockSpec((D_MODEL, TN), lambda b, j: (0, j)),
                  pl.BlockSpec((1, TN), mu_idx),
                  pl.BlockSpec((1, TN), mu_idx)],
        out_specs=[pl.BlockSpec((1, seq, TN), a_idx),
                   pl.BlockSpec((1, seq, TN), b_idx)],
        out_shape=[jax.ShapeDtypeStruct((bsz, seq, A_COLS), F32),
                   jax.ShapeDtypeStruct((bsz, seq, B_COLS), F32)],
        scratch_shapes=[pltpu.VMEM((seq, D_MODEL), BF16)],
        compiler_params=pltpu.CompilerParams(dimension_semantics=("arbitrary", "arbitrary"),
                                             vmem_limit_bytes=VMEM_LIMIT),
        name="proj_grid" if grid_shift else "proj_seq",
    )(x, shift, scale, norm_g, w_in_bf16, mu_h, mu_v)


def _scan_masks():
    c = CHUNK
    t = _iota((c, PACK), 0)
    s = _iota((c, PACK), 1) & (c - 1)
    tt = _iota((c, c), 0)
    ss = _iota((c, c), 1)
    f01 = lambda cond: jnp.where(cond, 1.0, 0.0).astype(F32)
    bm64 = _block_mask(PACK, PACK, 6, 6)
    bm128 = _block_mask(PACK, PACK, 7, 7)
    return dict(
        bm64=bm64, bm64_bf=bm64.astype(BF16), bm128=bm128,
        bm_hg_bf=_block_mask(4 * c, D_B, 6, 7).astype(BF16),
        eye=f01(t == s),
        incl=(f01(t >= s), f01(t <= s)),
        strict=(f01(t > s), f01(t < s)),
        tri=(f01(tt >= ss).astype(BF16), f01(tt <= ss).astype(BF16)),
    )


def _lockstep(gens):
    results = [None] * len(gens)
    live = list(enumerate(gens))
    while live:
        still = []
        for i, g in live:
            try:
                next(g)
                still.append((i, g))
            except StopIteration as stop:
                results[i] = stop.value
        live = still
    return results


def _rwkv_group(rt, kt, kh, bh, kb, bb, v, s_ref, egc, mk, d):
    c = CHUNK
    bd = lambda x: _masked(_tile_rows(_split(x), PACK // c), mk["bm64_bf"])
    kr = _split(jnp.concatenate([kt, rt], axis=0))
    aa_k = _mm(kr, bd(kh), NT)
    aa_b = _mm(kr, bd(bh), NT)
    s_bd = s_ref[...]
    krs = _mm(kr, s_bd, NT)
    yield
    a_kk = aa_k[:c] * mk["strict"][d]
    a_rk = aa_k[c:] * mk["incl"][d]
    n = -(aa_b[:c] * mk["strict"][d])
    a_rb = aa_b[c:] * mk["incl"][d]
    x = mk["eye"] + n
    p = _mm(n, bd(n))
    akv = _mm(jnp.concatenate([a_kk, a_rk], axis=0), bd(v))
    yield
    for i in range(5):
        if i < 4:
            xp = _mm(jnp.concatenate([x, p], axis=0), bd(p))
            x = x + xp[:c]
            p = xp[c:]
        else:
            x = x + _mm(x, bd(p))
        yield
    u = _mm(x, bd(krs[:c] + akv[:c]))
    yield
    y = krs[c:] + akv[c:] - _mm(a_rb, bd(u))
    vu = jnp.concatenate([v, -u], axis=0)
    kbb = jnp.concatenate([kb, bb], axis=0)
    s_ref[...] = (s_bd * egc + _mm(vu, kbb, TN_DIMS)) * mk["bm64"]
    return y


def _hgrn_dir(q, kf, vb, gh, s_refs, mk, d):
    c = CHUNK
    last = c - 1 if d == 0 else 0
    ghc = gh[last:last + 1]
    ghn = gh - gh[c // 2:c // 2 + 1]
    qn = q * jnp.exp(ghn)
    kn = kf * jnp.exp(-ghn)
    qt = q * jnp.exp(gh)
    kbh = kf * jnp.exp(ghc - gh)
    eghc = jnp.exp(ghc)
    bm_hg = mk["bm_hg_bf"]
    a_p = _mm(qn, _masked(_tile_rows(_split(kn), 4), bm_hg), NT) * mk["incl"][d]
    inter = []
    for grp, s_ref in enumerate(s_refs):
        sl = slice(PACK * grp, PACK * (grp + 1))
        st = s_ref[...]
        inter.append(_mm(qt[:, sl], st, NT))
        s_ref[...] = (st * eghc[:, sl] + _mm(vb[:, sl], kbh[:, sl], TN_DIMS)) * mk["bm128"]
    yield
    o = _mm(a_p, _masked(_tile_rows(_split(vb), 4), bm_hg))
    return o + jnp.concatenate(inter, axis=1)


def _scan_dir(pa_ref, pb_ref, bi, d, prm, sa_scr, sb_scr, mk):
    c = CHUNK
    w0, a0, wa2, k_k, k_a, lb = prm
    r = pa_ref[bi, :, 0:D_A]
    k = pa_ref[bi, :, D_A:2 * D_A]
    v = pa_ref[bi, :, 2 * D_A:3 * D_A]
    lo = pa_ref[bi, :, 4 * D_A + LORA * d:4 * D_A + LORA * (d + 1)]
    lo = jnp.where(_iota((c, LORA), 1) < LORA // 2, jnp.tanh(lo), lo)
    wa = _mm(lo, wa2[d])
    yield
    w_log = w0[d:d + 1] + wa[:, :D_A]
    lw = -jnp.exp(-_softplus(-w_log) - 0.5)
    a = _sigmoid(a0[d:d + 1] + wa[:, D_A:])
    kk = k * k_k
    bm64_bf = mk["bm64_bf"]
    sq = kk * kk
    n2 = jnp.concatenate([_mm_exact_rhs(sq[:, :PACK], bm64_bf), _mm_exact_rhs(sq[:, PACK:], bm64_bf)], axis=1)
    q = _silu(pb_ref[bi, :, 0:D_B])
    vb = pb_ref[bi, :, D_B:2 * D_B]
    f = lb[d:d + 1] + (1.0 - lb[d:d + 1]) * _sigmoid(pb_ref[bi, :, (2 + d) * D_B:(3 + d) * D_B])
    kf = 1.0 - f
    gf = jnp.log(f)
    gcum = _mm_exact_lhs(mk["tri"][d], jnp.concatenate([lw, gf], axis=1))
    yield
    kap = kk / jnp.maximum(jnp.sqrt(n2), 1e-12)
    keff = k * (1.0 + (a - 1.0) * k_a)
    b = kap * a
    last = c - 1 if d == 0 else 0
    g = gcum[:, :D_A]
    gc = g[last:last + 1]
    eg = jnp.exp(g)
    eng = jnp.exp(-g)
    ed = jnp.exp(gc - g)
    egc = jnp.exp(gc)
    rt = r * eg
    kt = kap * jnp.exp(g - lw)
    kh = keff * eng
    bh = b * eng
    kb = keff * ed
    bb = b * ed
    chains = []
    for grp in range(D_A // PACK):
        sl = slice(PACK * grp, PACK * (grp + 1))
        chains.append(_rwkv_group(rt[:, sl], kt[:, sl], kh[:, sl], bh[:, sl], kb[:, sl], bb[:, sl], v[:, sl],
                                  sa_scr.at[bi, d, grp], egc[:, sl], mk, d))
    chains.append(_hgrn_dir(q, kf, vb, gcum[:, D_A:], [sb_scr.at[bi, d, grp] for grp in range(D_B // PACK)],
                            mk, d))
    results = [None] * len(chains)
    live = list(enumerate(chains))
    while live:
        still = []
        for i, g in live:
            try:
                next(g)
                still.append((i, g))
            except StopIteration as stop:
                results[i] = stop.value
        live = still
        yield
    return jnp.concatenate(results[:-1], axis=1), results[-1]


def _scan_kernel(paf_ref, pbf_ref, pab_ref, pbb_ref, sa0_ref, sb0_ref, w0_ref, a0_ref, wa2_ref, kk_ref,
                 ka_ref, lb_ref, yf_ref, yb_ref, of_ref, ob_ref, *rest, want_state):
    if want_state:
        sa_out, sb_out, sa_scr, sb_scr = rest
    else:
        sa_scr, sb_scr = rest
    ci = pl.program_id(1)

    @pl.when(ci == 0)
    def _():
        sa_scr[...] = sa0_ref[...]
        sb_scr[...] = sb0_ref[...]

    mk = _scan_masks()
    prm = (w0_ref[...], a0_ref[...], wa2_ref, kk_ref[...], ka_ref[...], lb_ref[...])
    p_refs = ((paf_ref, pbf_ref), (pab_ref, pbb_ref))
    runs = [(bi, d) for bi in range(SCAN_BATCH) for d in range(2)]
    outs = _lockstep([_scan_dir(p_refs[d][0], p_refs[d][1], bi, d, prm, sa_scr, sb_scr, mk) for bi, d in runs])
    for (bi, d), (y, o) in zip(runs, outs):
        (yf_ref, yb_ref)[d][bi] = y
        (of_ref, ob_ref)[d][bi] = o

    if want_state:
        @pl.when(ci == pl.num_programs(1) - 1)
        def _():
            sa_out[...] = sa_scr[...]
            sb_out[...] = sb_scr[...]


def _scan_call(pa, pb, sa0_bd, sb0_bd, w0, a0, wa2, k_k, k_a, lb, want_state):
    bsz, seq, _ = pa.shape
    nc = seq // CHUNK
    nb = SCAN_BATCH
    fwd = lambda b, c: (b, c, 0)
    bwd = lambda b, c: (b, nc - 1 - c, 0)
    st = lambda b, c: (b, 0, 0, 0, 0)
    full = lambda shape: pl.BlockSpec(shape, lambda b, c: (0,) * len(shape))
    y_shape = jax.ShapeDtypeStruct((bsz, seq, D_A), F32)
    out_specs = [pl.BlockSpec((nb, CHUNK, D_A), fwd), pl.BlockSpec((nb, CHUNK, D_A), bwd),
                 pl.BlockSpec((nb, CHUNK, D_B), fwd), pl.BlockSpec((nb, CHUNK, D_B), bwd)]
    out_shape = [y_shape, y_shape, y_shape, y_shape]
    st_block = (nb, 2, 2, PACK, PACK)
    if want_state:
        out_specs += [pl.BlockSpec(st_block, st), pl.BlockSpec(st_block, st)]
        out_shape += [jax.ShapeDtypeStruct(sa0_bd.shape, F32), jax.ShapeDtypeStruct(sb0_bd.shape, F32)]
    return pl.pallas_call(
        functools.partial(_scan_kernel, want_state=want_state),
        grid=(bsz // nb, nc),
        in_specs=[pl.BlockSpec((nb, CHUNK, A_COLS), fwd), pl.BlockSpec((nb, CHUNK, B_COLS), fwd),
                  pl.BlockSpec((nb, CHUNK, A_COLS), bwd), pl.BlockSpec((nb, CHUNK, B_COLS), bwd),
                  pl.BlockSpec(st_block, st), pl.BlockSpec(st_block, st),
                  full((2, D_A)), full((2, D_A)), full((2, LORA, 2 * D_A)),
                  full((1, D_A)), full((1, D_A)), full((2, D_B))],
        out_specs=out_specs,
        out_shape=out_shape,
        scratch_shapes=[pltpu.VMEM(st_block, F32), pltpu.VMEM(st_block, F32)],
        compiler_params=pltpu.CompilerParams(dimension_semantics=("arbitrary", "arbitrary"),
                                             vmem_limit_bytes=VMEM_LIMIT),
        name="scan_state" if want_state else "scan",
    )(pa, pb, pa, pb, sa0_bd, sb0_bd, w0, a0, wa2, k_k, k_a, lb)


def _out_kernel(x_ref, pa_ref, zb_ref, yf_ref, yb_ref, of_ref, ob_ref, gate_ref, a0_ref, a2p_ref, ka_ref,
                rk_ref, lnw_ref, lnb_ref, og_ref, wout_ref, fg_ref, out_ref):
    pa = pa_ref[0]
    tm = pa.shape[0]
    r = pa[:, 0:D_A]
    k = pa[:, D_A:2 * D_A]
    v = pa[:, 2 * D_A:3 * D_A]
    za = pa[:, 3 * D_A:4 * D_A]
    bm64_bf = _block_mask(PACK, PACK, 6, 6).astype(BF16)

    def seg_sum(z):
        return jnp.concatenate([_mm_exact_rhs(z[:, :PACK], bm64_bf), _mm_exact_rhs(z[:, PACK:], bm64_bf)],
                               axis=1)

    y = yf_ref[0] + yb_ref[0]
    mu = seg_sum(y) * (1.0 / HEAD_A)
    dlt = y - mu
    var = seg_sum(dlt * dlt) * (1.0 / HEAD_A)
    yn = dlt * lax.rsqrt(var + GN_EPS) * lnw_ref[...] + lnb_ref[...]
    keffs = []
    for d in range(2):
        lo = pa[:, 4 * D_A + LORA * d:4 * D_A + LORA * (d + 1)]
        a = _sigmoid(a0_ref[d:d + 1] + _mm(lo, a2p_ref[d]))
        keffs.append(k * (1.0 + (a - 1.0) * ka_ref[...]))
    kmean = 0.5 * (keffs[0] + keffs[1])
    bonus = seg_sum(r * kmean * rk_ref[...]) * v
    out_a = (yn + bonus) * _silu(za)

    o = of_ref[0] + ob_ref[0]
    og = og_ref[...]
    zb = zb_ref[0]
    outs = [out_a]
    for h in range(D_B // HEAD_B):
        sl = slice(HEAD_B * h, HEAD_B * (h + 1))
        oh = o[:, sl]
        oh = oh * lax.rsqrt(jnp.mean(oh * oh, axis=-1, keepdims=True) + EPS) * og[:, sl]
        outs.append(oh * _silu(zb[:, sl]))
    mix = jnp.concatenate(outs, axis=1)
    proj = _dot(mix.astype(BF16), wout_ref[...])
    hs = x_ref[0] + gate_ref[0] * proj
    out_ref[0] = hs * lax.rsqrt(jnp.mean(hs * hs, axis=-1, keepdims=True) + EPS) * fg_ref[...]


def _out_call(x, pa, pb, yf, yb, of, ob, gate, a0, a2p, k_a, r_k, lnx_w, lnx_b, onorm_g, w_out_bf16, final_g):
    bsz, seq, _ = x.shape
    tm = 256
    tok = lambda b, i: (b, i, 0)
    full = lambda shape: pl.BlockSpec(shape, lambda b, i: (0,) * len(shape))
    return pl.pallas_call(
        _out_kernel,
        grid=(bsz, seq // tm),
        in_specs=[pl.BlockSpec((1, tm, D_MODEL), tok),
                  pl.BlockSpec((1, tm, A_COLS), tok),
                  pl.BlockSpec((1, tm, D_B), lambda b, i: (b, i, 4)),
                  pl.BlockSpec((1, tm, D_A), tok), pl.BlockSpec((1, tm, D_A), tok),
                  pl.BlockSpec((1, tm, D_B), tok), pl.BlockSpec((1, tm, D_B), tok),
                  pl.BlockSpec((1, 1, D_MODEL), lambda b, i: (b, 0, 0)),
                  full((2, D_A)), full((2, LORA, D_A)), full((1, D_A)), full((1, D_A)),
                  full((1, D_A)), full((1, D_A)), full((1, D_B)),
                  full((D_MODEL, D_MODEL)), full((1, D_MODEL))],
        out_specs=pl.BlockSpec((1, tm, D_MODEL), tok),
        out_shape=jax.ShapeDtypeStruct((bsz, seq, D_MODEL), F32),
        compiler_params=pltpu.CompilerParams(dimension_semantics=("arbitrary", "arbitrary"),
                                             vmem_limit_bytes=VMEM_LIMIT),
        name="out",
    )(x, pa, pb, yf, yb, of, ob, gate, a0, a2p, k_a, r_k, lnx_w, lnx_b, onorm_g, w_out_bf16, final_g)


def _to_block_diag(s, nb):
    *lead, h, n, m = s.shape
    s = s.reshape(*lead, h // nb, nb, n, m)
    eye = jnp.eye(nb, dtype=s.dtype)
    out = s[..., :, :, None, :] * eye[:, None, :, None]
    return out.reshape(*lead, h // nb, nb * n, nb * m)


def _from_block_diag(sbd, nb):
    *lead, g, nn, mm = sbd.shape
    n, m = nn // nb, mm // nb
    s = sbd.reshape(*lead, g, nb, n, nb, m)
    s = jnp.stack([s[..., i, :, i, :] for i in range(nb)], axis=-3)
    return s.reshape(*lead, g * nb, n, m)


def kernel(x_prompt, x_sample, state_rwkv, state_hgrn, c, c_ctx, norm_g, w_ada, b_ada, w_in, mu_h, mu_v, w0, w2,
           a0, a2, k_k, k_a, r_k, lnx_w, lnx_b, lb_logits, onorm_g, w_out, final_g):
    l = 0
    bp = x_prompt.shape[0]
    bs = x_sample.shape[0]
    lb_all = jnp.cumsum(jax.nn.softmax(lb_logits.astype(F32), axis=0), axis=0)
    lb = lb_all[l]

    w_in_bf = w_in[l].astype(BF16)
    w_out_bf = w_out[l].astype(BF16)
    zeros = jnp.zeros((2, LORA // 2, D_A), F32)
    wa2 = jnp.concatenate([jnp.concatenate([w2[l], zeros], axis=2),
                           jnp.concatenate([zeros, a2[l]], axis=2)], axis=1)
    a2p = jnp.concatenate([zeros, a2[l]], axis=1)
    row = lambda z: z.reshape(1, -1)

    cc = jnp.concatenate([c_ctx[None, :], c, jnp.zeros((16 - 1 - bs, D_MODEL), F32)], axis=0)
    m = _ada_call(cc, w_ada[l], row(b_ada[l]))
    shift, scale, gate = m[:, :D_MODEL], m[:, D_MODEL:2 * D_MODEL], m[:, 2 * D_MODEL:]
    ctx = lambda z: jnp.broadcast_to(z[0:1, None, :], (bp, 1, D_MODEL))
    lat = lambda z: z[1:1 + bs, None, :]

    def path(x, sh, sc, gt, sa0, sb0, grid_shift, want_state):
        pa, pb = _proj_call(x, sh, sc, row(norm_g[l]), w_in_bf, row(mu_h[l]), row(mu_v[l]), grid_shift)
        sa0_bd = _to_block_diag(sa0, PACK // HEAD_A)
        sb0_bd = _to_block_diag(jnp.swapaxes(sb0, -1, -2), PACK // HEAD_B)
        res = _scan_call(pa, pb, sa0_bd, sb0_bd, w0[l], a0[l], wa2, row(k_k[l]), row(k_a[l]), lb, want_state)
        yf, yb, of, ob = res[:4]
        y = _out_call(x, pa, pb, yf, yb, of, ob, gt, a0[l], a2p, row(k_a[l]), row(r_k[l]), row(lnx_w[l]),
                      row(lnx_b[l]), row(onorm_g[l]), w_out_bf, row(final_g))
        if not want_state:
            return y, None, None
        sa = _from_block_diag(res[4], PACK // HEAD_A)
        sb = jnp.swapaxes(_from_block_diag(res[5], PACK // HEAD_B), -1, -2)
        return y, sa, sb

    zero_a = jnp.zeros((bp, 2, D_A // HEAD_A, HEAD_A, HEAD_A), F32)
    zero_b = jnp.zeros((bp, 2, D_B // HEAD_B, HEAD_B, HEAD_B), F32)
    y_prompt, s_a, s_b = path(x_prompt, ctx(shift), ctx(scale), ctx(gate), zero_a, zero_b, False, True)
    y_sample, _, _ = path(x_sample, lat(shift), lat(scale), lat(gate), state_rwkv[:, l], state_hgrn[:, l],
                          True, False)
    return y_prompt, y_sample, s_a[:, None], s_b[:, None]
```

```python
import functools

import jax
import jax.numpy as jnp
from jax import lax
from jax.experimental import pallas as pl
from jax.experimental.pallas import tpu as pltpu

F32 = jnp.float32
BF16 = jnp.bfloat16

D_MODEL = 1024
D_A = 512
D_B = 512
HEAD_A = 64
HEAD_B = 128
LORA = 128
A_COLS = 4 * D_A + 2 * LORA
B_COLS = 5 * D_B
GRID_W = 64
CHUNK = 64
SCAN_BATCH = 2
SUM_TERMS = 2
EPS = 1e-6
GN_EPS = 64e-5
PACK = 256
TNA = 768
TNB = 512
NA = A_COLS // TNA
NB = B_COLS // TNB
PROJ_ROWS = 1024
VMEM_LIMIT = 56 * 1024 * 1024

NN = ((1,), (0,))
NT = ((1,), (1,))
TN_DIMS = ((0,), (0,))


def _dot(a, b, dims=NN):
    return lax.dot_general(a, b, (dims, ((), ())), preferred_element_type=F32)


def _split(x, passes=1):
    if isinstance(x, tuple):
        return x
    hi = x.astype(BF16)
    if passes == 1:
        return (hi,)
    return hi, (x - hi.astype(F32)).astype(BF16)


def _mm(a, b, dims=NN, passes=1):
    a = _split(a, passes)
    b = _split(b, passes)
    if len(a) == 1 or len(b) == 1:
        return _dot(a[0], b[0], dims)
    return _dot(a[0], b[0], dims) + (_dot(a[0], b[1], dims) + _dot(a[1], b[0], dims))


def _terms(x, n):
    out = []
    for _ in range(n - 1):
        h = x.astype(BF16)
        out.append(h)
        x = x - h.astype(F32)
    out.append(x.astype(BF16))
    return out


def _mm_exact_lhs(a_bf16, b, n=3):
    parts = [_dot(a_bf16, t) for t in _terms(b, n)]
    return functools.reduce(lambda acc, z: z + acc, reversed(parts))


def _mm_exact_rhs(a, b_bf16, n=3):
    parts = [_dot(t, b_bf16) for t in _terms(a, n)]
    return functools.reduce(lambda acc, z: z + acc, reversed(parts))


def _sigmoid(x):
    return 1.0 / (1.0 + jnp.exp(-x))


def _silu(x):
    return x * _sigmoid(x)


def _softplus(x):
    return jnp.maximum(x, 0.0) + jnp.log(1.0 + jnp.exp(-jnp.abs(x)))


def _iota(shape, dim):
    return lax.broadcasted_iota(jnp.int32, shape, dim)


def _block_mask(rows, cols, row_shift, col_shift):
    same = (_iota((rows, cols), 0) >> row_shift) == (_iota((rows, cols), 1) >> col_shift)
    return jnp.where(same, 1.0, 0.0).astype(F32)


def _tile_rows(x, n):
    if isinstance(x, tuple):
        return tuple(_tile_rows(t, n) for t in x)
    return jnp.concatenate([x] * n, axis=0)


def _masked(x, mask_bf16):
    if isinstance(x, tuple):
        return tuple(_masked(t, mask_bf16) for t in x)
    return x * mask_bf16


def _ada_kernel(c_ref, w_ref, b_ref, m_ref):
    m_ref[...] = _mm(_silu(c_ref[...]), w_ref[...], passes=3) + b_ref[...]


def _ada_call(cc, w_ada, b_ada):
    rows = cc.shape[0]
    return pl.pallas_call(
        _ada_kernel,
        grid=(3,),
        in_specs=[pl.BlockSpec((rows, D_MODEL), lambda j: (0, 0)),
                  pl.BlockSpec((D_MODEL, D_MODEL), lambda j: (0, j)),
                  pl.BlockSpec((1, D_MODEL), lambda j: (0, j))],
        out_specs=pl.BlockSpec((rows, D_MODEL), lambda j: (0, j)),
        out_shape=jax.ShapeDtypeStruct((rows, 3 * D_MODEL), F32),
        compiler_params=pltpu.CompilerParams(dimension_semantics=("arbitrary",),
                                             vmem_limit_bytes=VMEM_LIMIT),
        name="ada",
    )(cc, w_ada, b_ada)


def _proj_kernel(x_ref, sh_ref, sc_ref, g_ref, wa_ref, wb_ref, muh_ref, muv_ref, pa_ref, pb_ref, xm_ref, *,
                 nbp, seq, grid_shift):
    j = pl.program_id(1)
    rows = nbp * seq

    @pl.when(j == 0)
    def _():
        for b in range(nbp):
            x = x_ref[b]
            xn = x * lax.rsqrt(jnp.mean(x * x, axis=-1, keepdims=True) + EPS) * g_ref[...]
            xm_ref[b * seq:(b + 1) * seq, :] = (xn * (1.0 + sc_ref[b]) + sh_ref[b]).astype(BF16)

    @pl.when(j < NA)
    def _():
        p = _dot(xm_ref[...], wa_ref[...])
        w = GRID_W if grid_shift else seq
        nblk = rows // w
        pos = _iota((1, w, TNA), 1)
        to3 = lambda z: z.reshape(nblk, w, TNA)
        prev = jnp.where(pos == 0, 0.0, to3(pltpu.roll(p, 1, axis=0)))
        nxt = jnp.where(pos == w - 1, 0.0, to3(pltpu.roll(p, rows - 1, axis=0)))
        muh = muh_ref[...]
        out = (0.5 * muh) * (prev + nxt)
        if grid_shift:
            muv = muv_ref[...]
            p3 = to3(p)
            zblk = jnp.zeros((1, w, TNA), F32)
            up = jnp.concatenate([zblk, p3[:nblk - 1]], axis=0)
            dn = jnp.concatenate([p3[1:], zblk], axis=0)
            out = (1.0 - muh - muv) * p3 + out + (0.5 * muv) * (up + dn)
        else:
            out = (1.0 - muh) * to3(p) + out
        pa_ref[...] = out.reshape(nbp, seq, TNA)

    @pl.when(j >= NA)
    def _():
        pb_ref[...] = _dot(xm_ref[...], wb_ref[...]).reshape(nbp, seq, TNB)


def _proj_call(x, shift, scale, norm_g, wa_bf16, wb_bf16, mu_h, mu_v, grid_shift):
    bsz, seq, _ = x.shape
    nbp = 1 if grid_shift else PROJ_ROWS // seq
    kern = functools.partial(_proj_kernel, nbp=nbp, seq=seq, grid_shift=grid_shift)
    a_idx = lambda j: jnp.minimum(j, NA - 1)
    b_idx = lambda j: jnp.maximum(j - NA, 0)
    return pl.pallas_call(
        kern,
        grid=(bsz // nbp, NA + NB),
        in_specs=[pl.BlockSpec((nbp, seq, D_MODEL), lambda b, j: (b, 0, 0)),
                  pl.BlockSpec((nbp, 1, D_MODEL), lambda b, j: (b, 0, 0)),
                  pl.BlockSpec((nbp, 1, D_MODEL), lambda b, j: (b, 0, 0)),
                  pl.BlockSpec((1, D_MODEL), lambda b, j: (0, 0)),
                  pl.BlockSpec((D_MODEL, TNA), lambda b, j: (0, a_idx(j))),
                  pl.BlockSpec((D_MODEL, TNB), lambda b, j: (0, b_idx(j))),
                  pl.BlockSpec((1, TNA), lambda b, j: (0, a_idx(j))),
                  pl.BlockSpec((1, TNA), lambda b, j: (0, a_idx(j)))],
        out_specs=[pl.BlockSpec((nbp, seq, TNA), lambda b, j: (b, 0, a_idx(j))),
                   pl.BlockSpec((nbp, seq, TNB), lambda b, j: (b, 0, b_idx(j)))],
        out_shape=[jax.ShapeDtypeStruct((bsz, seq, A_COLS), F32),
                   jax.ShapeDtypeStruct((bsz, seq, B_COLS), F32)],
        scratch_shapes=[pltpu.VMEM((nbp * seq, D_MODEL), BF16)],
        compiler_params=pltpu.CompilerParams(dimension_semantics=("arbitrary", "arbitrary"),
                                             vmem_limit_bytes=VMEM_LIMIT),
        name="proj_grid" if grid_shift else "proj_seq",
    )(x, shift, scale, norm_g, wa_bf16, wb_bf16, mu_h, mu_v)


def _scan_masks():
    c = CHUNK
    t = _iota((c, PACK), 0)
    s = _iota((c, PACK), 1) & (c - 1)
    tt = _iota((c, c), 0)
    ss = _iota((c, c), 1)
    f01 = lambda cond: jnp.where(cond, 1.0, 0.0).astype(F32)
    bm64 = _block_mask(PACK, PACK, 6, 6)
    bm128 = _block_mask(PACK, PACK, 7, 7)
    return dict(
        bm64=bm64, bm64_bf=bm64.astype(BF16), bm128=bm128,
        bm_hg_bf=_block_mask(4 * c, D_B, 6, 7).astype(BF16),
        eye=f01(t == s),
        incl=(f01(t >= s), f01(t <= s)),
        strict=(f01(t > s), f01(t < s)),
        tri=(f01(tt >= ss).astype(BF16), f01(tt <= ss).astype(BF16)),
    )


def _lockstep(gens):
    results = [None] * len(gens)
    live = list(enumerate(gens))
    while live:
        still = []
        for i, g in live:
            try:
                next(g)
                still.append((i, g))
            except StopIteration as stop:
                results[i] = stop.value
        live = still
    return results


def _rwkv_group(rt, kt, kh, bh, kb, bb, v, s_ref, egc, mk, d):
    c = CHUNK
    bd = lambda x: _masked(_tile_rows(_split(x), PACK // c), mk["bm64_bf"])
    kr = _split(jnp.concatenate([kt, rt], axis=0))
    aa_k = _mm(kr, bd(kh), NT)
    aa_b = _mm(kr, bd(bh), NT)
    s_bd = s_ref[...]
    krs = _mm(kr, s_bd, NT)
    yield
    a_kk = aa_k[:c] * mk["strict"][d]
    a_rk = aa_k[c:] * mk["incl"][d]
    n = -(aa_b[:c] * mk["strict"][d])
    a_rb = aa_b[c:] * mk["incl"][d]
    x = mk["eye"] + n
    p = _mm(n, bd(n))
    akv = _mm(jnp.concatenate([a_kk, a_rk], axis=0), bd(v))
    yield
    for i in range(5):
        if i < 4:
            xp = _mm(jnp.concatenate([x, p], axis=0), bd(p))
            x = x + xp[:c]
            p = xp[c:]
        else:
            x = x + _mm(x, bd(p))
        yield
    u = _mm(x, bd(krs[:c] + akv[:c]))
    yield
    y = krs[c:] + akv[c:] - _mm(a_rb, bd(u))
    vu = jnp.concatenate([v, -u], axis=0)
    kbb = jnp.concatenate([kb, bb], axis=0)
    s_ref[...] = (s_bd * egc + _mm(vu, kbb, TN_DIMS)) * mk["bm64"]
    return y


def _hgrn_dir(q, kf, vb, gh, s_refs, mk, d):
    c = CHUNK
    last = c - 1 if d == 0 else 0
    ghc = gh[last:last + 1]
    ghm = gh[c // 2:c // 2 + 1]
    ghn = gh - ghm
    qn = q * jnp.exp(ghn)
    kn = kf * jnp.exp(-ghn)
    qt = qn * jnp.exp(ghm)
    kbh = kn * jnp.exp(ghc - ghm)
    eghc = jnp.exp(ghc)
    bm_hg = mk["bm_hg_bf"]
    a_p = _mm(qn, _masked(_tile_rows(_split(kn), 4), bm_hg), NT) * mk["incl"][d]
    inter = []
    for grp, s_ref in enumerate(s_refs):
        sl = slice(PACK * grp, PACK * (grp + 1))
        st = s_ref[...]
        inter.append(_mm(qt[:, sl], st, NT))
        s_ref[...] = (st * eghc[:, sl] + _mm(vb[:, sl], kbh[:, sl], TN_DIMS)) * mk["bm128"]
    yield
    o = _mm(a_p, _masked(_tile_rows(_split(vb), 4), bm_hg))
    return o + jnp.concatenate(inter, axis=1)


def _scan_dir(pa_ref, pb_ref, bi, d, prm, sa_scr, sb_scr, mk):
    c = CHUNK
    w0, a0, wa2, k_k, k_a, lb = prm
    r = pa_ref[bi, :, 0:D_A]
    k = pa_ref[bi, :, D_A:2 * D_A]
    v = pa_ref[bi, :, 2 * D_A:3 * D_A]
    lo = pa_ref[bi, :, 4 * D_A + LORA * d:4 * D_A + LORA * (d + 1)]
    lo = jnp.where(_iota((c, LORA), 1) < LORA // 2, jnp.tanh(lo), lo)
    wa = _mm(lo, wa2[d])
    yield
    w_log = w0[d:d + 1] + wa[:, :D_A]
    lw = -jnp.exp(-_softplus(-w_log) - 0.5)
    a = _sigmoid(a0[d:d + 1] + wa[:, D_A:])
    kk = k * k_k
    bm64_bf = mk["bm64_bf"]
    sq = kk * kk
    n2 = jnp.concatenate([_mm_exact_rhs(sq[:, :PACK], bm64_bf, SUM_TERMS),
                          _mm_exact_rhs(sq[:, PACK:], bm64_bf, SUM_TERMS)], axis=1)
    q = _silu(pb_ref[bi, :, 0:D_B])
    vb = pb_ref[bi, :, D_B:2 * D_B]
    f = lb[d:d + 1] + (1.0 - lb[d:d + 1]) * _sigmoid(pb_ref[bi, :, (2 + d) * D_B:(3 + d) * D_B])
    kf = 1.0 - f
    gf = jnp.log(f)
    gcum = _mm_exact_lhs(mk["tri"][d], jnp.concatenate([lw, gf], axis=1), SUM_TERMS)
    yield
    kap = kk / jnp.maximum(jnp.sqrt(n2), 1e-12)
    keff = k * (1.0 + (a - 1.0) * k_a)
    b = kap * a
    last = c - 1 if d == 0 else 0
    g = gcum[:, :D_A]
    gc = g[last:last + 1]
    eg = jnp.exp(g)
    eng = jnp.exp(-g)
    egc = jnp.exp(gc)
    rt = r * eg
    kt = kap * jnp.exp(g - lw)
    kh = keff * eng
    bh = b * eng
    kb = kh * egc
    bb = bh * egc
    chains = []
    for grp in range(D_A // PACK):
        sl = slice(PACK * grp, PACK * (grp + 1))
        chains.append(_rwkv_group(rt[:, sl], kt[:, sl], kh[:, sl], bh[:, sl], kb[:, sl], bb[:, sl], v[:, sl],
                                  sa_scr.at[bi, d, grp], egc[:, sl], mk, d))
    chains.append(_hgrn_dir(q, kf, vb, gcum[:, D_A:], [sb_scr.at[bi, d, grp] for grp in range(D_B // PACK)],
                            mk, d))
    results = [None] * len(chains)
    live = list(enumerate(chains))
    while live:
        still = []
        for i, g in live:
            try:
                next(g)
                still.append((i, g))
            except StopIteration as stop:
                results[i] = stop.value
        live = still
        yield
    return jnp.concatenate(results[:-1], axis=1), results[-1]


def _state_slots():
    return [(bi, d, grp) for bi in range(SCAN_BATCH) for d in range(2) for grp in range(D_A // PACK)]


def _load_states(sa0_ref, sb0_ref, sa_scr, sb_scr):
    spread = jnp.where(_iota((HEAD_A, PACK), 0) == (_iota((HEAD_A, PACK), 1) & (HEAD_A - 1)), 1.0, 0.0)
    spread = spread.astype(BF16)
    bm64 = _block_mask(PACK, PACK, 6, 6)
    zero = jnp.zeros((HEAD_B, HEAD_B), F32)
    for bi, d, grp in _state_slots():
        sa_scr[bi, d, grp] = _mm_exact_rhs(sa0_ref[bi, d, grp], spread) * bm64
        h0 = sb0_ref[bi, d, 2 * grp].T
        h1 = sb0_ref[bi, d, 2 * grp + 1].T
        sb_scr[bi, d, grp] = jnp.concatenate([jnp.concatenate([h0, zero], axis=1),
                                              jnp.concatenate([zero, h1], axis=1)], axis=0)


def _store_states(sa_scr, sb_scr, sa_out, sb_out):
    gather = jnp.where((_iota((PACK, HEAD_A), 0) & (HEAD_A - 1)) == _iota((PACK, HEAD_A), 1), 1.0, 0.0)
    gather = gather.astype(BF16)
    for bi, d, grp in _state_slots():
        sa_out[bi, d, grp] = _mm_exact_rhs(sa_scr[bi, d, grp], gather)
        for hh in range(PACK // HEAD_B):
            blk = sb_scr[bi, d, grp, HEAD_B * hh:HEAD_B * (hh + 1), HEAD_B * hh:HEAD_B * (hh + 1)]
            sb_out[bi, d, 2 * grp + hh] = blk.T


def _scan_kernel(paf_ref, pbf_ref, pab_ref, pbb_ref, *rest, from_zero):
    if from_zero:
        (w0_ref, a0_ref, wa2_ref, kk_ref, ka_ref, lb_ref, yf_ref, yb_ref, of_ref, ob_ref, sa_out, sb_out,
         sa_scr, sb_scr) = rest
    else:
        (sa0_ref, sb0_ref, w0_ref, a0_ref, wa2_ref, kk_ref, ka_ref, lb_ref, yf_ref, yb_ref, of_ref, ob_ref,
         sa_scr, sb_scr) = rest
    ci = pl.program_id(1)

    @pl.when(ci == 0)
    def _():
        if from_zero:
            sa_scr[...] = jnp.zeros(sa_scr.shape, F32)
            sb_scr[...] = jnp.zeros(sb_scr.shape, F32)
        else:
            _load_states(sa0_ref, sb0_ref, sa_scr, sb_scr)

    mk = _scan_masks()
    prm = (w0_ref[...], a0_ref[...], wa2_ref, kk_ref[...], ka_ref[...], lb_ref[...])
    p_refs = ((paf_ref, pbf_ref), (pab_ref, pbb_ref))
    runs = [(bi, d) for bi in range(SCAN_BATCH) for d in range(2)]
    outs = _lockstep([_scan_dir(p_refs[d][0], p_refs[d][1], bi, d, prm, sa_scr, sb_scr, mk) for bi, d in runs])
    for (bi, d), (y, o) in zip(runs, outs):
        (yf_ref, yb_ref)[d][bi] = y
        (of_ref, ob_ref)[d][bi] = o

    if from_zero:
        @pl.when(ci == pl.num_programs(1) - 1)
        def _():
            _store_states(sa_scr, sb_scr, sa_out, sb_out)


def _scan_call(pa, pb, states, w0, a0, wa2, k_k, k_a, lb):
    bsz, seq, _ = pa.shape
    nc = seq // CHUNK
    nb = SCAN_BATCH
    from_zero = states is None
    fwd = lambda b, c: (b, c, 0)
    bwd = lambda b, c: (b, nc - 1 - c, 0)
    st = lambda b, c: (b, 0, 0, 0, 0)
    full = lambda shape: pl.BlockSpec(shape, lambda b, c: (0,) * len(shape))
    sa_block = (nb, 2, D_A // PACK, PACK, HEAD_A)
    sb_block = (nb, 2, D_B // HEAD_B, HEAD_B, HEAD_B)
    y_shape = jax.ShapeDtypeStruct((bsz, seq, D_A), F32)
    out_specs = [pl.BlockSpec((nb, CHUNK, D_A), fwd), pl.BlockSpec((nb, CHUNK, D_A), bwd),
                 pl.BlockSpec((nb, CHUNK, D_B), fwd), pl.BlockSpec((nb, CHUNK, D_B), bwd)]
    out_shape = [y_shape, y_shape, y_shape, y_shape]
    in_specs = [pl.BlockSpec((nb, CHUNK, A_COLS), fwd), pl.BlockSpec((nb, CHUNK, B_COLS), fwd),
                pl.BlockSpec((nb, CHUNK, A_COLS), bwd), pl.BlockSpec((nb, CHUNK, B_COLS), bwd)]
    args = [pa, pb, pa, pb]
    if from_zero:
        out_specs += [pl.BlockSpec(sa_block, st), pl.BlockSpec(sb_block, st)]
        out_shape += [jax.ShapeDtypeStruct((bsz,) + sa_block[1:], F32),
                      jax.ShapeDtypeStruct((bsz,) + sb_block[1:], F32)]
    else:
        in_specs += [pl.BlockSpec(sa_block, st), pl.BlockSpec(sb_block, st)]
        args += list(states)
    in_specs += [full((2, D_A)), full((2, D_A)), full((2, LORA, 2 * D_A)),
                 full((1, D_A)), full((1, D_A)), full((2, D_B))]
    args += [w0, a0, wa2, k_k, k_a, lb]
    bd_block = (nb, 2, D_A // PACK, PACK, PACK)
    return pl.pallas_call(
        functools.partial(_scan_kernel, from_zero=from_zero),
        grid=(bsz // nb, nc),
        in_specs=in_specs,
        out_specs=out_specs,
        out_shape=out_shape,
        scratch_shapes=[pltpu.VMEM(bd_block, F32), pltpu.VMEM(bd_block, F32)],
        compiler_params=pltpu.CompilerParams(dimension_semantics=("arbitrary", "arbitrary"),
                                             vmem_limit_bytes=VMEM_LIMIT),
        name="scan_state" if from_zero else "scan",
    )(*args)


def _out_kernel(x_ref, pa_ref, zb_ref, yf_ref, yb_ref, of_ref, ob_ref, gate_ref, a0_ref, a2p_ref, ka_ref,
                rk_ref, lnw_ref, lnb_ref, og_ref, wout_ref, fg_ref, out_ref):
    pa = pa_ref[0]
    tm = pa.shape[0]
    r = pa[:, 0:D_A]
    k = pa[:, D_A:2 * D_A]
    v = pa[:, 2 * D_A:3 * D_A]
    za = pa[:, 3 * D_A:4 * D_A]
    bm64_bf = _block_mask(PACK, PACK, 6, 6).astype(BF16)

    def seg_sum(z):
        return jnp.concatenate([_mm_exact_rhs(z[:, :PACK], bm64_bf, SUM_TERMS),
                                _mm_exact_rhs(z[:, PACK:], bm64_bf, SUM_TERMS)], axis=1)

    y = yf_ref[0] + yb_ref[0]
    mu = seg_sum(y) * (1.0 / HEAD_A)
    dlt = y - mu
    var = seg_sum(dlt * dlt) * (1.0 / HEAD_A)
    yn = dlt * lax.rsqrt(var + GN_EPS) * lnw_ref[...] + lnb_ref[...]
    keffs = []
    for d in range(2):
        lo = pa[:, 4 * D_A + LORA * d:4 * D_A + LORA * (d + 1)]
        a = _sigmoid(a0_ref[d:d + 1] + _mm(lo, a2p_ref[d]))
        keffs.append(k * (1.0 + (a - 1.0) * ka_ref[...]))
    kmean = 0.5 * (keffs[0] + keffs[1])
    bonus = seg_sum(r * kmean * rk_ref[...]) * v
    out_a = (yn + bonus) * _silu(za)

    o = of_ref[0] + ob_ref[0]
    og = og_ref[...]
    zb = zb_ref[0]
    outs = [out_a]
    for h in range(D_B // HEAD_B):
        sl = slice(HEAD_B * h, HEAD_B * (h + 1))
        oh = o[:, sl]
        oh = oh * lax.rsqrt(jnp.mean(oh * oh, axis=-1, keepdims=True) + EPS) * og[:, sl]
        outs.append(oh * _silu(zb[:, sl]))
    mix = jnp.concatenate(outs, axis=1)
    proj = _dot(mix.astype(BF16), wout_ref[...])
    hs = x_ref[0] + gate_ref[0] * proj
    out_ref[0] = hs * lax.rsqrt(jnp.mean(hs * hs, axis=-1, keepdims=True) + EPS) * fg_ref[...]


def _out_call(x, pa, pb, yf, yb, of, ob, gate, a0, a2p, k_a, r_k, lnx_w, lnx_b, onorm_g, w_out_bf16, final_g):
    bsz, seq, _ = x.shape
    tm = 256
    tok = lambda b, i: (b, i, 0)
    full = lambda shape: pl.BlockSpec(shape, lambda b, i: (0,) * len(shape))
    return pl.pallas_call(
        _out_kernel,
        grid=(bsz, seq // tm),
        in_specs=[pl.BlockSpec((1, tm, D_MODEL), tok),
                  pl.BlockSpec((1, tm, A_COLS), tok),
                  pl.BlockSpec((1, tm, D_B), lambda b, i: (b, i, 4)),
                  pl.BlockSpec((1, tm, D_A), tok), pl.BlockSpec((1, tm, D_A), tok),
                  pl.BlockSpec((1, tm, D_B), tok), pl.BlockSpec((1, tm, D_B), tok),
                  pl.BlockSpec((1, 1, D_MODEL), lambda b, i: (b, 0, 0)),
                  full((2, D_A)), full((2, LORA, D_A)), full((1, D_A)), full((1, D_A)),
                  full((1, D_A)), full((1, D_A)), full((1, D_B)),
                  full((D_MODEL, D_MODEL)), full((1, D_MODEL))],
        out_specs=pl.BlockSpec((1, tm, D_MODEL), tok),
        out_shape=jax.ShapeDtypeStruct((bsz, seq, D_MODEL), F32),
        compiler_params=pltpu.CompilerParams(dimension_semantics=("arbitrary", "arbitrary"),
                                             vmem_limit_bytes=VMEM_LIMIT),
        name="out",
    )(x, pa, pb, yf, yb, of, ob, gate, a0, a2p, k_a, r_k, lnx_w, lnx_b, onorm_g, w_out_bf16, final_g)


def kernel(x_prompt, x_sample, state_rwkv, state_hgrn, c, c_ctx, norm_g, w_ada, b_ada, w_in, mu_h, mu_v, w0, w2,
           a0, a2, k_k, k_a, r_k, lnx_w, lnx_b, lb_logits, onorm_g, w_out, final_g):
    l = 0
    bp = x_prompt.shape[0]
    bs = x_sample.shape[0]
    lb_all = jnp.cumsum(jax.nn.softmax(lb_logits.astype(F32), axis=0), axis=0)
    lb = lb_all[l]

    wa_bf = w_in[l, :, :A_COLS].astype(BF16)
    wb_bf = w_in[l, :, A_COLS:].astype(BF16)
    w_out_bf = w_out[l].astype(BF16)
    zeros = jnp.zeros((2, LORA // 2, D_A), F32)
    wa2 = jnp.concatenate([jnp.concatenate([w2[l], zeros], axis=2),
                           jnp.concatenate([zeros, a2[l]], axis=2)], axis=1)
    a2p = jnp.concatenate([zeros, a2[l]], axis=1)
    row = lambda z: z.reshape(1, -1)

    cc = jnp.concatenate([c_ctx[None, :], c, jnp.zeros((16 - 1 - bs, D_MODEL), F32)], axis=0)
    m = _ada_call(cc, w_ada[l], row(b_ada[l]))
    shift, scale, gate = m[:, :D_MODEL], m[:, D_MODEL:2 * D_MODEL], m[:, 2 * D_MODEL:]
    ctx = lambda z: jnp.broadcast_to(z[0:1, None, :], (bp, 1, D_MODEL))
    lat = lambda z: z[1:1 + bs, None, :]

    def path(x, sh, sc, gt, states, grid_shift):
        pa, pb = _proj_call(x, sh, sc, row(norm_g[l]), wa_bf, wb_bf, row(mu_h[l]), row(mu_v[l]), grid_shift)
        res = _scan_call(pa, pb, states, w0[l], a0[l], wa2, row(k_k[l]), row(k_a[l]), lb)
        yf, yb, of, ob = res[:4]
        y = _out_call(x, pa, pb, yf, yb, of, ob, gt, a0[l], a2p, row(k_a[l]), row(r_k[l]), row(lnx_w[l]),
                      row(lnx_b[l]), row(onorm_g[l]), w_out_bf, row(final_g))
        return y, res[4:]

    y_prompt, (s_a, s_b) = path(x_prompt, ctx(shift), ctx(scale), ctx(gate), None, False)
    groups_a = D_A // PACK
    sa0 = state_rwkv[:, l].reshape(bs, 2, groups_a, PACK, HEAD_A)
    y_sample, _ = path(x_sample, lat(shift), lat(scale), lat(gate), (sa0, state_hgrn[:, l]), True)
    s_a = s_a.reshape(bp, 1, 2, D_A // HEAD_A, HEAD_A, HEAD_A)
    return y_prompt, y_sample, s_a, s_b[:, None]
```

```python
import functools

import jax
import jax.numpy as jnp
from jax import lax
from jax.experimental import pallas as pl
from jax.experimental.pallas import tpu as pltpu

F32 = jnp.float32
BF16 = jnp.bfloat16

D_MODEL = 1024
D_A = 512
D_B = 512
HEAD_A = 64
HEAD_B = 128
LORA = 128
A_COLS = 4 * D_A + 2 * LORA
B_COLS = 5 * D_B
GRID_W = 64
CHUNK = 64
SCAN_BATCH = 2
SUM_TERMS = 2
BLOCK_ROWS = 32
BLOCK_LANES = 128
EPS = 1e-6
GN_EPS = 64e-5
DECAY_SCALE = 0.6065306597126334
PACK = 256
TNA = 768
TNB = 512
NA = A_COLS // TNA
NB = B_COLS // TNB
PROJ_ROWS = 1024
VMEM_LIMIT = 56 * 1024 * 1024

NN = ((1,), (0,))
NT = ((1,), (1,))
TN_DIMS = ((0,), (0,))


def _dot(a, b, dims=NN):
    return lax.dot_general(a, b, (dims, ((), ())), preferred_element_type=F32)


def _split(x, passes=1):
    if isinstance(x, tuple):
        return x
    hi = x.astype(BF16)
    if passes == 1:
        return (hi,)
    return hi, (x - hi.astype(F32)).astype(BF16)


def _mm(a, b, dims=NN, passes=1):
    a = _split(a, passes)
    b = _split(b, passes)
    if len(a) == 1 or len(b) == 1:
        return _dot(a[0], b[0], dims)
    return _dot(a[0], b[0], dims) + (_dot(a[0], b[1], dims) + _dot(a[1], b[0], dims))


def _terms(x, n):
    out = []
    for _ in range(n - 1):
        h = x.astype(BF16)
        out.append(h)
        x = x - h.astype(F32)
    out.append(x.astype(BF16))
    return out


def _mm_exact_lhs(a_bf16, b, n=3):
    parts = [_dot(a_bf16, t) for t in _terms(b, n)]
    return functools.reduce(lambda acc, z: z + acc, reversed(parts))


def _mm_exact_rhs(a, b_bf16, n=3):
    parts = [_dot(t, b_bf16) for t in _terms(a, n)]
    return functools.reduce(lambda acc, z: z + acc, reversed(parts))


def _sigmoid(x):
    return 1.0 / (1.0 + jnp.exp(-x))


def _silu(x):
    return x * _sigmoid(x)


def _softplus(x):
    return jnp.maximum(x, 0.0) + jnp.log(1.0 + jnp.exp(-jnp.abs(x)))


def _iota(shape, dim):
    return lax.broadcasted_iota(jnp.int32, shape, dim)


def _block_mask(rows, cols, row_shift, col_shift):
    same = (_iota((rows, cols), 0) >> row_shift) == (_iota((rows, cols), 1) >> col_shift)
    return jnp.where(same, 1.0, 0.0).astype(F32)


def _tile_rows(x, n):
    if isinstance(x, tuple):
        return tuple(_tile_rows(t, n) for t in x)
    return jnp.concatenate([x] * n, axis=0)


def _masked(x, mask_bf16):
    if isinstance(x, tuple):
        return tuple(_masked(t, mask_bf16) for t in x)
    return x * mask_bf16


def _ada_kernel(c_ref, w_ref, b_ref, m_ref):
    m_ref[...] = _mm(_silu(c_ref[...]), w_ref[...], passes=3) + b_ref[...]


def _ada_call(cc, w_ada, b_ada):
    rows = cc.shape[0]
    return pl.pallas_call(
        _ada_kernel,
        grid=(3,),
        in_specs=[pl.BlockSpec((rows, D_MODEL), lambda j: (0, 0)),
                  pl.BlockSpec((D_MODEL, D_MODEL), lambda j: (0, j)),
                  pl.BlockSpec((1, D_MODEL), lambda j: (0, j))],
        out_specs=pl.BlockSpec((rows, D_MODEL), lambda j: (0, j)),
        out_shape=jax.ShapeDtypeStruct((rows, 3 * D_MODEL), F32),
        compiler_params=pltpu.CompilerParams(dimension_semantics=("arbitrary",),
                                             vmem_limit_bytes=VMEM_LIMIT),
        name="ada",
    )(cc, w_ada, b_ada)


def _proj_kernel(x_ref, sh_ref, sc_ref, g_ref, wa_ref, wb_ref, muh_ref, muv_ref, pa_ref, pb_ref, xm_ref, *,
                 nbp, seq, grid_shift):
    j = pl.program_id(1)
    rows = nbp * seq

    @pl.when(j == 0)
    def _():
        for b in range(nbp):
            x = x_ref[b]
            xn = x * lax.rsqrt(jnp.mean(x * x, axis=-1, keepdims=True) + EPS) * g_ref[...]
            xm_ref[b * seq:(b + 1) * seq, :] = (xn * (1.0 + sc_ref[b]) + sh_ref[b]).astype(BF16)

    @pl.when(j < NA)
    def _():
        p = _dot(xm_ref[...], wa_ref[...])
        w = GRID_W if grid_shift else seq
        nblk = rows // w
        pos = _iota((1, w, TNA), 1)
        to3 = lambda z: z.reshape(nblk, w, TNA)
        prev = jnp.where(pos == 0, 0.0, to3(pltpu.roll(p, 1, axis=0)))
        nxt = jnp.where(pos == w - 1, 0.0, to3(pltpu.roll(p, rows - 1, axis=0)))
        muh = muh_ref[...]
        out = (0.5 * muh) * (prev + nxt)
        if grid_shift:
            muv = muv_ref[...]
            p3 = to3(p)
            zblk = jnp.zeros((1, w, TNA), F32)
            up = jnp.concatenate([zblk, p3[:nblk - 1]], axis=0)
            dn = jnp.concatenate([p3[1:], zblk], axis=0)
            out = (1.0 - muh - muv) * p3 + out + (0.5 * muv) * (up + dn)
        else:
            out = (1.0 - muh) * to3(p) + out
        pa_ref[...] = out.reshape(nbp, seq, TNA)

    @pl.when(j >= NA)
    def _():
        pb_ref[...] = _dot(xm_ref[...], wb_ref[...]).reshape(nbp, seq, TNB)


def _proj_call(x, shift, scale, norm_g, wa_bf16, wb_bf16, mu_h, mu_v, grid_shift):
    bsz, seq, _ = x.shape
    nbp = 1 if grid_shift else PROJ_ROWS // seq
    kern = functools.partial(_proj_kernel, nbp=nbp, seq=seq, grid_shift=grid_shift)
    a_idx = lambda j: jnp.minimum(j, NA - 1)
    b_idx = lambda j: jnp.maximum(j - NA, 0)
    return pl.pallas_call(
        kern,
        grid=(bsz // nbp, NA + NB),
        in_specs=[pl.BlockSpec((nbp, seq, D_MODEL), lambda b, j: (b, 0, 0)),
                  pl.BlockSpec((nbp, 1, D_MODEL), lambda b, j: (b, 0, 0)),
                  pl.BlockSpec((nbp, 1, D_MODEL), lambda b, j: (b, 0, 0)),
                  pl.BlockSpec((1, D_MODEL), lambda b, j: (0, 0)),
                  pl.BlockSpec((D_MODEL, TNA), lambda b, j: (0, a_idx(j))),
                  pl.BlockSpec((D_MODEL, TNB), lambda b, j: (0, b_idx(j))),
                  pl.BlockSpec((1, TNA), lambda b, j: (0, a_idx(j))),
                  pl.BlockSpec((1, TNA), lambda b, j: (0, a_idx(j)))],
        out_specs=[pl.BlockSpec((nbp, seq, TNA), lambda b, j: (b, 0, a_idx(j))),
                   pl.BlockSpec((nbp, seq, TNB), lambda b, j: (b, 0, b_idx(j)))],
        out_shape=[jax.ShapeDtypeStruct((bsz, seq, A_COLS), F32),
                   jax.ShapeDtypeStruct((bsz, seq, B_COLS), F32)],
        scratch_shapes=[pltpu.VMEM((nbp * seq, D_MODEL), BF16)],
        compiler_params=pltpu.CompilerParams(dimension_semantics=("arbitrary", "arbitrary"),
                                             vmem_limit_bytes=VMEM_LIMIT),
        name="proj_grid" if grid_shift else "proj_seq",
    )(x, shift, scale, norm_g, wa_bf16, wb_bf16, mu_h, mu_v)


def _scan_masks():
    c = CHUNK
    t = _iota((c, PACK), 0)
    s = _iota((c, PACK), 1) & (c - 1)
    tt = _iota((c, c), 0)
    ss = _iota((c, c), 1)
    f01 = lambda cond: jnp.where(cond, 1.0, 0.0).astype(F32)
    bm64 = _block_mask(PACK, PACK, 6, 6)
    bm128 = _block_mask(PACK, PACK, 7, 7)
    return dict(
        bm64=bm64, bm64_bf=bm64.astype(BF16), bm128=bm128,
        bm_hg_bf=_block_mask(4 * c, D_B, 6, 7).astype(BF16),
        eye=f01(t == s),
        incl=(f01(t >= s), f01(t <= s)),
        strict=(f01(t > s), f01(t < s)),
        tri=(f01(tt >= ss).astype(BF16), f01(tt <= ss).astype(BF16)),
    )


def _lockstep(gens):
    results = [None] * len(gens)
    live = list(enumerate(gens))
    while live:
        still = []
        for i, g in live:
            try:
                next(g)
                still.append((i, g))
            except StopIteration as stop:
                results[i] = stop.value
        live = still
    return results


def _rwkv_group(rt, kt, kh, bh, kb, bb, v, s_ref, egc, mk, d):
    c = CHUNK
    bd = lambda x: _masked(_tile_rows(_split(x), PACK // c), mk["bm64_bf"])
    kr = _split(jnp.concatenate([kt, rt], axis=0))
    aa_k = _mm(kr, bd(kh), NT)
    aa_b = _mm(kr, bd(bh), NT)
    s_bd = s_ref[...]
    krs = _mm(kr, s_bd, NT)
    yield
    a_kk = aa_k[:c] * mk["strict"][d]
    a_rk = aa_k[c:] * mk["incl"][d]
    n = -(aa_b[:c] * mk["strict"][d])
    a_rb = aa_b[c:] * mk["incl"][d]
    x = mk["eye"] + n
    p = _mm(n, bd(n))
    akv = _mm(jnp.concatenate([a_kk, a_rk], axis=0), bd(v))
    yield
    for i in range(5):
        if i < 4:
            xp = _mm(jnp.concatenate([x, p], axis=0), bd(p))
            x = x + xp[:c]
            p = xp[c:]
        else:
            x = x + _mm(x, bd(p))
        yield
    u = _mm(x, bd(krs[:c] + akv[:c]))
    yield
    y = krs[c:] + akv[c:] - _mm(a_rb, bd(u))
    vu = jnp.concatenate([v.astype(BF16), (-u).astype(BF16)], axis=0)
    kbb = jnp.concatenate([kb, bb], axis=0)
    s_ref[...] = (s_bd * egc + _mm(vu, kbb, TN_DIMS)) * mk["bm64"]
    return y


def _hgrn_dir(qn, kn, qt, kbh, vb, eghc, s_refs, mk, d):
    bm_hg = mk["bm_hg_bf"]
    a_p = _mm(qn, _masked(_tile_rows(_split(kn), 4), bm_hg), NT) * mk["incl"][d]
    inter = []
    for grp, s_ref in enumerate(s_refs):
        sl = slice(PACK * grp, PACK * (grp + 1))
        st = s_ref[...]
        inter.append(_mm(qt[:, sl], st, NT))
        s_ref[...] = (st * eghc[:, sl] + _mm(vb[:, sl], kbh[:, sl], TN_DIMS)) * mk["bm128"]
    yield
    o = _mm(a_p, _masked(_tile_rows(_split(vb), 4), bm_hg))
    return o + jnp.concatenate(inter, axis=1)


def _blockwise(fn, operands, n_out, rows=CHUNK, width=D_A):
    cols = []
    for l0 in range(0, width, BLOCK_LANES):
        parts = []
        for r0 in range(0, rows, BLOCK_ROWS):
            blk = [z[(slice(None) if z.shape[0] == 1 else slice(r0, r0 + BLOCK_ROWS)), l0:l0 + BLOCK_LANES]
                   for z in operands]
            parts.append(fn(*blk))
        cols.append([jnp.concatenate([p[i] for p in parts], axis=0) for i in range(n_out)])
    return [jnp.concatenate([col[i] for col in cols], axis=1) for i in range(n_out)]


def _sum_small_first(parts):
    return functools.reduce(lambda acc, z: z + acc, reversed(parts))


def _chunk_operands(pa_ref, pb_ref, bi, d, prm, mk):
    c = CHUNK
    n_t = SUM_TERMS
    w0, a0, wa2, k_k, k_a, lb = prm
    cols = lambda ref, i, width=D_A: ref.at[bi, :, i * width:(i + 1) * width]
    r_ref, k_ref, v_ref = cols(pa_ref, 0), cols(pa_ref, 1), cols(pa_ref, 2)
    lo = pa_ref[bi, :, 4 * D_A + LORA * d:4 * D_A + LORA * (d + 1)]
    lo = jnp.where(_iota((c, LORA), 1) < LORA // 2, jnp.tanh(lo), lo)
    wa = _mm(lo, wa2[d])
    yield

    def gates(wa_w, wa_a, k, fr, w0_, a0_, kk_, lb_):
        lw = -DECAY_SCALE * _sigmoid(w0_ + wa_w)
        a = _sigmoid(a0_ + wa_a)
        kk = k * kk_
        f = lb_ + (1.0 - lb_) * _sigmoid(fr)
        gf = jnp.log(f)
        return (lw, a, kk, 1.0 - f, *_terms(kk * kk, n_t), *_terms(lw, n_t), *_terms(gf, n_t))

    res = _blockwise(gates, [wa[:, :D_A], wa[:, D_A:], k_ref, cols(pb_ref, 2 + d, D_B), w0[d:d + 1], a0[d:d + 1],
                             k_k, lb[d:d + 1]], 4 + 3 * n_t)
    lw, a, kk, kf = res[:4]
    sq_t, lw_t, gf_t = res[4:4 + n_t], res[4 + n_t:4 + 2 * n_t], res[4 + 2 * n_t:]
    seg = lambda lanes: _sum_small_first([_dot(t[:, lanes], mk["bm64_bf"]) for t in sq_t])
    n2 = jnp.concatenate([seg(slice(0, PACK)), seg(slice(PACK, D_A))], axis=1)
    gcum = _sum_small_first([_dot(mk["tri"][d], jnp.concatenate([tl, tg], axis=1))
                             for tl, tg in zip(lw_t, gf_t)])
    yield
    last = c - 1 if d == 0 else 0
    g, gh = gcum[:, :D_A], gcum[:, D_A:]
    gc, ghc, ghm = g[last:last + 1], gh[last:last + 1], gh[c // 2:c // 2 + 1]
    egc, eghc = jnp.exp(gc), jnp.exp(ghc)

    def rwkv_operands(r, k, kk_, n2_, a_, g_, lw_, ka_, egc_):
        kap = kk_ * lax.rsqrt(jnp.maximum(n2_, 1e-24))
        keff = k * (1.0 + (a_ - 1.0) * ka_)
        eng = jnp.exp(-g_)
        kh = keff * eng
        bh = (kap * a_) * eng
        out = (r * jnp.exp(g_), kap * jnp.exp(g_ - lw_), kh, bh, kh * egc_, bh * egc_)
        return tuple(z.astype(BF16) for z in out)

    rt, kt, kh, bh, kb, bb = _blockwise(rwkv_operands, [r_ref, k_ref, kk, n2, a, g, lw, k_a, egc], 6)

    def hgrn_operands(qr, kf_, gh_, ghm_, e_m, e_cm):
        q = _silu(qr)
        ghn = gh_ - ghm_
        qn = q * jnp.exp(ghn)
        kn = kf_ * jnp.exp(-ghn)
        return tuple(z.astype(BF16) for z in (qn, kn, qn * e_m, kn * e_cm))

    qn, kn, qt, kbh = _blockwise(hgrn_operands, [cols(pb_ref, 0, D_B), kf, gh, ghm, jnp.exp(ghm),
                                                 jnp.exp(ghc - ghm)], 4, width=D_B)
    v = v_ref[...].astype(BF16)
    vb = pb_ref[bi, :, D_B:2 * D_B].astype(BF16)
    return (rt, kt, kh, bh, kb, bb, v, qn, kn, qt, kbh, vb), (egc, eghc)


def _scan_dir(pa_ref, pb_ref, bi, d, prm, sa_scr, sb_scr, mk):
    ops, (egc, eghc) = yield from _chunk_operands(pa_ref, pb_ref, bi, d, prm, mk)
    rt, kt, kh, bh, kb, bb, v, qn, kn, qt, kbh, vb = ops
    chains = []
    for grp in range(D_A // PACK):
        sl = slice(PACK * grp, PACK * (grp + 1))
        chains.append(_rwkv_group(rt[:, sl], kt[:, sl], kh[:, sl], bh[:, sl], kb[:, sl], bb[:, sl], v[:, sl],
                                  sa_scr.at[bi, d, grp], egc[:, sl], mk, d))
    chains.append(_hgrn_dir(qn, kn, qt, kbh, vb, eghc, [sb_scr.at[bi, d, grp] for grp in range(D_B // PACK)],
                            mk, d))
    results = [None] * len(chains)
    live = list(enumerate(chains))
    while live:
        still = []
        for i, g in live:
            try:
                next(g)
                still.append((i, g))
            except StopIteration as stop:
                results[i] = stop.value
        live = still
        yield
    return jnp.concatenate(results[:-1], axis=1), results[-1]


def _state_slots():
    return [(bi, d, grp) for bi in range(SCAN_BATCH) for d in range(2) for grp in range(D_A // PACK)]


def _load_states(sa0_ref, sb0_ref, sa_scr, sb_scr):
    spread = jnp.where(_iota((HEAD_A, PACK), 0) == (_iota((HEAD_A, PACK), 1) & (HEAD_A - 1)), 1.0, 0.0)
    spread = spread.astype(BF16)
    bm64 = _block_mask(PACK, PACK, 6, 6)
    zero = jnp.zeros((HEAD_B, HEAD_B), F32)
    for bi, d, grp in _state_slots():
        sa_scr[bi, d, grp] = _mm_exact_rhs(sa0_ref[bi, d, grp], spread) * bm64
        h0 = sb0_ref[bi, d, 2 * grp].T
        h1 = sb0_ref[bi, d, 2 * grp + 1].T
        sb_scr[bi, d, grp] = jnp.concatenate([jnp.concatenate([h0, zero], axis=1),
                                              jnp.concatenate([zero, h1], axis=1)], axis=0)


def _store_states(sa_scr, sb_scr, sa_out, sb_out):
    gather = jnp.where((_iota((PACK, HEAD_A), 0) & (HEAD_A - 1)) == _iota((PACK, HEAD_A), 1), 1.0, 0.0)
    gather = gather.astype(BF16)
    for bi, d, grp in _state_slots():
        sa_out[bi, d, grp] = _mm_exact_rhs(sa_scr[bi, d, grp], gather)
        for hh in range(PACK // HEAD_B):
            blk = sb_scr[bi, d, grp, HEAD_B * hh:HEAD_B * (hh + 1), HEAD_B * hh:HEAD_B * (hh + 1)]
            sb_out[bi, d, 2 * grp + hh] = blk.T


def _scan_kernel(paf_ref, pbf_ref, pab_ref, pbb_ref, *rest, from_zero):
    if from_zero:
        (w0_ref, a0_ref, wa2_ref, kk_ref, ka_ref, lb_ref, yf_ref, yb_ref, of_ref, ob_ref, sa_out, sb_out,
         sa_scr, sb_scr) = rest
    else:
        (sa0_ref, sb0_ref, w0_ref, a0_ref, wa2_ref, kk_ref, ka_ref, lb_ref, yf_ref, yb_ref, of_ref, ob_ref,
         sa_scr, sb_scr) = rest
    ci = pl.program_id(1)

    @pl.when(ci == 0)
    def _():
        if from_zero:
            sa_scr[...] = jnp.zeros(sa_scr.shape, F32)
            sb_scr[...] = jnp.zeros(sb_scr.shape, F32)
        else:
            _load_states(sa0_ref, sb0_ref, sa_scr, sb_scr)

    mk = _scan_masks()
    prm = (w0_ref[...], a0_ref[...], wa2_ref, kk_ref[...], ka_ref[...], lb_ref[...])
    p_refs = ((paf_ref, pbf_ref), (pab_ref, pbb_ref))
    runs = [(bi, d) for bi in range(SCAN_BATCH) for d in range(2)]
    outs = _lockstep([_scan_dir(p_refs[d][0], p_refs[d][1], bi, d, prm, sa_scr, sb_scr, mk) for bi, d in runs])
    for (bi, d), (y, o) in zip(runs, outs):
        (yf_ref, yb_ref)[d][bi] = y
        (of_ref, ob_ref)[d][bi] = o

    if from_zero:
        @pl.when(ci == pl.num_programs(1) - 1)
        def _():
            _store_states(sa_scr, sb_scr, sa_out, sb_out)


def _scan_call(pa, pb, states, w0, a0, wa2, k_k, k_a, lb):
    bsz, seq, _ = pa.shape
    nc = seq // CHUNK
    nb = SCAN_BATCH
    from_zero = states is None
    fwd = lambda b, c: (b, c, 0)
    bwd = lambda b, c: (b, nc - 1 - c, 0)
    st = lambda b, c: (b, 0, 0, 0, 0)
    full = lambda shape: pl.BlockSpec(shape, lambda b, c: (0,) * len(shape))
    sa_block = (nb, 2, D_A // PACK, PACK, HEAD_A)
    sb_block = (nb, 2, D_B // HEAD_B, HEAD_B, HEAD_B)
    y_shape = jax.ShapeDtypeStruct((bsz, seq, D_A), F32)
    out_specs = [pl.BlockSpec((nb, CHUNK, D_A), fwd), pl.BlockSpec((nb, CHUNK, D_A), bwd),
                 pl.BlockSpec((nb, CHUNK, D_B), fwd), pl.BlockSpec((nb, CHUNK, D_B), bwd)]
    out_shape = [y_shape, y_shape, y_shape, y_shape]
    in_specs = [pl.BlockSpec((nb, CHUNK, A_COLS), fwd), pl.BlockSpec((nb, CHUNK, B_COLS), fwd),
                pl.BlockSpec((nb, CHUNK, A_COLS), bwd), pl.BlockSpec((nb, CHUNK, B_COLS), bwd)]
    args = [pa, pb, pa, pb]
    if from_zero:
        out_specs += [pl.BlockSpec(sa_block, st), pl.BlockSpec(sb_block, st)]
        out_shape += [jax.ShapeDtypeStruct((bsz,) + sa_block[1:], F32),
                      jax.ShapeDtypeStruct((bsz,) + sb_block[1:], F32)]
    else:
        in_specs += [pl.BlockSpec(sa_block, st), pl.BlockSpec(sb_block, st)]
        args += list(states)
    in_specs += [full((2, D_A)), full((2, D_A)), full((2, LORA, 2 * D_A)),
                 full((1, D_A)), full((1, D_A)), full((2, D_B))]
    args += [w0, a0, wa2, k_k, k_a, lb]
    bd_block = (nb, 2, D_A // PACK, PACK, PACK)
    return pl.pallas_call(
        functools.partial(_scan_kernel, from_zero=from_zero),
        grid=(bsz // nb, nc),
        in_specs=in_specs,
        out_specs=out_specs,
        out_shape=out_shape,
        scratch_shapes=[pltpu.VMEM(bd_block, F32), pltpu.VMEM(bd_block, F32)],
        compiler_params=pltpu.CompilerParams(dimension_semantics=("arbitrary", "arbitrary"),
                                             vmem_limit_bytes=VMEM_LIMIT),
        name="scan_state" if from_zero else "scan",
    )(*args)


def _out_kernel(x_ref, pa_ref, zb_ref, yf_ref, yb_ref, of_ref, ob_ref, gate_ref, a0_ref, a2p_ref, ka_ref,
                rk_ref, lnw_ref, lnb_ref, og_ref, wout_ref, fg_ref, out_ref):
    pa = pa_ref[0]
    tm = pa.shape[0]
    r = pa[:, 0:D_A]
    k = pa[:, D_A:2 * D_A]
    v = pa[:, 2 * D_A:3 * D_A]
    za = pa[:, 3 * D_A:4 * D_A]
    bm64_bf = _block_mask(PACK, PACK, 6, 6).astype(BF16)

    def seg_sum(z):
        return jnp.concatenate([_mm_exact_rhs(z[:, :PACK], bm64_bf, SUM_TERMS),
                                _mm_exact_rhs(z[:, PACK:], bm64_bf, SUM_TERMS)], axis=1)

    y = yf_ref[0] + yb_ref[0]
    mu = seg_sum(y) * (1.0 / HEAD_A)
    dlt = y - mu
    var = seg_sum(dlt * dlt) * (1.0 / HEAD_A)
    yn = dlt * lax.rsqrt(var + GN_EPS) * lnw_ref[...] + lnb_ref[...]
    keffs = []
    for d in range(2):
        lo = pa[:, 4 * D_A + LORA * d:4 * D_A + LORA * (d + 1)]
        a = _sigmoid(a0_ref[d:d + 1] + _mm(lo, a2p_ref[d]))
        keffs.append(k * (1.0 + (a - 1.0) * ka_ref[...]))
    kmean = 0.5 * (keffs[0] + keffs[1])
    bonus = seg_sum(r * kmean * rk_ref[...]) * v
    out_a = (yn + bonus) * _silu(za)

    o = of_ref[0] + ob_ref[0]
    og = og_ref[...]
    zb = zb_ref[0]
    outs = [out_a]
    for h in range(D_B // HEAD_B):
        sl = slice(HEAD_B * h, HEAD_B * (h + 1))
        oh = o[:, sl]
        oh = oh * lax.rsqrt(jnp.mean(oh * oh, axis=-1, keepdims=True) + EPS) * og[:, sl]
        outs.append(oh * _silu(zb[:, sl]))
    mix = jnp.concatenate(outs, axis=1)
    proj = _dot(mix.astype(BF16), wout_ref[...])
    hs = x_ref[0] + gate_ref[0] * proj
    out_ref[0] = hs * lax.rsqrt(jnp.mean(hs * hs, axis=-1, keepdims=True) + EPS) * fg_ref[...]


def _out_call(x, pa, pb, yf, yb, of, ob, gate, a0, a2p, k_a, r_k, lnx_w, lnx_b, onorm_g, w_out_bf16, final_g):
    bsz, seq, _ = x.shape
    tm = 256
    tok = lambda b, i: (b, i, 0)
    full = lambda shape: pl.BlockSpec(shape, lambda b, i: (0,) * len(shape))
    return pl.pallas_call(
        _out_kernel,
        grid=(bsz, seq // tm),
        in_specs=[pl.BlockSpec((1, tm, D_MODEL), tok),
                  pl.BlockSpec((1, tm, A_COLS), tok),
                  pl.BlockSpec((1, tm, D_B), lambda b, i: (b, i, 4)),
                  pl.BlockSpec((1, tm, D_A), tok), pl.BlockSpec((1, tm, D_A), tok),
                  pl.BlockSpec((1, tm, D_B), tok), pl.BlockSpec((1, tm, D_B), tok),
                  pl.BlockSpec((1, 1, D_MODEL), lambda b, i: (b, 0, 0)),
                  full((2, D_A)), full((2, LORA, D_A)), full((1, D_A)), full((1, D_A)),
                  full((1, D_A)), full((1, D_A)), full((1, D_B)),
                  full((D_MODEL, D_MODEL)), full((1, D_MODEL))],
        out_specs=pl.BlockSpec((1, tm, D_MODEL), tok),
        out_shape=jax.ShapeDtypeStruct((bsz, seq, D_MODEL), F32),
        compiler_params=pltpu.CompilerParams(dimension_semantics=("arbitrary", "arbitrary"),
                                             vmem_limit_bytes=VMEM_LIMIT),
        name="out",
    )(x, pa, pb, yf, yb, of, ob, gate, a0, a2p, k_a, r_k, lnx_w, lnx_b, onorm_g, w_out_bf16, final_g)


def kernel(x_prompt, x_sample, state_rwkv, state_hgrn, c, c_ctx, norm_g, w_ada, b_ada, w_in, mu_h, mu_v, w0, w2,
           a0, a2, k_k, k_a, r_k, lnx_w, lnx_b, lb_logits, onorm_g, w_out, final_g):
    l = 0
    bp = x_prompt.shape[0]
    bs = x_sample.shape[0]
    lb_all = jnp.cumsum(jax.nn.softmax(lb_logits.astype(F32), axis=0), axis=0)
    lb = lb_all[l]

    wa_bf = w_in[l, :, :A_COLS].astype(BF16)
    wb_bf = w_in[l, :, A_COLS:].astype(BF16)
    w_out_bf = w_out[l].astype(BF16)
    zeros = jnp.zeros((2, LORA // 2, D_A), F32)
    wa2 = jnp.concatenate([jnp.concatenate([w2[l], zeros], axis=2),
                           jnp.concatenate([zeros, a2[l]], axis=2)], axis=1)
    a2p = jnp.concatenate([zeros, a2[l]], axis=1)
    row = lambda z: z.reshape(1, -1)

    cc = jnp.concatenate([c_ctx[None, :], c, jnp.zeros((16 - 1 - bs, D_MODEL), F32)], axis=0)
    m = _ada_call(cc, w_ada[l], row(b_ada[l]))
    shift, scale, gate = m[:, :D_MODEL], m[:, D_MODEL:2 * D_MODEL], m[:, 2 * D_MODEL:]
    ctx = lambda z: jnp.broadcast_to(z[0:1, None, :], (bp, 1, D_MODEL))
    lat = lambda z: z[1:1 + bs, None, :]

    def path(x, sh, sc, gt, states, grid_shift):
        pa, pb = _proj_call(x, sh, sc, row(norm_g[l]), wa_bf, wb_bf, row(mu_h[l]), row(mu_v[l]), grid_shift)
        res = _scan_call(pa, pb, states, w0[l], a0[l], wa2, row(k_k[l]), row(k_a[l]), lb)
        yf, yb, of, ob = res[:4]
        y = _out_call(x, pa, pb, yf, yb, of, ob, gt, a0[l], a2p, row(k_a[l]), row(r_k[l]), row(lnx_w[l]),
                      row(lnx_b[l]), row(onorm_g[l]), w_out_bf, row(final_g))
        return y, res[4:]

    y_prompt, (s_a, s_b) = path(x_prompt, ctx(shift), ctx(scale), ctx(gate), None, False)
    groups_a = D_A // PACK
    sa0 = state_rwkv[:, l].reshape(bs, 2, groups_a, PACK, HEAD_A)
    y_sample, _ = path(x_sample, lat(shift), lat(scale), lat(gate), (sa0, state_hgrn[:, l]), True)
    s_a = s_a.reshape(bp, 1, 2, D_A // HEAD_A, HEAD_A, HEAD_A)
    return y_prompt, y_sample, s_a, s_b[:, None]
```

```python
import functools

import jax
import jax.numpy as jnp
from jax import lax
from jax.experimental import pallas as pl
from jax.experimental.pallas import tpu as pltpu

F32 = jnp.float32
BF16 = jnp.bfloat16

D_MODEL = 1024
D_A = 512
D_B = 512
HEAD_A = 64
HEAD_B = 128
LORA = 128
A_COLS = 4 * D_A + 2 * LORA
B_COLS = 5 * D_B
GRID_W = 64
CHUNK = 64
SCAN_BATCH = 2
SUM_TERMS = 2
BLOCK_ROWS = 32
BLOCK_LANES = 128
EPS = 1e-6
GN_EPS = 64e-5
DECAY_SCALE = 0.6065306597126334
PACK = 256
TP = 512
J_LORA = 4 * D_A // TP
J_GATE = J_LORA + 1 + 3 * D_B // TP
J_END = J_GATE + 2 * D_B // TP
PROJ_ROWS = 1024
VMEM_LIMIT = 56 * 1024 * 1024

NN = ((1,), (0,))
NT = ((1,), (1,))
TN_DIMS = ((0,), (0,))


def _dot(a, b, dims=NN):
    return lax.dot_general(a, b, (dims, ((), ())), preferred_element_type=F32)


def _split(x, passes=1):
    if isinstance(x, tuple):
        return x
    hi = x.astype(BF16)
    if passes == 1:
        return (hi,)
    return hi, (x - hi.astype(F32)).astype(BF16)


def _mm(a, b, dims=NN, passes=1):
    a = _split(a, passes)
    b = _split(b, passes)
    if len(a) == 1 or len(b) == 1:
        return _dot(a[0], b[0], dims)
    return _dot(a[0], b[0], dims) + (_dot(a[0], b[1], dims) + _dot(a[1], b[0], dims))


def _terms(x, n):
    out = []
    for _ in range(n - 1):
        h = x.astype(BF16)
        out.append(h)
        x = x - h.astype(F32)
    out.append(x.astype(BF16))
    return out


def _mm_exact_lhs(a_bf16, b, n=3):
    parts = [_dot(a_bf16, t) for t in _terms(b, n)]
    return functools.reduce(lambda acc, z: z + acc, reversed(parts))


def _mm_exact_rhs(a, b_bf16, n=3):
    parts = [_dot(t, b_bf16) for t in _terms(a, n)]
    return functools.reduce(lambda acc, z: z + acc, reversed(parts))


def _sigmoid(x):
    return 1.0 / (1.0 + jnp.exp(-x))


def _silu(x):
    return x * _sigmoid(x)


def _softplus(x):
    return jnp.maximum(x, 0.0) + jnp.log(1.0 + jnp.exp(-jnp.abs(x)))


def _iota(shape, dim):
    return lax.broadcasted_iota(jnp.int32, shape, dim)


def _block_mask(rows, cols, row_shift, col_shift):
    same = (_iota((rows, cols), 0) >> row_shift) == (_iota((rows, cols), 1) >> col_shift)
    return jnp.where(same, 1.0, 0.0).astype(F32)


def _tile_rows(x, n):
    if isinstance(x, tuple):
        return tuple(_tile_rows(t, n) for t in x)
    return jnp.concatenate([x] * n, axis=0)


def _masked(x, mask_bf16):
    if isinstance(x, tuple):
        return tuple(_masked(t, mask_bf16) for t in x)
    return x * mask_bf16


def _ada_kernel(c_ref, w_ref, b_ref, m_ref):
    m_ref[...] = _mm(_silu(c_ref[...]), w_ref[...], passes=3) + b_ref[...]


def _ada_call(cc, w_ada, b_ada):
    rows = cc.shape[0]
    return pl.pallas_call(
        _ada_kernel,
        grid=(3,),
        in_specs=[pl.BlockSpec((rows, D_MODEL), lambda j: (0, 0)),
                  pl.BlockSpec((D_MODEL, D_MODEL), lambda j: (0, j)),
                  pl.BlockSpec((1, D_MODEL), lambda j: (0, j))],
        out_specs=pl.BlockSpec((rows, D_MODEL), lambda j: (0, j)),
        out_shape=jax.ShapeDtypeStruct((rows, 3 * D_MODEL), F32),
        compiler_params=pltpu.CompilerParams(dimension_semantics=("arbitrary",),
                                             vmem_limit_bytes=VMEM_LIMIT),
        name="ada",
    )(cc, w_ada, b_ada)


def _proj_kernel(x_ref, sh_ref, sc_ref, g_ref, wa_ref, wl_ref, wb_ref, wf_ref, muha_ref, muva_ref, muhl_ref,
                 muvl_ref, pa_ref, plo_ref, pb_ref, pf_ref, xm_ref, *, nbp, seq, grid_shift):
    j = pl.program_id(1)
    rows = nbp * seq

    @pl.when(j == 0)
    def _():
        for b in range(nbp):
            x = x_ref[b]
            xn = x * lax.rsqrt(jnp.mean(x * x, axis=-1, keepdims=True) + EPS) * g_ref[...]
            xm_ref[b * seq:(b + 1) * seq, :] = (xn * (1.0 + sc_ref[b]) + sh_ref[b]).astype(BF16)

    def shifted(p, muh, muv):
        tn = p.shape[1]
        w = GRID_W if grid_shift else seq
        nblk = rows // w
        pos = _iota((1, w, tn), 1)
        to3 = lambda z: z.reshape(nblk, w, tn)
        prev = jnp.where(pos == 0, 0.0, to3(pltpu.roll(p, 1, axis=0)))
        nxt = jnp.where(pos == w - 1, 0.0, to3(pltpu.roll(p, rows - 1, axis=0)))
        out = (0.5 * muh) * (prev + nxt)
        if grid_shift:
            p3 = to3(p)
            zblk = jnp.zeros((1, w, tn), F32)
            up = jnp.concatenate([zblk, p3[:nblk - 1]], axis=0)
            dn = jnp.concatenate([p3[1:], zblk], axis=0)
            out = (1.0 - muh - muv) * p3 + out + (0.5 * muv) * (up + dn)
        else:
            out = (1.0 - muh) * to3(p) + out
        return out.reshape(nbp, seq, tn)

    @pl.when(j < J_LORA)
    def _():
        pa_ref[...] = shifted(_dot(xm_ref[...], wa_ref[...]), muha_ref[...], muva_ref[...]).astype(BF16)

    @pl.when(j == J_LORA)
    def _():
        plo_ref[...] = shifted(_dot(xm_ref[...], wl_ref[...]), muhl_ref[...], muvl_ref[...])

    @pl.when((j > J_LORA) & (j < J_GATE))
    def _():
        pb_ref[...] = _dot(xm_ref[...], wb_ref[...]).reshape(nbp, seq, TP).astype(BF16)

    @pl.when(j >= J_GATE)
    def _():
        pf_ref[...] = _dot(xm_ref[...], wf_ref[...]).reshape(nbp, seq, TP)


def _proj_call(x, shift, scale, norm_g, w_parts, mu_h, mu_v, grid_shift):
    bsz, seq, _ = x.shape
    wa, wl, wb, wf = w_parts
    nbp = 1 if grid_shift else PROJ_ROWS // seq
    kern = functools.partial(_proj_kernel, nbp=nbp, seq=seq, grid_shift=grid_shift)
    a_idx = lambda j: jnp.minimum(j, J_LORA - 1)
    b_idx = lambda j: jnp.clip(j - J_LORA - 1, 0, J_GATE - J_LORA - 2)
    f_idx = lambda j: jnp.maximum(j - J_GATE, 0)
    tok = lambda idx: (lambda b, j: (b, 0, idx(j)))
    col = lambda idx: (lambda b, j: (0, idx(j)))
    zero = lambda j: 0
    return pl.pallas_call(
        kern,
        grid=(bsz // nbp, J_END),
        in_specs=[pl.BlockSpec((nbp, seq, D_MODEL), tok(zero)),
                  pl.BlockSpec((nbp, 1, D_MODEL), tok(zero)),
                  pl.BlockSpec((nbp, 1, D_MODEL), tok(zero)),
                  pl.BlockSpec((1, D_MODEL), col(zero)),
                  pl.BlockSpec((D_MODEL, TP), col(a_idx)),
                  pl.BlockSpec((D_MODEL, 2 * LORA), col(zero)),
                  pl.BlockSpec((D_MODEL, TP), col(b_idx)),
                  pl.BlockSpec((D_MODEL, TP), col(f_idx)),
                  pl.BlockSpec((1, TP), col(a_idx)), pl.BlockSpec((1, TP), col(a_idx)),
                  pl.BlockSpec((1, 2 * LORA), col(zero)), pl.BlockSpec((1, 2 * LORA), col(zero))],
        out_specs=[pl.BlockSpec((nbp, seq, TP), tok(a_idx)),
                   pl.BlockSpec((nbp, seq, 2 * LORA), tok(zero)),
                   pl.BlockSpec((nbp, seq, TP), tok(b_idx)),
                   pl.BlockSpec((nbp, seq, TP), tok(f_idx))],
        out_shape=[jax.ShapeDtypeStruct((bsz, seq, 4 * D_A), BF16),
                   jax.ShapeDtypeStruct((bsz, seq, 2 * LORA), F32),
                   jax.ShapeDtypeStruct((bsz, seq, 3 * D_B), BF16),
                   jax.ShapeDtypeStruct((bsz, seq, 2 * D_B), F32)],
        scratch_shapes=[pltpu.VMEM((nbp * seq, D_MODEL), BF16)],
        compiler_params=pltpu.CompilerParams(dimension_semantics=("arbitrary", "arbitrary"),
                                             vmem_limit_bytes=VMEM_LIMIT),
        name="proj_grid" if grid_shift else "proj_seq",
    )(x, shift, scale, norm_g, wa, wl, wb, wf, mu_h[:, :4 * D_A], mu_v[:, :4 * D_A], mu_h[:, 4 * D_A:],
      mu_v[:, 4 * D_A:])


def _scan_masks():
    c = CHUNK
    t = _iota((c, PACK), 0)
    s = _iota((c, PACK), 1) & (c - 1)
    tt = _iota((c, c), 0)
    ss = _iota((c, c), 1)
    f01 = lambda cond: jnp.where(cond, 1.0, 0.0).astype(F32)
    bm64 = _block_mask(PACK, PACK, 6, 6)
    bm128 = _block_mask(PACK, PACK, 7, 7)
    return dict(
        bm64=bm64, bm64_bf=bm64.astype(BF16), bm128=bm128,
        bm_hg_bf=_block_mask(4 * c, D_B, 6, 7).astype(BF16),
        eye=f01(t == s),
        incl=(f01(t >= s), f01(t <= s)),
        strict=(f01(t > s), f01(t < s)),
        tri=(f01(tt >= ss).astype(BF16), f01(tt <= ss).astype(BF16)),
    )


def _lockstep(gens):
    results = [None] * len(gens)
    live = list(enumerate(gens))
    while live:
        still = []
        for i, g in live:
            try:
                next(g)
                still.append((i, g))
            except StopIteration as stop:
                results[i] = stop.value
        live = still
    return results


def _rwkv_group(rt, kt, kh, bh, kb, bb, v, s_ref, egc, mk, d):
    c = CHUNK
    bd = lambda x: _masked(_tile_rows(_split(x), PACK // c), mk["bm64_bf"])
    kr = _split(jnp.concatenate([kt, rt], axis=0))
    aa_k = _mm(kr, bd(kh), NT)
    aa_b = _mm(kr, bd(bh), NT)
    s_bd = s_ref[...]
    krs = _mm(kr, s_bd, NT)
    yield
    a_kk = aa_k[:c] * mk["strict"][d]
    a_rk = aa_k[c:] * mk["incl"][d]
    n = -(aa_b[:c] * mk["strict"][d])
    a_rb = aa_b[c:] * mk["incl"][d]
    x = mk["eye"] + n
    p = _mm(n, bd(n))
    akv = _mm(jnp.concatenate([a_kk, a_rk], axis=0), bd(v))
    yield
    for i in range(5):
        if i < 4:
            xp = _mm(jnp.concatenate([x, p], axis=0), bd(p))
            x = x + xp[:c]
            p = xp[c:]
        else:
            x = x + _mm(x, bd(p))
        yield
    u = _mm(x, bd(krs[:c] + akv[:c]))
    yield
    y = krs[c:] + akv[c:] - _mm(a_rb, bd(u))
    vu = jnp.concatenate([v.astype(BF16), (-u).astype(BF16)], axis=0)
    kbb = jnp.concatenate([kb, bb], axis=0)
    s_ref[...] = (s_bd * egc + _mm(vu, kbb, TN_DIMS)) * mk["bm64"]
    return y


def _hgrn_dir(qn, kn, qt, kbh, vb, eghc, s_refs, mk, d):
    bm_hg = mk["bm_hg_bf"]
    a_p = _mm(qn, _masked(_tile_rows(_split(kn), 4), bm_hg), NT) * mk["incl"][d]
    inter = []
    for grp, s_ref in enumerate(s_refs):
        sl = slice(PACK * grp, PACK * (grp + 1))
        st = s_ref[...]
        inter.append(_mm(qt[:, sl], st, NT))
        s_ref[...] = (st * eghc[:, sl] + _mm(vb[:, sl], kbh[:, sl], TN_DIMS)) * mk["bm128"]
    yield
    o = _mm(a_p, _masked(_tile_rows(_split(vb), 4), bm_hg))
    return o + jnp.concatenate(inter, axis=1)


def _blockwise(fn, operands, n_out, rows=CHUNK, width=D_A):
    cols = []
    for l0 in range(0, width, BLOCK_LANES):
        parts = []
        for r0 in range(0, rows, BLOCK_ROWS):
            blk = [z[(slice(None) if z.shape[0] == 1 else slice(r0, r0 + BLOCK_ROWS)), l0:l0 + BLOCK_LANES]
                   for z in operands]
            parts.append(fn(*blk))
        cols.append([jnp.concatenate([p[i] for p in parts], axis=0) for i in range(n_out)])
    return [jnp.concatenate([col[i] for col in cols], axis=1) for i in range(n_out)]


def _sum_small_first(parts):
    return functools.reduce(lambda acc, z: z + acc, reversed(parts))


def _chunk_operands(p_refs, bi, d, prm, mk):
    c = CHUNK
    n_t = SUM_TERMS
    w0, a0, wa2, k_k, k_a, lb = prm
    pa_ref, lo_ref, pb_ref, pf_ref = p_refs
    cols = lambda ref, i, width=D_A: ref.at[bi, :, i * width:(i + 1) * width]
    r_ref, k_ref, v_ref = cols(pa_ref, 0), cols(pa_ref, 1), cols(pa_ref, 2)
    lo = lo_ref[bi]
    lo = jnp.where(_iota((c, LORA), 1) < LORA // 2, jnp.tanh(lo), lo)
    wa = _mm(lo, wa2[d])
    yield

    def gates(wa_w, wa_a, k, fr, w0_, a0_, kk_, lb_):
        lw = -DECAY_SCALE * _sigmoid(w0_ + wa_w)
        a = _sigmoid(a0_ + wa_a)
        kk = k.astype(F32) * kk_
        f = lb_ + (1.0 - lb_) * _sigmoid(fr)
        gf = jnp.log(f)
        return (lw, a, kk, 1.0 - f, *_terms(kk * kk, n_t), *_terms(lw, n_t), *_terms(gf, n_t))

    res = _blockwise(gates, [wa[:, :D_A], wa[:, D_A:], k_ref, pf_ref.at[bi], w0[d:d + 1], a0[d:d + 1],
                             k_k, lb[d:d + 1]], 4 + 3 * n_t)
    lw, a, kk, kf = res[:4]
    sq_t, lw_t, gf_t = res[4:4 + n_t], res[4 + n_t:4 + 2 * n_t], res[4 + 2 * n_t:]
    seg = lambda lanes: _sum_small_first([_dot(t[:, lanes], mk["bm64_bf"]) for t in sq_t])
    n2 = jnp.concatenate([seg(slice(0, PACK)), seg(slice(PACK, D_A))], axis=1)
    gcum = _sum_small_first([_dot(mk["tri"][d], jnp.concatenate([tl, tg], axis=1))
                             for tl, tg in zip(lw_t, gf_t)])
    yield
    last = c - 1 if d == 0 else 0
    g, gh = gcum[:, :D_A], gcum[:, D_A:]
    gc, ghc, ghm = g[last:last + 1], gh[last:last + 1], gh[c // 2:c // 2 + 1]
    egc, eghc = jnp.exp(gc), jnp.exp(ghc)

    def rwkv_operands(r, k, kk_, n2_, a_, g_, lw_, ka_, egc_):
        kap = kk_ * lax.rsqrt(jnp.maximum(n2_, 1e-24))
        keff = k.astype(F32) * (1.0 + (a_ - 1.0) * ka_)
        eng = jnp.exp(-g_)
        kh = keff * eng
        bh = (kap * a_) * eng
        out = (r.astype(F32) * jnp.exp(g_), kap * jnp.exp(g_ - lw_), kh, bh, kh * egc_, bh * egc_)
        return tuple(z.astype(BF16) for z in out)

    rt, kt, kh, bh, kb, bb = _blockwise(rwkv_operands, [r_ref, k_ref, kk, n2, a, g, lw, k_a, egc], 6)

    def hgrn_operands(qr, kf_, gh_, ghm_, e_m, e_cm):
        q = _silu(qr.astype(F32))
        ghn = gh_ - ghm_
        qn = q * jnp.exp(ghn)
        kn = kf_ * jnp.exp(-ghn)
        return tuple(z.astype(BF16) for z in (qn, kn, qn * e_m, kn * e_cm))

    qn, kn, qt, kbh = _blockwise(hgrn_operands, [cols(pb_ref, 0, D_B), kf, gh, ghm, jnp.exp(ghm),
                                                 jnp.exp(ghc - ghm)], 4, width=D_B)
    v = v_ref[...]
    vb = pb_ref[bi, :, D_B:2 * D_B]
    return (rt, kt, kh, bh, kb, bb, v, qn, kn, qt, kbh, vb), (egc, eghc)


def _scan_dir(p_refs, bi, d, prm, sa_scr, sb_scr, mk):
    ops, (egc, eghc) = yield from _chunk_operands(p_refs, bi, d, prm, mk)
    rt, kt, kh, bh, kb, bb, v, qn, kn, qt, kbh, vb = ops
    chains = []
    for grp in range(D_A // PACK):
        sl = slice(PACK * grp, PACK * (grp + 1))
        chains.append(_rwkv_group(rt[:, sl], kt[:, sl], kh[:, sl], bh[:, sl], kb[:, sl], bb[:, sl], v[:, sl],
                                  sa_scr.at[bi, d, grp], egc[:, sl], mk, d))
    chains.append(_hgrn_dir(qn, kn, qt, kbh, vb, eghc, [sb_scr.at[bi, d, grp] for grp in range(D_B // PACK)],
                            mk, d))
    results = [None] * len(chains)
    live = list(enumerate(chains))
    while live:
        still = []
        for i, g in live:
            try:
                next(g)
                still.append((i, g))
            except StopIteration as stop:
                results[i] = stop.value
        live = still
        yield
    return jnp.concatenate(results[:-1], axis=1), results[-1]


def _state_slots():
    return [(bi, d, grp) for bi in range(SCAN_BATCH) for d in range(2) for grp in range(D_A // PACK)]


def _load_states(sa0_ref, sb0_ref, sa_scr, sb_scr):
    spread = jnp.where(_iota((HEAD_A, PACK), 0) == (_iota((HEAD_A, PACK), 1) & (HEAD_A - 1)), 1.0, 0.0)
    spread = spread.astype(BF16)
    bm64 = _block_mask(PACK, PACK, 6, 6)
    zero = jnp.zeros((HEAD_B, HEAD_B), F32)
    for bi, d, grp in _state_slots():
        sa_scr[bi, d, grp] = _mm_exact_rhs(sa0_ref[bi, d, grp], spread) * bm64
        h0 = sb0_ref[bi, d, 2 * grp].T
        h1 = sb0_ref[bi, d, 2 * grp + 1].T
        sb_scr[bi, d, grp] = jnp.concatenate([jnp.concatenate([h0, zero], axis=1),
                                              jnp.concatenate([zero, h1], axis=1)], axis=0)


def _store_states(sa_scr, sb_scr, sa_out, sb_out):
    gather = jnp.where((_iota((PACK, HEAD_A), 0) & (HEAD_A - 1)) == _iota((PACK, HEAD_A), 1), 1.0, 0.0)
    gather = gather.astype(BF16)
    for bi, d, grp in _state_slots():
        sa_out[bi, d, grp] = _mm_exact_rhs(sa_scr[bi, d, grp], gather)
        for hh in range(PACK // HEAD_B):
            blk = sb_scr[bi, d, grp, HEAD_B * hh:HEAD_B * (hh + 1), HEAD_B * hh:HEAD_B * (hh + 1)]
            sb_out[bi, d, 2 * grp + hh] = blk.T


def _scan_kernel(*refs, from_zero):
    p_refs, rest = (refs[0:4], refs[4:8]), refs[8:]
    if from_zero:
        (w0_ref, a0_ref, wa2_ref, kk_ref, ka_ref, lb_ref, yf_ref, yb_ref, of_ref, ob_ref, sa_out, sb_out,
         sa_scr, sb_scr) = rest
    else:
        (sa0_ref, sb0_ref, w0_ref, a0_ref, wa2_ref, kk_ref, ka_ref, lb_ref, yf_ref, yb_ref, of_ref, ob_ref,
         sa_scr, sb_scr) = rest
    ci = pl.program_id(1)

    @pl.when(ci == 0)
    def _():
        if from_zero:
            sa_scr[...] = jnp.zeros(sa_scr.shape, F32)
            sb_scr[...] = jnp.zeros(sb_scr.shape, F32)
        else:
            _load_states(sa0_ref, sb0_ref, sa_scr, sb_scr)

    mk = _scan_masks()
    prm = (w0_ref[...], a0_ref[...], wa2_ref, kk_ref[...], ka_ref[...], lb_ref[...])
    runs = [(bi, d) for bi in range(SCAN_BATCH) for d in range(2)]
    outs = _lockstep([_scan_dir(p_refs[d], bi, d, prm, sa_scr, sb_scr, mk) for bi, d in runs])
    for (bi, d), (y, o) in zip(runs, outs):
        (yf_ref, yb_ref)[d][bi] = y.astype(BF16)
        (of_ref, ob_ref)[d][bi] = o.astype(BF16)

    if from_zero:
        @pl.when(ci == pl.num_programs(1) - 1)
        def _():
            _store_states(sa_scr, sb_scr, sa_out, sb_out)


def _scan_call(proj, states, w0, a0, wa2, k_k, k_a, lb):
    pa, plo, pb, pf = proj
    bsz, seq, _ = pa.shape
    nc = seq // CHUNK
    nb = SCAN_BATCH
    from_zero = states is None
    fwd = lambda b, c: (b, c, 0)
    bwd = lambda b, c: (b, nc - 1 - c, 0)
    st = lambda b, c: (b, 0, 0, 0, 0)
    full = lambda shape: pl.BlockSpec(shape, lambda b, c: (0,) * len(shape))
    sa_block = (nb, 2, D_A // PACK, PACK, HEAD_A)
    sb_block = (nb, 2, D_B // HEAD_B, HEAD_B, HEAD_B)
    y_shape = jax.ShapeDtypeStruct((bsz, seq, D_A), BF16)
    out_specs = [pl.BlockSpec((nb, CHUNK, D_A), fwd), pl.BlockSpec((nb, CHUNK, D_A), bwd),
                 pl.BlockSpec((nb, CHUNK, D_B), fwd), pl.BlockSpec((nb, CHUNK, D_B), bwd)]
    out_shape = [y_shape, y_shape, y_shape, y_shape]

    def chunk_blocks(d):
        chunk = (lambda c: c) if d == 0 else (lambda c: nc - 1 - c)
        return [pl.BlockSpec((nb, CHUNK, 3 * D_A), lambda b, c: (b, chunk(c), 0)),
                pl.BlockSpec((nb, CHUNK, LORA), lambda b, c: (b, chunk(c), d)),
                pl.BlockSpec((nb, CHUNK, 2 * D_B), lambda b, c: (b, chunk(c), 0)),
                pl.BlockSpec((nb, CHUNK, D_B), lambda b, c: (b, chunk(c), d))]

    in_specs = chunk_blocks(0) + chunk_blocks(1)
    args = [pa, plo, pb, pf] * 2
    if from_zero:
        out_specs += [pl.BlockSpec(sa_block, st), pl.BlockSpec(sb_block, st)]
        out_shape += [jax.ShapeDtypeStruct((bsz,) + sa_block[1:], F32),
                      jax.ShapeDtypeStruct((bsz,) + sb_block[1:], F32)]
    else:
        in_specs += [pl.BlockSpec(sa_block, st), pl.BlockSpec(sb_block, st)]
        args += list(states)
    in_specs += [full((2, D_A)), full((2, D_A)), full((2, LORA, 2 * D_A)),
                 full((1, D_A)), full((1, D_A)), full((2, D_B))]
    args += [w0, a0, wa2, k_k, k_a, lb]
    bd_block = (nb, 2, D_A // PACK, PACK, PACK)
    return pl.pallas_call(
        functools.partial(_scan_kernel, from_zero=from_zero),
        grid=(bsz // nb, nc),
        in_specs=in_specs,
        out_specs=out_specs,
        out_shape=out_shape,
        scratch_shapes=[pltpu.VMEM(bd_block, F32), pltpu.VMEM(bd_block, F32)],
        compiler_params=pltpu.CompilerParams(dimension_semantics=("arbitrary", "arbitrary"),
                                             vmem_limit_bytes=VMEM_LIMIT),
        name="scan_state" if from_zero else "scan",
    )(*args)


def _out_kernel(x_ref, pa_ref, plo_ref, zb_ref, yf_ref, yb_ref, of_ref, ob_ref, gate_ref, a0_ref, a2p_ref, ka_ref,
                rk_ref, lnw_ref, lnb_ref, og_ref, wout_ref, fg_ref, out_ref):
    up = lambda z: z.astype(F32)
    r = up(pa_ref[0, :, 0:D_A])
    k = up(pa_ref[0, :, D_A:2 * D_A])
    v = up(pa_ref[0, :, 2 * D_A:3 * D_A])
    za = up(pa_ref[0, :, 3 * D_A:4 * D_A])
    bm64_bf = _block_mask(PACK, PACK, 6, 6).astype(BF16)

    def seg_sum(z):
        return jnp.concatenate([_mm_exact_rhs(z[:, :PACK], bm64_bf, SUM_TERMS),
                                _mm_exact_rhs(z[:, PACK:], bm64_bf, SUM_TERMS)], axis=1)

    y = up(yf_ref[0]) + up(yb_ref[0])
    mu = seg_sum(y) * (1.0 / HEAD_A)
    dlt = y - mu
    var = seg_sum(dlt * dlt) * (1.0 / HEAD_A)
    yn = dlt * lax.rsqrt(var + GN_EPS) * lnw_ref[...] + lnb_ref[...]
    keffs = []
    for d in range(2):
        lo = plo_ref[0, :, LORA * d:LORA * (d + 1)]
        a = _sigmoid(a0_ref[d:d + 1] + _mm(lo, a2p_ref[d]))
        keffs.append(k * (1.0 + (a - 1.0) * ka_ref[...]))
    kmean = 0.5 * (keffs[0] + keffs[1])
    bonus = seg_sum(r * kmean * rk_ref[...]) * v
    out_a = (yn + bonus) * _silu(za)

    o = up(of_ref[0]) + up(ob_ref[0])
    og = og_ref[...]
    zb = up(zb_ref[0])
    outs = [out_a]
    for h in range(D_B // HEAD_B):
        sl = slice(HEAD_B * h, HEAD_B * (h + 1))
        oh = o[:, sl]
        oh = oh * lax.rsqrt(jnp.mean(oh * oh, axis=-1, keepdims=True) + EPS) * og[:, sl]
        outs.append(oh * _silu(zb[:, sl]))
    mix = jnp.concatenate(outs, axis=1)
    proj = _dot(mix.astype(BF16), wout_ref[...])
    hs = x_ref[0] + gate_ref[0] * proj
    out_ref[0] = hs * lax.rsqrt(jnp.mean(hs * hs, axis=-1, keepdims=True) + EPS) * fg_ref[...]


def _out_call(x, proj, yf, yb, of, ob, gate, a0, a2p, k_a, r_k, lnx_w, lnx_b, onorm_g, w_out_bf16, final_g):
    pa, plo, pb, _ = proj
    bsz, seq, _ = x.shape
    tm = 256
    tok = lambda b, i: (b, i, 0)
    full = lambda shape: pl.BlockSpec(shape, lambda b, i: (0,) * len(shape))
    return pl.pallas_call(
        _out_kernel,
        grid=(bsz, seq // tm),
        in_specs=[pl.BlockSpec((1, tm, D_MODEL), tok),
                  pl.BlockSpec((1, tm, 4 * D_A), tok),
                  pl.BlockSpec((1, tm, 2 * LORA), tok),
                  pl.BlockSpec((1, tm, D_B), lambda b, i: (b, i, 2)),
                  pl.BlockSpec((1, tm, D_A), tok), pl.BlockSpec((1, tm, D_A), tok),
                  pl.BlockSpec((1, tm, D_B), tok), pl.BlockSpec((1, tm, D_B), tok),
                  pl.BlockSpec((1, 1, D_MODEL), lambda b, i: (b, 0, 0)),
                  full((2, D_A)), full((2, LORA, D_A)), full((1, D_A)), full((1, D_A)),
                  full((1, D_A)), full((1, D_A)), full((1, D_B)),
                  full((D_MODEL, D_MODEL)), full((1, D_MODEL))],
        out_specs=pl.BlockSpec((1, tm, D_MODEL), tok),
        out_shape=jax.ShapeDtypeStruct((bsz, seq, D_MODEL), F32),
        compiler_params=pltpu.CompilerParams(dimension_semantics=("arbitrary", "arbitrary"),
                                             vmem_limit_bytes=VMEM_LIMIT),
        name="out",
    )(x, pa, plo, pb, yf, yb, of, ob, gate, a0, a2p, k_a, r_k, lnx_w, lnx_b, onorm_g, w_out_bf16, final_g)


def kernel(x_prompt, x_sample, state_rwkv, state_hgrn, c, c_ctx, norm_g, w_ada, b_ada, w_in, mu_h, mu_v, w0, w2,
           a0, a2, k_k, k_a, r_k, lnx_w, lnx_b, lb_logits, onorm_g, w_out, final_g):
    l = 0
    bp = x_prompt.shape[0]
    bs = x_sample.shape[0]
    lb_all = jnp.cumsum(jax.nn.softmax(lb_logits.astype(F32), axis=0), axis=0)
    lb = lb_all[l]

    w_bf = w_in[l].astype(BF16)
    hg = A_COLS
    w_parts = (w_bf[:, :4 * D_A], w_bf[:, 4 * D_A:hg],
               jnp.concatenate([w_bf[:, hg:hg + 2 * D_B], w_bf[:, hg + 4 * D_B:]], axis=1),
               w_bf[:, hg + 2 * D_B:hg + 4 * D_B])
    w_out_bf = w_out[l].astype(BF16)
    zeros = jnp.zeros((2, LORA // 2, D_A), F32)
    wa2 = jnp.concatenate([jnp.concatenate([w2[l], zeros], axis=2),
                           jnp.concatenate([zeros, a2[l]], axis=2)], axis=1)
    a2p = jnp.concatenate([zeros, a2[l]], axis=1)
    row = lambda z: z.reshape(1, -1)

    cc = jnp.concatenate([c_ctx[None, :], c, jnp.zeros((16 - 1 - bs, D_MODEL), F32)], axis=0)
    m = _ada_call(cc, w_ada[l], row(b_ada[l]))
    shift, scale, gate = m[:, :D_MODEL], m[:, D_MODEL:2 * D_MODEL], m[:, 2 * D_MODEL:]
    ctx = lambda z: jnp.broadcast_to(z[0:1, None, :], (bp, 1, D_MODEL))
    lat = lambda z: z[1:1 + bs, None, :]

    def path(x, sh, sc, gt, states, grid_shift):
        proj = _proj_call(x, sh, sc, row(norm_g[l]), w_parts, row(mu_h[l]), row(mu_v[l]), grid_shift)
        res = _scan_call(proj, states, w0[l], a0[l], wa2, row(k_k[l]), row(k_a[l]), lb)
        yf, yb, of, ob = res[:4]
        y = _out_call(x, proj, yf, yb, of, ob, gt, a0[l], a2p, row(k_a[l]), row(r_k[l]), row(lnx_w[l]),
                      row(lnx_b[l]), row(onorm_g[l]), w_out_bf, row(final_g))
        return y, res[4:]

    y_prompt, (s_a, s_b) = path(x_prompt, ctx(shift), ctx(scale), ctx(gate), None, False)
    groups_a = D_A // PACK
    sa0 = state_rwkv[:, l].reshape(bs, 2, groups_a, PACK, HEAD_A)
    y_sample, _ = path(x_sample, lat(shift), lat(scale), lat(gate), (sa0, state_hgrn[:, l]), True)
    s_a = s_a.reshape(bp, 1, 2, D_A // HEAD_A, HEAD_A, HEAD_A)
    return y_prompt, y_sample, s_a, s_b[:, None]
```

```python
import functools

import jax
import jax.numpy as jnp
from jax import lax
from jax.experimental import pallas as pl
from jax.experimental.pallas import tpu as pltpu

F32 = jnp.float32
BF16 = jnp.bfloat16

D_MODEL = 1024
D_A = 512
D_B = 512
HEAD_A = 64
HEAD_B = 128
LOG2_HEAD_A = HEAD_A.bit_length() - 1
LOG2_HEAD_B = HEAD_B.bit_length() - 1
LORA = 128
A_COLS = 4 * D_A + 2 * LORA
B_COLS = 5 * D_B
GRID_W = 64
CHUNK = 64
LOG2_CHUNK = CHUNK.bit_length() - 1
SCAN_BATCH = 2
SUM_TERMS = 2
BLOCK_ROWS = 32
BLOCK_LANES = 128
EPS = 1e-6
GN_EPS = 64e-5
DECAY_SCALE = 0.6065306597126334
PACK = 256
TNA = 768
TNB = 512
NA = A_COLS // TNA
NB = B_COLS // TNB
PROJ_ROWS = 2048
OUT_ROWS = 512
VMEM_LIMIT = 56 * 1024 * 1024

NN = ((1,), (0,))
NT = ((1,), (1,))
TN_DIMS = ((0,), (0,))


def _dot(a, b, dims=NN):
    return lax.dot_general(a, b, (dims, ((), ())), preferred_element_type=F32)


def _split(x, passes=1):
    if isinstance(x, tuple):
        return x
    hi = x.astype(BF16)
    if passes == 1:
        return (hi,)
    return hi, (x - hi.astype(F32)).astype(BF16)


def _mm(a, b, dims=NN, passes=1):
    a = _split(a, passes)
    b = _split(b, passes)
    if len(a) == 1 or len(b) == 1:
        return _dot(a[0], b[0], dims)
    return _dot(a[0], b[0], dims) + (_dot(a[0], b[1], dims) + _dot(a[1], b[0], dims))


def _terms(x, n):
    out = []
    for _ in range(n - 1):
        h = x.astype(BF16)
        out.append(h)
        x = x - h.astype(F32)
    out.append(x.astype(BF16))
    return out


def _sum_small_first(parts):
    return functools.reduce(lambda acc, z: z + acc, reversed(parts))


def _mm_exact_rhs(a, b_bf16, n=3):
    return _sum_small_first([_dot(t, b_bf16) for t in _terms(a, n)])


def _sigmoid(x):
    return 1.0 / (1.0 + jnp.exp(-x))


def _silu(x):
    return x * _sigmoid(x)


def _iota(shape, dim):
    return lax.broadcasted_iota(jnp.int32, shape, dim)


def _block_mask(rows, cols, row_shift, col_shift):
    same = (_iota((rows, cols), 0) >> row_shift) == (_iota((rows, cols), 1) >> col_shift)
    return jnp.where(same, 1.0, 0.0).astype(F32)


def _head_mask_a():
    return _block_mask(PACK, PACK, LOG2_HEAD_A, LOG2_HEAD_A)


def _tile_rows(x, n):
    if isinstance(x, tuple):
        return tuple(_tile_rows(t, n) for t in x)
    return jnp.concatenate([x] * n, axis=0)


def _masked(x, mask_bf16):
    if isinstance(x, tuple):
        return tuple(_masked(t, mask_bf16) for t in x)
    return x * mask_bf16


def _ada_kernel(c_ref, w_ref, b_ref, m_ref):
    m_ref[...] = _mm(_silu(c_ref[...]), w_ref[...], passes=3) + b_ref[...]


def _ada_call(cc, w_ada, b_ada):
    rows = cc.shape[0]
    return pl.pallas_call(
        _ada_kernel,
        grid=(3,),
        in_specs=[pl.BlockSpec((rows, D_MODEL), lambda j: (0, 0)),
                  pl.BlockSpec((D_MODEL, D_MODEL), lambda j: (0, j)),
                  pl.BlockSpec((1, D_MODEL), lambda j: (0, j))],
        out_specs=pl.BlockSpec((rows, D_MODEL), lambda j: (0, j)),
        out_shape=jax.ShapeDtypeStruct((rows, 3 * D_MODEL), F32),
        compiler_params=pltpu.CompilerParams(dimension_semantics=("arbitrary",),
                                             vmem_limit_bytes=VMEM_LIMIT),
        name="ada",
    )(cc, w_ada, b_ada)


def _proj_kernel(x_ref, sh_ref, sc_ref, g_ref, wa_ref, wb_ref, muh_ref, muv_ref, pa_ref, pb_ref, xm_ref, *,
                 nbp, seq, grid_shift):
    j = pl.program_id(1)
    rows = nbp * seq

    @pl.when(j == 0)
    def _():
        for b in range(nbp):
            x = x_ref[b]
            xn = x * lax.rsqrt(jnp.mean(x * x, axis=-1, keepdims=True) + EPS) * g_ref[...]
            xm_ref[b * seq:(b + 1) * seq, :] = (xn * (1.0 + sc_ref[b]) + sh_ref[b]).astype(BF16)

    @pl.when(j < NA)
    def _():
        p = _dot(xm_ref[...], wa_ref[...])
        w = GRID_W if grid_shift else seq
        nblk = rows // w
        pos = _iota((1, w, TNA), 1)
        to3 = lambda z: z.reshape(nblk, w, TNA)
        prev = jnp.where(pos == 0, 0.0, to3(pltpu.roll(p, 1, axis=0)))
        nxt = jnp.where(pos == w - 1, 0.0, to3(pltpu.roll(p, rows - 1, axis=0)))
        muh = muh_ref[...]
        out = (0.5 * muh) * (prev + nxt)
        if grid_shift:
            muv = muv_ref[...]
            p3 = to3(p)
            zblk = jnp.zeros((1, w, TNA), F32)
            up = jnp.concatenate([zblk, p3[:nblk - 1]], axis=0)
            dn = jnp.concatenate([p3[1:], zblk], axis=0)
            out = (1.0 - muh - muv) * p3 + out + (0.5 * muv) * (up + dn)
        else:
            out = (1.0 - muh) * to3(p) + out
        pa_ref[...] = out.reshape(nbp, seq, TNA)

    @pl.when(j >= NA)
    def _():
        pb_ref[...] = _dot(xm_ref[...], wb_ref[...]).reshape(nbp, seq, TNB)


def _proj_call(x, shift, scale, norm_g, wa_bf16, wb_bf16, mu_h, mu_v, grid_shift):
    bsz, seq, _ = x.shape
    nbp = 1 if grid_shift else PROJ_ROWS // seq
    kern = functools.partial(_proj_kernel, nbp=nbp, seq=seq, grid_shift=grid_shift)
    a_idx = lambda j: jnp.minimum(j, NA - 1)
    b_idx = lambda j: jnp.maximum(j - NA, 0)
    return pl.pallas_call(
        kern,
        grid=(bsz // nbp, NA + NB),
        in_specs=[pl.BlockSpec((nbp, seq, D_MODEL), lambda b, j: (b, 0, 0)),
                  pl.BlockSpec((nbp, 1, D_MODEL), lambda b, j: (b, 0, 0)),
                  pl.BlockSpec((nbp, 1, D_MODEL), lambda b, j: (b, 0, 0)),
                  pl.BlockSpec((1, D_MODEL), lambda b, j: (0, 0)),
                  pl.BlockSpec((D_MODEL, TNA), lambda b, j: (0, a_idx(j))),
                  pl.BlockSpec((D_MODEL, TNB), lambda b, j: (0, b_idx(j))),
                  pl.BlockSpec((1, TNA), lambda b, j: (0, a_idx(j))),
                  pl.BlockSpec((1, TNA), lambda b, j: (0, a_idx(j)))],
        out_specs=[pl.BlockSpec((nbp, seq, TNA), lambda b, j: (b, 0, a_idx(j))),
                   pl.BlockSpec((nbp, seq, TNB), lambda b, j: (b, 0, b_idx(j)))],
        out_shape=[jax.ShapeDtypeStruct((bsz, seq, A_COLS), F32),
                   jax.ShapeDtypeStruct((bsz, seq, B_COLS), F32)],
        scratch_shapes=[pltpu.VMEM((nbp * seq, D_MODEL), BF16)],
        compiler_params=pltpu.CompilerParams(dimension_semantics=("arbitrary", "arbitrary"),
                                             vmem_limit_bytes=VMEM_LIMIT),
        name="proj_grid" if grid_shift else "proj_seq",
    )(x, shift, scale, norm_g, wa_bf16, wb_bf16, mu_h, mu_v)


def _scan_masks():
    c = CHUNK
    t = _iota((c, PACK), 0)
    s = _iota((c, PACK), 1) & (c - 1)
    tt = _iota((c, c), 0)
    ss = _iota((c, c), 1)
    f01 = lambda cond: jnp.where(cond, 1.0, 0.0).astype(F32)
    bm64 = _head_mask_a()
    bm128 = _block_mask(PACK, PACK, LOG2_HEAD_B, LOG2_HEAD_B)
    return dict(
        bm64=bm64, bm64_bf=bm64.astype(BF16), bm128=bm128,
        bm_hg_bf=_block_mask(4 * c, D_B, LOG2_CHUNK, LOG2_HEAD_B).astype(BF16),
        eye=f01(t == s),
        incl=(f01(t >= s), f01(t <= s)),
        strict=(f01(t > s), f01(t < s)),
        tri=(f01(tt >= ss).astype(BF16), f01(tt <= ss).astype(BF16)),
    )


def _lockstep(gens):
    results = [None] * len(gens)
    live = list(enumerate(gens))
    while live:
        still = []
        for i, g in live:
            try:
                next(g)
                still.append((i, g))
            except StopIteration as stop:
                results[i] = stop.value
        live = still
    return results


def _rwkv_group(rt, kt, kh, bh, kb, bb, v, s_ref, egc, mk, d):
    c = CHUNK
    bd = lambda x: _masked(_tile_rows(_split(x), PACK // c), mk["bm64_bf"])
    kr = _split(jnp.concatenate([kt, rt], axis=0))
    aa_k = _mm(kr, bd(kh), NT)
    aa_b = _mm(kr, bd(bh), NT)
    s_bd = s_ref[...]
    krs = _mm(kr, s_bd, NT)
    yield
    a_kk = aa_k[:c] * mk["strict"][d]
    a_rk = aa_k[c:] * mk["incl"][d]
    n = -(aa_b[:c] * mk["strict"][d])
    a_rb = aa_b[c:] * mk["incl"][d]
    x = mk["eye"] + n
    p = _mm(n, bd(n))
    akv = _mm(jnp.concatenate([a_kk, a_rk], axis=0), bd(v))
    yield
    for i in range(5):
        if i < 4:
            xp = _mm(jnp.concatenate([x, p], axis=0), bd(p))
            x = x + xp[:c]
            p = xp[c:]
        else:
            x = x + _mm(x, bd(p))
        yield
    u = _mm(x, bd(krs[:c] + akv[:c]))
    yield
    y = krs[c:] + akv[c:] - _mm(a_rb, bd(u))
    vu = jnp.concatenate([v, (-u).astype(BF16)], axis=0)
    kbb = jnp.concatenate([kb, bb], axis=0)
    s_ref[...] = (s_bd * egc + _mm(vu, kbb, TN_DIMS)) * mk["bm64"]
    return y


def _hgrn_dir(qn, kn, qt, kbh, vb, eghc, s_refs, mk, d):
    bm_hg = mk["bm_hg_bf"]
    heads = D_B // HEAD_B
    a_p = _mm(qn, _masked(_tile_rows(_split(kn), heads), bm_hg), NT) * mk["incl"][d]
    inter = []
    for grp, s_ref in enumerate(s_refs):
        sl = slice(PACK * grp, PACK * (grp + 1))
        st = s_ref[...]
        inter.append(_mm(qt[:, sl], st, NT))
        s_ref[...] = (st * eghc[:, sl] + _mm(vb[:, sl], kbh[:, sl], TN_DIMS)) * mk["bm128"]
    yield
    o = _mm(a_p, _masked(_tile_rows(_split(vb), heads), bm_hg))
    return o + jnp.concatenate(inter, axis=1)


def _blockwise(fn, operands, n_out, rows=CHUNK, width=D_A):
    cols = []
    for l0 in range(0, width, BLOCK_LANES):
        parts = []
        for r0 in range(0, rows, BLOCK_ROWS):
            blk = [z[(slice(None) if z.shape[0] == 1 else slice(r0, r0 + BLOCK_ROWS)), l0:l0 + BLOCK_LANES]
                   for z in operands]
            parts.append(fn(*blk))
        cols.append([jnp.concatenate([p[i] for p in parts], axis=0) for i in range(n_out)])
    return [jnp.concatenate([col[i] for col in cols], axis=1) for i in range(n_out)]


def _chunk_operands(pa_ref, pb_ref, bi, d, prm, mk):
    c = CHUNK
    n_t = SUM_TERMS
    w0, a0, wa2, k_k, k_a, lb = prm
    cols = lambda ref, i, width=D_A: ref.at[bi, :, i * width:(i + 1) * width]
    r_ref, k_ref, v_ref = cols(pa_ref, 0), cols(pa_ref, 1), cols(pa_ref, 2)
    lo = pa_ref[bi, :, 4 * D_A + LORA * d:4 * D_A + LORA * (d + 1)]
    lo = jnp.where(_iota((c, LORA), 1) < LORA // 2, jnp.tanh(lo), lo)
    wa = _mm(lo, wa2[d])
    yield

    def gates(wa_w, wa_a, k, fr, w0_, a0_, kk_, lb_):
        lw = -DECAY_SCALE * _sigmoid(w0_ + wa_w)
        a = _sigmoid(a0_ + wa_a)
        kk = k * kk_
        f = lb_ + (1.0 - lb_) * _sigmoid(fr)
        gf = jnp.log(f)
        return (lw, a, kk, 1.0 - f, *_terms(kk * kk, n_t), *_terms(lw, n_t), *_terms(gf, n_t))

    res = _blockwise(gates, [wa[:, :D_A], wa[:, D_A:], k_ref, cols(pb_ref, 2 + d, D_B), w0[d:d + 1], a0[d:d + 1],
                             k_k, lb[d:d + 1]], 4 + 3 * n_t)
    lw, a, kk, kf = res[:4]
    sq_t, lw_t, gf_t = res[4:4 + n_t], res[4 + n_t:4 + 2 * n_t], res[4 + 2 * n_t:]
    seg = lambda lanes: _sum_small_first([_dot(t[:, lanes], mk["bm64_bf"]) for t in sq_t])
    n2 = jnp.concatenate([seg(slice(0, PACK)), seg(slice(PACK, D_A))], axis=1)
    gcum = _sum_small_first([_dot(mk["tri"][d], jnp.concatenate([tl, tg], axis=1))
                             for tl, tg in zip(lw_t, gf_t)])
    yield
    last = c - 1 if d == 0 else 0
    g, gh = gcum[:, :D_A], gcum[:, D_A:]
    gc, ghc, ghm = g[last:last + 1], gh[last:last + 1], gh[c // 2:c // 2 + 1]
    egc, eghc = jnp.exp(gc), jnp.exp(ghc)

    def rwkv_operands(r, k, kk_, n2_, a_, g_, lw_, ka_, egc_):
        kap = kk_ * lax.rsqrt(jnp.maximum(n2_, 1e-24))
        keff = k * (1.0 + (a_ - 1.0) * ka_)
        eng = jnp.exp(-g_)
        kh = keff * eng
        bh = (kap * a_) * eng
        out = (r * jnp.exp(g_), kap * jnp.exp(g_ - lw_), kh, bh, kh * egc_, bh * egc_)
        return tuple(z.astype(BF16) for z in out)

    rt, kt, kh, bh, kb, bb = _blockwise(rwkv_operands, [r_ref, k_ref, kk, n2, a, g, lw, k_a, egc], 6)

    def hgrn_operands(qr, kf_, gh_, ghm_, e_m, e_cm):
        q = _silu(qr)
        ghn = gh_ - ghm_
        qn = q * jnp.exp(ghn)
        kn = kf_ * jnp.exp(-ghn)
        return tuple(z.astype(BF16) for z in (qn, kn, qn * e_m, kn * e_cm))

    qn, kn, qt, kbh = _blockwise(hgrn_operands, [cols(pb_ref, 0, D_B), kf, gh, ghm, jnp.exp(ghm),
                                                 jnp.exp(ghc - ghm)], 4, width=D_B)
    v = v_ref[...].astype(BF16)
    vb = pb_ref[bi, :, D_B:2 * D_B].astype(BF16)
    return (rt, kt, kh, bh, kb, bb, v, qn, kn, qt, kbh, vb), (egc, eghc)


def _scan_dir(pa_ref, pb_ref, bi, d, prm, sa_scr, sb_scr, mk):
    ops, (egc, eghc) = yield from _chunk_operands(pa_ref, pb_ref, bi, d, prm, mk)
    rt, kt, kh, bh, kb, bb, v, qn, kn, qt, kbh, vb = ops
    chains = []
    for grp in range(D_A // PACK):
        sl = slice(PACK * grp, PACK * (grp + 1))
        chains.append(_rwkv_group(rt[:, sl], kt[:, sl], kh[:, sl], bh[:, sl], kb[:, sl], bb[:, sl], v[:, sl],
                                  sa_scr.at[bi, d, grp], egc[:, sl], mk, d))
    chains.append(_hgrn_dir(qn, kn, qt, kbh, vb, eghc, [sb_scr.at[bi, d, grp] for grp in range(D_B // PACK)],
                            mk, d))
    results = [None] * len(chains)
    live = list(enumerate(chains))
    while live:
        still = []
        for i, g in live:
            try:
                next(g)
                still.append((i, g))
            except StopIteration as stop:
                results[i] = stop.value
        live = still
        yield
    return jnp.concatenate(results[:-1], axis=1), results[-1]


def _state_slots():
    return [(bi, d, grp) for bi in range(SCAN_BATCH) for d in range(2) for grp in range(D_A // PACK)]


def _load_states(sa0_ref, sb0_ref, sa_scr, sb_scr):
    spread = jnp.where(_iota((HEAD_A, PACK), 0) == (_iota((HEAD_A, PACK), 1) & (HEAD_A - 1)), 1.0, 0.0)
    spread = spread.astype(BF16)
    bm64 = _head_mask_a()
    zero = jnp.zeros((HEAD_B, HEAD_B), F32)
    for bi, d, grp in _state_slots():
        sa_scr[bi, d, grp] = _mm_exact_rhs(sa0_ref[bi, d, grp], spread) * bm64
        h0 = sb0_ref[bi, d, 2 * grp].T
        h1 = sb0_ref[bi, d, 2 * grp + 1].T
        sb_scr[bi, d, grp] = jnp.concatenate([jnp.concatenate([h0, zero], axis=1),
                                              jnp.concatenate([zero, h1], axis=1)], axis=0)


def _store_states(sa_scr, sb_scr, sa_out, sb_out):
    gather = jnp.where((_iota((PACK, HEAD_A), 0) & (HEAD_A - 1)) == _iota((PACK, HEAD_A), 1), 1.0, 0.0)
    gather = gather.astype(BF16)
    for bi, d, grp in _state_slots():
        sa_out[bi, d, grp] = _mm_exact_rhs(sa_scr[bi, d, grp], gather)
        for hh in range(PACK // HEAD_B):
            blk = sb_scr[bi, d, grp, HEAD_B * hh:HEAD_B * (hh + 1), HEAD_B * hh:HEAD_B * (hh + 1)]
            sb_out[bi, d, 2 * grp + hh] = blk.T


def _scan_kernel(paf_ref, pbf_ref, pab_ref, pbb_ref, *rest, from_zero):
    if from_zero:
        (w0_ref, a0_ref, wa2_ref, kk_ref, ka_ref, lb_ref, yf_ref, yb_ref, of_ref, ob_ref, sa_out, sb_out,
         sa_scr, sb_scr) = rest
    else:
        (sa0_ref, sb0_ref, w0_ref, a0_ref, wa2_ref, kk_ref, ka_ref, lb_ref, yf_ref, yb_ref, of_ref, ob_ref,
         sa_scr, sb_scr) = rest
    ci = pl.program_id(1)

    @pl.when(ci == 0)
    def _():
        if from_zero:
            sa_scr[...] = jnp.zeros(sa_scr.shape, F32)
            sb_scr[...] = jnp.zeros(sb_scr.shape, F32)
        else:
            _load_states(sa0_ref, sb0_ref, sa_scr, sb_scr)

    mk = _scan_masks()
    prm = (w0_ref[...], a0_ref[...], wa2_ref, kk_ref[...], ka_ref[...], lb_ref[...])
    p_refs = ((paf_ref, pbf_ref), (pab_ref, pbb_ref))
    runs = [(bi, d) for bi in range(SCAN_BATCH) for d in range(2)]
    outs = _lockstep([_scan_dir(p_refs[d][0], p_refs[d][1], bi, d, prm, sa_scr, sb_scr, mk) for bi, d in runs])
    for (bi, d), (y, o) in zip(runs, outs):
        (yf_ref, yb_ref)[d][bi] = y
        (of_ref, ob_ref)[d][bi] = o

    if from_zero:
        @pl.when(ci == pl.num_programs(1) - 1)
        def _():
            _store_states(sa_scr, sb_scr, sa_out, sb_out)


def _scan_call(pa, pb, states, w0, a0, wa2, k_k, k_a, lb):
    bsz, seq, _ = pa.shape
    nc = seq // CHUNK
    nb = SCAN_BATCH
    from_zero = states is None
    fwd = lambda b, c: (b, c, 0)
    bwd = lambda b, c: (b, nc - 1 - c, 0)
    st = lambda b, c: (b, 0, 0, 0, 0)
    full = lambda shape: pl.BlockSpec(shape, lambda b, c: (0,) * len(shape))
    sa_block = (nb, 2, D_A // PACK, PACK, HEAD_A)
    sb_block = (nb, 2, D_B // HEAD_B, HEAD_B, HEAD_B)
    y_shape = jax.ShapeDtypeStruct((bsz, seq, D_A), F32)
    out_specs = [pl.BlockSpec((nb, CHUNK, D_A), fwd), pl.BlockSpec((nb, CHUNK, D_A), bwd),
                 pl.BlockSpec((nb, CHUNK, D_B), fwd), pl.BlockSpec((nb, CHUNK, D_B), bwd)]
    out_shape = [y_shape, y_shape, y_shape, y_shape]
    in_specs = [pl.BlockSpec((nb, CHUNK, A_COLS), fwd), pl.BlockSpec((nb, CHUNK, B_COLS), fwd),
                pl.BlockSpec((nb, CHUNK, A_COLS), bwd), pl.BlockSpec((nb, CHUNK, B_COLS), bwd)]
    args = [pa, pb, pa, pb]
    if from_zero:
        out_specs += [pl.BlockSpec(sa_block, st), pl.BlockSpec(sb_block, st)]
        out_shape += [jax.ShapeDtypeStruct((bsz,) + sa_block[1:], F32),
                      jax.ShapeDtypeStruct((bsz,) + sb_block[1:], F32)]
    else:
        in_specs += [pl.BlockSpec(sa_block, st), pl.BlockSpec(sb_block, st)]
        args += list(states)
    in_specs += [full((2, D_A)), full((2, D_A)), full((2, LORA, 2 * D_A)),
                 full((1, D_A)), full((1, D_A)), full((2, D_B))]
    args += [w0, a0, wa2, k_k, k_a, lb]
    bd_block = (nb, 2, D_A // PACK, PACK, PACK)
    return pl.pallas_call(
        functools.partial(_scan_kernel, from_zero=from_zero),
        grid=(bsz // nb, nc),
        in_specs=in_specs,
        out_specs=out_specs,
        out_shape=out_shape,
        scratch_shapes=[pltpu.VMEM(bd_block, F32), pltpu.VMEM(bd_block, F32)],
        compiler_params=pltpu.CompilerParams(dimension_semantics=("arbitrary", "arbitrary"),
                                             vmem_limit_bytes=VMEM_LIMIT),
        name="scan_state" if from_zero else "scan",
    )(*args)


def _out_kernel(x_ref, pa_ref, zb_ref, yf_ref, yb_ref, of_ref, ob_ref, gate_ref, a0_ref, a2p_ref, ka_ref,
                rk_ref, lnw_ref, lnb_ref, og_ref, wout_ref, fg_ref, out_ref):
    pa = pa_ref[0]
    r = pa[:, 0:D_A]
    k = pa[:, D_A:2 * D_A]
    v = pa[:, 2 * D_A:3 * D_A]
    za = pa[:, 3 * D_A:4 * D_A]
    bm64_bf = _head_mask_a().astype(BF16)

    def seg_sum(z):
        return jnp.concatenate([_mm_exact_rhs(z[:, :PACK], bm64_bf, SUM_TERMS),
                                _mm_exact_rhs(z[:, PACK:], bm64_bf, SUM_TERMS)], axis=1)

    y = yf_ref[0] + yb_ref[0]
    mu = seg_sum(y) * (1.0 / HEAD_A)
    dlt = y - mu
    var = seg_sum(dlt * dlt) * (1.0 / HEAD_A)
    yn = dlt * lax.rsqrt(var + GN_EPS) * lnw_ref[...] + lnb_ref[...]
    keffs = []
    for d in range(2):
        lo = pa[:, 4 * D_A + LORA * d:4 * D_A + LORA * (d + 1)]
        a = _sigmoid(a0_ref[d:d + 1] + _mm(lo, a2p_ref[d]))
        keffs.append(k * (1.0 + (a - 1.0) * ka_ref[...]))
    kmean = 0.5 * (keffs[0] + keffs[1])
    bonus = seg_sum(r * kmean * rk_ref[...]) * v
    out_a = (yn + bonus) * _silu(za)

    o = of_ref[0] + ob_ref[0]
    og = og_ref[...]
    zb = zb_ref[0]
    outs = [out_a]
    for h in range(D_B // HEAD_B):
        sl = slice(HEAD_B * h, HEAD_B * (h + 1))
        oh = o[:, sl]
        oh = oh * lax.rsqrt(jnp.mean(oh * oh, axis=-1, keepdims=True) + EPS) * og[:, sl]
        outs.append(oh * _silu(zb[:, sl]))
    mix = jnp.concatenate(outs, axis=1)
    proj = _dot(mix.astype(BF16), wout_ref[...])
    hs = x_ref[0] + gate_ref[0] * proj
    out_ref[0] = hs * lax.rsqrt(jnp.mean(hs * hs, axis=-1, keepdims=True) + EPS) * fg_ref[...]


def _out_call(x, pa, pb, yf, yb, of, ob, gate, a0, a2p, k_a, r_k, lnx_w, lnx_b, onorm_g, w_out_bf16, final_g):
    bsz, seq, _ = x.shape
    tm = min(OUT_ROWS, seq)
    tok = lambda b, i: (b, i, 0)
    full = lambda shape: pl.BlockSpec(shape, lambda b, i: (0,) * len(shape))
    return pl.pallas_call(
        _out_kernel,
        grid=(bsz, seq // tm),
        in_specs=[pl.BlockSpec((1, tm, D_MODEL), tok),
                  pl.BlockSpec((1, tm, A_COLS), tok),
                  pl.BlockSpec((1, tm, D_B), lambda b, i: (b, i, 4)),
                  pl.BlockSpec((1, tm, D_A), tok), pl.BlockSpec((1, tm, D_A), tok),
                  pl.BlockSpec((1, tm, D_B), tok), pl.BlockSpec((1, tm, D_B), tok),
                  pl.BlockSpec((1, 1, D_MODEL), lambda b, i: (b, 0, 0)),
                  full((2, D_A)), full((2, LORA, D_A)), full((1, D_A)), full((1, D_A)),
                  full((1, D_A)), full((1, D_A)), full((1, D_B)),
                  full((D_MODEL, D_MODEL)), full((1, D_MODEL))],
        out_specs=pl.BlockSpec((1, tm, D_MODEL), tok),
        out_shape=jax.ShapeDtypeStruct((bsz, seq, D_MODEL), F32),
        compiler_params=pltpu.CompilerParams(dimension_semantics=("arbitrary", "arbitrary"),
                                             vmem_limit_bytes=VMEM_LIMIT),
        name="out",
    )(x, pa, pb, yf, yb, of, ob, gate, a0, a2p, k_a, r_k, lnx_w, lnx_b, onorm_g, w_out_bf16, final_g)


def kernel(x_prompt, x_sample, state_rwkv, state_hgrn, c, c_ctx, norm_g, w_ada, b_ada, w_in, mu_h, mu_v, w0, w2,
           a0, a2, k_k, k_a, r_k, lnx_w, lnx_b, lb_logits, onorm_g, w_out, final_g):
    l = 0
    bp = x_prompt.shape[0]
    bs = x_sample.shape[0]
    lb_all = jnp.cumsum(jax.nn.softmax(lb_logits.astype(F32), axis=0), axis=0)
    lb = lb_all[l]

    wa_bf = w_in[l, :, :A_COLS].astype(BF16)
    wb_bf = w_in[l, :, A_COLS:].astype(BF16)
    w_out_bf = w_out[l].astype(BF16)
    zeros = jnp.zeros((2, LORA // 2, D_A), F32)
    wa2 = jnp.concatenate([jnp.concatenate([w2[l], zeros], axis=2),
                           jnp.concatenate([zeros, a2[l]], axis=2)], axis=1)
    a2p = jnp.concatenate([zeros, a2[l]], axis=1)
    row = lambda z: z.reshape(1, -1)

    cc = jnp.concatenate([c_ctx[None, :], c, jnp.zeros((16 - 1 - bs, D_MODEL), F32)], axis=0)
    m = _ada_call(cc, w_ada[l], row(b_ada[l]))
    shift, scale, gate = m[:, :D_MODEL], m[:, D_MODEL:2 * D_MODEL], m[:, 2 * D_MODEL:]
    ctx = lambda z: jnp.broadcast_to(z[0:1, None, :], (bp, 1, D_MODEL))
    lat = lambda z: z[1:1 + bs, None, :]

    def path(x, sh, sc, gt, states, grid_shift):
        pa, pb = _proj_call(x, sh, sc, row(norm_g[l]), wa_bf, wb_bf, row(mu_h[l]), row(mu_v[l]), grid_shift)
        res = _scan_call(pa, pb, states, w0[l], a0[l], wa2, row(k_k[l]), row(k_a[l]), lb)
        yf, yb, of, ob = res[:4]
        y = _out_call(x, pa, pb, yf, yb, of, ob, gt, a0[l], a2p, row(k_a[l]), row(r_k[l]), row(lnx_w[l]),
                      row(lnx_b[l]), row(onorm_g[l]), w_out_bf, row(final_g))
        return y, res[4:]

    y_prompt, (s_a, s_b) = path(x_prompt, ctx(shift), ctx(scale), ctx(gate), None, False)
    groups_a = D_A // PACK
    sa0 = state_rwkv[:, l].reshape(bs, 2, groups_a, PACK, HEAD_A)
    y_sample, _ = path(x_sample, lat(shift), lat(scale), lat(gate), (sa0, state_hgrn[:, l]), True)
    s_a = s_a.reshape(bp, 1, 2, D_A // HEAD_A, HEAD_A, HEAD_A)
    return y_prompt, y_sample, s_a, s_b[:, None]
```

```python
import functools

import jax
import jax.numpy as jnp
from jax import lax
from jax.experimental import pallas as pl
from jax.experimental.pallas import tpu as pltpu

F32 = jnp.float32
BF16 = jnp.bfloat16

D_MODEL = 1024
D_A = 512
D_B = 512
HEAD_A = 64
HEAD_B = 128
LOG2_HEAD_A = HEAD_A.bit_length() - 1
LOG2_HEAD_B = HEAD_B.bit_length() - 1
LORA = 128
A_COLS = 4 * D_A + 2 * LORA
B_COLS = 5 * D_B
GRID_W = 64
CHUNK = 64
LOG2_CHUNK = CHUNK.bit_length() - 1
SCAN_BATCH = 2
SUM_TERMS = 2
BLOCK_ROWS = 32
BLOCK_LANES = 128
EPS = 1e-6
GN_EPS = 64e-5
DECAY_SCALE = 0.6065306597126334
NEG_LOG2E = -1.4426950408889634
PACK = 256
TNA = 768
TNB = 512
NA = A_COLS // TNA
NB = B_COLS // TNB
PROJ_ROWS = 2048
OUT_ROWS = 512
VMEM_LIMIT = 56 * 1024 * 1024

NN = ((1,), (0,))
NT = ((1,), (1,))
TN_DIMS = ((0,), (0,))


def _dot(a, b, dims=NN):
    return lax.dot_general(a, b, (dims, ((), ())), preferred_element_type=F32)


def _split(x, passes=1):
    if isinstance(x, tuple):
        return x
    hi = x.astype(BF16)
    if passes == 1:
        return (hi,)
    return hi, (x - hi.astype(F32)).astype(BF16)


def _mm(a, b, dims=NN, passes=1):
    a = _split(a, passes)
    b = _split(b, passes)
    if len(a) == 1 or len(b) == 1:
        return _dot(a[0], b[0], dims)
    return _dot(a[0], b[0], dims) + (_dot(a[0], b[1], dims) + _dot(a[1], b[0], dims))


def _terms(x, n):
    out = []
    for _ in range(n - 1):
        h = x.astype(BF16)
        out.append(h)
        x = x - h.astype(F32)
    out.append(x.astype(BF16))
    return out


def _sum_small_first(parts):
    return functools.reduce(lambda acc, z: z + acc, reversed(parts))


def _mm_exact_rhs(a, b_bf16, n=3):
    return _sum_small_first([_dot(t, b_bf16) for t in _terms(a, n)])


def _exp_neg(x):
    return jnp.exp2(x * NEG_LOG2E)


def _sigmoid(x):
    return 1.0 / (1.0 + _exp_neg(x))


def _silu(x):
    return x * _sigmoid(x)


def _iota(shape, dim):
    return lax.broadcasted_iota(jnp.int32, shape, dim)


def _block_mask(rows, cols, row_shift, col_shift):
    same = (_iota((rows, cols), 0) >> row_shift) == (_iota((rows, cols), 1) >> col_shift)
    return jnp.where(same, 1.0, 0.0).astype(F32)


def _head_mask_a():
    return _block_mask(PACK, PACK, LOG2_HEAD_A, LOG2_HEAD_A)


def _tile_rows(x, n):
    if isinstance(x, tuple):
        return tuple(_tile_rows(t, n) for t in x)
    return jnp.concatenate([x] * n, axis=0)


def _masked(x, mask_bf16):
    if isinstance(x, tuple):
        return tuple(_masked(t, mask_bf16) for t in x)
    return x * mask_bf16


def _ada_kernel(c_ref, w_ref, b_ref, m_ref):
    m_ref[...] = _mm(_silu(c_ref[...]), w_ref[...], passes=3) + b_ref[...]


def _ada_call(cc, w_ada, b_ada):
    rows = cc.shape[0]
    return pl.pallas_call(
        _ada_kernel,
        grid=(3,),
        in_specs=[pl.BlockSpec((rows, D_MODEL), lambda j: (0, 0)),
                  pl.BlockSpec((D_MODEL, D_MODEL), lambda j: (0, j)),
                  pl.BlockSpec((1, D_MODEL), lambda j: (0, j))],
        out_specs=pl.BlockSpec((rows, D_MODEL), lambda j: (0, j)),
        out_shape=jax.ShapeDtypeStruct((rows, 3 * D_MODEL), F32),
        compiler_params=pltpu.CompilerParams(dimension_semantics=("arbitrary",),
                                             vmem_limit_bytes=VMEM_LIMIT),
        name="ada",
    )(cc, w_ada, b_ada)


def _proj_kernel(x_ref, sh_ref, sc_ref, g_ref, wa_ref, wb_ref, muh_ref, muv_ref, pa_ref, pb_ref, xm_ref, *,
                 nbp, seq, grid_shift):
    j = pl.program_id(1)
    rows = nbp * seq

    @pl.when(j == 0)
    def _():
        for b in range(nbp):
            x = x_ref[b]
            xn = x * lax.rsqrt(jnp.mean(x * x, axis=-1, keepdims=True) + EPS) * g_ref[...]
            xm_ref[b * seq:(b + 1) * seq, :] = (xn * (1.0 + sc_ref[b]) + sh_ref[b]).astype(BF16)

    @pl.when(j < NA)
    def _():
        p = _dot(xm_ref[...], wa_ref[...])
        w = GRID_W if grid_shift else seq
        nblk = rows // w
        pos = _iota((1, w, TNA), 1)
        to3 = lambda z: z.reshape(nblk, w, TNA)
        prev = jnp.where(pos == 0, 0.0, to3(pltpu.roll(p, 1, axis=0)))
        nxt = jnp.where(pos == w - 1, 0.0, to3(pltpu.roll(p, rows - 1, axis=0)))
        muh = muh_ref[...]
        out = (0.5 * muh) * (prev + nxt)
        if grid_shift:
            muv = muv_ref[...]
            p3 = to3(p)
            zblk = jnp.zeros((1, w, TNA), F32)
            up = jnp.concatenate([zblk, p3[:nblk - 1]], axis=0)
            dn = jnp.concatenate([p3[1:], zblk], axis=0)
            out = (1.0 - muh - muv) * p3 + out + (0.5 * muv) * (up + dn)
        else:
            out = (1.0 - muh) * to3(p) + out
        pa_ref[...] = out.reshape(nbp, seq, TNA)

    @pl.when(j >= NA)
    def _():
        pb_ref[...] = _dot(xm_ref[...], wb_ref[...]).reshape(nbp, seq, TNB)


def _proj_call(x, shift, scale, norm_g, wa_bf16, wb_bf16, mu_h, mu_v, grid_shift):
    bsz, seq, _ = x.shape
    nbp = 1 if grid_shift else PROJ_ROWS // seq
    kern = functools.partial(_proj_kernel, nbp=nbp, seq=seq, grid_shift=grid_shift)
    a_idx = lambda j: jnp.minimum(j, NA - 1)
    b_idx = lambda j: jnp.maximum(j - NA, 0)
    return pl.pallas_call(
        kern,
        grid=(bsz // nbp, NA + NB),
        in_specs=[pl.BlockSpec((nbp, seq, D_MODEL), lambda b, j: (b, 0, 0)),
                  pl.BlockSpec((nbp, 1, D_MODEL), lambda b, j: (b, 0, 0)),
                  pl.BlockSpec((nbp, 1, D_MODEL), lambda b, j: (b, 0, 0)),
                  pl.BlockSpec((1, D_MODEL), lambda b, j: (0, 0)),
                  pl.BlockSpec((D_MODEL, TNA), lambda b, j: (0, a_idx(j))),
                  pl.BlockSpec((D_MODEL, TNB), lambda b, j: (0, b_idx(j))),
                  pl.BlockSpec((1, TNA), lambda b, j: (0, a_idx(j))),
                  pl.BlockSpec((1, TNA), lambda b, j: (0, a_idx(j)))],
        out_specs=[pl.BlockSpec((nbp, seq, TNA), lambda b, j: (b, 0, a_idx(j))),
                   pl.BlockSpec((nbp, seq, TNB), lambda b, j: (b, 0, b_idx(j)))],
        out_shape=[jax.ShapeDtypeStruct((bsz, seq, A_COLS), F32),
                   jax.ShapeDtypeStruct((bsz, seq, B_COLS), F32)],
        scratch_shapes=[pltpu.VMEM((nbp * seq, D_MODEL), BF16)],
        compiler_params=pltpu.CompilerParams(dimension_semantics=("arbitrary", "arbitrary"),
                                             vmem_limit_bytes=VMEM_LIMIT),
        name="proj_grid" if grid_shift else "proj_seq",
    )(x, shift, scale, norm_g, wa_bf16, wb_bf16, mu_h, mu_v)


def _scan_masks():
    c = CHUNK
    t = _iota((c, PACK), 0)
    s = _iota((c, PACK), 1) & (c - 1)
    tt = _iota((c, c), 0)
    ss = _iota((c, c), 1)
    f01 = lambda cond: jnp.where(cond, 1.0, 0.0).astype(F32)
    bm64 = _head_mask_a()
    bm128 = _block_mask(PACK, PACK, LOG2_HEAD_B, LOG2_HEAD_B)
    return dict(
        bm64=bm64, bm64_bf=bm64.astype(BF16), bm128=bm128,
        bm_hg_bf=_block_mask(4 * c, D_B, LOG2_CHUNK, LOG2_HEAD_B).astype(BF16),
        eye=f01(t == s),
        incl=(f01(t >= s), f01(t <= s)),
        strict=(f01(t > s), f01(t < s)),
        tri=(f01(tt >= ss).astype(BF16), f01(tt <= ss).astype(BF16)),
    )


def _lockstep(gens):
    results = [None] * len(gens)
    live = list(enumerate(gens))
    while live:
        still = []
        for i, g in live:
            try:
                next(g)
                still.append((i, g))
            except StopIteration as stop:
                results[i] = stop.value
        live = still
    return results


def _rwkv_group(rt, kt, kh, bh, kb, bb, v, s_ref, egc, mk, d):
    c = CHUNK
    bd = lambda x: _masked(_tile_rows(_split(x), PACK // c), mk["bm64_bf"])
    kr = _split(jnp.concatenate([kt, rt], axis=0))
    aa_k = _mm(kr, bd(kh), NT)
    aa_b = _mm(kr, bd(bh), NT)
    s_bd = s_ref[...]
    krs = _mm(kr, s_bd, NT)
    yield
    a_kk = aa_k[:c] * mk["strict"][d]
    a_rk = aa_k[c:] * mk["incl"][d]
    n = -(aa_b[:c] * mk["strict"][d])
    a_rb = aa_b[c:] * mk["incl"][d]
    x = mk["eye"] + n
    p = _mm(n, bd(n))
    akv = _mm(jnp.concatenate([a_kk, a_rk], axis=0), bd(v))
    yield
    for i in range(5):
        if i < 4:
            xp = _mm(jnp.concatenate([x, p], axis=0), bd(p))
            x = x + xp[:c]
            p = xp[c:]
        else:
            x = x + _mm(x, bd(p))
        yield
    u = _mm(x, bd(krs[:c] + akv[:c]))
    yield
    y = krs[c:] + akv[c:] - _mm(a_rb, bd(u))
    vu = jnp.concatenate([v, (-u).astype(BF16)], axis=0)
    kbb = jnp.concatenate([kb, bb], axis=0)
    s_ref[...] = (s_bd * egc + _mm(vu, kbb, TN_DIMS)) * mk["bm64"]
    return y


def _hgrn_dir(qn, kn, qt, kbh, vb, eghc, s_refs, mk, d):
    bm_hg = mk["bm_hg_bf"]
    heads = D_B // HEAD_B
    a_p = _mm(qn, _masked(_tile_rows(_split(kn), heads), bm_hg), NT) * mk["incl"][d]
    inter = []
    for grp, s_ref in enumerate(s_refs):
        sl = slice(PACK * grp, PACK * (grp + 1))
        st = s_ref[...]
        inter.append(_mm(qt[:, sl], st, NT))
        s_ref[...] = (st * eghc[:, sl] + _mm(vb[:, sl], kbh[:, sl], TN_DIMS)) * mk["bm128"]
    yield
    o = _mm(a_p, _masked(_tile_rows(_split(vb), heads), bm_hg))
    return o + jnp.concatenate(inter, axis=1)


def _blockwise(fn, operands, n_out, rows=CHUNK, width=D_A):
    cols = []
    for l0 in range(0, width, BLOCK_LANES):
        parts = []
        for r0 in range(0, rows, BLOCK_ROWS):
            blk = [z[(slice(None) if z.shape[0] == 1 else slice(r0, r0 + BLOCK_ROWS)), l0:l0 + BLOCK_LANES]
                   for z in operands]
            parts.append(fn(*blk))
        cols.append([jnp.concatenate([p[i] for p in parts], axis=0) for i in range(n_out)])
    return [jnp.concatenate([col[i] for col in cols], axis=1) for i in range(n_out)]


def _chunk_operands(pa_ref, pb_ref, bi, d, prm, mk):
    c = CHUNK
    n_t = SUM_TERMS
    w0, a0, wa2, k_k, k_a, lb = prm
    cols = lambda ref, i, width=D_A: ref.at[bi, :, i * width:(i + 1) * width]
    r_ref, k_ref, v_ref = cols(pa_ref, 0), cols(pa_ref, 1), cols(pa_ref, 2)
    lo = pa_ref[bi, :, 4 * D_A + LORA * d:4 * D_A + LORA * (d + 1)]
    lo = jnp.where(_iota((c, LORA), 1) < LORA // 2, jnp.tanh(lo), lo)
    wa = _mm(lo, wa2[d])
    yield

    def gates(wa_w, wa_a, k, fr, w0_, a0_, kk_, lb_):
        lw = -DECAY_SCALE * _sigmoid(w0_ + wa_w)
        a = _sigmoid(a0_ + wa_a)
        kk = k * kk_
        f = lb_ + (1.0 - lb_) * _sigmoid(fr)
        gf = jnp.log(f)
        return (lw, a, kk, 1.0 - f, (kk * kk).astype(BF16), *_terms(lw, n_t), *_terms(gf, n_t))

    res = _blockwise(gates, [wa[:, :D_A], wa[:, D_A:], k_ref, cols(pb_ref, 2 + d, D_B), w0[d:d + 1], a0[d:d + 1],
                             k_k, lb[d:d + 1]], 5 + 2 * n_t)
    lw, a, kk, kf, sq = res[:5]
    lw_t, gf_t = res[5:5 + n_t], res[5 + n_t:]
    seg = lambda lanes: _dot(sq[:, lanes], mk["bm64_bf"])
    n2 = jnp.concatenate([seg(slice(0, PACK)), seg(slice(PACK, D_A))], axis=1)
    gcum = _sum_small_first([_dot(mk["tri"][d], jnp.concatenate([tl, tg], axis=1))
                             for tl, tg in zip(lw_t, gf_t)])
    yield
    last = c - 1 if d == 0 else 0
    g, gh = gcum[:, :D_A], gcum[:, D_A:]
    gc, ghc, ghm = g[last:last + 1], gh[last:last + 1], gh[c // 2:c // 2 + 1]
    egc, eghc = jnp.exp(gc), jnp.exp(ghc)

    def rwkv_operands(r, k, kk_, n2_, a_, g_, lw_, ka_, egc_):
        kap = kk_ * lax.rsqrt(jnp.maximum(n2_, 1e-24))
        keff = k * (1.0 + (a_ - 1.0) * ka_)
        eng = _exp_neg(g_)
        kh = keff * eng
        bh = (kap * a_) * eng
        out = (r * jnp.exp(g_), kap * jnp.exp(g_ - lw_), kh, bh, kh * egc_, bh * egc_)
        return tuple(z.astype(BF16) for z in out)

    rt, kt, kh, bh, kb, bb = _blockwise(rwkv_operands, [r_ref, k_ref, kk, n2, a, g, lw, k_a, egc], 6)

    def hgrn_operands(qr, kf_, gh_, ghm_, e_m, e_cm):
        q = _silu(qr)
        ghn = gh_ - ghm_
        qn = q * jnp.exp(ghn)
        kn = kf_ * _exp_neg(ghn)
        return tuple(z.astype(BF16) for z in (qn, kn, qn * e_m, kn * e_cm))

    qn, kn, qt, kbh = _blockwise(hgrn_operands, [cols(pb_ref, 0, D_B), kf, gh, ghm, jnp.exp(ghm),
                                                 jnp.exp(ghc - ghm)], 4, width=D_B)
    v = v_ref[...].astype(BF16)
    vb = pb_ref[bi, :, D_B:2 * D_B].astype(BF16)
    return (rt, kt, kh, bh, kb, bb, v, qn, kn, qt, kbh, vb), (egc, eghc)


def _scan_dir(pa_ref, pb_ref, bi, d, prm, sa_scr, sb_scr, mk):
    ops, (egc, eghc) = yield from _chunk_operands(pa_ref, pb_ref, bi, d, prm, mk)
    rt, kt, kh, bh, kb, bb, v, qn, kn, qt, kbh, vb = ops
    chains = []
    for grp in range(D_A // PACK):
        sl = slice(PACK * grp, PACK * (grp + 1))
        chains.append(_rwkv_group(rt[:, sl], kt[:, sl], kh[:, sl], bh[:, sl], kb[:, sl], bb[:, sl], v[:, sl],
                                  sa_scr.at[bi, d, grp], egc[:, sl], mk, d))
    chains.append(_hgrn_dir(qn, kn, qt, kbh, vb, eghc, [sb_scr.at[bi, d, grp] for grp in range(D_B // PACK)],
                            mk, d))
    results = [None] * len(chains)
    live = list(enumerate(chains))
    while live:
        still = []
        for i, g in live:
            try:
                next(g)
                still.append((i, g))
            except StopIteration as stop:
                results[i] = stop.value
        live = still
        yield
    return jnp.concatenate(results[:-1], axis=1), results[-1]


def _state_slots():
    return [(bi, d, grp) for bi in range(SCAN_BATCH) for d in range(2) for grp in range(D_A // PACK)]


def _load_states(sa0_ref, sb0_ref, sa_scr, sb_scr):
    spread = jnp.where(_iota((HEAD_A, PACK), 0) == (_iota((HEAD_A, PACK), 1) & (HEAD_A - 1)), 1.0, 0.0)
    spread = spread.astype(BF16)
    bm64 = _head_mask_a()
    zero = jnp.zeros((HEAD_B, HEAD_B), F32)
    for bi, d, grp in _state_slots():
        sa_scr[bi, d, grp] = _mm_exact_rhs(sa0_ref[bi, d, grp], spread) * bm64
        h0 = sb0_ref[bi, d, 2 * grp].T
        h1 = sb0_ref[bi, d, 2 * grp + 1].T
        sb_scr[bi, d, grp] = jnp.concatenate([jnp.concatenate([h0, zero], axis=1),
                                              jnp.concatenate([zero, h1], axis=1)], axis=0)


def _store_states(sa_scr, sb_scr, sa_out, sb_out):
    gather = jnp.where((_iota((PACK, HEAD_A), 0) & (HEAD_A - 1)) == _iota((PACK, HEAD_A), 1), 1.0, 0.0)
    gather = gather.astype(BF16)
    for bi, d, grp in _state_slots():
        sa_out[bi, d, grp] = _mm_exact_rhs(sa_scr[bi, d, grp], gather)
        for hh in range(PACK // HEAD_B):
            blk = sb_scr[bi, d, grp, HEAD_B * hh:HEAD_B * (hh + 1), HEAD_B * hh:HEAD_B * (hh + 1)]
            sb_out[bi, d, 2 * grp + hh] = blk.T


def _scan_kernel(paf_ref, pbf_ref, pab_ref, pbb_ref, *rest, from_zero):
    if from_zero:
        (w0_ref, a0_ref, wa2_ref, kk_ref, ka_ref, lb_ref, yf_ref, yb_ref, of_ref, ob_ref, sa_out, sb_out,
         sa_scr, sb_scr) = rest
    else:
        (sa0_ref, sb0_ref, w0_ref, a0_ref, wa2_ref, kk_ref, ka_ref, lb_ref, yf_ref, yb_ref, of_ref, ob_ref,
         sa_scr, sb_scr) = rest
    ci = pl.program_id(1)

    @pl.when(ci == 0)
    def _():
        if from_zero:
            sa_scr[...] = jnp.zeros(sa_scr.shape, F32)
            sb_scr[...] = jnp.zeros(sb_scr.shape, F32)
        else:
            _load_states(sa0_ref, sb0_ref, sa_scr, sb_scr)

    mk = _scan_masks()
    prm = (w0_ref[...], a0_ref[...], wa2_ref, kk_ref[...], ka_ref[...], lb_ref[...])
    p_refs = ((paf_ref, pbf_ref), (pab_ref, pbb_ref))
    runs = [(bi, d) for bi in range(SCAN_BATCH) for d in range(2)]
    outs = _lockstep([_scan_dir(p_refs[d][0], p_refs[d][1], bi, d, prm, sa_scr, sb_scr, mk) for bi, d in runs])
    for (bi, d), (y, o) in zip(runs, outs):
        (yf_ref, yb_ref)[d][bi] = y
        (of_ref, ob_ref)[d][bi] = o

    if from_zero:
        @pl.when(ci == pl.num_programs(1) - 1)
        def _():
            _store_states(sa_scr, sb_scr, sa_out, sb_out)


def _scan_call(pa, pb, states, w0, a0, wa2, k_k, k_a, lb):
    bsz, seq, _ = pa.shape
    nc = seq // CHUNK
    nb = SCAN_BATCH
    from_zero = states is None
    fwd = lambda b, c: (b, c, 0)
    bwd = lambda b, c: (b, nc - 1 - c, 0)
    st = lambda b, c: (b, 0, 0, 0, 0)
    full = lambda shape: pl.BlockSpec(shape, lambda b, c: (0,) * len(shape))
    sa_block = (nb, 2, D_A // PACK, PACK, HEAD_A)
    sb_block = (nb, 2, D_B // HEAD_B, HEAD_B, HEAD_B)
    y_shape = jax.ShapeDtypeStruct((bsz, seq, D_A), F32)
    out_specs = [pl.BlockSpec((nb, CHUNK, D_A), fwd), pl.BlockSpec((nb, CHUNK, D_A), bwd),
                 pl.BlockSpec((nb, CHUNK, D_B), fwd), pl.BlockSpec((nb, CHUNK, D_B), bwd)]
    out_shape = [y_shape, y_shape, y_shape, y_shape]
    in_specs = [pl.BlockSpec((nb, CHUNK, A_COLS), fwd), pl.BlockSpec((nb, CHUNK, B_COLS), fwd),
                pl.BlockSpec((nb, CHUNK, A_COLS), bwd), pl.BlockSpec((nb, CHUNK, B_COLS), bwd)]
    args = [pa, pb, pa, pb]
    if from_zero:
        out_specs += [pl.BlockSpec(sa_block, st), pl.BlockSpec(sb_block, st)]
        out_shape += [jax.ShapeDtypeStruct((bsz,) + sa_block[1:], F32),
                      jax.ShapeDtypeStruct((bsz,) + sb_block[1:], F32)]
    else:
        in_specs += [pl.BlockSpec(sa_block, st), pl.BlockSpec(sb_block, st)]
        args += list(states)
    in_specs += [full((2, D_A)), full((2, D_A)), full((2, LORA, 2 * D_A)),
                 full((1, D_A)), full((1, D_A)), full((2, D_B))]
    args += [w0, a0, wa2, k_k, k_a, lb]
    bd_block = (nb, 2, D_A // PACK, PACK, PACK)
    return pl.pallas_call(
        functools.partial(_scan_kernel, from_zero=from_zero),
        grid=(bsz // nb, nc),
        in_specs=in_specs,
        out_specs=out_specs,
        out_shape=out_shape,
        scratch_shapes=[pltpu.VMEM(bd_block, F32), pltpu.VMEM(bd_block, F32)],
        compiler_params=pltpu.CompilerParams(dimension_semantics=("arbitrary", "arbitrary"),
                                             vmem_limit_bytes=VMEM_LIMIT),
        name="scan_state" if from_zero else "scan",
    )(*args)


def _out_kernel(x_ref, pa_ref, zb_ref, yf_ref, yb_ref, of_ref, ob_ref, gate_ref, a0_ref, a2p_ref, ka_ref,
                rk_ref, lnw_ref, lnb_ref, og_ref, wout_ref, fg_ref, out_ref):
    pa = pa_ref[0]
    r = pa[:, 0:D_A]
    k = pa[:, D_A:2 * D_A]
    v = pa[:, 2 * D_A:3 * D_A]
    za = pa[:, 3 * D_A:4 * D_A]
    bm64_bf = _head_mask_a().astype(BF16)

    def seg_sum(z):
        return jnp.concatenate([_mm_exact_rhs(z[:, :PACK], bm64_bf, SUM_TERMS),
                                _mm_exact_rhs(z[:, PACK:], bm64_bf, SUM_TERMS)], axis=1)

    y = yf_ref[0] + yb_ref[0]
    mu = seg_sum(y) * (1.0 / HEAD_A)
    dlt = y - mu
    var = seg_sum(dlt * dlt) * (1.0 / HEAD_A)
    yn = dlt * lax.rsqrt(var + GN_EPS) * lnw_ref[...] + lnb_ref[...]
    keffs = []
    for d in range(2):
        lo = pa[:, 4 * D_A + LORA * d:4 * D_A + LORA * (d + 1)]
        a = _sigmoid(a0_ref[d:d + 1] + _mm(lo, a2p_ref[d]))
        keffs.append(k * (1.0 + (a - 1.0) * ka_ref[...]))
    kmean = 0.5 * (keffs[0] + keffs[1])
    bonus = seg_sum(r * kmean * rk_ref[...]) * v
    out_a = (yn + bonus) * _silu(za)

    o = of_ref[0] + ob_ref[0]
    og = og_ref[...]
    zb = zb_ref[0]
    outs = [out_a]
    for h in range(D_B // HEAD_B):
        sl = slice(HEAD_B * h, HEAD_B * (h + 1))
        oh = o[:, sl]
        oh = oh * lax.rsqrt(jnp.mean(oh * oh, axis=-1, keepdims=True) + EPS) * og[:, sl]
        outs.append(oh * _silu(zb[:, sl]))
    mix = jnp.concatenate(outs, axis=1)
    proj = _dot(mix.astype(BF16), wout_ref[...])
    hs = x_ref[0] + gate_ref[0] * proj
    out_ref[0] = hs * lax.rsqrt(jnp.mean(hs * hs, axis=-1, keepdims=True) + EPS) * fg_ref[...]


def _out_call(x, pa, pb, yf, yb, of, ob, gate, a0, a2p, k_a, r_k, lnx_w, lnx_b, onorm_g, w_out_bf16, final_g):
    bsz, seq, _ = x.shape
    tm = min(OUT_ROWS, seq)
    tok = lambda b, i: (b, i, 0)
    full = lambda shape: pl.BlockSpec(shape, lambda b, i: (0,) * len(shape))
    return pl.pallas_call(
        _out_kernel,
        grid=(bsz, seq // tm),
        in_specs=[pl.BlockSpec((1, tm, D_MODEL), tok),
                  pl.BlockSpec((1, tm, A_COLS), tok),
                  pl.BlockSpec((1, tm, D_B), lambda b, i: (b, i, 4)),
                  pl.BlockSpec((1, tm, D_A), tok), pl.BlockSpec((1, tm, D_A), tok),
                  pl.BlockSpec((1, tm, D_B), tok), pl.BlockSpec((1, tm, D_B), tok),
                  pl.BlockSpec((1, 1, D_MODEL), lambda b, i: (b, 0, 0)),
                  full((2, D_A)), full((2, LORA, D_A)), full((1, D_A)), full((1, D_A)),
                  full((1, D_A)), full((1, D_A)), full((1, D_B)),
                  full((D_MODEL, D_MODEL)), full((1, D_MODEL))],
        out_specs=pl.BlockSpec((1, tm, D_MODEL), tok),
        out_shape=jax.ShapeDtypeStruct((bsz, seq, D_MODEL), F32),
        compiler_params=pltpu.CompilerParams(dimension_semantics=("arbitrary", "arbitrary"),
                                             vmem_limit_bytes=VMEM_LIMIT),
        name="out",
    )(x, pa, pb, yf, yb, of, ob, gate, a0, a2p, k_a, r_k, lnx_w, lnx_b, onorm_g, w_out_bf16, final_g)


def kernel(x_prompt, x_sample, state_rwkv, state_hgrn, c, c_ctx, norm_g, w_ada, b_ada, w_in, mu_h, mu_v, w0, w2,
           a0, a2, k_k, k_a, r_k, lnx_w, lnx_b, lb_logits, onorm_g, w_out, final_g):
    l = 0
    bp = x_prompt.shape[0]
    bs = x_sample.shape[0]
    lb_all = jnp.cumsum(jax.nn.softmax(lb_logits.astype(F32), axis=0), axis=0)
    lb = lb_all[l]

    wa_bf = w_in[l, :, :A_COLS].astype(BF16)
    wb_bf = w_in[l, :, A_COLS:].astype(BF16)
    w_out_bf = w_out[l].astype(BF16)
    zeros = jnp.zeros((2, LORA // 2, D_A), F32)
    wa2 = jnp.concatenate([jnp.concatenate([w2[l], zeros], axis=2),
                           jnp.concatenate([zeros, a2[l]], axis=2)], axis=1)
    a2p = jnp.concatenate([zeros, a2[l]], axis=1)
    row = lambda z: z.reshape(1, -1)

    cc = jnp.concatenate([c_ctx[None, :], c, jnp.zeros((16 - 1 - bs, D_MODEL), F32)], axis=0)
    m = _ada_call(cc, w_ada[l], row(b_ada[l]))
    shift, scale, gate = m[:, :D_MODEL], m[:, D_MODEL:2 * D_MODEL], m[:, 2 * D_MODEL:]
    ctx = lambda z: jnp.broadcast_to(z[0:1, None, :], (bp, 1, D_MODEL))
    lat = lambda z: z[1:1 + bs, None, :]

    def path(x, sh, sc, gt, states, grid_shift):
        pa, pb = _proj_call(x, sh, sc, row(norm_g[l]), wa_bf, wb_bf, row(mu_h[l]), row(mu_v[l]), grid_shift)
        res = _scan_call(pa, pb, states, w0[l], a0[l], wa2, row(k_k[l]), row(k_a[l]), lb)
        yf, yb, of, ob = res[:4]
        y = _out_call(x, pa, pb, yf, yb, of, ob, gt, a0[l], a2p, row(k_a[l]), row(r_k[l]), row(lnx_w[l]),
                      row(lnx_b[l]), row(onorm_g[l]), w_out_bf, row(final_g))
        return y, res[4:]

    y_prompt, (s_a, s_b) = path(x_prompt, ctx(shift), ctx(scale), ctx(gate), None, False)
    groups_a = D_A // PACK
    sa0 = state_rwkv[:, l].reshape(bs, 2, groups_a, PACK, HEAD_A)
    y_sample, _ = path(x_sample, lat(shift), lat(scale), lat(gate), (sa0, state_hgrn[:, l]), True)
    s_a = s_a.reshape(bp, 1, 2, D_A // HEAD_A, HEAD_A, HEAD_A)
    return y_prompt, y_sample, s_a, s_b[:, None]
```

```python
import functools

import jax
import jax.numpy as jnp
from jax import lax
from jax.experimental import pallas as pl
from jax.experimental.pallas import tpu as pltpu

F32 = jnp.float32
BF16 = jnp.bfloat16

D_MODEL = 1024
D_A = 512
D_B = 512
HEAD_A = 64
HEAD_B = 128
LOG2_HEAD_A = HEAD_A.bit_length() - 1
LOG2_HEAD_B = HEAD_B.bit_length() - 1
LORA = 128
A_COLS = 4 * D_A + 2 * LORA
B_COLS = 5 * D_B
GRID_W = 64
CHUNK = 64
LOG2_CHUNK = CHUNK.bit_length() - 1
SCAN_BATCH = 2
SUM_TERMS = 2
BLOCK_ROWS = 32
BLOCK_LANES = 128
EPS = 1e-6
GN_EPS = 64e-5
DECAY_SCALE = 0.6065306597126334
NEG_LOG2E = -1.4426950408889634
PACK = 256
TNA = 768
TNB = 512
NA = A_COLS // TNA
NB = B_COLS // TNB
PROJ_ROWS = 2048
OUT_ROWS = 512
VMEM_LIMIT = 56 * 1024 * 1024

NN = ((1,), (0,))
NT = ((1,), (1,))
TN_DIMS = ((0,), (0,))


def _dot(a, b, dims=NN):
    return lax.dot_general(a, b, (dims, ((), ())), preferred_element_type=F32)


def _split(x, passes=1):
    if isinstance(x, tuple):
        return x
    hi = x.astype(BF16)
    if passes == 1:
        return (hi,)
    return hi, (x - hi.astype(F32)).astype(BF16)


def _mm(a, b, dims=NN, passes=1):
    a = _split(a, passes)
    b = _split(b, passes)
    if len(a) == 1 or len(b) == 1:
        return _dot(a[0], b[0], dims)
    return _dot(a[0], b[0], dims) + (_dot(a[0], b[1], dims) + _dot(a[1], b[0], dims))


def _terms(x, n):
    out = []
    for _ in range(n - 1):
        h = x.astype(BF16)
        out.append(h)
        x = x - h.astype(F32)
    out.append(x.astype(BF16))
    return out


def _sum_small_first(parts):
    return functools.reduce(lambda acc, z: z + acc, reversed(parts))


def _mm_exact_rhs(a, b_bf16, n=3):
    return _sum_small_first([_dot(t, b_bf16) for t in _terms(a, n)])


def _exp_neg(x):
    return jnp.exp2(x * NEG_LOG2E)


def _sigmoid(x):
    return 0.5 * jnp.tanh(0.5 * x) + 0.5


def _silu(x):
    return x * _sigmoid(x)


def _iota(shape, dim):
    return lax.broadcasted_iota(jnp.int32, shape, dim)


def _block_mask(rows, cols, row_shift, col_shift):
    same = (_iota((rows, cols), 0) >> row_shift) == (_iota((rows, cols), 1) >> col_shift)
    return jnp.where(same, 1.0, 0.0).astype(F32)


def _head_mask_a():
    return _block_mask(PACK, PACK, LOG2_HEAD_A, LOG2_HEAD_A)


def _tile_rows(x, n):
    if isinstance(x, tuple):
        return tuple(_tile_rows(t, n) for t in x)
    return jnp.concatenate([x] * n, axis=0)


def _masked(x, mask_bf16):
    if isinstance(x, tuple):
        return tuple(_masked(t, mask_bf16) for t in x)
    return x * mask_bf16


def _ada_kernel(c_ref, w_ref, b_ref, m_ref):
    m_ref[...] = _mm(_silu(c_ref[...]), w_ref[...], passes=3) + b_ref[...]


def _ada_call(cc, w_ada, b_ada):
    rows = cc.shape[0]
    return pl.pallas_call(
        _ada_kernel,
        grid=(3,),
        in_specs=[pl.BlockSpec((rows, D_MODEL), lambda j: (0, 0)),
                  pl.BlockSpec((D_MODEL, D_MODEL), lambda j: (0, j)),
                  pl.BlockSpec((1, D_MODEL), lambda j: (0, j))],
        out_specs=pl.BlockSpec((rows, D_MODEL), lambda j: (0, j)),
        out_shape=jax.ShapeDtypeStruct((rows, 3 * D_MODEL), F32),
        compiler_params=pltpu.CompilerParams(dimension_semantics=("arbitrary",),
                                             vmem_limit_bytes=VMEM_LIMIT),
        name="ada",
    )(cc, w_ada, b_ada)


def _proj_kernel(x_ref, sh_ref, sc_ref, g_ref, wa_ref, wb_ref, muh_ref, muv_ref, pa_ref, pb_ref, xm_ref, *,
                 nbp, seq, grid_shift):
    j = pl.program_id(1)
    rows = nbp * seq

    @pl.when(j == 0)
    def _():
        for b in range(nbp):
            x = x_ref[b]
            xn = x * lax.rsqrt(jnp.mean(x * x, axis=-1, keepdims=True) + EPS) * g_ref[...]
            xm_ref[b * seq:(b + 1) * seq, :] = (xn * (1.0 + sc_ref[b]) + sh_ref[b]).astype(BF16)

    @pl.when(j < NA)
    def _():
        p = _dot(xm_ref[...], wa_ref[...])
        w = GRID_W if grid_shift else seq
        nblk = rows // w
        pos = _iota((1, w, TNA), 1)
        to3 = lambda z: z.reshape(nblk, w, TNA)
        prev = jnp.where(pos == 0, 0.0, to3(pltpu.roll(p, 1, axis=0)))
        nxt = jnp.where(pos == w - 1, 0.0, to3(pltpu.roll(p, rows - 1, axis=0)))
        muh = muh_ref[...]
        out = (0.5 * muh) * (prev + nxt)
        if grid_shift:
            muv = muv_ref[...]
            p3 = to3(p)
            zblk = jnp.zeros((1, w, TNA), F32)
            up = jnp.concatenate([zblk, p3[:nblk - 1]], axis=0)
            dn = jnp.concatenate([p3[1:], zblk], axis=0)
            out = (1.0 - muh - muv) * p3 + out + (0.5 * muv) * (up + dn)
        else:
            out = (1.0 - muh) * to3(p) + out
        pa_ref[...] = out.reshape(nbp, seq, TNA)

    @pl.when(j >= NA)
    def _():
        pb_ref[...] = _dot(xm_ref[...], wb_ref[...]).reshape(nbp, seq, TNB)


def _proj_call(x, shift, scale, norm_g, wa_bf16, wb_bf16, mu_h, mu_v, grid_shift):
    bsz, seq, _ = x.shape
    nbp = 1 if grid_shift else PROJ_ROWS // seq
    kern = functools.partial(_proj_kernel, nbp=nbp, seq=seq, grid_shift=grid_shift)
    a_idx = lambda j: jnp.minimum(j, NA - 1)
    b_idx = lambda j: jnp.maximum(j - NA, 0)
    return pl.pallas_call(
        kern,
        grid=(bsz // nbp, NA + NB),
        in_specs=[pl.BlockSpec((nbp, seq, D_MODEL), lambda b, j: (b, 0, 0)),
                  pl.BlockSpec((nbp, 1, D_MODEL), lambda b, j: (b, 0, 0)),
                  pl.BlockSpec((nbp, 1, D_MODEL), lambda b, j: (b, 0, 0)),
                  pl.BlockSpec((1, D_MODEL), lambda b, j: (0, 0)),
                  pl.BlockSpec((D_MODEL, TNA), lambda b, j: (0, a_idx(j))),
                  pl.BlockSpec((D_MODEL, TNB), lambda b, j: (0, b_idx(j))),
                  pl.BlockSpec((1, TNA), lambda b, j: (0, a_idx(j))),
                  pl.BlockSpec((1, TNA), lambda b, j: (0, a_idx(j)))],
        out_specs=[pl.BlockSpec((nbp, seq, TNA), lambda b, j: (b, 0, a_idx(j))),
                   pl.BlockSpec((nbp, seq, TNB), lambda b, j: (b, 0, b_idx(j)))],
        out_shape=[jax.ShapeDtypeStruct((bsz, seq, A_COLS), F32),
                   jax.ShapeDtypeStruct((bsz, seq, B_COLS), F32)],
        scratch_shapes=[pltpu.VMEM((nbp * seq, D_MODEL), BF16)],
        compiler_params=pltpu.CompilerParams(dimension_semantics=("arbitrary", "arbitrary"),
                                             vmem_limit_bytes=VMEM_LIMIT),
        name="proj_grid" if grid_shift else "proj_seq",
    )(x, shift, scale, norm_g, wa_bf16, wb_bf16, mu_h, mu_v)


def _scan_masks():
    c = CHUNK
    t = _iota((c, PACK), 0)
    s = _iota((c, PACK), 1) & (c - 1)
    tt = _iota((c, c), 0)
    ss = _iota((c, c), 1)
    f01 = lambda cond: jnp.where(cond, 1.0, 0.0).astype(F32)
    bm64 = _head_mask_a()
    bm128 = _block_mask(PACK, PACK, LOG2_HEAD_B, LOG2_HEAD_B)
    return dict(
        bm64=bm64, bm64_bf=bm64.astype(BF16), bm128=bm128,
        bm_hg_bf=_block_mask(4 * c, D_B, LOG2_CHUNK, LOG2_HEAD_B).astype(BF16),
        eye=f01(t == s),
        incl=(f01(t >= s), f01(t <= s)),
        strict=(f01(t > s), f01(t < s)),
        tri=(f01(tt >= ss).astype(BF16), f01(tt <= ss).astype(BF16)),
    )


_MASK_KEYS = ("bm64", "bm64_bf", "bm128", "bm_hg_bf", "eye", "incl", "strict", "tri")
_MASK_SCRATCH = (((PACK, PACK), F32), ((PACK, PACK), BF16), ((PACK, PACK), F32), ((4 * CHUNK, D_B), BF16),
                 ((CHUNK, PACK), F32), ((2, CHUNK, PACK), F32), ((2, CHUNK, PACK), F32), ((2, CHUNK, CHUNK), BF16))


def _store_masks(mask_refs):
    masks = _scan_masks()
    for ref, key in zip(mask_refs, _MASK_KEYS):
        val = masks[key]
        if isinstance(val, tuple):
            for i, z in enumerate(val):
                ref[i] = z
        else:
            ref[...] = val


class _Masks:
    def __init__(self, mask_refs):
        self._refs = dict(zip(_MASK_KEYS, mask_refs))

    def __getitem__(self, key):
        ref = self._refs[key]
        return (ref[0], ref[1]) if len(ref.shape) == 3 else ref[...]


def _lockstep(gens):
    results = [None] * len(gens)
    live = list(enumerate(gens))
    while live:
        still = []
        for i, g in live:
            try:
                next(g)
                still.append((i, g))
            except StopIteration as stop:
                results[i] = stop.value
        live = still
    return results


def _rwkv_group(rt, kt, kh, bh, kb, bb, v, s_ref, egc, mk, d):
    c = CHUNK
    bd = lambda x: _masked(_tile_rows(_split(x), PACK // c), mk["bm64_bf"])
    kr = _split(jnp.concatenate([kt, rt], axis=0))
    aa_k = _mm(kr, bd(kh), NT)
    aa_b = _mm(kr, bd(bh), NT)
    s_bd = s_ref[...]
    krs = _mm(kr, s_bd, NT)
    yield
    a_kk = aa_k[:c] * mk["strict"][d]
    a_rk = aa_k[c:] * mk["incl"][d]
    n = -(aa_b[:c] * mk["strict"][d])
    a_rb = aa_b[c:] * mk["incl"][d]
    x = mk["eye"] + n
    p = _mm(n, bd(n))
    akv = _mm(jnp.concatenate([a_kk, a_rk], axis=0), bd(v))
    yield
    for i in range(5):
        if i < 4:
            xp = _mm(jnp.concatenate([x, p], axis=0), bd(p))
            x = x + xp[:c]
            p = xp[c:]
        else:
            x = x + _mm(x, bd(p))
        yield
    u = _mm(x, bd(krs[:c] + akv[:c]))
    yield
    y = krs[c:] + akv[c:] - _mm(a_rb, bd(u))
    vu = jnp.concatenate([v, (-u).astype(BF16)], axis=0)
    kbb = jnp.concatenate([kb, bb], axis=0)
    s_ref[...] = (s_bd * egc + _mm(vu, kbb, TN_DIMS)) * mk["bm64"]
    return y


def _hgrn_dir(qn, kn, qt, kbh, vb, eghc, s_refs, mk, d):
    bm_hg = mk["bm_hg_bf"]
    heads = D_B // HEAD_B
    a_p = _mm(qn, _masked(_tile_rows(_split(kn), heads), bm_hg), NT) * mk["incl"][d]
    inter = []
    for grp, s_ref in enumerate(s_refs):
        sl = slice(PACK * grp, PACK * (grp + 1))
        st = s_ref[...]
        inter.append(_mm(qt[:, sl], st, NT))
        s_ref[...] = (st * eghc[:, sl] + _mm(vb[:, sl], kbh[:, sl], TN_DIMS)) * mk["bm128"]
    yield
    o = _mm(a_p, _masked(_tile_rows(_split(vb), heads), bm_hg))
    return o + jnp.concatenate(inter, axis=1)


def _blockwise(fn, operands, n_out, rows=CHUNK, width=D_A):
    cols = []
    for l0 in range(0, width, BLOCK_LANES):
        parts = []
        for r0 in range(0, rows, BLOCK_ROWS):
            blk = [z[(slice(None) if z.shape[0] == 1 else slice(r0, r0 + BLOCK_ROWS)), l0:l0 + BLOCK_LANES]
                   for z in operands]
            parts.append(fn(*blk))
        cols.append([jnp.concatenate([p[i] for p in parts], axis=0) for i in range(n_out)])
    return [jnp.concatenate([col[i] for col in cols], axis=1) for i in range(n_out)]


def _chunk_operands(pa_ref, pb_ref, bi, d, prm, mk):
    c = CHUNK
    n_t = SUM_TERMS
    w0, a0, wa2, k_k, k_a, lb = prm
    cols = lambda ref, i, width=D_A: ref.at[bi, :, i * width:(i + 1) * width]
    r_ref, k_ref, v_ref = cols(pa_ref, 0), cols(pa_ref, 1), cols(pa_ref, 2)
    lo = pa_ref[bi, :, 4 * D_A + LORA * d:4 * D_A + LORA * (d + 1)]
    lo = jnp.where(_iota((c, LORA), 1) < LORA // 2, jnp.tanh(lo), lo)
    wa = _mm(lo, wa2[d])
    yield

    def gates(wa_w, wa_a, k, fr, w0_, a0_, kk_, lb_):
        lw = -DECAY_SCALE * _sigmoid(w0_ + wa_w)
        a = _sigmoid(a0_ + wa_a)
        kk = k * kk_
        f = lb_ + (1.0 - lb_) * _sigmoid(fr)
        gf = jnp.log(f)
        return (lw, a, kk, 1.0 - f, (kk * kk).astype(BF16), *_terms(lw, n_t), *_terms(gf, n_t))

    res = _blockwise(gates, [wa[:, :D_A], wa[:, D_A:], k_ref, cols(pb_ref, 2 + d, D_B), w0[d:d + 1], a0[d:d + 1],
                             k_k, lb[d:d + 1]], 5 + 2 * n_t)
    lw, a, kk, kf, sq = res[:5]
    lw_t, gf_t = res[5:5 + n_t], res[5 + n_t:]
    seg = lambda lanes: _dot(sq[:, lanes], mk["bm64_bf"])
    n2 = jnp.concatenate([seg(slice(0, PACK)), seg(slice(PACK, D_A))], axis=1)
    gcum = _sum_small_first([_dot(mk["tri"][d], jnp.concatenate([tl, tg], axis=1))
                             for tl, tg in zip(lw_t, gf_t)])
    yield
    last = c - 1 if d == 0 else 0
    g, gh = gcum[:, :D_A], gcum[:, D_A:]
    gc, ghc, ghm = g[last:last + 1], gh[last:last + 1], gh[c // 2:c // 2 + 1]
    egc, eghc = jnp.exp(gc), jnp.exp(ghc)

    def rwkv_operands(r, k, kk_, n2_, a_, g_, lw_, ka_, egc_):
        kap = kk_ * lax.rsqrt(jnp.maximum(n2_, 1e-24))
        keff = k * (1.0 + (a_ - 1.0) * ka_)
        eng = _exp_neg(g_)
        kh = keff * eng
        bh = (kap * a_) * eng
        out = (r * jnp.exp(g_), kap * jnp.exp(g_ - lw_), kh, bh, kh * egc_, bh * egc_)
        return tuple(z.astype(BF16) for z in out)

    rt, kt, kh, bh, kb, bb = _blockwise(rwkv_operands, [r_ref, k_ref, kk, n2, a, g, lw, k_a, egc], 6)

    def hgrn_operands(qr, kf_, gh_, ghm_, e_m, e_cm):
        q = _silu(qr)
        ghn = gh_ - ghm_
        qn = q * jnp.exp(ghn)
        kn = kf_ * _exp_neg(ghn)
        return tuple(z.astype(BF16) for z in (qn, kn, qn * e_m, kn * e_cm))

    qn, kn, qt, kbh = _blockwise(hgrn_operands, [cols(pb_ref, 0, D_B), kf, gh, ghm, jnp.exp(ghm),
                                                 jnp.exp(ghc - ghm)], 4, width=D_B)
    v = v_ref[...].astype(BF16)
    vb = pb_ref[bi, :, D_B:2 * D_B].astype(BF16)
    return (rt, kt, kh, bh, kb, bb, v, qn, kn, qt, kbh, vb), (egc, eghc)


def _scan_dir(pa_ref, pb_ref, bi, d, prm, sa_scr, sb_scr, mk):
    ops, (egc, eghc) = yield from _chunk_operands(pa_ref, pb_ref, bi, d, prm, mk)
    rt, kt, kh, bh, kb, bb, v, qn, kn, qt, kbh, vb = ops
    chains = []
    for grp in range(D_A // PACK):
        sl = slice(PACK * grp, PACK * (grp + 1))
        chains.append(_rwkv_group(rt[:, sl], kt[:, sl], kh[:, sl], bh[:, sl], kb[:, sl], bb[:, sl], v[:, sl],
                                  sa_scr.at[bi, d, grp], egc[:, sl], mk, d))
    chains.append(_hgrn_dir(qn, kn, qt, kbh, vb, eghc, [sb_scr.at[bi, d, grp] for grp in range(D_B // PACK)],
                            mk, d))
    results = [None] * len(chains)
    live = list(enumerate(chains))
    while live:
        still = []
        for i, g in live:
            try:
                next(g)
                still.append((i, g))
            except StopIteration as stop:
                results[i] = stop.value
        live = still
        yield
    return jnp.concatenate(results[:-1], axis=1), results[-1]


def _state_slots():
    return [(bi, d, grp) for bi in range(SCAN_BATCH) for d in range(2) for grp in range(D_A // PACK)]


def _load_states(sa0_ref, sb0_ref, sa_scr, sb_scr):
    spread = jnp.where(_iota((HEAD_A, PACK), 0) == (_iota((HEAD_A, PACK), 1) & (HEAD_A - 1)), 1.0, 0.0)
    spread = spread.astype(BF16)
    bm64 = _head_mask_a()
    zero = jnp.zeros((HEAD_B, HEAD_B), F32)
    for bi, d, grp in _state_slots():
        sa_scr[bi, d, grp] = _mm_exact_rhs(sa0_ref[bi, d, grp], spread) * bm64
        h0 = sb0_ref[bi, d, 2 * grp].T
        h1 = sb0_ref[bi, d, 2 * grp + 1].T
        sb_scr[bi, d, grp] = jnp.concatenate([jnp.concatenate([h0, zero], axis=1),
                                              jnp.concatenate([zero, h1], axis=1)], axis=0)


def _store_states(sa_scr, sb_scr, sa_out, sb_out):
    gather = jnp.where((_iota((PACK, HEAD_A), 0) & (HEAD_A - 1)) == _iota((PACK, HEAD_A), 1), 1.0, 0.0)
    gather = gather.astype(BF16)
    for bi, d, grp in _state_slots():
        sa_out[bi, d, grp] = _mm_exact_rhs(sa_scr[bi, d, grp], gather)
        for hh in range(PACK // HEAD_B):
            blk = sb_scr[bi, d, grp, HEAD_B * hh:HEAD_B * (hh + 1), HEAD_B * hh:HEAD_B * (hh + 1)]
            sb_out[bi, d, 2 * grp + hh] = blk.T


def _scan_kernel(paf_ref, pbf_ref, pab_ref, pbb_ref, *rest, from_zero):
    rest, mask_refs = rest[:-len(_MASK_KEYS)], rest[-len(_MASK_KEYS):]
    if from_zero:
        (w0_ref, a0_ref, wa2_ref, kk_ref, ka_ref, lb_ref, yf_ref, yb_ref, of_ref, ob_ref, sa_out, sb_out,
         sa_scr, sb_scr) = rest
    else:
        (sa0_ref, sb0_ref, w0_ref, a0_ref, wa2_ref, kk_ref, ka_ref, lb_ref, yf_ref, yb_ref, of_ref, ob_ref,
         sa_scr, sb_scr) = rest
    ci = pl.program_id(1)

    @pl.when(ci == 0)
    def _():
        _store_masks(mask_refs)
        if from_zero:
            sa_scr[...] = jnp.zeros(sa_scr.shape, F32)
            sb_scr[...] = jnp.zeros(sb_scr.shape, F32)
        else:
            _load_states(sa0_ref, sb0_ref, sa_scr, sb_scr)

    mk = _Masks(mask_refs)
    prm = (w0_ref[...], a0_ref[...], wa2_ref, kk_ref[...], ka_ref[...], lb_ref[...])
    p_refs = ((paf_ref, pbf_ref), (pab_ref, pbb_ref))
    runs = [(bi, d) for bi in range(SCAN_BATCH) for d in range(2)]
    outs = _lockstep([_scan_dir(p_refs[d][0], p_refs[d][1], bi, d, prm, sa_scr, sb_scr, mk) for bi, d in runs])
    for (bi, d), (y, o) in zip(runs, outs):
        (yf_ref, yb_ref)[d][bi] = y
        (of_ref, ob_ref)[d][bi] = o

    if from_zero:
        @pl.when(ci == pl.num_programs(1) - 1)
        def _():
            _store_states(sa_scr, sb_scr, sa_out, sb_out)


def _scan_call(pa, pb, states, w0, a0, wa2, k_k, k_a, lb):
    bsz, seq, _ = pa.shape
    nc = seq // CHUNK
    nb = SCAN_BATCH
    from_zero = states is None
    fwd = lambda b, c: (b, c, 0)
    bwd = lambda b, c: (b, nc - 1 - c, 0)
    st = lambda b, c: (b, 0, 0, 0, 0)
    full = lambda shape: pl.BlockSpec(shape, lambda b, c: (0,) * len(shape))
    sa_block = (nb, 2, D_A // PACK, PACK, HEAD_A)
    sb_block = (nb, 2, D_B // HEAD_B, HEAD_B, HEAD_B)
    y_shape = jax.ShapeDtypeStruct((bsz, seq, D_A), F32)
    out_specs = [pl.BlockSpec((nb, CHUNK, D_A), fwd), pl.BlockSpec((nb, CHUNK, D_A), bwd),
                 pl.BlockSpec((nb, CHUNK, D_B), fwd), pl.BlockSpec((nb, CHUNK, D_B), bwd)]
    out_shape = [y_shape, y_shape, y_shape, y_shape]
    in_specs = [pl.BlockSpec((nb, CHUNK, A_COLS), fwd), pl.BlockSpec((nb, CHUNK, B_COLS), fwd),
                pl.BlockSpec((nb, CHUNK, A_COLS), bwd), pl.BlockSpec((nb, CHUNK, B_COLS), bwd)]
    args = [pa, pb, pa, pb]
    if from_zero:
        out_specs += [pl.BlockSpec(sa_block, st), pl.BlockSpec(sb_block, st)]
        out_shape += [jax.ShapeDtypeStruct((bsz,) + sa_block[1:], F32),
                      jax.ShapeDtypeStruct((bsz,) + sb_block[1:], F32)]
    else:
        in_specs += [pl.BlockSpec(sa_block, st), pl.BlockSpec(sb_block, st)]
        args += list(states)
    in_specs += [full((2, D_A)), full((2, D_A)), full((2, LORA, 2 * D_A)),
                 full((1, D_A)), full((1, D_A)), full((2, D_B))]
    args += [w0, a0, wa2, k_k, k_a, lb]
    bd_block = (nb, 2, D_A // PACK, PACK, PACK)
    return pl.pallas_call(
        functools.partial(_scan_kernel, from_zero=from_zero),
        grid=(bsz // nb, nc),
        in_specs=in_specs,
        out_specs=out_specs,
        out_shape=out_shape,
        scratch_shapes=[pltpu.VMEM(bd_block, F32), pltpu.VMEM(bd_block, F32)]
                       + [pltpu.VMEM(shape, dtype) for shape, dtype in _MASK_SCRATCH],
        compiler_params=pltpu.CompilerParams(dimension_semantics=("arbitrary", "arbitrary"),
                                             vmem_limit_bytes=VMEM_LIMIT),
        name="scan_state" if from_zero else "scan",
    )(*args)


def _out_kernel(x_ref, pa_ref, zb_ref, yf_ref, yb_ref, of_ref, ob_ref, gate_ref, a0_ref, a2p_ref, ka_ref,
                rk_ref, lnw_ref, lnb_ref, og_ref, wout_ref, fg_ref, out_ref):
    pa = pa_ref[0]
    r = pa[:, 0:D_A]
    k = pa[:, D_A:2 * D_A]
    v = pa[:, 2 * D_A:3 * D_A]
    za = pa[:, 3 * D_A:4 * D_A]
    bm64_bf = _head_mask_a().astype(BF16)

    def seg_sum(z):
        return jnp.concatenate([_mm_exact_rhs(z[:, :PACK], bm64_bf, SUM_TERMS),
                                _mm_exact_rhs(z[:, PACK:], bm64_bf, SUM_TERMS)], axis=1)

    y = yf_ref[0] + yb_ref[0]
    mu = seg_sum(y) * (1.0 / HEAD_A)
    dlt = y - mu
    var = seg_sum(dlt * dlt) * (1.0 / HEAD_A)
    yn = dlt * lax.rsqrt(var + GN_EPS) * lnw_ref[...] + lnb_ref[...]
    keffs = []
    for d in range(2):
        lo = pa[:, 4 * D_A + LORA * d:4 * D_A + LORA * (d + 1)]
        a = _sigmoid(a0_ref[d:d + 1] + _mm(lo, a2p_ref[d]))
        keffs.append(k * (1.0 + (a - 1.0) * ka_ref[...]))
    kmean = 0.5 * (keffs[0] + keffs[1])
    bonus = seg_sum(r * kmean * rk_ref[...]) * v
    out_a = (yn + bonus) * _silu(za)

    o = of_ref[0] + ob_ref[0]
    og = og_ref[...]
    zb = zb_ref[0]
    outs = [out_a]
    for h in range(D_B // HEAD_B):
        sl = slice(HEAD_B * h, HEAD_B * (h + 1))
        oh = o[:, sl]
        oh = oh * lax.rsqrt(jnp.mean(oh * oh, axis=-1, keepdims=True) + EPS) * og[:, sl]
        outs.append(oh * _silu(zb[:, sl]))
    mix = jnp.concatenate(outs, axis=1)
    proj = _dot(mix.astype(BF16), wout_ref[...])
    hs = x_ref[0] + gate_ref[0] * proj
    out_ref[0] = hs * lax.rsqrt(jnp.mean(hs * hs, axis=-1, keepdims=True) + EPS) * fg_ref[...]


def _out_call(x, pa, pb, yf, yb, of, ob, gate, a0, a2p, k_a, r_k, lnx_w, lnx_b, onorm_g, w_out_bf16, final_g):
    bsz, seq, _ = x.shape
    tm = min(OUT_ROWS, seq)
    tok = lambda b, i: (b, i, 0)
    full = lambda shape: pl.BlockSpec(shape, lambda b, i: (0,) * len(shape))
    return pl.pallas_call(
        _out_kernel,
        grid=(bsz, seq // tm),
        in_specs=[pl.BlockSpec((1, tm, D_MODEL), tok),
                  pl.BlockSpec((1, tm, A_COLS), tok),
                  pl.BlockSpec((1, tm, D_B), lambda b, i: (b, i, 4)),
                  pl.BlockSpec((1, tm, D_A), tok), pl.BlockSpec((1, tm, D_A), tok),
                  pl.BlockSpec((1, tm, D_B), tok), pl.BlockSpec((1, tm, D_B), tok),
                  pl.BlockSpec((1, 1, D_MODEL), lambda b, i: (b, 0, 0)),
                  full((2, D_A)), full((2, LORA, D_A)), full((1, D_A)), full((1, D_A)),
                  full((1, D_A)), full((1, D_A)), full((1, D_B)),
                  full((D_MODEL, D_MODEL)), full((1, D_MODEL))],
        out_specs=pl.BlockSpec((1, tm, D_MODEL), tok),
        out_shape=jax.ShapeDtypeStruct((bsz, seq, D_MODEL), F32),
        compiler_params=pltpu.CompilerParams(dimension_semantics=("arbitrary", "arbitrary"),
                                             vmem_limit_bytes=VMEM_LIMIT),
        name="out",
    )(x, pa, pb, yf, yb, of, ob, gate, a0, a2p, k_a, r_k, lnx_w, lnx_b, onorm_g, w_out_bf16, final_g)


def kernel(x_prompt, x_sample, state_rwkv, state_hgrn, c, c_ctx, norm_g, w_ada, b_ada, w_in, mu_h, mu_v, w0, w2,
           a0, a2, k_k, k_a, r_k, lnx_w, lnx_b, lb_logits, onorm_g, w_out, final_g):
    l = 0
    bp = x_prompt.shape[0]
    bs = x_sample.shape[0]
    lb_all = jnp.cumsum(jax.nn.softmax(lb_logits.astype(F32), axis=0), axis=0)
    lb = lb_all[l]

    wa_bf = w_in[l, :, :A_COLS].astype(BF16)
    wb_bf = w_in[l, :, A_COLS:].astype(BF16)
    w_out_bf = w_out[l].astype(BF16)
    zeros = jnp.zeros((2, LORA // 2, D_A), F32)
    wa2 = jnp.concatenate([jnp.concatenate([w2[l], zeros], axis=2),
                           jnp.concatenate([zeros, a2[l]], axis=2)], axis=1)
    a2p = jnp.concatenate([zeros, a2[l]], axis=1)
    row = lambda z: z.reshape(1, -1)

    cc = jnp.concatenate([c_ctx[None, :], c, jnp.zeros((16 - 1 - bs, D_MODEL), F32)], axis=0)
    m = _ada_call(cc, w_ada[l], row(b_ada[l]))
    shift, scale, gate = m[:, :D_MODEL], m[:, D_MODEL:2 * D_MODEL], m[:, 2 * D_MODEL:]
    ctx = lambda z: jnp.broadcast_to(z[0:1, None, :], (bp, 1, D_MODEL))
    lat = lambda z: z[1:1 + bs, None, :]

    def path(x, sh, sc, gt, states, grid_shift):
        pa, pb = _proj_call(x, sh, sc, row(norm_g[l]), wa_bf, wb_bf, row(mu_h[l]), row(mu_v[l]), grid_shift)
        res = _scan_call(pa, pb, states, w0[l], a0[l], wa2, row(k_k[l]), row(k_a[l]), lb)
        yf, yb, of, ob = res[:4]
        y = _out_call(x, pa, pb, yf, yb, of, ob, gt, a0[l], a2p, row(k_a[l]), row(r_k[l]), row(lnx_w[l]),
                      row(lnx_b[l]), row(onorm_g[l]), w_out_bf, row(final_g))
        return y, res[4:]

    y_prompt, (s_a, s_b) = path(x_prompt, ctx(shift), ctx(scale), ctx(gate), None, False)
    groups_a = D_A // PACK
    sa0 = state_rwkv[:, l].reshape(bs, 2, groups_a, PACK, HEAD_A)
    y_sample, _ = path(x_sample, lat(shift), lat(scale), lat(gate), (sa0, state_hgrn[:, l]), True)
    s_a = s_a.reshape(bp, 1, 2, D_A // HEAD_A, HEAD_A, HEAD_A)
    return y_prompt, y_sample, s_a, s_b[:, None]
```

```python
import functools

import jax
import jax.numpy as jnp
from jax import lax
from jax.experimental import pallas as pl
from jax.experimental.pallas import tpu as pltpu

F32 = jnp.float32
BF16 = jnp.bfloat16

D_MODEL = 1024
D_A = 512
D_B = 512
HEAD_A = 64
HEAD_B = 128
LOG2_HEAD_A = HEAD_A.bit_length() - 1
LOG2_HEAD_B = HEAD_B.bit_length() - 1
LORA = 128
A_COLS = 4 * D_A + 2 * LORA
B_COLS = 5 * D_B
GRID_W = 64
CHUNK = 64
LOG2_CHUNK = CHUNK.bit_length() - 1
SCAN_BATCH = 4
SUM_TERMS = 2
BLOCK_ROWS = 32
BLOCK_LANES = 128
EPS = 1e-6
GN_EPS = 64e-5
DECAY_SCALE = 0.6065306597126334
NEG_LOG2E = -1.4426950408889634
PACK = 256
TNA = 768
TNB = 512
NA = A_COLS // TNA
NB = B_COLS // TNB
PROJ_ROWS = 2048
OUT_ROWS = 512
VMEM_LIMIT = 56 * 1024 * 1024

NN = ((1,), (0,))
NT = ((1,), (1,))
TN_DIMS = ((0,), (0,))


def _dot(a, b, dims=NN):
    return lax.dot_general(a, b, (dims, ((), ())), preferred_element_type=F32)


def _split(x, passes=1):
    if isinstance(x, tuple):
        return x
    hi = x.astype(BF16)
    if passes == 1:
        return (hi,)
    return hi, (x - hi.astype(F32)).astype(BF16)


def _mm(a, b, dims=NN, passes=1):
    a = _split(a, passes)
    b = _split(b, passes)
    if len(a) == 1 or len(b) == 1:
        return _dot(a[0], b[0], dims)
    return _dot(a[0], b[0], dims) + (_dot(a[0], b[1], dims) + _dot(a[1], b[0], dims))


def _terms(x, n):
    out = []
    for _ in range(n - 1):
        h = x.astype(BF16)
        out.append(h)
        x = x - h.astype(F32)
    out.append(x.astype(BF16))
    return out


def _sum_small_first(parts):
    return functools.reduce(lambda acc, z: z + acc, reversed(parts))


def _mm_exact_rhs(a, b_bf16, n=3):
    return _sum_small_first([_dot(t, b_bf16) for t in _terms(a, n)])


def _exp_neg(x):
    return jnp.exp2(x * NEG_LOG2E)


def _sigmoid(x):
    return 0.5 * jnp.tanh(0.5 * x) + 0.5


def _silu(x):
    return x * _sigmoid(x)


def _iota(shape, dim):
    return lax.broadcasted_iota(jnp.int32, shape, dim)


def _block_mask(rows, cols, row_shift, col_shift):
    same = (_iota((rows, cols), 0) >> row_shift) == (_iota((rows, cols), 1) >> col_shift)
    return jnp.where(same, 1.0, 0.0).astype(F32)


def _head_mask_a():
    return _block_mask(PACK, PACK, LOG2_HEAD_A, LOG2_HEAD_A)


def _tile_rows(x, n):
    if isinstance(x, tuple):
        return tuple(_tile_rows(t, n) for t in x)
    return jnp.concatenate([x] * n, axis=0)


def _masked(x, mask_bf16):
    if isinstance(x, tuple):
        return tuple(_masked(t, mask_bf16) for t in x)
    return x * mask_bf16


def _ada_kernel(c_ref, w_ref, b_ref, m_ref):
    m_ref[...] = _mm(_silu(c_ref[...]), w_ref[...], passes=3) + b_ref[...]


def _ada_call(cc, w_ada, b_ada):
    rows = cc.shape[0]
    return pl.pallas_call(
        _ada_kernel,
        grid=(3,),
        in_specs=[pl.BlockSpec((rows, D_MODEL), lambda j: (0, 0)),
                  pl.BlockSpec((D_MODEL, D_MODEL), lambda j: (0, j)),
                  pl.BlockSpec((1, D_MODEL), lambda j: (0, j))],
        out_specs=pl.BlockSpec((rows, D_MODEL), lambda j: (0, j)),
        out_shape=jax.ShapeDtypeStruct((rows, 3 * D_MODEL), F32),
        compiler_params=pltpu.CompilerParams(dimension_semantics=("arbitrary",),
                                             vmem_limit_bytes=VMEM_LIMIT),
        name="ada",
    )(cc, w_ada, b_ada)


def _proj_kernel(x_ref, sh_ref, sc_ref, g_ref, wa_ref, wb_ref, muh_ref, muv_ref, pa_ref, pb_ref, xm_ref, *,
                 nbp, seq, grid_shift):
    j = pl.program_id(1)
    rows = nbp * seq

    @pl.when(j == 0)
    def _():
        for b in range(nbp):
            x = x_ref[b]
            xn = x * lax.rsqrt(jnp.mean(x * x, axis=-1, keepdims=True) + EPS) * g_ref[...]
            xm_ref[b * seq:(b + 1) * seq, :] = (xn * (1.0 + sc_ref[b]) + sh_ref[b]).astype(BF16)

    @pl.when(j < NA)
    def _():
        p = _dot(xm_ref[...], wa_ref[...])
        w = GRID_W if grid_shift else seq
        nblk = rows // w
        pos = _iota((1, w, TNA), 1)
        to3 = lambda z: z.reshape(nblk, w, TNA)
        prev = jnp.where(pos == 0, 0.0, to3(pltpu.roll(p, 1, axis=0)))
        nxt = jnp.where(pos == w - 1, 0.0, to3(pltpu.roll(p, rows - 1, axis=0)))
        muh = muh_ref[...]
        out = (0.5 * muh) * (prev + nxt)
        if grid_shift:
            muv = muv_ref[...]
            p3 = to3(p)
            zblk = jnp.zeros((1, w, TNA), F32)
            up = jnp.concatenate([zblk, p3[:nblk - 1]], axis=0)
            dn = jnp.concatenate([p3[1:], zblk], axis=0)
            out = (1.0 - muh - muv) * p3 + out + (0.5 * muv) * (up + dn)
        else:
            out = (1.0 - muh) * to3(p) + out
        pa_ref[...] = out.reshape(nbp, seq, TNA)

    @pl.when(j >= NA)
    def _():
        pb_ref[...] = _dot(xm_ref[...], wb_ref[...]).reshape(nbp, seq, TNB)


def _proj_call(x, shift, scale, norm_g, wa_bf16, wb_bf16, mu_h, mu_v, grid_shift):
    bsz, seq, _ = x.shape
    nbp = 1 if grid_shift else PROJ_ROWS // seq
    kern = functools.partial(_proj_kernel, nbp=nbp, seq=seq, grid_shift=grid_shift)
    a_idx = lambda j: jnp.minimum(j, NA - 1)
    b_idx = lambda j: jnp.maximum(j - NA, 0)
    return pl.pallas_call(
        kern,
        grid=(bsz // nbp, NA + NB),
        in_specs=[pl.BlockSpec((nbp, seq, D_MODEL), lambda b, j: (b, 0, 0)),
                  pl.BlockSpec((nbp, 1, D_MODEL), lambda b, j: (b, 0, 0)),
                  pl.BlockSpec((nbp, 1, D_MODEL), lambda b, j: (b, 0, 0)),
                  pl.BlockSpec((1, D_MODEL), lambda b, j: (0, 0)),
                  pl.BlockSpec((D_MODEL, TNA), lambda b, j: (0, a_idx(j))),
                  pl.BlockSpec((D_MODEL, TNB), lambda b, j: (0, b_idx(j))),
                  pl.BlockSpec((1, TNA), lambda b, j: (0, a_idx(j))),
                  pl.BlockSpec((1, TNA), lambda b, j: (0, a_idx(j)))],
        out_specs=[pl.BlockSpec((nbp, seq, TNA), lambda b, j: (b, 0, a_idx(j))),
                   pl.BlockSpec((nbp, seq, TNB), lambda b, j: (b, 0, b_idx(j)))],
        out_shape=[jax.ShapeDtypeStruct((bsz, seq, A_COLS), F32),
                   jax.ShapeDtypeStruct((bsz, seq, B_COLS), F32)],
        scratch_shapes=[pltpu.VMEM((nbp * seq, D_MODEL), BF16)],
        compiler_params=pltpu.CompilerParams(dimension_semantics=("arbitrary", "arbitrary"),
                                             vmem_limit_bytes=VMEM_LIMIT),
        name="proj_grid" if grid_shift else "proj_seq",
    )(x, shift, scale, norm_g, wa_bf16, wb_bf16, mu_h, mu_v)


def _scan_masks():
    c = CHUNK
    t = _iota((c, PACK), 0)
    s = _iota((c, PACK), 1) & (c - 1)
    tt = _iota((c, c), 0)
    ss = _iota((c, c), 1)
    f01 = lambda cond: jnp.where(cond, 1.0, 0.0).astype(F32)
    bm64 = _head_mask_a()
    bm128 = _block_mask(PACK, PACK, LOG2_HEAD_B, LOG2_HEAD_B)
    return dict(
        bm64=bm64, bm64_bf=bm64.astype(BF16), bm128=bm128,
        bm_hg_bf=_block_mask(4 * c, D_B, LOG2_CHUNK, LOG2_HEAD_B).astype(BF16),
        eye=f01(t == s),
        incl=(f01(t >= s), f01(t <= s)),
        strict=(f01(t > s), f01(t < s)),
        tri=(f01(tt >= ss).astype(BF16), f01(tt <= ss).astype(BF16)),
    )


_MASK_KEYS = ("bm64", "bm64_bf", "bm128", "bm_hg_bf", "eye", "incl", "strict", "tri")
_MASK_SCRATCH = (((PACK, PACK), F32), ((PACK, PACK), BF16), ((PACK, PACK), F32), ((4 * CHUNK, D_B), BF16),
                 ((CHUNK, PACK), F32), ((2, CHUNK, PACK), F32), ((2, CHUNK, PACK), F32), ((2, CHUNK, CHUNK), BF16))


def _store_masks(mask_refs):
    masks = _scan_masks()
    for ref, key in zip(mask_refs, _MASK_KEYS):
        val = masks[key]
        if isinstance(val, tuple):
            for i, z in enumerate(val):
                ref[i] = z
        else:
            ref[...] = val


class _Masks:
    def __init__(self, mask_refs):
        self._refs = dict(zip(_MASK_KEYS, mask_refs))

    def __getitem__(self, key):
        ref = self._refs[key]
        return (ref[0], ref[1]) if len(ref.shape) == 3 else ref[...]


def _lockstep(gens):
    results = [None] * len(gens)
    live = list(enumerate(gens))
    while live:
        still = []
        for i, g in live:
            try:
                next(g)
                still.append((i, g))
            except StopIteration as stop:
                results[i] = stop.value
        live = still
    return results


def _rwkv_group(rt, kt, kh, bh, kb, bb, v, s_ref, egc, mk, d):
    c = CHUNK
    bd = lambda x: _masked(_tile_rows(_split(x), PACK // c), mk["bm64_bf"])
    kr = _split(jnp.concatenate([kt, rt], axis=0))
    aa_k = _mm(kr, bd(kh), NT)
    aa_b = _mm(kr, bd(bh), NT)
    s_bd = s_ref[...]
    krs = _mm(kr, s_bd, NT)
    yield
    a_kk = aa_k[:c] * mk["strict"][d]
    a_rk = aa_k[c:] * mk["incl"][d]
    n = -(aa_b[:c] * mk["strict"][d])
    a_rb = aa_b[c:] * mk["incl"][d]
    x = mk["eye"] + n
    p = _mm(n, bd(n))
    akv = _mm(jnp.concatenate([a_kk, a_rk], axis=0), bd(v))
    yield
    for i in range(5):
        if i < 4:
            xp = _mm(jnp.concatenate([x, p], axis=0), bd(p))
            x = x + xp[:c]
            p = xp[c:]
        else:
            x = x + _mm(x, bd(p))
        yield
    u = _mm(x, bd(krs[:c] + akv[:c]))
    yield
    y = krs[c:] + akv[c:] - _mm(a_rb, bd(u))
    vu = jnp.concatenate([v, (-u).astype(BF16)], axis=0)
    kbb = jnp.concatenate([kb, bb], axis=0)
    s_ref[...] = (s_bd * egc + _mm(vu, kbb, TN_DIMS)) * mk["bm64"]
    return y


def _hgrn_dir(qn, kn, qt, kbh, vb, eghc, s_refs, mk, d):
    bm_hg = mk["bm_hg_bf"]
    heads = D_B // HEAD_B
    a_p = _mm(qn, _masked(_tile_rows(_split(kn), heads), bm_hg), NT) * mk["incl"][d]
    inter = []
    for grp, s_ref in enumerate(s_refs):
        sl = slice(PACK * grp, PACK * (grp + 1))
        st = s_ref[...]
        inter.append(_mm(qt[:, sl], st, NT))
        s_ref[...] = (st * eghc[:, sl] + _mm(vb[:, sl], kbh[:, sl], TN_DIMS)) * mk["bm128"]
    yield
    o = _mm(a_p, _masked(_tile_rows(_split(vb), heads), bm_hg))
    return o + jnp.concatenate(inter, axis=1)


def _blockwise(fn, operands, n_out, rows=CHUNK, width=D_A):
    cols = []
    for l0 in range(0, width, BLOCK_LANES):
        parts = []
        for r0 in range(0, rows, BLOCK_ROWS):
            blk = [z[(slice(None) if z.shape[0] == 1 else slice(r0, r0 + BLOCK_ROWS)), l0:l0 + BLOCK_LANES]
                   for z in operands]
            parts.append(fn(*blk))
        cols.append([jnp.concatenate([p[i] for p in parts], axis=0) for i in range(n_out)])
    return [jnp.concatenate([col[i] for col in cols], axis=1) for i in range(n_out)]


def _chunk_operands(pa_ref, pb_ref, bi, d, prm, mk):
    c = CHUNK
    n_t = SUM_TERMS
    w0, a0, wa2, k_k, k_a, lb = prm
    cols = lambda ref, i, width=D_A: ref.at[bi, :, i * width:(i + 1) * width]
    r_ref, k_ref, v_ref = cols(pa_ref, 0), cols(pa_ref, 1), cols(pa_ref, 2)
    lo = pa_ref[bi, :, 4 * D_A + LORA * d:4 * D_A + LORA * (d + 1)]
    lo = jnp.where(_iota((c, LORA), 1) < LORA // 2, jnp.tanh(lo), lo)
    wa = _mm(lo, wa2[d])
    yield

    def gates(wa_w, wa_a, k, fr, w0_, a0_, kk_, lb_):
        lw = -DECAY_SCALE * _sigmoid(w0_ + wa_w)
        a = _sigmoid(a0_ + wa_a)
        kk = k * kk_
        f = lb_ + (1.0 - lb_) * _sigmoid(fr)
        gf = jnp.log(f)
        return (lw, a, kk, 1.0 - f, (kk * kk).astype(BF16), *_terms(lw, n_t), *_terms(gf, n_t))

    res = _blockwise(gates, [wa[:, :D_A], wa[:, D_A:], k_ref, cols(pb_ref, 2 + d, D_B), w0[d:d + 1], a0[d:d + 1],
                             k_k, lb[d:d + 1]], 5 + 2 * n_t)
    lw, a, kk, kf, sq = res[:5]
    lw_t, gf_t = res[5:5 + n_t], res[5 + n_t:]
    seg = lambda lanes: _dot(sq[:, lanes], mk["bm64_bf"])
    n2 = jnp.concatenate([seg(slice(0, PACK)), seg(slice(PACK, D_A))], axis=1)
    gcum = _sum_small_first([_dot(mk["tri"][d], jnp.concatenate([tl, tg], axis=1))
                             for tl, tg in zip(lw_t, gf_t)])
    yield
    last = c - 1 if d == 0 else 0
    g, gh = gcum[:, :D_A], gcum[:, D_A:]
    gc, ghc, ghm = g[last:last + 1], gh[last:last + 1], gh[c // 2:c // 2 + 1]
    egc, eghc = jnp.exp(gc), jnp.exp(ghc)

    def rwkv_operands(r, k, kk_, n2_, a_, g_, lw_, ka_, egc_):
        kap = kk_ * lax.rsqrt(jnp.maximum(n2_, 1e-24))
        keff = k * (1.0 + (a_ - 1.0) * ka_)
        eng = _exp_neg(g_)
        kh = keff * eng
        bh = (kap * a_) * eng
        out = (r * jnp.exp(g_), kap * jnp.exp(g_ - lw_), kh, bh, kh * egc_, bh * egc_)
        return tuple(z.astype(BF16) for z in out)

    rt, kt, kh, bh, kb, bb = _blockwise(rwkv_operands, [r_ref, k_ref, kk, n2, a, g, lw, k_a, egc], 6)

    def hgrn_operands(qr, kf_, gh_, ghm_, e_m, e_cm):
        q = _silu(qr)
        ghn = gh_ - ghm_
        qn = q * jnp.exp(ghn)
        kn = kf_ * _exp_neg(ghn)
        return tuple(z.astype(BF16) for z in (qn, kn, qn * e_m, kn * e_cm))

    qn, kn, qt, kbh = _blockwise(hgrn_operands, [cols(pb_ref, 0, D_B), kf, gh, ghm, jnp.exp(ghm),
                                                 jnp.exp(ghc - ghm)], 4, width=D_B)
    v = v_ref[...].astype(BF16)
    vb = pb_ref[bi, :, D_B:2 * D_B].astype(BF16)
    return (rt, kt, kh, bh, kb, bb, v, qn, kn, qt, kbh, vb), (egc, eghc)


def _scan_dir(pa_ref, pb_ref, bi, d, prm, sa_scr, sb_scr, mk):
    ops, (egc, eghc) = yield from _chunk_operands(pa_ref, pb_ref, bi, d, prm, mk)
    rt, kt, kh, bh, kb, bb, v, qn, kn, qt, kbh, vb = ops
    chains = []
    for grp in range(D_A // PACK):
        sl = slice(PACK * grp, PACK * (grp + 1))
        chains.append(_rwkv_group(rt[:, sl], kt[:, sl], kh[:, sl], bh[:, sl], kb[:, sl], bb[:, sl], v[:, sl],
                                  sa_scr.at[bi, d, grp], egc[:, sl], mk, d))
    chains.append(_hgrn_dir(qn, kn, qt, kbh, vb, eghc, [sb_scr.at[bi, d, grp] for grp in range(D_B // PACK)],
                            mk, d))
    results = [None] * len(chains)
    live = list(enumerate(chains))
    while live:
        still = []
        for i, g in live:
            try:
                next(g)
                still.append((i, g))
            except StopIteration as stop:
                results[i] = stop.value
        live = still
        yield
    return jnp.concatenate(results[:-1], axis=1), results[-1]


def _state_slots():
    return [(bi, d, grp) for bi in range(SCAN_BATCH) for d in range(2) for grp in range(D_A // PACK)]


def _load_states(sa0_ref, sb0_ref, sa_scr, sb_scr):
    spread = jnp.where(_iota((HEAD_A, PACK), 0) == (_iota((HEAD_A, PACK), 1) & (HEAD_A - 1)), 1.0, 0.0)
    spread = spread.astype(BF16)
    bm64 = _head_mask_a()
    zero = jnp.zeros((HEAD_B, HEAD_B), F32)
    for bi, d, grp in _state_slots():
        sa_scr[bi, d, grp] = _mm_exact_rhs(sa0_ref[bi, d, grp], spread) * bm64
        h0 = sb0_ref[bi, d, 2 * grp].T
        h1 = sb0_ref[bi, d, 2 * grp + 1].T
        sb_scr[bi, d, grp] = jnp.concatenate([jnp.concatenate([h0, zero], axis=1),
                                              jnp.concatenate([zero, h1], axis=1)], axis=0)


def _store_states(sa_scr, sb_scr, sa_out, sb_out):
    gather = jnp.where((_iota((PACK, HEAD_A), 0) & (HEAD_A - 1)) == _iota((PACK, HEAD_A), 1), 1.0, 0.0)
    gather = gather.astype(BF16)
    for bi, d, grp in _state_slots():
        sa_out[bi, d, grp] = _mm_exact_rhs(sa_scr[bi, d, grp], gather)
        for hh in range(PACK // HEAD_B):
            blk = sb_scr[bi, d, grp, HEAD_B * hh:HEAD_B * (hh + 1), HEAD_B * hh:HEAD_B * (hh + 1)]
            sb_out[bi, d, 2 * grp + hh] = blk.T


def _scan_kernel(paf_ref, pbf_ref, pab_ref, pbb_ref, *rest, from_zero):
    rest, mask_refs = rest[:-len(_MASK_KEYS)], rest[-len(_MASK_KEYS):]
    if from_zero:
        (w0_ref, a0_ref, wa2_ref, kk_ref, ka_ref, lb_ref, yf_ref, yb_ref, of_ref, ob_ref, sa_out, sb_out,
         sa_scr, sb_scr) = rest
    else:
        (sa0_ref, sb0_ref, w0_ref, a0_ref, wa2_ref, kk_ref, ka_ref, lb_ref, yf_ref, yb_ref, of_ref, ob_ref,
         sa_scr, sb_scr) = rest
    ci = pl.program_id(1)

    @pl.when(ci == 0)
    def _():
        _store_masks(mask_refs)
        if from_zero:
            sa_scr[...] = jnp.zeros(sa_scr.shape, F32)
            sb_scr[...] = jnp.zeros(sb_scr.shape, F32)
        else:
            _load_states(sa0_ref, sb0_ref, sa_scr, sb_scr)

    mk = _Masks(mask_refs)
    prm = (w0_ref[...], a0_ref[...], wa2_ref, kk_ref[...], ka_ref[...], lb_ref[...])
    p_refs = ((paf_ref, pbf_ref), (pab_ref, pbb_ref))
    runs = [(bi, d) for bi in range(SCAN_BATCH) for d in range(2)]
    outs = _lockstep([_scan_dir(p_refs[d][0], p_refs[d][1], bi, d, prm, sa_scr, sb_scr, mk) for bi, d in runs])
    for (bi, d), (y, o) in zip(runs, outs):
        (yf_ref, yb_ref)[d][bi] = y
        (of_ref, ob_ref)[d][bi] = o

    if from_zero:
        @pl.when(ci == pl.num_programs(1) - 1)
        def _():
            _store_states(sa_scr, sb_scr, sa_out, sb_out)


def _scan_call(pa, pb, states, w0, a0, wa2, k_k, k_a, lb):
    bsz, seq, _ = pa.shape
    nc = seq // CHUNK
    nb = SCAN_BATCH
    from_zero = states is None
    fwd = lambda b, c: (b, c, 0)
    bwd = lambda b, c: (b, nc - 1 - c, 0)
    st = lambda b, c: (b, 0, 0, 0, 0)
    full = lambda shape: pl.BlockSpec(shape, lambda b, c: (0,) * len(shape))
    sa_block = (nb, 2, D_A // PACK, PACK, HEAD_A)
    sb_block = (nb, 2, D_B // HEAD_B, HEAD_B, HEAD_B)
    y_shape = jax.ShapeDtypeStruct((bsz, seq, D_A), F32)
    out_specs = [pl.BlockSpec((nb, CHUNK, D_A), fwd), pl.BlockSpec((nb, CHUNK, D_A), bwd),
                 pl.BlockSpec((nb, CHUNK, D_B), fwd), pl.BlockSpec((nb, CHUNK, D_B), bwd)]
    out_shape = [y_shape, y_shape, y_shape, y_shape]
    in_specs = [pl.BlockSpec((nb, CHUNK, A_COLS), fwd), pl.BlockSpec((nb, CHUNK, B_COLS), fwd),
                pl.BlockSpec((nb, CHUNK, A_COLS), bwd), pl.BlockSpec((nb, CHUNK, B_COLS), bwd)]
    args = [pa, pb, pa, pb]
    if from_zero:
        out_specs += [pl.BlockSpec(sa_block, st), pl.BlockSpec(sb_block, st)]
        out_shape += [jax.ShapeDtypeStruct((bsz,) + sa_block[1:], F32),
                      jax.ShapeDtypeStruct((bsz,) + sb_block[1:], F32)]
    else:
        in_specs += [pl.BlockSpec(sa_block, st), pl.BlockSpec(sb_block, st)]
        args += list(states)
    in_specs += [full((2, D_A)), full((2, D_A)), full((2, LORA, 2 * D_A)),
                 full((1, D_A)), full((1, D_A)), full((2, D_B))]
    args += [w0, a0, wa2, k_k, k_a, lb]
    bd_block = (nb, 2, D_A // PACK, PACK, PACK)
    return pl.pallas_call(
        functools.partial(_scan_kernel, from_zero=from_zero),
        grid=(bsz // nb, nc),
        in_specs=in_specs,
        out_specs=out_specs,
        out_shape=out_shape,
        scratch_shapes=[pltpu.VMEM(bd_block, F32), pltpu.VMEM(bd_block, F32)]
                       + [pltpu.VMEM(shape, dtype) for shape, dtype in _MASK_SCRATCH],
        compiler_params=pltpu.CompilerParams(dimension_semantics=("arbitrary", "arbitrary"),
                                             vmem_limit_bytes=VMEM_LIMIT),
        name="scan_state" if from_zero else "scan",
    )(*args)


def _out_kernel(x_ref, pa_ref, zb_ref, yf_ref, yb_ref, of_ref, ob_ref, gate_ref, a0_ref, a2p_ref, ka_ref,
                rk_ref, lnw_ref, lnb_ref, og_ref, wout_ref, fg_ref, out_ref):
    pa = pa_ref[0]
    r = pa[:, 0:D_A]
    k = pa[:, D_A:2 * D_A]
    v = pa[:, 2 * D_A:3 * D_A]
    za = pa[:, 3 * D_A:4 * D_A]
    bm64_bf = _head_mask_a().astype(BF16)

    def seg_sum(z):
        return jnp.concatenate([_mm_exact_rhs(z[:, :PACK], bm64_bf, SUM_TERMS),
                                _mm_exact_rhs(z[:, PACK:], bm64_bf, SUM_TERMS)], axis=1)

    y = yf_ref[0] + yb_ref[0]
    mu = seg_sum(y) * (1.0 / HEAD_A)
    dlt = y - mu
    var = seg_sum(dlt * dlt) * (1.0 / HEAD_A)
    yn = dlt * lax.rsqrt(var + GN_EPS) * lnw_ref[...] + lnb_ref[...]
    keffs = []
    for d in range(2):
        lo = pa[:, 4 * D_A + LORA * d:4 * D_A + LORA * (d + 1)]
        a = _sigmoid(a0_ref[d:d + 1] + _mm(lo, a2p_ref[d]))
        keffs.append(k * (1.0 + (a - 1.0) * ka_ref[...]))
    kmean = 0.5 * (keffs[0] + keffs[1])
    bonus = seg_sum(r * kmean * rk_ref[...]) * v
    out_a = (yn + bonus) * _silu(za)

    o = of_ref[0] + ob_ref[0]
    og = og_ref[...]
    zb = zb_ref[0]
    outs = [out_a]
    for h in range(D_B // HEAD_B):
        sl = slice(HEAD_B * h, HEAD_B * (h + 1))
        oh = o[:, sl]
        oh = oh * lax.rsqrt(jnp.mean(oh * oh, axis=-1, keepdims=True) + EPS) * og[:, sl]
        outs.append(oh * _silu(zb[:, sl]))
    mix = jnp.concatenate(outs, axis=1)
    proj = _dot(mix.astype(BF16), wout_ref[...])
    hs = x_ref[0] + gate_ref[0] * proj
    out_ref[0] = hs * lax.rsqrt(jnp.mean(hs * hs, axis=-1, keepdims=True) + EPS) * fg_ref[...]


def _out_call(x, pa, pb, yf, yb, of, ob, gate, a0, a2p, k_a, r_k, lnx_w, lnx_b, onorm_g, w_out_bf16, final_g):
    bsz, seq, _ = x.shape
    tm = min(OUT_ROWS, seq)
    tok = lambda b, i: (b, i, 0)
    full = lambda shape: pl.BlockSpec(shape, lambda b, i: (0,) * len(shape))
    return pl.pallas_call(
        _out_kernel,
        grid=(bsz, seq // tm),
        in_specs=[pl.BlockSpec((1, tm, D_MODEL), tok),
                  pl.BlockSpec((1, tm, A_COLS), tok),
                  pl.BlockSpec((1, tm, D_B), lambda b, i: (b, i, 4)),
                  pl.BlockSpec((1, tm, D_A), tok), pl.BlockSpec((1, tm, D_A), tok),
                  pl.BlockSpec((1, tm, D_B), tok), pl.BlockSpec((1, tm, D_B), tok),
                  pl.BlockSpec((1, 1, D_MODEL), lambda b, i: (b, 0, 0)),
                  full((2, D_A)), full((2, LORA, D_A)), full((1, D_A)), full((1, D_A)),
                  full((1, D_A)), full((1, D_A)), full((1, D_B)),
                  full((D_MODEL, D_MODEL)), full((1, D_MODEL))],
        out_specs=pl.BlockSpec((1, tm, D_MODEL), tok),
        out_shape=jax.ShapeDtypeStruct((bsz, seq, D_MODEL), F32),
        compiler_params=pltpu.CompilerParams(dimension_semantics=("arbitrary", "arbitrary"),
                                             vmem_limit_bytes=VMEM_LIMIT),
        name="out",
    )(x, pa, pb, yf, yb, of, ob, gate, a0, a2p, k_a, r_k, lnx_w, lnx_b, onorm_g, w_out_bf16, final_g)


def kernel(x_prompt, x_sample, state_rwkv, state_hgrn, c, c_ctx, norm_g, w_ada, b_ada, w_in, mu_h, mu_v, w0, w2,
           a0, a2, k_k, k_a, r_k, lnx_w, lnx_b, lb_logits, onorm_g, w_out, final_g):
    l = 0
    bp = x_prompt.shape[0]
    bs = x_sample.shape[0]
    lb_all = jnp.cumsum(jax.nn.softmax(lb_logits.astype(F32), axis=0), axis=0)
    lb = lb_all[l]

    wa_bf = w_in[l, :, :A_COLS].astype(BF16)
    wb_bf = w_in[l, :, A_COLS:].astype(BF16)
    w_out_bf = w_out[l].astype(BF16)
    zeros = jnp.zeros((2, LORA // 2, D_A), F32)
    wa2 = jnp.concatenate([jnp.concatenate([w2[l], zeros], axis=2),
                           jnp.concatenate([zeros, a2[l]], axis=2)], axis=1)
    a2p = jnp.concatenate([zeros, a2[l]], axis=1)
    row = lambda z: z.reshape(1, -1)

    cc = jnp.concatenate([c_ctx[None, :], c, jnp.zeros((16 - 1 - bs, D_MODEL), F32)], axis=0)
    m = _ada_call(cc, w_ada[l], row(b_ada[l]))
    shift, scale, gate = m[:, :D_MODEL], m[:, D_MODEL:2 * D_MODEL], m[:, 2 * D_MODEL:]
    ctx = lambda z: jnp.broadcast_to(z[0:1, None, :], (bp, 1, D_MODEL))
    lat = lambda z: z[1:1 + bs, None, :]

    def path(x, sh, sc, gt, states, grid_shift):
        pa, pb = _proj_call(x, sh, sc, row(norm_g[l]), wa_bf, wb_bf, row(mu_h[l]), row(mu_v[l]), grid_shift)
        res = _scan_call(pa, pb, states, w0[l], a0[l], wa2, row(k_k[l]), row(k_a[l]), lb)
        yf, yb, of, ob = res[:4]
        y = _out_call(x, pa, pb, yf, yb, of, ob, gt, a0[l], a2p, row(k_a[l]), row(r_k[l]), row(lnx_w[l]),
                      row(lnx_b[l]), row(onorm_g[l]), w_out_bf, row(final_g))
        return y, res[4:]

    y_prompt, (s_a, s_b) = path(x_prompt, ctx(shift), ctx(scale), ctx(gate), None, False)
    groups_a = D_A // PACK
    sa0 = state_rwkv[:, l].reshape(bs, 2, groups_a, PACK, HEAD_A)
    y_sample, _ = path(x_sample, lat(shift), lat(scale), lat(gate), (sa0, state_hgrn[:, l]), True)
    s_a = s_a.reshape(bp, 1, 2, D_A // HEAD_A, HEAD_A, HEAD_A)
    return y_prompt, y_sample, s_a, s_b[:, None]
```

```python
import functools

import jax
import jax.numpy as jnp
from jax import lax
from jax.experimental import pallas as pl
from jax.experimental.pallas import tpu as pltpu

F32 = jnp.float32
BF16 = jnp.bfloat16

D_MODEL = 1024
D_A = 512
D_B = 512
HEAD_A = 64
HEAD_B = 128
LOG2_HEAD_A = HEAD_A.bit_length() - 1
LOG2_HEAD_B = HEAD_B.bit_length() - 1
LORA = 128
A_COLS = 4 * D_A + 2 * LORA
B_COLS = 5 * D_B
GRID_W = 64
CHUNK = 64
LOG2_CHUNK = CHUNK.bit_length() - 1
SCAN_BATCH = 4
SUM_TERMS = 2
BLOCK_ROWS = 32
BLOCK_LANES = 128
EPS = 1e-6
GN_EPS = 64e-5
DECAY_SCALE = 0.6065306597126334
NEG_LOG2E = -1.4426950408889634
PACK = 256
TNA = 768
TNB = 512
NA = A_COLS // TNA
NB = B_COLS // TNB
PROJ_ROWS = 2048
OUT_ROWS = 512
VMEM_LIMIT = 56 * 1024 * 1024

NN = ((1,), (0,))
NT = ((1,), (1,))
TN_DIMS = ((0,), (0,))


def _dot(a, b, dims=NN):
    return lax.dot_general(a, b, (dims, ((), ())), preferred_element_type=F32)


def _split(x, passes=1):
    if isinstance(x, tuple):
        return x
    hi = x.astype(BF16)
    if passes == 1:
        return (hi,)
    return hi, (x - hi.astype(F32)).astype(BF16)


def _mm(a, b, dims=NN, passes=1):
    a = _split(a, passes)
    b = _split(b, passes)
    if len(a) == 1 or len(b) == 1:
        return _dot(a[0], b[0], dims)
    return _dot(a[0], b[0], dims) + (_dot(a[0], b[1], dims) + _dot(a[1], b[0], dims))


def _terms(x, n):
    out = []
    for _ in range(n - 1):
        h = x.astype(BF16)
        out.append(h)
        x = x - h.astype(F32)
    out.append(x.astype(BF16))
    return out


def _sum_small_first(parts):
    return functools.reduce(lambda acc, z: z + acc, reversed(parts))


def _mm_exact_rhs(a, b_bf16, n=3):
    return _sum_small_first([_dot(t, b_bf16) for t in _terms(a, n)])


def _exp_neg(x):
    return jnp.exp2(x * NEG_LOG2E)


def _sigmoid(x):
    return 0.5 * jnp.tanh(0.5 * x) + 0.5


def _silu(x):
    return x * _sigmoid(x)


def _iota(shape, dim):
    return lax.broadcasted_iota(jnp.int32, shape, dim)


def _block_mask(rows, cols, row_shift, col_shift):
    same = (_iota((rows, cols), 0) >> row_shift) == (_iota((rows, cols), 1) >> col_shift)
    return jnp.where(same, 1.0, 0.0).astype(F32)


def _head_mask_a():
    return _block_mask(PACK, PACK, LOG2_HEAD_A, LOG2_HEAD_A)


def _tile_rows(x, n):
    if isinstance(x, tuple):
        return tuple(_tile_rows(t, n) for t in x)
    return jnp.concatenate([x] * n, axis=0)


def _masked(x, mask_bf16):
    if isinstance(x, tuple):
        return tuple(_masked(t, mask_bf16) for t in x)
    return x * mask_bf16


def _ada_kernel(c_ref, w_ref, b_ref, m_ref):
    m_ref[...] = _mm(_silu(c_ref[...]), w_ref[...], passes=3) + b_ref[...]


def _ada_call(cc, w_ada, b_ada):
    rows = cc.shape[0]
    return pl.pallas_call(
        _ada_kernel,
        grid=(3,),
        in_specs=[pl.BlockSpec((rows, D_MODEL), lambda j: (0, 0)),
                  pl.BlockSpec((D_MODEL, D_MODEL), lambda j: (0, j)),
                  pl.BlockSpec((1, D_MODEL), lambda j: (0, j))],
        out_specs=pl.BlockSpec((rows, D_MODEL), lambda j: (0, j)),
        out_shape=jax.ShapeDtypeStruct((rows, 3 * D_MODEL), F32),
        compiler_params=pltpu.CompilerParams(dimension_semantics=("arbitrary",),
                                             vmem_limit_bytes=VMEM_LIMIT),
        name="ada",
    )(cc, w_ada, b_ada)


def _proj_kernel(x_ref, sh_ref, sc_ref, g_ref, wa_ref, wb_ref, muh_ref, muv_ref, pa_ref, pb_ref, xm_ref, *,
                 nbp, seq, grid_shift):
    j = pl.program_id(1)
    rows = nbp * seq

    @pl.when(j == 0)
    def _():
        for b in range(nbp):
            x = x_ref[b]
            xn = x * lax.rsqrt(jnp.mean(x * x, axis=-1, keepdims=True) + EPS) * g_ref[...]
            xm_ref[b * seq:(b + 1) * seq, :] = (xn * (1.0 + sc_ref[b]) + sh_ref[b]).astype(BF16)

    @pl.when(j < NA)
    def _():
        p = _dot(xm_ref[...], wa_ref[...])
        w = GRID_W if grid_shift else seq
        nblk = rows // w
        pos = _iota((1, w, TNA), 1)
        to3 = lambda z: z.reshape(nblk, w, TNA)
        prev = jnp.where(pos == 0, 0.0, to3(pltpu.roll(p, 1, axis=0)))
        nxt = jnp.where(pos == w - 1, 0.0, to3(pltpu.roll(p, rows - 1, axis=0)))
        muh = muh_ref[...]
        out = (0.5 * muh) * (prev + nxt)
        if grid_shift:
            muv = muv_ref[...]
            p3 = to3(p)
            zblk = jnp.zeros((1, w, TNA), F32)
            up = jnp.concatenate([zblk, p3[:nblk - 1]], axis=0)
            dn = jnp.concatenate([p3[1:], zblk], axis=0)
            out = (1.0 - muh - muv) * p3 + out + (0.5 * muv) * (up + dn)
        else:
            out = (1.0 - muh) * to3(p) + out
        pa_ref[...] = out.reshape(nbp, seq, TNA)

    @pl.when(j >= NA)
    def _():
        pb_ref[...] = _dot(xm_ref[...], wb_ref[...]).reshape(nbp, seq, TNB)


def _proj_call(x, shift, scale, norm_g, wa_bf16, wb_bf16, mu_h, mu_v, grid_shift):
    bsz, seq, _ = x.shape
    nbp = 1 if grid_shift else PROJ_ROWS // seq
    kern = functools.partial(_proj_kernel, nbp=nbp, seq=seq, grid_shift=grid_shift)
    a_idx = lambda j: jnp.minimum(j, NA - 1)
    b_idx = lambda j: jnp.maximum(j - NA, 0)
    return pl.pallas_call(
        kern,
        grid=(bsz // nbp, NA + NB),
        in_specs=[pl.BlockSpec((nbp, seq, D_MODEL), lambda b, j: (b, 0, 0)),
                  pl.BlockSpec((nbp, 1, D_MODEL), lambda b, j: (b, 0, 0)),
                  pl.BlockSpec((nbp, 1, D_MODEL), lambda b, j: (b, 0, 0)),
                  pl.BlockSpec((1, D_MODEL), lambda b, j: (0, 0)),
                  pl.BlockSpec((D_MODEL, TNA), lambda b, j: (0, a_idx(j))),
                  pl.BlockSpec((D_MODEL, TNB), lambda b, j: (0, b_idx(j))),
                  pl.BlockSpec((1, TNA), lambda b, j: (0, a_idx(j))),
                  pl.BlockSpec((1, TNA), lambda b, j: (0, a_idx(j)))],
        out_specs=[pl.BlockSpec((nbp, seq, TNA), lambda b, j: (b, 0, a_idx(j))),
                   pl.BlockSpec((nbp, seq, TNB), lambda b, j: (b, 0, b_idx(j)))],
        out_shape=[jax.ShapeDtypeStruct((bsz, seq, A_COLS), F32),
                   jax.ShapeDtypeStruct((bsz, seq, B_COLS), F32)],
        scratch_shapes=[pltpu.VMEM((nbp * seq, D_MODEL), BF16)],
        compiler_params=pltpu.CompilerParams(dimension_semantics=("arbitrary", "arbitrary"),
                                             vmem_limit_bytes=VMEM_LIMIT),
        name="proj_grid" if grid_shift else "proj_seq",
    )(x, shift, scale, norm_g, wa_bf16, wb_bf16, mu_h, mu_v)


def _scan_masks():
    c = CHUNK
    t = _iota((c, PACK), 0)
    s = _iota((c, PACK), 1) & (c - 1)
    tt = _iota((c, c), 0)
    ss = _iota((c, c), 1)
    f01 = lambda cond: jnp.where(cond, 1.0, 0.0).astype(F32)
    bm64 = _head_mask_a()
    bm128 = _block_mask(PACK, PACK, LOG2_HEAD_B, LOG2_HEAD_B)
    return dict(
        bm64=bm64, bm64_bf=bm64.astype(BF16), bm128=bm128,
        bm_hg_bf=_block_mask(4 * c, D_B, LOG2_CHUNK, LOG2_HEAD_B).astype(BF16),
        eye=f01(t == s),
        incl=(f01(t >= s), f01(t <= s)),
        strict=(f01(t > s), f01(t < s)),
        tri=(f01(tt >= ss).astype(BF16), f01(tt <= ss).astype(BF16)),
    )


_MASK_KEYS = ("bm64", "bm64_bf", "bm128", "bm_hg_bf", "eye", "incl", "strict", "tri")
_MASK_SCRATCH = (((PACK, PACK), F32), ((PACK, PACK), BF16), ((PACK, PACK), F32), ((4 * CHUNK, D_B), BF16),
                 ((CHUNK, PACK), F32), ((2, CHUNK, PACK), F32), ((2, CHUNK, PACK), F32), ((2, CHUNK, CHUNK), BF16))


def _store_masks(mask_refs):
    masks = _scan_masks()
    for ref, key in zip(mask_refs, _MASK_KEYS):
        val = masks[key]
        if isinstance(val, tuple):
            for i, z in enumerate(val):
                ref[i] = z
        else:
            ref[...] = val


class _Masks:
    def __init__(self, mask_refs):
        self._refs = dict(zip(_MASK_KEYS, mask_refs))

    def __getitem__(self, key):
        ref = self._refs[key]
        return (ref[0], ref[1]) if len(ref.shape) == 3 else ref[...]


def _lockstep(gens):
    results = [None] * len(gens)
    live = list(enumerate(gens))
    while live:
        still = []
        for i, g in live:
            try:
                next(g)
                still.append((i, g))
            except StopIteration as stop:
                results[i] = stop.value
        live = still
    return results


def _rwkv_group(rt, kt, kh, bh, kb, bb, v, s_ref, egc, mk, d):
    c = CHUNK
    bd = lambda x: _masked(_tile_rows(_split(x), PACK // c), mk["bm64_bf"])
    kr = _split(jnp.concatenate([kt, rt], axis=0))
    aa_k = _mm(kr, bd(kh), NT)
    aa_b = _mm(kr, bd(bh), NT)
    s_bd = s_ref[...]
    krs = _mm(kr, s_bd, NT)
    yield
    a_kk = aa_k[:c] * mk["strict"][d]
    a_rk = aa_k[c:] * mk["incl"][d]
    n = -(aa_b[:c] * mk["strict"][d])
    a_rb = aa_b[c:] * mk["incl"][d]
    x = mk["eye"] + n
    p = _mm(n, bd(n))
    akv = _mm(jnp.concatenate([a_kk, a_rk], axis=0), bd(v))
    yield
    for i in range(5):
        if i < 4:
            xp = _mm(jnp.concatenate([x, p], axis=0), bd(p))
            x = x + xp[:c]
            p = xp[c:]
        else:
            x = x + _mm(x, bd(p))
        yield
    u = _mm(x, bd(krs[:c] + akv[:c]))
    yield
    y = krs[c:] + akv[c:] - _mm(a_rb, bd(u))
    vu = jnp.concatenate([v, (-u).astype(BF16)], axis=0)
    kbb = jnp.concatenate([kb, bb], axis=0)
    s_ref[...] = (s_bd * egc + _mm(vu, kbb, TN_DIMS)) * mk["bm64"]
    return y


def _hgrn_dir(qn, kn, qt, kbh, vb, eghc, s_refs, mk, d):
    bm_hg = mk["bm_hg_bf"]
    heads = D_B // HEAD_B
    a_p = _mm(qn, _masked(_tile_rows(_split(kn), heads), bm_hg), NT) * mk["incl"][d]
    inter = []
    for grp, s_ref in enumerate(s_refs):
        sl = slice(PACK * grp, PACK * (grp + 1))
        st = s_ref[...]
        inter.append(_mm(qt[:, sl], st, NT))
        s_ref[...] = (st * eghc[:, sl] + _mm(vb[:, sl], kbh[:, sl], TN_DIMS)) * mk["bm128"]
    yield
    o = _mm(a_p, _masked(_tile_rows(_split(vb), heads), bm_hg))
    return o + jnp.concatenate(inter, axis=1)


def _blockwise(fn, operands, n_out, rows=CHUNK, width=D_A):
    cols = []
    for l0 in range(0, width, BLOCK_LANES):
        parts = []
        for r0 in range(0, rows, BLOCK_ROWS):
            blk = [z[(slice(None) if z.shape[0] == 1 else slice(r0, r0 + BLOCK_ROWS)), l0:l0 + BLOCK_LANES]
                   for z in operands]
            parts.append(fn(*blk))
        cols.append([jnp.concatenate([p[i] for p in parts], axis=0) for i in range(n_out)])
    return [jnp.concatenate([col[i] for col in cols], axis=1) for i in range(n_out)]


def _chunk_operands(pa_ref, pb_ref, bi, d, prm, mk):
    c = CHUNK
    n_t = SUM_TERMS
    w0, a0, wa2, k_k, k_a, lb = prm
    cols = lambda ref, i, width=D_A: ref.at[bi, :, i * width:(i + 1) * width]
    r_ref, k_ref, v_ref = cols(pa_ref, 0), cols(pa_ref, 1), cols(pa_ref, 2)
    lo = pa_ref[bi, :, 4 * D_A + LORA * d:4 * D_A + LORA * (d + 1)]
    lo = jnp.where(_iota((c, LORA), 1) < LORA // 2, jnp.tanh(lo), lo)
    wa = _mm(lo, wa2[d])
    yield

    def gates(wa_w, wa_a, k, fr, w0_, a0_, kk_, lb_):
        lw = -DECAY_SCALE * _sigmoid(w0_ + wa_w)
        a = _sigmoid(a0_ + wa_a)
        kk = k * kk_
        f = lb_ + (1.0 - lb_) * _sigmoid(fr)
        gf = jnp.log(f)
        return (lw, a, kk, 1.0 - f, (kk * kk).astype(BF16), *_terms(lw, n_t), *_terms(gf, n_t))

    res = _blockwise(gates, [wa[:, :D_A], wa[:, D_A:], k_ref, cols(pb_ref, 2 + d, D_B), w0[d:d + 1], a0[d:d + 1],
                             k_k, lb[d:d + 1]], 5 + 2 * n_t)
    lw, a, kk, kf, sq = res[:5]
    lw_t, gf_t = res[5:5 + n_t], res[5 + n_t:]
    seg = lambda lanes: _dot(sq[:, lanes], mk["bm64_bf"])
    n2 = jnp.concatenate([seg(slice(0, PACK)), seg(slice(PACK, D_A))], axis=1)
    gcum = _sum_small_first([_dot(mk["tri"][d], jnp.concatenate([tl, tg], axis=1))
                             for tl, tg in zip(lw_t, gf_t)])
    yield
    last = c - 1 if d == 0 else 0
    g, gh = gcum[:, :D_A], gcum[:, D_A:]
    gc, ghc, ghm = g[last:last + 1], gh[last:last + 1], gh[c // 2:c // 2 + 1]
    egc, eghc = jnp.exp(gc), jnp.exp(ghc)

    def rwkv_operands(r, k, kk_, n2_, a_, g_, lw_, ka_, egc_):
        kap = kk_ * lax.rsqrt(jnp.maximum(n2_, 1e-24))
        keff = k * (1.0 + (a_ - 1.0) * ka_)
        eng = _exp_neg(g_)
        kh = keff * eng
        bh = (kap * a_) * eng
        out = (r * jnp.exp(g_), kap * jnp.exp(g_ - lw_), kh, bh, kh * egc_, bh * egc_)
        return tuple(z.astype(BF16) for z in out)

    rt, kt, kh, bh, kb, bb = _blockwise(rwkv_operands, [r_ref, k_ref, kk, n2, a, g, lw, k_a, egc], 6)

    def hgrn_operands(qr, kf_, gh_, ghm_, e_m, e_cm):
        q = _silu(qr)
        ghn = gh_ - ghm_
        qn = q * jnp.exp(ghn)
        kn = kf_ * _exp_neg(ghn)
        return tuple(z.astype(BF16) for z in (qn, kn, qn * e_m, kn * e_cm))

    qn, kn, qt, kbh = _blockwise(hgrn_operands, [cols(pb_ref, 0, D_B), kf, gh, ghm, jnp.exp(ghm),
                                                 jnp.exp(ghc - ghm)], 4, width=D_B)
    v = v_ref[...].astype(BF16)
    vb = pb_ref[bi, :, D_B:2 * D_B].astype(BF16)
    return (rt, kt, kh, bh, kb, bb, v, qn, kn, qt, kbh, vb), (egc, eghc)


def _scan_dir(pa_ref, pb_ref, bi, d, prm, sa_scr, sb_scr, mk):
    ops, (egc, eghc) = yield from _chunk_operands(pa_ref, pb_ref, bi, d, prm, mk)
    rt, kt, kh, bh, kb, bb, v, qn, kn, qt, kbh, vb = ops
    chains = []
    for grp in range(D_A // PACK):
        sl = slice(PACK * grp, PACK * (grp + 1))
        chains.append(_rwkv_group(rt[:, sl], kt[:, sl], kh[:, sl], bh[:, sl], kb[:, sl], bb[:, sl], v[:, sl],
                                  sa_scr.at[bi, d, grp], egc[:, sl], mk, d))
    chains.append(_hgrn_dir(qn, kn, qt, kbh, vb, eghc, [sb_scr.at[bi, d, grp] for grp in range(D_B // PACK)],
                            mk, d))
    results = [None] * len(chains)
    live = list(enumerate(chains))
    while live:
        still = []
        for i, g in live:
            try:
                next(g)
                still.append((i, g))
            except StopIteration as stop:
                results[i] = stop.value
        live = still
        yield
    return jnp.concatenate(results[:-1], axis=1), results[-1]


def _state_slots():
    return [(bi, d, grp) for bi in range(SCAN_BATCH) for d in range(2) for grp in range(D_A // PACK)]


def _load_states(sa0_ref, sb0_ref, sa_scr, sb_scr):
    spread = jnp.where(_iota((HEAD_A, PACK), 0) == (_iota((HEAD_A, PACK), 1) & (HEAD_A - 1)), 1.0, 0.0)
    spread = spread.astype(BF16)
    bm64 = _head_mask_a()
    zero = jnp.zeros((HEAD_B, HEAD_B), F32)
    for bi, d, grp in _state_slots():
        sa_scr[bi, d, grp] = _mm_exact_rhs(sa0_ref[bi, d, grp], spread) * bm64
        h0 = sb0_ref[bi, d, 2 * grp].T
        h1 = sb0_ref[bi, d, 2 * grp + 1].T
        sb_scr[bi, d, grp] = jnp.concatenate([jnp.concatenate([h0, zero], axis=1),
                                              jnp.concatenate([zero, h1], axis=1)], axis=0)


def _store_states(sa_scr, sb_scr, sa_out, sb_out):
    gather = jnp.where((_iota((PACK, HEAD_A), 0) & (HEAD_A - 1)) == _iota((PACK, HEAD_A), 1), 1.0, 0.0)
    gather = gather.astype(BF16)
    for bi, d, grp in _state_slots():
        sa_out[bi, d, grp] = _mm_exact_rhs(sa_scr[bi, d, grp], gather)
        for hh in range(PACK // HEAD_B):
            blk = sb_scr[bi, d, grp, HEAD_B * hh:HEAD_B * (hh + 1), HEAD_B * hh:HEAD_B * (hh + 1)]
            sb_out[bi, d, 2 * grp + hh] = blk.T


def _scan_kernel(paf_ref, pbf_ref, pab_ref, pbb_ref, *rest, from_zero):
    rest, mask_refs = rest[:-len(_MASK_KEYS)], rest[-len(_MASK_KEYS):]
    if from_zero:
        (w0_ref, a0_ref, wa2_ref, kk_ref, ka_ref, lb_ref, yf_ref, yb_ref, of_ref, ob_ref, sa_out, sb_out,
         sa_scr, sb_scr) = rest
    else:
        (sa0_ref, sb0_ref, w0_ref, a0_ref, wa2_ref, kk_ref, ka_ref, lb_ref, yf_ref, yb_ref, of_ref, ob_ref,
         sa_scr, sb_scr) = rest
    ci = pl.program_id(1)

    @pl.when(ci == 0)
    def _():
        _store_masks(mask_refs)
        if from_zero:
            sa_scr[...] = jnp.zeros(sa_scr.shape, F32)
            sb_scr[...] = jnp.zeros(sb_scr.shape, F32)
        else:
            _load_states(sa0_ref, sb0_ref, sa_scr, sb_scr)

    mk = _Masks(mask_refs)
    prm = (w0_ref[...], a0_ref[...], wa2_ref, kk_ref[...], ka_ref[...], lb_ref[...])
    p_refs = ((paf_ref, pbf_ref), (pab_ref, pbb_ref))
    runs = [(bi, d) for bi in range(SCAN_BATCH) for d in range(2)]
    outs = _lockstep([_scan_dir(p_refs[d][0], p_refs[d][1], bi, d, prm, sa_scr, sb_scr, mk) for bi, d in runs])
    for (bi, d), (y, o) in zip(runs, outs):
        (yf_ref, yb_ref)[d][bi] = y
        (of_ref, ob_ref)[d][bi] = o

    if from_zero:
        @pl.when(ci == pl.num_programs(1) - 1)
        def _():
            _store_states(sa_scr, sb_scr, sa_out, sb_out)


def _scan_call(pa, pb, states, w0, a0, wa2, k_k, k_a, lb):
    bsz, seq, _ = pa.shape
    nc = seq // CHUNK
    nb = SCAN_BATCH
    from_zero = states is None
    fwd = lambda b, c: (b, c, 0)
    bwd = lambda b, c: (b, nc - 1 - c, 0)
    st = lambda b, c: (b, 0, 0, 0, 0)
    full = lambda shape: pl.BlockSpec(shape, lambda b, c: (0,) * len(shape))
    sa_block = (nb, 2, D_A // PACK, PACK, HEAD_A)
    sb_block = (nb, 2, D_B // HEAD_B, HEAD_B, HEAD_B)
    y_shape = jax.ShapeDtypeStruct((bsz, seq, D_A), F32)
    out_specs = [pl.BlockSpec((nb, CHUNK, D_A), fwd), pl.BlockSpec((nb, CHUNK, D_A), bwd),
                 pl.BlockSpec((nb, CHUNK, D_B), fwd), pl.BlockSpec((nb, CHUNK, D_B), bwd)]
    out_shape = [y_shape, y_shape, y_shape, y_shape]
    in_specs = [pl.BlockSpec((nb, CHUNK, A_COLS), fwd), pl.BlockSpec((nb, CHUNK, B_COLS), fwd),
                pl.BlockSpec((nb, CHUNK, A_COLS), bwd), pl.BlockSpec((nb, CHUNK, B_COLS), bwd)]
    args = [pa, pb, pa, pb]
    if from_zero:
        out_specs += [pl.BlockSpec(sa_block, st), pl.BlockSpec(sb_block, st)]
        out_shape += [jax.ShapeDtypeStruct((bsz,) + sa_block[1:], F32),
                      jax.ShapeDtypeStruct((bsz,) + sb_block[1:], F32)]
    else:
        in_specs += [pl.BlockSpec(sa_block, st), pl.BlockSpec(sb_block, st)]
        args += list(states)
    in_specs += [full((2, D_A)), full((2, D_A)), full((2, LORA, 2 * D_A)),
                 full((1, D_A)), full((1, D_A)), full((2, D_B))]
    args += [w0, a0, wa2, k_k, k_a, lb]
    bd_block = (nb, 2, D_A // PACK, PACK, PACK)
    return pl.pallas_call(
        functools.partial(_scan_kernel, from_zero=from_zero),
        grid=(bsz // nb, nc),
        in_specs=in_specs,
        out_specs=out_specs,
        out_shape=out_shape,
        scratch_shapes=[pltpu.VMEM(bd_block, F32), pltpu.VMEM(bd_block, F32)]
                       + [pltpu.VMEM(shape, dtype) for shape, dtype in _MASK_SCRATCH],
        compiler_params=pltpu.CompilerParams(dimension_semantics=("arbitrary", "arbitrary"),
                                             vmem_limit_bytes=VMEM_LIMIT),
        name="scan_state" if from_zero else "scan",
    )(*args)


def _out_kernel(x_ref, pa_ref, zb_ref, yf_ref, yb_ref, of_ref, ob_ref, gate_ref, a0_ref, a2p_ref, ka_ref,
                rk_ref, lnw_ref, lnb_ref, og_ref, wout_ref, fg_ref, out_ref):
    pa = pa_ref[0]
    r = pa[:, 0:D_A]
    k = pa[:, D_A:2 * D_A]
    v = pa[:, 2 * D_A:3 * D_A]
    za = pa[:, 3 * D_A:4 * D_A]
    bm64_bf = _head_mask_a().astype(BF16)

    def seg_sum(z):
        zb16 = z.astype(BF16)
        return jnp.concatenate([_dot(zb16[:, :PACK], bm64_bf), _dot(zb16[:, PACK:], bm64_bf)], axis=1)

    y = yf_ref[0] + yb_ref[0]
    mu = seg_sum(y) * (1.0 / HEAD_A)
    dlt = y - mu
    var = seg_sum(dlt * dlt) * (1.0 / HEAD_A)
    yn = dlt * lax.rsqrt(var + GN_EPS) * lnw_ref[...] + lnb_ref[...]
    a_dirs = []
    for d in range(2):
        lo = pa[:, 4 * D_A + LORA * d:4 * D_A + LORA * (d + 1)]
        a_dirs.append(_sigmoid(a0_ref[d:d + 1] + _mm(lo, a2p_ref[d])))
    kmean = k * (1.0 + (0.5 * (a_dirs[0] + a_dirs[1]) - 1.0) * ka_ref[...])
    bonus = seg_sum(r * kmean * rk_ref[...]) * v
    out_a = (yn + bonus) * _silu(za)

    o = of_ref[0] + ob_ref[0]
    og = og_ref[...]
    zb = zb_ref[0]
    outs = [out_a]
    for h in range(D_B // HEAD_B):
        sl = slice(HEAD_B * h, HEAD_B * (h + 1))
        oh = o[:, sl]
        oh = oh * lax.rsqrt(jnp.mean(oh * oh, axis=-1, keepdims=True) + EPS) * og[:, sl]
        outs.append(oh * _silu(zb[:, sl]))
    mix = jnp.concatenate(outs, axis=1)
    proj = _dot(mix.astype(BF16), wout_ref[...])
    hs = x_ref[0] + gate_ref[0] * proj
    out_ref[0] = hs * lax.rsqrt(jnp.mean(hs * hs, axis=-1, keepdims=True) + EPS) * fg_ref[...]


def _out_call(x, pa, pb, yf, yb, of, ob, gate, a0, a2p, k_a, r_k, lnx_w, lnx_b, onorm_g, w_out_bf16, final_g):
    bsz, seq, _ = x.shape
    tm = min(OUT_ROWS, seq)
    tok = lambda b, i: (b, i, 0)
    full = lambda shape: pl.BlockSpec(shape, lambda b, i: (0,) * len(shape))
    return pl.pallas_call(
        _out_kernel,
        grid=(bsz, seq // tm),
        in_specs=[pl.BlockSpec((1, tm, D_MODEL), tok),
                  pl.BlockSpec((1, tm, A_COLS), tok),
                  pl.BlockSpec((1, tm, D_B), lambda b, i: (b, i, 4)),
                  pl.BlockSpec((1, tm, D_A), tok), pl.BlockSpec((1, tm, D_A), tok),
                  pl.BlockSpec((1, tm, D_B), tok), pl.BlockSpec((1, tm, D_B), tok),
                  pl.BlockSpec((1, 1, D_MODEL), lambda b, i: (b, 0, 0)),
                  full((2, D_A)), full((2, LORA, D_A)), full((1, D_A)), full((1, D_A)),
                  full((1, D_A)), full((1, D_A)), full((1, D_B)),
                  full((D_MODEL, D_MODEL)), full((1, D_MODEL))],
        out_specs=pl.BlockSpec((1, tm, D_MODEL), tok),
        out_shape=jax.ShapeDtypeStruct((bsz, seq, D_MODEL), F32),
        compiler_params=pltpu.CompilerParams(dimension_semantics=("arbitrary", "arbitrary"),
                                             vmem_limit_bytes=VMEM_LIMIT),
        name="out",
    )(x, pa, pb, yf, yb, of, ob, gate, a0, a2p, k_a, r_k, lnx_w, lnx_b, onorm_g, w_out_bf16, final_g)


def kernel(x_prompt, x_sample, state_rwkv, state_hgrn, c, c_ctx, norm_g, w_ada, b_ada, w_in, mu_h, mu_v, w0, w2,
           a0, a2, k_k, k_a, r_k, lnx_w, lnx_b, lb_logits, onorm_g, w_out, final_g):
    l = 0
    bp = x_prompt.shape[0]
    bs = x_sample.shape[0]
    lb_all = jnp.cumsum(jax.nn.softmax(lb_logits.astype(F32), axis=0), axis=0)
    lb = lb_all[l]

    wa_bf = w_in[l, :, :A_COLS].astype(BF16)
    wb_bf = w_in[l, :, A_COLS:].astype(BF16)
    w_out_bf = w_out[l].astype(BF16)
    zeros = jnp.zeros((2, LORA // 2, D_A), F32)
    wa2 = jnp.concatenate([jnp.concatenate([w2[l], zeros], axis=2),
                           jnp.concatenate([zeros, a2[l]], axis=2)], axis=1)
    a2p = jnp.concatenate([zeros, a2[l]], axis=1)
    row = lambda z: z.reshape(1, -1)

    cc = jnp.concatenate([c_ctx[None, :], c, jnp.zeros((16 - 1 - bs, D_MODEL), F32)], axis=0)
    m = _ada_call(cc, w_ada[l], row(b_ada[l]))
    shift, scale, gate = m[:, :D_MODEL], m[:, D_MODEL:2 * D_MODEL], m[:, 2 * D_MODEL:]
    ctx = lambda z: jnp.broadcast_to(z[0:1, None, :], (bp, 1, D_MODEL))
    lat = lambda z: z[1:1 + bs, None, :]

    def path(x, sh, sc, gt, states, grid_shift):
        pa, pb = _proj_call(x, sh, sc, row(norm_g[l]), wa_bf, wb_bf, row(mu_h[l]), row(mu_v[l]), grid_shift)
        res = _scan_call(pa, pb, states, w0[l], a0[l], wa2, row(k_k[l]), row(k_a[l]), lb)
        yf, yb, of, ob = res[:4]
        y = _out_call(x, pa, pb, yf, yb, of, ob, gt, a0[l], a2p, row(k_a[l]), row(r_k[l]), row(lnx_w[l]),
                      row(lnx_b[l]), row(onorm_g[l]), w_out_bf, row(final_g))
        return y, res[4:]

    y_prompt, (s_a, s_b) = path(x_prompt, ctx(shift), ctx(scale), ctx(gate), None, False)
    groups_a = D_A // PACK
    sa0 = state_rwkv[:, l].reshape(bs, 2, groups_a, PACK, HEAD_A)
    y_sample, _ = path(x_sample, lat(shift), lat(scale), lat(gate), (sa0, state_hgrn[:, l]), True)
    s_a = s_a.reshape(bp, 1, 2, D_A // HEAD_A, HEAD_A, HEAD_A)
    return y_prompt, y_sample, s_a, s_b[:, None]
```

```python
import functools

import jax
import jax.numpy as jnp
from jax import lax
from jax.experimental import pallas as pl
from jax.experimental.pallas import tpu as pltpu

F32 = jnp.float32
BF16 = jnp.bfloat16

D_MODEL = 1024
D_A = 512
D_B = 512
HEAD_A = 64
HEAD_B = 128
LOG2_HEAD_A = HEAD_A.bit_length() - 1
LORA = 128
A_COLS = 4 * D_A + 2 * LORA
B_COLS = 5 * D_B
GRID_W = 64
CHUNK = 64
SCAN_BATCH = 4
SUM_TERMS = 2
BLOCK_ROWS = 32
BLOCK_LANES = 128
EPS = 1e-6
GN_EPS = 64e-5
DECAY_SCALE = 0.6065306597126334
NEG_LOG2E = -1.4426950408889634
PACK = 256
TNA = 768
TNB = 512
NA = A_COLS // TNA
NB = B_COLS // TNB
PROJ_ROWS = 2048
OUT_ROWS = 512
VMEM_LIMIT = 56 * 1024 * 1024

NN = ((1,), (0,))
NT = ((1,), (1,))
TN_DIMS = ((0,), (0,))


def _dot(a, b, dims=NN):
    return lax.dot_general(a, b, (dims, ((), ())), preferred_element_type=F32)


def _split(x, passes=1):
    if isinstance(x, tuple):
        return x
    hi = x.astype(BF16)
    if passes == 1:
        return (hi,)
    return hi, (x - hi.astype(F32)).astype(BF16)


def _mm(a, b, dims=NN, passes=1):
    a = _split(a, passes)
    b = _split(b, passes)
    if len(a) == 1 or len(b) == 1:
        return _dot(a[0], b[0], dims)
    return _dot(a[0], b[0], dims) + (_dot(a[0], b[1], dims) + _dot(a[1], b[0], dims))


def _terms(x, n):
    out = []
    for _ in range(n - 1):
        h = x.astype(BF16)
        out.append(h)
        x = x - h.astype(F32)
    out.append(x.astype(BF16))
    return out


def _sum_small_first(parts):
    return functools.reduce(lambda acc, z: z + acc, reversed(parts))


def _mm_exact_rhs(a, b_bf16, n=3):
    return _sum_small_first([_dot(t, b_bf16) for t in _terms(a, n)])


def _exp_neg(x):
    return jnp.exp2(x * NEG_LOG2E)


def _sigmoid(x):
    return 0.5 * jnp.tanh(0.5 * x) + 0.5


def _silu(x):
    return x * _sigmoid(x)


def _iota(shape, dim):
    return lax.broadcasted_iota(jnp.int32, shape, dim)


def _block_mask(rows, cols, row_shift, col_shift):
    same = (_iota((rows, cols), 0) >> row_shift) == (_iota((rows, cols), 1) >> col_shift)
    return jnp.where(same, 1.0, 0.0).astype(F32)


def _head_mask_a():
    return _block_mask(PACK, PACK, LOG2_HEAD_A, LOG2_HEAD_A)


def _ada_kernel(c_ref, w_ref, b_ref, m_ref):
    m_ref[...] = _mm(_silu(c_ref[...]), w_ref[...], passes=3) + b_ref[...]


def _ada_call(cc, w_ada, b_ada):
    rows = cc.shape[0]
    return pl.pallas_call(
        _ada_kernel,
        grid=(3,),
        in_specs=[pl.BlockSpec((rows, D_MODEL), lambda j: (0, 0)),
                  pl.BlockSpec((D_MODEL, D_MODEL), lambda j: (0, j)),
                  pl.BlockSpec((1, D_MODEL), lambda j: (0, j))],
        out_specs=pl.BlockSpec((rows, D_MODEL), lambda j: (0, j)),
        out_shape=jax.ShapeDtypeStruct((rows, 3 * D_MODEL), F32),
        compiler_params=pltpu.CompilerParams(dimension_semantics=("arbitrary",),
                                             vmem_limit_bytes=VMEM_LIMIT),
        name="ada",
    )(cc, w_ada, b_ada)


def _proj_kernel(x_ref, sh_ref, sc_ref, g_ref, wa_ref, wb_ref, muh_ref, muv_ref, pa_ref, pb_ref, xm_ref, *,
                 nbp, seq, grid_shift):
    j = pl.program_id(1)
    rows = nbp * seq

    @pl.when(j == 0)
    def _():
        for b in range(nbp):
            x = x_ref[b]
            xn = x * lax.rsqrt(jnp.mean(x * x, axis=-1, keepdims=True) + EPS) * g_ref[...]
            xm_ref[b * seq:(b + 1) * seq, :] = (xn * (1.0 + sc_ref[b]) + sh_ref[b]).astype(BF16)

    @pl.when(j < NA)
    def _():
        p = _dot(xm_ref[...], wa_ref[...])
        w = GRID_W if grid_shift else seq
        nblk = rows // w
        pos = _iota((1, w, TNA), 1)
        to3 = lambda z: z.reshape(nblk, w, TNA)
        prev = jnp.where(pos == 0, 0.0, to3(pltpu.roll(p, 1, axis=0)))
        nxt = jnp.where(pos == w - 1, 0.0, to3(pltpu.roll(p, rows - 1, axis=0)))
        muh = muh_ref[...]
        out = (0.5 * muh) * (prev + nxt)
        if grid_shift:
            muv = muv_ref[...]
            p3 = to3(p)
            zblk = jnp.zeros((1, w, TNA), F32)
            up = jnp.concatenate([zblk, p3[:nblk - 1]], axis=0)
            dn = jnp.concatenate([p3[1:], zblk], axis=0)
            out = (1.0 - muh - muv) * p3 + out + (0.5 * muv) * (up + dn)
        else:
            out = (1.0 - muh) * to3(p) + out
        pa_ref[...] = out.reshape(nbp, seq, TNA)

    @pl.when(j >= NA)
    def _():
        pb_ref[...] = _dot(xm_ref[...], wb_ref[...]).reshape(nbp, seq, TNB)


def _proj_call(x, shift, scale, norm_g, wa_bf16, wb_bf16, mu_h, mu_v, grid_shift):
    bsz, seq, _ = x.shape
    nbp = 1 if grid_shift else PROJ_ROWS // seq
    kern = functools.partial(_proj_kernel, nbp=nbp, seq=seq, grid_shift=grid_shift)
    a_idx = lambda j: jnp.minimum(j, NA - 1)
    b_idx = lambda j: jnp.maximum(j - NA, 0)
    return pl.pallas_call(
        kern,
        grid=(bsz // nbp, NA + NB),
        in_specs=[pl.BlockSpec((nbp, seq, D_MODEL), lambda b, j: (b, 0, 0)),
                  pl.BlockSpec((nbp, 1, D_MODEL), lambda b, j: (b, 0, 0)),
                  pl.BlockSpec((nbp, 1, D_MODEL), lambda b, j: (b, 0, 0)),
                  pl.BlockSpec((1, D_MODEL), lambda b, j: (0, 0)),
                  pl.BlockSpec((D_MODEL, TNA), lambda b, j: (0, a_idx(j))),
                  pl.BlockSpec((D_MODEL, TNB), lambda b, j: (0, b_idx(j))),
                  pl.BlockSpec((1, TNA), lambda b, j: (0, a_idx(j))),
                  pl.BlockSpec((1, TNA), lambda b, j: (0, a_idx(j)))],
        out_specs=[pl.BlockSpec((nbp, seq, TNA), lambda b, j: (b, 0, a_idx(j))),
                   pl.BlockSpec((nbp, seq, TNB), lambda b, j: (b, 0, b_idx(j)))],
        out_shape=[jax.ShapeDtypeStruct((bsz, seq, A_COLS), F32),
                   jax.ShapeDtypeStruct((bsz, seq, B_COLS), F32)],
        scratch_shapes=[pltpu.VMEM((nbp * seq, D_MODEL), BF16)],
        compiler_params=pltpu.CompilerParams(dimension_semantics=("arbitrary", "arbitrary"),
                                             vmem_limit_bytes=VMEM_LIMIT),
        name="proj_grid" if grid_shift else "proj_seq",
    )(x, shift, scale, norm_g, wa_bf16, wb_bf16, mu_h, mu_v)


def _scan_masks():
    c = CHUNK
    t = _iota((c, PACK), 0)
    s = _iota((c, PACK), 1) & (c - 1)
    tt = _iota((c, c), 0)
    ss = _iota((c, c), 1)
    f01 = lambda cond: jnp.where(cond, 1.0, 0.0).astype(F32)
    return dict(
        bm64_bf=_head_mask_a().astype(BF16),
        eye=f01(t == s),
        incl=(f01(t >= s), f01(t <= s)),
        strict=(f01(t > s), f01(t < s)),
        tri=(f01(tt >= ss).astype(BF16), f01(tt <= ss).astype(BF16)),
    )


_MASK_KEYS = ("bm64_bf", "eye", "incl", "strict", "tri")
_MASK_SCRATCH = (((PACK, PACK), BF16), ((CHUNK, PACK), F32), ((2, CHUNK, PACK), F32), ((2, CHUNK, PACK), F32),
                 ((2, CHUNK, CHUNK), BF16))


def _store_masks(mask_refs):
    masks = _scan_masks()
    for ref, key in zip(mask_refs, _MASK_KEYS):
        val = masks[key]
        if isinstance(val, tuple):
            for i, z in enumerate(val):
                ref[i] = z
        else:
            ref[...] = val


class _Masks:
    def __init__(self, mask_refs):
        self._refs = dict(zip(_MASK_KEYS, mask_refs))

    def __getitem__(self, key):
        ref = self._refs[key]
        return (ref[0], ref[1]) if len(ref.shape) == 3 else ref[...]


def _half_lane_masks(dtype):
    lane = _iota((1, BLOCK_LANES), 1)
    return (jnp.where(lane < HEAD_A, 1.0, 0.0).astype(dtype), jnp.where(lane >= HEAD_A, 1.0, 0.0).astype(dtype))


def _block_diag_a(x):
    halves = _half_lane_masks(x.dtype)
    zero = jnp.zeros((x.shape[0], BLOCK_LANES), x.dtype)
    rows = []
    for h in range(PACK // HEAD_A):
        tile = h * HEAD_A // BLOCK_LANES
        piece = x[:, BLOCK_LANES * tile:BLOCK_LANES * (tile + 1)] * halves[h % 2]
        rows.append(jnp.concatenate([piece, zero] if tile == 0 else [zero, piece], axis=1))
    return jnp.concatenate(rows, axis=0)


def _block_diag_b(x):
    heads = D_B // HEAD_B
    zero = jnp.zeros((x.shape[0], HEAD_B), x.dtype)
    return jnp.concatenate(
        [jnp.concatenate([x[:, HEAD_B * h:HEAD_B * (h + 1)] if t == h else zero for t in range(heads)], axis=1)
         for h in range(heads)], axis=0)


def _lockstep(gens):
    results = [None] * len(gens)
    live = list(enumerate(gens))
    while live:
        still = []
        for i, g in live:
            try:
                next(g)
                still.append((i, g))
            except StopIteration as stop:
                results[i] = stop.value
        live = still
    return results


def _rwkv_group(rt, kt, kh, bh, kb, bb, v, s_ref, egc, mk, d):
    c = CHUNK
    bd = lambda x: _block_diag_a(x.astype(BF16))
    kr = _split(jnp.concatenate([kt, rt], axis=0))
    aa_k = _mm(kr, bd(kh), NT)
    aa_b = _mm(kr, bd(bh), NT)
    s_bd = s_ref[...]
    krs = _mm(kr, s_bd, NT)
    yield
    a_kk = aa_k[:c] * mk["strict"][d]
    a_rk = aa_k[c:] * mk["incl"][d]
    n = -(aa_b[:c] * mk["strict"][d])
    a_rb = aa_b[c:] * mk["incl"][d]
    x = mk["eye"] + n
    p = _mm(n, bd(n))
    akv = _mm(jnp.concatenate([a_kk, a_rk], axis=0), bd(v))
    yield
    for i in range(5):
        if i < 4:
            xp = _mm(jnp.concatenate([x, p], axis=0), bd(p))
            x = x + xp[:c]
            p = xp[c:]
        else:
            x = x + _mm(x, bd(p))
        yield
    u = _mm(x, bd(krs[:c] + akv[:c]))
    yield
    y = krs[c:] + akv[c:] - _mm(a_rb, bd(u))
    vu = jnp.concatenate([v, (-u).astype(BF16)], axis=0)
    kbb = jnp.concatenate([kb, bb], axis=0)
    upd = _mm(vu, kbb, TN_DIMS)
    halves = _half_lane_masks(F32)
    zero = jnp.zeros((HEAD_A, BLOCK_LANES), F32)
    rows = []
    for h in range(PACK // HEAD_A):
        tile = h * HEAD_A // BLOCK_LANES
        rs, ls = slice(HEAD_A * h, HEAD_A * (h + 1)), slice(BLOCK_LANES * tile, BLOCK_LANES * (tile + 1))
        piece = (s_bd[rs, ls] * egc[:, ls] + upd[rs, ls]) * halves[h % 2]
        rows.append(jnp.concatenate([piece, zero] if tile == 0 else [zero, piece], axis=1))
    s_ref[...] = jnp.concatenate(rows, axis=0)
    return y


def _hgrn_dir(qn, kn, qt, kbh, vb, eghc, s_refs, mk, d):
    a_p = _mm(qn, _block_diag_b(kn.astype(BF16)), NT) * mk["incl"][d]
    inter = []
    for grp, s_ref in enumerate(s_refs):
        sl = slice(PACK * grp, PACK * (grp + 1))
        st = s_ref[...]
        inter.append(_mm(qt[:, sl], st, NT))
        upd = _mm(vb[:, sl], kbh[:, sl], TN_DIMS)
        decay = eghc[:, sl]
        zero = jnp.zeros((HEAD_B, HEAD_B), F32)
        blocks = [st[b0:b0 + HEAD_B, b0:b0 + HEAD_B] * decay[:, b0:b0 + HEAD_B] + upd[b0:b0 + HEAD_B, b0:b0 + HEAD_B]
                  for b0 in range(0, PACK, HEAD_B)]
        s_ref[...] = jnp.concatenate([jnp.concatenate([blocks[0], zero], axis=1),
                                      jnp.concatenate([zero, blocks[1]], axis=1)], axis=0)
    yield
    o = _mm(a_p, _block_diag_b(vb.astype(BF16)))
    return o + jnp.concatenate(inter, axis=1)


def _blockwise(fn, operands, n_out, rows=CHUNK, width=D_A):
    cols = []
    for l0 in range(0, width, BLOCK_LANES):
        parts = []
        for r0 in range(0, rows, BLOCK_ROWS):
            blk = [z[(slice(None) if z.shape[0] == 1 else slice(r0, r0 + BLOCK_ROWS)), l0:l0 + BLOCK_LANES]
                   for z in operands]
            parts.append(fn(*blk))
        cols.append([jnp.concatenate([p[i] for p in parts], axis=0) for i in range(n_out)])
    return [jnp.concatenate([col[i] for col in cols], axis=1) for i in range(n_out)]


def _chunk_operands(pa_ref, pb_ref, bi, d, prm, mk):
    c = CHUNK
    n_t = SUM_TERMS
    w0, a0, wa2, k_k, k_a, lb = prm
    cols = lambda ref, i, width=D_A: ref.at[bi, :, i * width:(i + 1) * width]
    r_ref, k_ref, v_ref = cols(pa_ref, 0), cols(pa_ref, 1), cols(pa_ref, 2)
    lo = pa_ref[bi, :, 4 * D_A + LORA * d:4 * D_A + LORA * (d + 1)]
    lo = jnp.where(_iota((c, LORA), 1) < LORA // 2, jnp.tanh(lo), lo)
    wa = _mm(lo, wa2[d])
    yield

    def gates(wa_w, wa_a, k, fr, w0_, a0_, kk_, lb_):
        lw = -DECAY_SCALE * _sigmoid(w0_ + wa_w)
        a = _sigmoid(a0_ + wa_a)
        kk = k * kk_
        f = lb_ + (1.0 - lb_) * _sigmoid(fr)
        gf = jnp.log(f)
        return (lw, a, kk, 1.0 - f, (kk * kk).astype(BF16), *_terms(lw, n_t), *_terms(gf, n_t))

    res = _blockwise(gates, [wa[:, :D_A], wa[:, D_A:], k_ref, cols(pb_ref, 2 + d, D_B), w0[d:d + 1], a0[d:d + 1],
                             k_k, lb[d:d + 1]], 5 + 2 * n_t)
    lw, a, kk, kf, sq = res[:5]
    lw_t, gf_t = res[5:5 + n_t], res[5 + n_t:]
    seg = lambda lanes: _dot(sq[:, lanes], mk["bm64_bf"])
    n2 = jnp.concatenate([seg(slice(0, PACK)), seg(slice(PACK, D_A))], axis=1)
    gcum = _sum_small_first([_dot(mk["tri"][d], jnp.concatenate([tl, tg], axis=1))
                             for tl, tg in zip(lw_t, gf_t)])
    yield
    last = c - 1 if d == 0 else 0
    g, gh = gcum[:, :D_A], gcum[:, D_A:]
    gc, ghc, ghm = g[last:last + 1], gh[last:last + 1], gh[c // 2:c // 2 + 1]
    egc, eghc = jnp.exp(gc), jnp.exp(ghc)

    def rwkv_operands(r, k, kk_, n2_, a_, g_, lw_, ka_, egc_):
        kap = kk_ * lax.rsqrt(jnp.maximum(n2_, 1e-24))
        keff = k * (1.0 + (a_ - 1.0) * ka_)
        eng = _exp_neg(g_)
        kh = keff * eng
        bh = (kap * a_) * eng
        out = (r * jnp.exp(g_), kap * jnp.exp(g_ - lw_), kh, bh, kh * egc_, bh * egc_)
        return tuple(z.astype(BF16) for z in out)

    rt, kt, kh, bh, kb, bb = _blockwise(rwkv_operands, [r_ref, k_ref, kk, n2, a, g, lw, k_a, egc], 6)

    def hgrn_operands(qr, kf_, gh_, ghm_, e_m, e_cm):
        q = _silu(qr)
        ghn = gh_ - ghm_
        qn = q * jnp.exp(ghn)
        kn = kf_ * _exp_neg(ghn)
        return tuple(z.astype(BF16) for z in (qn, kn, qn * e_m, kn * e_cm))

    qn, kn, qt, kbh = _blockwise(hgrn_operands, [cols(pb_ref, 0, D_B), kf, gh, ghm, jnp.exp(ghm),
                                                 jnp.exp(ghc - ghm)], 4, width=D_B)
    v = v_ref[...].astype(BF16)
    vb = pb_ref[bi, :, D_B:2 * D_B].astype(BF16)
    return (rt, kt, kh, bh, kb, bb, v, qn, kn, qt, kbh, vb), (egc, eghc)


def _scan_dir(pa_ref, pb_ref, bi, d, prm, sa_scr, sb_scr, mk):
    ops, (egc, eghc) = yield from _chunk_operands(pa_ref, pb_ref, bi, d, prm, mk)
    rt, kt, kh, bh, kb, bb, v, qn, kn, qt, kbh, vb = ops
    chains = []
    for grp in range(D_A // PACK):
        sl = slice(PACK * grp, PACK * (grp + 1))
        chains.append(_rwkv_group(rt[:, sl], kt[:, sl], kh[:, sl], bh[:, sl], kb[:, sl], bb[:, sl], v[:, sl],
                                  sa_scr.at[bi, d, grp], egc[:, sl], mk, d))
    chains.append(_hgrn_dir(qn, kn, qt, kbh, vb, eghc, [sb_scr.at[bi, d, grp] for grp in range(D_B // PACK)],
                            mk, d))
    results = [None] * len(chains)
    live = list(enumerate(chains))
    while live:
        still = []
        for i, g in live:
            try:
                next(g)
                still.append((i, g))
            except StopIteration as stop:
                results[i] = stop.value
        live = still
        yield
    return jnp.concatenate(results[:-1], axis=1), results[-1]


def _state_slots():
    return [(bi, d, grp) for bi in range(SCAN_BATCH) for d in range(2) for grp in range(D_A // PACK)]


def _load_states(sa0_ref, sb0_ref, sa_scr, sb_scr):
    spread = jnp.where(_iota((HEAD_A, PACK), 0) == (_iota((HEAD_A, PACK), 1) & (HEAD_A - 1)), 1.0, 0.0)
    spread = spread.astype(BF16)
    bm64 = _head_mask_a()
    zero = jnp.zeros((HEAD_B, HEAD_B), F32)
    for bi, d, grp in _state_slots():
        sa_scr[bi, d, grp] = _mm_exact_rhs(sa0_ref[bi, d, grp], spread) * bm64
        h0 = sb0_ref[bi, d, 2 * grp].T
        h1 = sb0_ref[bi, d, 2 * grp + 1].T
        sb_scr[bi, d, grp] = jnp.concatenate([jnp.concatenate([h0, zero], axis=1),
                                              jnp.concatenate([zero, h1], axis=1)], axis=0)


def _store_states(sa_scr, sb_scr, sa_out, sb_out):
    gather = jnp.where((_iota((PACK, HEAD_A), 0) & (HEAD_A - 1)) == _iota((PACK, HEAD_A), 1), 1.0, 0.0)
    gather = gather.astype(BF16)
    for bi, d, grp in _state_slots():
        sa_out[bi, d, grp] = _mm_exact_rhs(sa_scr[bi, d, grp], gather)
        for hh in range(PACK // HEAD_B):
            blk = sb_scr[bi, d, grp, HEAD_B * hh:HEAD_B * (hh + 1), HEAD_B * hh:HEAD_B * (hh + 1)]
            sb_out[bi, d, 2 * grp + hh] = blk.T


def _scan_kernel(paf_ref, pbf_ref, pab_ref, pbb_ref, *rest, from_zero):
    rest, mask_refs = rest[:-len(_MASK_KEYS)], rest[-len(_MASK_KEYS):]
    if from_zero:
        (w0_ref, a0_ref, wa2_ref, kk_ref, ka_ref, lb_ref, yf_ref, yb_ref, of_ref, ob_ref, sa_out, sb_out,
         sa_scr, sb_scr) = rest
    else:
        (sa0_ref, sb0_ref, w0_ref, a0_ref, wa2_ref, kk_ref, ka_ref, lb_ref, yf_ref, yb_ref, of_ref, ob_ref,
         sa_scr, sb_scr) = rest
    ci = pl.program_id(1)

    @pl.when(ci == 0)
    def _():
        _store_masks(mask_refs)
        if from_zero:
            sa_scr[...] = jnp.zeros(sa_scr.shape, F32)
            sb_scr[...] = jnp.zeros(sb_scr.shape, F32)
        else:
            _load_states(sa0_ref, sb0_ref, sa_scr, sb_scr)

    mk = _Masks(mask_refs)
    prm = (w0_ref[...], a0_ref[...], wa2_ref, kk_ref[...], ka_ref[...], lb_ref[...])
    p_refs = ((paf_ref, pbf_ref), (pab_ref, pbb_ref))
    runs = [(bi, d) for bi in range(SCAN_BATCH) for d in range(2)]
    outs = _lockstep([_scan_dir(p_refs[d][0], p_refs[d][1], bi, d, prm, sa_scr, sb_scr, mk) for bi, d in runs])
    for (bi, d), (y, o) in zip(runs, outs):
        (yf_ref, yb_ref)[d][bi] = y
        (of_ref, ob_ref)[d][bi] = o

    if from_zero:
        @pl.when(ci == pl.num_programs(1) - 1)
        def _():
            _store_states(sa_scr, sb_scr, sa_out, sb_out)


def _scan_call(pa, pb, states, w0, a0, wa2, k_k, k_a, lb):
    bsz, seq, _ = pa.shape
    nc = seq // CHUNK
    nb = SCAN_BATCH
    from_zero = states is None
    fwd = lambda b, c: (b, c, 0)
    bwd = lambda b, c: (b, nc - 1 - c, 0)
    st = lambda b, c: (b, 0, 0, 0, 0)
    full = lambda shape: pl.BlockSpec(shape, lambda b, c: (0,) * len(shape))
    sa_block = (nb, 2, D_A // PACK, PACK, HEAD_A)
    sb_block = (nb, 2, D_B // HEAD_B, HEAD_B, HEAD_B)
    y_shape = jax.ShapeDtypeStruct((bsz, seq, D_A), F32)
    out_specs = [pl.BlockSpec((nb, CHUNK, D_A), fwd), pl.BlockSpec((nb, CHUNK, D_A), bwd),
                 pl.BlockSpec((nb, CHUNK, D_B), fwd), pl.BlockSpec((nb, CHUNK, D_B), bwd)]
    out_shape = [y_shape, y_shape, y_shape, y_shape]
    in_specs = [pl.BlockSpec((nb, CHUNK, A_COLS), fwd), pl.BlockSpec((nb, CHUNK, B_COLS), fwd),
                pl.BlockSpec((nb, CHUNK, A_COLS), bwd), pl.BlockSpec((nb, CHUNK, B_COLS), bwd)]
    args = [pa, pb, pa, pb]
    if from_zero:
        out_specs += [pl.BlockSpec(sa_block, st), pl.BlockSpec(sb_block, st)]
        out_shape += [jax.ShapeDtypeStruct((bsz,) + sa_block[1:], F32),
                      jax.ShapeDtypeStruct((bsz,) + sb_block[1:], F32)]
    else:
        in_specs += [pl.BlockSpec(sa_block, st), pl.BlockSpec(sb_block, st)]
        args += list(states)
    in_specs += [full((2, D_A)), full((2, D_A)), full((2, LORA, 2 * D_A)),
                 full((1, D_A)), full((1, D_A)), full((2, D_B))]
    args += [w0, a0, wa2, k_k, k_a, lb]
    bd_block = (nb, 2, D_A // PACK, PACK, PACK)
    return pl.pallas_call(
        functools.partial(_scan_kernel, from_zero=from_zero),
        grid=(bsz // nb, nc),
        in_specs=in_specs,
        out_specs=out_specs,
        out_shape=out_shape,
        scratch_shapes=[pltpu.VMEM(bd_block, F32), pltpu.VMEM(bd_block, F32)]
                       + [pltpu.VMEM(shape, dtype) for shape, dtype in _MASK_SCRATCH],
        compiler_params=pltpu.CompilerParams(dimension_semantics=("arbitrary", "arbitrary"),
                                             vmem_limit_bytes=VMEM_LIMIT),
        name="scan_state" if from_zero else "scan",
    )(*args)


def _out_kernel(x_ref, pa_ref, zb_ref, yf_ref, yb_ref, of_ref, ob_ref, gate_ref, a0_ref, a2p_ref, ka_ref,
                rk_ref, lnw_ref, lnb_ref, og_ref, wout_ref, fg_ref, out_ref):
    pa = pa_ref[0]
    r = pa[:, 0:D_A]
    k = pa[:, D_A:2 * D_A]
    v = pa[:, 2 * D_A:3 * D_A]
    za = pa[:, 3 * D_A:4 * D_A]
    bm64_bf = _head_mask_a().astype(BF16)

    def seg_sum(z):
        zb16 = z.astype(BF16)
        return jnp.concatenate([_dot(zb16[:, :PACK], bm64_bf), _dot(zb16[:, PACK:], bm64_bf)], axis=1)

    y = yf_ref[0] + yb_ref[0]
    mu = seg_sum(y) * (1.0 / HEAD_A)
    dlt = y - mu
    var = seg_sum(dlt * dlt) * (1.0 / HEAD_A)
    yn = dlt * lax.rsqrt(var + GN_EPS) * lnw_ref[...] + lnb_ref[...]
    a_dirs = []
    for d in range(2):
        lo = pa[:, 4 * D_A + LORA * d:4 * D_A + LORA * (d + 1)]
        a_dirs.append(_sigmoid(a0_ref[d:d + 1] + _mm(lo, a2p_ref[d])))
    kmean = k * (1.0 + (0.5 * (a_dirs[0] + a_dirs[1]) - 1.0) * ka_ref[...])
    bonus = seg_sum(r * kmean * rk_ref[...]) * v
    out_a = (yn + bonus) * _silu(za)

    o = of_ref[0] + ob_ref[0]
    og = og_ref[...]
    zb = zb_ref[0]
    outs = [out_a]
    for h in range(D_B // HEAD_B):
        sl = slice(HEAD_B * h, HEAD_B * (h + 1))
        oh = o[:, sl]
        oh = oh * lax.rsqrt(jnp.mean(oh * oh, axis=-1, keepdims=True) + EPS) * og[:, sl]
        outs.append(oh * _silu(zb[:, sl]))
    mix = jnp.concatenate(outs, axis=1)
    proj = _dot(mix.astype(BF16), wout_ref[...])
    hs = x_ref[0] + gate_ref[0] * proj
    out_ref[0] = hs * lax.rsqrt(jnp.mean(hs * hs, axis=-1, keepdims=True) + EPS) * fg_ref[...]


def _out_call(x, pa, pb, yf, yb, of, ob, gate, a0, a2p, k_a, r_k, lnx_w, lnx_b, onorm_g, w_out_bf16, final_g):
    bsz, seq, _ = x.shape
    tm = min(OUT_ROWS, seq)
    tok = lambda b, i: (b, i, 0)
    full = lambda shape: pl.BlockSpec(shape, lambda b, i: (0,) * len(shape))
    return pl.pallas_call(
        _out_kernel,
        grid=(bsz, seq // tm),
        in_specs=[pl.BlockSpec((1, tm, D_MODEL), tok),
                  pl.BlockSpec((1, tm, A_COLS), tok),
                  pl.BlockSpec((1, tm, D_B), lambda b, i: (b, i, 4)),
                  pl.BlockSpec((1, tm, D_A), tok), pl.BlockSpec((1, tm, D_A), tok),
                  pl.BlockSpec((1, tm, D_B), tok), pl.BlockSpec((1, tm, D_B), tok),
                  pl.BlockSpec((1, 1, D_MODEL), lambda b, i: (b, 0, 0)),
                  full((2, D_A)), full((2, LORA, D_A)), full((1, D_A)), full((1, D_A)),
                  full((1, D_A)), full((1, D_A)), full((1, D_B)),
                  full((D_MODEL, D_MODEL)), full((1, D_MODEL))],
        out_specs=pl.BlockSpec((1, tm, D_MODEL), tok),
        out_shape=jax.ShapeDtypeStruct((bsz, seq, D_MODEL), F32),
        compiler_params=pltpu.CompilerParams(dimension_semantics=("arbitrary", "arbitrary"),
                                             vmem_limit_bytes=VMEM_LIMIT),
        name="out",
    )(x, pa, pb, yf, yb, of, ob, gate, a0, a2p, k_a, r_k, lnx_w, lnx_b, onorm_g, w_out_bf16, final_g)


def kernel(x_prompt, x_sample, state_rwkv, state_hgrn, c, c_ctx, norm_g, w_ada, b_ada, w_in, mu_h, mu_v, w0, w2,
           a0, a2, k_k, k_a, r_k, lnx_w, lnx_b, lb_logits, onorm_g, w_out, final_g):
    l = 0
    bp = x_prompt.shape[0]
    bs = x_sample.shape[0]
    lb_all = jnp.cumsum(jax.nn.softmax(lb_logits.astype(F32), axis=0), axis=0)
    lb = lb_all[l]

    wa_bf = w_in[l, :, :A_COLS].astype(BF16)
    wb_bf = w_in[l, :, A_COLS:].astype(BF16)
    w_out_bf = w_out[l].astype(BF16)
    zeros = jnp.zeros((2, LORA // 2, D_A), F32)
    wa2 = jnp.concatenate([jnp.concatenate([w2[l], zeros], axis=2),
                           jnp.concatenate([zeros, a2[l]], axis=2)], axis=1)
    a2p = jnp.concatenate([zeros, a2[l]], axis=1)
    row = lambda z: z.reshape(1, -1)

    cc = jnp.concatenate([c_ctx[None, :], c, jnp.zeros((16 - 1 - bs, D_MODEL), F32)], axis=0)
    m = _ada_call(cc, w_ada[l], row(b_ada[l]))
    shift, scale, gate = m[:, :D_MODEL], m[:, D_MODEL:2 * D_MODEL], m[:, 2 * D_MODEL:]
    ctx = lambda z: jnp.broadcast_to(z[0:1, None, :], (bp, 1, D_MODEL))
    lat = lambda z: z[1:1 + bs, None, :]

    def path(x, sh, sc, gt, states, grid_shift):
        pa, pb = _proj_call(x, sh, sc, row(norm_g[l]), wa_bf, wb_bf, row(mu_h[l]), row(mu_v[l]), grid_shift)
        res = _scan_call(pa, pb, states, w0[l], a0[l], wa2, row(k_k[l]), row(k_a[l]), lb)
        yf, yb, of, ob = res[:4]
        y = _out_call(x, pa, pb, yf, yb, of, ob, gt, a0[l], a2p, row(k_a[l]), row(r_k[l]), row(lnx_w[l]),
                      row(lnx_b[l]), row(onorm_g[l]), w_out_bf, row(final_g))
        return y, res[4:]

    y_prompt, (s_a, s_b) = path(x_prompt, ctx(shift), ctx(scale), ctx(gate), None, False)
    groups_a = D_A // PACK
    sa0 = state_rwkv[:, l].reshape(bs, 2, groups_a, PACK, HEAD_A)
    y_sample, _ = path(x_sample, lat(shift), lat(scale), lat(gate), (sa0, state_hgrn[:, l]), True)
    s_a = s_a.reshape(bp, 1, 2, D_A // HEAD_A, HEAD_A, HEAD_A)
    return y_prompt, y_sample, s_a, s_b[:, None]
```

```python
import functools

import jax
import jax.numpy as jnp
from jax import lax
from jax.experimental import pallas as pl
from jax.experimental.pallas import tpu as pltpu

F32 = jnp.float32
BF16 = jnp.bfloat16

D_MODEL = 1024
D_A = 512
D_B = 512
HEAD_A = 64
HEAD_B = 128
LOG2_HEAD_A = HEAD_A.bit_length() - 1
LORA = 128
A_COLS = 4 * D_A + 2 * LORA
B_COLS = 5 * D_B
GRID_W = 64
CHUNK = 64
SCAN_BATCH = 4
SUM_TERMS = 2
BLOCK_ROWS = 32
BLOCK_LANES = 128
EPS = 1e-6
GN_EPS = 64e-5
DECAY_SCALE = 0.6065306597126334
NEG_LOG2E = -1.4426950408889634
PACK = 256
TNA = 768
TNB = 512
NA = A_COLS // TNA
NB = B_COLS // TNB
PROJ_ROWS = 2048
OUT_ROWS = 512
VMEM_LIMIT = 56 * 1024 * 1024

NN = ((1,), (0,))
NT = ((1,), (1,))
TN_DIMS = ((0,), (0,))


def _dot(a, b, dims=NN):
    return lax.dot_general(a, b, (dims, ((), ())), preferred_element_type=F32)


def _split(x, passes=1):
    if isinstance(x, tuple):
        return x
    hi = x.astype(BF16)
    if passes == 1:
        return (hi,)
    return hi, (x - hi.astype(F32)).astype(BF16)


def _mm(a, b, dims=NN, passes=1):
    a = _split(a, passes)
    b = _split(b, passes)
    if len(a) == 1 or len(b) == 1:
        return _dot(a[0], b[0], dims)
    return _dot(a[0], b[0], dims) + (_dot(a[0], b[1], dims) + _dot(a[1], b[0], dims))


def _terms(x, n):
    out = []
    for _ in range(n - 1):
        h = x.astype(BF16)
        out.append(h)
        x = x - h.astype(F32)
    out.append(x.astype(BF16))
    return out


def _sum_small_first(parts):
    return functools.reduce(lambda acc, z: z + acc, reversed(parts))


def _mm_exact_rhs(a, b_bf16, n=3):
    return _sum_small_first([_dot(t, b_bf16) for t in _terms(a, n)])


def _exp_neg(x):
    return jnp.exp2(x * NEG_LOG2E)


def _sigmoid(x):
    return 0.5 * jnp.tanh(0.5 * x) + 0.5


def _silu(x):
    return x * _sigmoid(x)


def _iota(shape, dim):
    return lax.broadcasted_iota(jnp.int32, shape, dim)


def _block_mask(rows, cols, row_shift, col_shift):
    same = (_iota((rows, cols), 0) >> row_shift) == (_iota((rows, cols), 1) >> col_shift)
    return jnp.where(same, 1.0, 0.0).astype(F32)


def _head_mask_a():
    return _block_mask(PACK, PACK, LOG2_HEAD_A, LOG2_HEAD_A)


def _ada_kernel(c_ref, w_ref, b_ref, m_ref):
    m_ref[...] = _mm(_silu(c_ref[...]), w_ref[...], passes=3) + b_ref[...]


def _ada_call(cc, w_ada, b_ada):
    rows = cc.shape[0]
    return pl.pallas_call(
        _ada_kernel,
        grid=(3,),
        in_specs=[pl.BlockSpec((rows, D_MODEL), lambda j: (0, 0)),
                  pl.BlockSpec((D_MODEL, D_MODEL), lambda j: (0, j)),
                  pl.BlockSpec((1, D_MODEL), lambda j: (0, j))],
        out_specs=pl.BlockSpec((rows, D_MODEL), lambda j: (0, j)),
        out_shape=jax.ShapeDtypeStruct((rows, 3 * D_MODEL), F32),
        compiler_params=pltpu.CompilerParams(dimension_semantics=("arbitrary",),
                                             vmem_limit_bytes=VMEM_LIMIT),
        name="ada",
    )(cc, w_ada, b_ada)


def _proj_kernel(x_ref, sh_ref, sc_ref, g_ref, wa_ref, wb_ref, muh_ref, muv_ref, pa_ref, pb_ref, xm_ref, *,
                 nbp, seq, grid_shift):
    j = pl.program_id(1)
    rows = nbp * seq

    @pl.when(j == 0)
    def _():
        for b in range(nbp):
            x = x_ref[b]
            xn = x * lax.rsqrt(jnp.mean(x * x, axis=-1, keepdims=True) + EPS) * g_ref[...]
            xm_ref[b * seq:(b + 1) * seq, :] = (xn * (1.0 + sc_ref[b]) + sh_ref[b]).astype(BF16)

    @pl.when(j < NA)
    def _():
        p = _dot(xm_ref[...], wa_ref[...])
        w = GRID_W if grid_shift else seq
        nblk = rows // w
        pos = _iota((1, w, TNA), 1)
        to3 = lambda z: z.reshape(nblk, w, TNA)
        prev = jnp.where(pos == 0, 0.0, to3(pltpu.roll(p, 1, axis=0)))
        nxt = jnp.where(pos == w - 1, 0.0, to3(pltpu.roll(p, rows - 1, axis=0)))
        muh = muh_ref[...]
        out = (0.5 * muh) * (prev + nxt)
        if grid_shift:
            muv = muv_ref[...]
            p3 = to3(p)
            zblk = jnp.zeros((1, w, TNA), F32)
            up = jnp.concatenate([zblk, p3[:nblk - 1]], axis=0)
            dn = jnp.concatenate([p3[1:], zblk], axis=0)
            out = (1.0 - muh - muv) * p3 + out + (0.5 * muv) * (up + dn)
        else:
            out = (1.0 - muh) * to3(p) + out
        pa_ref[...] = out.reshape(nbp, seq, TNA)

    @pl.when(j >= NA)
    def _():
        pb_ref[...] = _dot(xm_ref[...], wb_ref[...]).reshape(nbp, seq, TNB)


def _proj_call(x, shift, scale, norm_g, wa_bf16, wb_bf16, mu_h, mu_v, grid_shift):
    bsz, seq, _ = x.shape
    nbp = 1 if grid_shift else PROJ_ROWS // seq
    kern = functools.partial(_proj_kernel, nbp=nbp, seq=seq, grid_shift=grid_shift)
    a_idx = lambda j: jnp.minimum(j, NA - 1)
    b_idx = lambda j: jnp.maximum(j - NA, 0)
    return pl.pallas_call(
        kern,
        grid=(bsz // nbp, NA + NB),
        in_specs=[pl.BlockSpec((nbp, seq, D_MODEL), lambda b, j: (b, 0, 0)),
                  pl.BlockSpec((nbp, 1, D_MODEL), lambda b, j: (b, 0, 0)),
                  pl.BlockSpec((nbp, 1, D_MODEL), lambda b, j: (b, 0, 0)),
                  pl.BlockSpec((1, D_MODEL), lambda b, j: (0, 0)),
                  pl.BlockSpec((D_MODEL, TNA), lambda b, j: (0, a_idx(j))),
                  pl.BlockSpec((D_MODEL, TNB), lambda b, j: (0, b_idx(j))),
                  pl.BlockSpec((1, TNA), lambda b, j: (0, a_idx(j))),
                  pl.BlockSpec((1, TNA), lambda b, j: (0, a_idx(j)))],
        out_specs=[pl.BlockSpec((nbp, seq, TNA), lambda b, j: (b, 0, a_idx(j))),
                   pl.BlockSpec((nbp, seq, TNB), lambda b, j: (b, 0, b_idx(j)))],
        out_shape=[jax.ShapeDtypeStruct((bsz, seq, A_COLS), F32),
                   jax.ShapeDtypeStruct((bsz, seq, B_COLS), F32)],
        scratch_shapes=[pltpu.VMEM((nbp * seq, D_MODEL), BF16)],
        compiler_params=pltpu.CompilerParams(dimension_semantics=("arbitrary", "arbitrary"),
                                             vmem_limit_bytes=VMEM_LIMIT),
        name="proj_grid" if grid_shift else "proj_seq",
    )(x, shift, scale, norm_g, wa_bf16, wb_bf16, mu_h, mu_v)


def _scan_masks():
    c = CHUNK
    t = _iota((c, PACK), 0)
    s = _iota((c, PACK), 1) & (c - 1)
    tt = _iota((c, c), 0)
    ss = _iota((c, c), 1)
    f01 = lambda cond: jnp.where(cond, 1.0, 0.0).astype(F32)
    return dict(
        bm64_bf=_head_mask_a().astype(BF16),
        eye=f01(t == s),
        incl=(f01(t >= s), f01(t <= s)),
        strict=(f01(t > s), f01(t < s)),
        tri=(f01(tt >= ss).astype(BF16), f01(tt <= ss).astype(BF16)),
    )


_MASK_KEYS = ("bm64_bf", "eye", "incl", "strict", "tri")
_MASK_SCRATCH = (((PACK, PACK), BF16), ((CHUNK, PACK), F32), ((2, CHUNK, PACK), F32), ((2, CHUNK, PACK), F32),
                 ((2, CHUNK, CHUNK), BF16))


def _store_masks(mask_refs):
    masks = _scan_masks()
    for ref, key in zip(mask_refs, _MASK_KEYS):
        val = masks[key]
        if isinstance(val, tuple):
            for i, z in enumerate(val):
                ref[i] = z
        else:
            ref[...] = val


class _Masks:
    def __init__(self, mask_refs):
        self._refs = dict(zip(_MASK_KEYS, mask_refs))

    def __getitem__(self, key):
        ref = self._refs[key]
        return (ref[0], ref[1]) if len(ref.shape) == 3 else ref[...]


def _half_lane_masks(dtype):
    lane = _iota((1, BLOCK_LANES), 1)
    return (jnp.where(lane < HEAD_A, 1.0, 0.0).astype(dtype), jnp.where(lane >= HEAD_A, 1.0, 0.0).astype(dtype))


def _block_diag_a(x):
    halves = _half_lane_masks(x.dtype)
    zero = jnp.zeros((x.shape[0], BLOCK_LANES), x.dtype)
    rows = []
    for h in range(PACK // HEAD_A):
        tile = h * HEAD_A // BLOCK_LANES
        piece = x[:, BLOCK_LANES * tile:BLOCK_LANES * (tile + 1)] * halves[h % 2]
        rows.append(jnp.concatenate([piece, zero] if tile == 0 else [zero, piece], axis=1))
    return jnp.concatenate(rows, axis=0)


def _block_diag_b(x):
    heads = D_B // HEAD_B
    zero = jnp.zeros((x.shape[0], HEAD_B), x.dtype)
    return jnp.concatenate(
        [jnp.concatenate([x[:, HEAD_B * h:HEAD_B * (h + 1)] if t == h else zero for t in range(heads)], axis=1)
         for h in range(heads)], axis=0)


def _lockstep(gens):
    results = [None] * len(gens)
    live = list(enumerate(gens))
    while live:
        still = []
        for i, g in live:
            try:
                next(g)
                still.append((i, g))
            except StopIteration as stop:
                results[i] = stop.value
        live = still
    return results


def _rwkv_group(rt, kt, kh, bh, kb, bb, v, s_ref, egc, mk, d):
    c = CHUNK
    bd = lambda x: _block_diag_a(x.astype(BF16))
    kr = _split(jnp.concatenate([kt, rt], axis=0))
    aa_k = _mm(kr, bd(kh), NT)
    aa_b = _mm(kr, bd(bh), NT)
    s_bd = s_ref[...]
    krs = _mm(kr, s_bd, NT)
    yield
    a_kk = aa_k[:c] * mk["strict"][d]
    a_rk = aa_k[c:] * mk["incl"][d]
    n = -(aa_b[:c] * mk["strict"][d])
    a_rb = aa_b[c:] * mk["incl"][d]
    x = mk["eye"] + n
    p = _mm(n, bd(n))
    akv = _mm(jnp.concatenate([a_kk, a_rk], axis=0), bd(v))
    yield
    for i in range(5):
        if i < 4:
            xp = _mm(jnp.concatenate([x, p], axis=0), bd(p))
            x = x + xp[:c]
            p = xp[c:]
        else:
            x = x + _mm(x, bd(p))
        yield
    u = _mm(x, bd(krs[:c] + akv[:c]))
    yield
    y = krs[c:] + akv[c:] - _mm(a_rb, bd(u))
    vu = jnp.concatenate([v, (-u).astype(BF16)], axis=0)
    kbb = jnp.concatenate([kb, bb], axis=0)
    upd = _mm(vu, kbb, TN_DIMS)
    halves = _half_lane_masks(F32)
    zero = jnp.zeros((HEAD_A, BLOCK_LANES), F32)
    rows = []
    for h in range(PACK // HEAD_A):
        tile = h * HEAD_A // BLOCK_LANES
        rs, ls = slice(HEAD_A * h, HEAD_A * (h + 1)), slice(BLOCK_LANES * tile, BLOCK_LANES * (tile + 1))
        piece = (s_bd[rs, ls] * egc[:, ls] + upd[rs, ls]) * halves[h % 2]
        rows.append(jnp.concatenate([piece, zero] if tile == 0 else [zero, piece], axis=1))
    s_ref[...] = jnp.concatenate(rows, axis=0)
    return y


def _hgrn_dir(qn, kn, qt, kbh, vb, eghc, s_refs, mk, d):
    a_p = _mm(qn, _block_diag_b(kn.astype(BF16)), NT) * mk["incl"][d]
    inter = []
    for grp, s_ref in enumerate(s_refs):
        sl = slice(PACK * grp, PACK * (grp + 1))
        st = s_ref[...]
        inter.append(_mm(qt[:, sl], st, NT))
        upd = _mm(vb[:, sl], kbh[:, sl], TN_DIMS)
        decay = eghc[:, sl]
        zero = jnp.zeros((HEAD_B, HEAD_B), F32)
        blocks = [st[b0:b0 + HEAD_B, b0:b0 + HEAD_B] * decay[:, b0:b0 + HEAD_B] + upd[b0:b0 + HEAD_B, b0:b0 + HEAD_B]
                  for b0 in range(0, PACK, HEAD_B)]
        s_ref[...] = jnp.concatenate([jnp.concatenate([blocks[0], zero], axis=1),
                                      jnp.concatenate([zero, blocks[1]], axis=1)], axis=0)
    yield
    o = _mm(a_p, _block_diag_b(vb.astype(BF16)))
    return o + jnp.concatenate(inter, axis=1)


def _blockwise(fn, operands, n_out, rows=CHUNK, width=D_A):
    cols = []
    for l0 in range(0, width, BLOCK_LANES):
        parts = []
        for r0 in range(0, rows, BLOCK_ROWS):
            blk = [z[(slice(None) if z.shape[0] == 1 else slice(r0, r0 + BLOCK_ROWS)), l0:l0 + BLOCK_LANES]
                   for z in operands]
            parts.append(fn(*blk))
        cols.append([jnp.concatenate([p[i] for p in parts], axis=0) for i in range(n_out)])
    return [jnp.concatenate([col[i] for col in cols], axis=1) for i in range(n_out)]


def _chunk_operands(pa_ref, pb_ref, bi, d, prm, mk):
    c = CHUNK
    n_t = SUM_TERMS
    w0, a0, wa2, k_k, k_a, lb = prm
    cols = lambda ref, i, width=D_A: ref.at[bi, :, i * width:(i + 1) * width]
    r_ref, k_ref, v_ref = cols(pa_ref, 0), cols(pa_ref, 1), cols(pa_ref, 2)
    lo = pa_ref[bi, :, 4 * D_A + LORA * d:4 * D_A + LORA * (d + 1)]
    lo = jnp.where(_iota((c, LORA), 1) < LORA // 2, jnp.tanh(lo), lo)
    wa = _mm(lo, wa2[d])
    yield

    def gates(wa_w, wa_a, k, fr, w0_, a0_, kk_, lb_):
        lw = -DECAY_SCALE * _sigmoid(w0_ + wa_w)
        a = _sigmoid(a0_ + wa_a)
        kk = k * kk_
        f = lb_ + (1.0 - lb_) * _sigmoid(fr)
        gf = jnp.log(f)
        return (lw, a, kk, 1.0 - f, (kk * kk).astype(BF16), *_terms(lw, n_t), *_terms(gf, n_t))

    res = _blockwise(gates, [wa[:, :D_A], wa[:, D_A:], k_ref, cols(pb_ref, 2 + d, D_B), w0[d:d + 1], a0[d:d + 1],
                             k_k, lb[d:d + 1]], 5 + 2 * n_t)
    lw, a, kk, kf, sq = res[:5]
    lw_t, gf_t = res[5:5 + n_t], res[5 + n_t:]
    seg = lambda lanes: _dot(sq[:, lanes], mk["bm64_bf"])
    n2 = jnp.concatenate([seg(slice(0, PACK)), seg(slice(PACK, D_A))], axis=1)
    gcum = _sum_small_first([_dot(mk["tri"][d], jnp.concatenate([tl, tg], axis=1))
                             for tl, tg in zip(lw_t, gf_t)])
    yield
    last = c - 1 if d == 0 else 0
    g, gh = gcum[:, :D_A], gcum[:, D_A:]
    gc, ghc, ghm = g[last:last + 1], gh[last:last + 1], gh[c // 2:c // 2 + 1]
    egc, eghc = jnp.exp(gc), jnp.exp(ghc)

    def rwkv_operands(r, k, kk_, n2_, a_, g_, lw_, ka_, egc_):
        kap = kk_ * lax.rsqrt(jnp.maximum(n2_, 1e-24))
        keff = k * (1.0 + (a_ - 1.0) * ka_)
        eng = _exp_neg(g_)
        kh = keff * eng
        bh = (kap * a_) * eng
        out = (r * jnp.exp(g_), kap * jnp.exp(g_ - lw_), kh, bh, kh * egc_, bh * egc_)
        return tuple(z.astype(BF16) for z in out)

    rt, kt, kh, bh, kb, bb = _blockwise(rwkv_operands, [r_ref, k_ref, kk, n2, a, g, lw, k_a, egc], 6)

    def hgrn_operands(qr, kf_, gh_, ghm_, e_m, e_cm):
        q = _silu(qr)
        ghn = gh_ - ghm_
        qn = q * jnp.exp(ghn)
        kn = kf_ * _exp_neg(ghn)
        return tuple(z.astype(BF16) for z in (qn, kn, qn * e_m, kn * e_cm))

    qn, kn, qt, kbh = _blockwise(hgrn_operands, [cols(pb_ref, 0, D_B), kf, gh, ghm, jnp.exp(ghm),
                                                 jnp.exp(ghc - ghm)], 4, width=D_B)
    v = v_ref[...].astype(BF16)
    vb = pb_ref[bi, :, D_B:2 * D_B].astype(BF16)
    return (rt, kt, kh, bh, kb, bb, v, qn, kn, qt, kbh, vb), (egc, eghc)


def _scan_dir(pa_ref, pb_ref, bi, d, prm, sa_scr, sb_scr, mk):
    ops, (egc, eghc) = yield from _chunk_operands(pa_ref, pb_ref, bi, d, prm, mk)
    rt, kt, kh, bh, kb, bb, v, qn, kn, qt, kbh, vb = ops
    chains = []
    for grp in range(D_A // PACK):
        sl = slice(PACK * grp, PACK * (grp + 1))
        chains.append(_rwkv_group(rt[:, sl], kt[:, sl], kh[:, sl], bh[:, sl], kb[:, sl], bb[:, sl], v[:, sl],
                                  sa_scr.at[bi, d, grp], egc[:, sl], mk, d))
    chains.append(_hgrn_dir(qn, kn, qt, kbh, vb, eghc, [sb_scr.at[bi, d, grp] for grp in range(D_B // PACK)],
                            mk, d))
    results = [None] * len(chains)
    live = list(enumerate(chains))
    while live:
        still = []
        for i, g in live:
            try:
                next(g)
                still.append((i, g))
            except StopIteration as stop:
                results[i] = stop.value
        live = still
        yield
    return jnp.concatenate(results[:-1], axis=1), results[-1]


def _state_slots():
    return [(bi, d, grp) for bi in range(SCAN_BATCH) for d in range(2) for grp in range(D_A // PACK)]


def _load_states(sa0_ref, sb0_ref, sa_scr, sb_scr):
    spread = jnp.where(_iota((HEAD_A, PACK), 0) == (_iota((HEAD_A, PACK), 1) & (HEAD_A - 1)), 1.0, 0.0)
    spread = spread.astype(BF16)
    bm64 = _head_mask_a()
    zero = jnp.zeros((HEAD_B, HEAD_B), F32)
    for bi, d, grp in _state_slots():
        sa_scr[bi, d, grp] = _mm_exact_rhs(sa0_ref[bi, d, grp], spread) * bm64
        h0 = sb0_ref[bi, d, 2 * grp].T
        h1 = sb0_ref[bi, d, 2 * grp + 1].T
        sb_scr[bi, d, grp] = jnp.concatenate([jnp.concatenate([h0, zero], axis=1),
                                              jnp.concatenate([zero, h1], axis=1)], axis=0)


def _store_states(sa_scr, sb_scr, sa_out, sb_out):
    gather = jnp.where((_iota((PACK, HEAD_A), 0) & (HEAD_A - 1)) == _iota((PACK, HEAD_A), 1), 1.0, 0.0)
    gather = gather.astype(BF16)
    for bi, d, grp in _state_slots():
        sa_out[bi, d, grp] = _mm_exact_rhs(sa_scr[bi, d, grp], gather)
        for hh in range(PACK // HEAD_B):
            blk = sb_scr[bi, d, grp, HEAD_B * hh:HEAD_B * (hh + 1), HEAD_B * hh:HEAD_B * (hh + 1)]
            sb_out[bi, d, 2 * grp + hh] = blk.T


def _scan_kernel(paf_ref, pbf_ref, pab_ref, pbb_ref, *rest, from_zero):
    rest, mask_refs = rest[:-len(_MASK_KEYS)], rest[-len(_MASK_KEYS):]
    if from_zero:
        (w0_ref, a0_ref, wa2_ref, kk_ref, ka_ref, lb_ref, yf_ref, yb_ref, of_ref, ob_ref, sa_out, sb_out,
         sa_scr, sb_scr) = rest
    else:
        (sa0_ref, sb0_ref, w0_ref, a0_ref, wa2_ref, kk_ref, ka_ref, lb_ref, yf_ref, yb_ref, of_ref, ob_ref,
         sa_scr, sb_scr) = rest
    ci = pl.program_id(1)

    @pl.when(ci == 0)
    def _():
        _store_masks(mask_refs)
        if from_zero:
            sa_scr[...] = jnp.zeros(sa_scr.shape, F32)
            sb_scr[...] = jnp.zeros(sb_scr.shape, F32)
        else:
            _load_states(sa0_ref, sb0_ref, sa_scr, sb_scr)

    mk = _Masks(mask_refs)
    prm = (w0_ref[...], a0_ref[...], wa2_ref, kk_ref[...], ka_ref[...], lb_ref[...])
    p_refs = ((paf_ref, pbf_ref), (pab_ref, pbb_ref))
    runs = [(bi, d) for bi in range(SCAN_BATCH) for d in range(2)]
    outs = _lockstep([_scan_dir(p_refs[d][0], p_refs[d][1], bi, d, prm, sa_scr, sb_scr, mk) for bi, d in runs])
    for (bi, d), (y, o) in zip(runs, outs):
        (yf_ref, yb_ref)[d][bi] = y.astype(BF16)
        (of_ref, ob_ref)[d][bi] = o.astype(BF16)

    if from_zero:
        @pl.when(ci == pl.num_programs(1) - 1)
        def _():
            _store_states(sa_scr, sb_scr, sa_out, sb_out)


def _scan_call(pa, pb, states, w0, a0, wa2, k_k, k_a, lb):
    bsz, seq, _ = pa.shape
    nc = seq // CHUNK
    nb = SCAN_BATCH
    from_zero = states is None
    fwd = lambda b, c: (b, c, 0)
    bwd = lambda b, c: (b, nc - 1 - c, 0)
    st = lambda b, c: (b, 0, 0, 0, 0)
    full = lambda shape: pl.BlockSpec(shape, lambda b, c: (0,) * len(shape))
    sa_block = (nb, 2, D_A // PACK, PACK, HEAD_A)
    sb_block = (nb, 2, D_B // HEAD_B, HEAD_B, HEAD_B)
    y_shape = jax.ShapeDtypeStruct((bsz, seq, D_A), BF16)
    out_specs = [pl.BlockSpec((nb, CHUNK, D_A), fwd), pl.BlockSpec((nb, CHUNK, D_A), bwd),
                 pl.BlockSpec((nb, CHUNK, D_B), fwd), pl.BlockSpec((nb, CHUNK, D_B), bwd)]
    out_shape = [y_shape, y_shape, y_shape, y_shape]
    in_specs = [pl.BlockSpec((nb, CHUNK, A_COLS), fwd), pl.BlockSpec((nb, CHUNK, B_COLS), fwd),
                pl.BlockSpec((nb, CHUNK, A_COLS), bwd), pl.BlockSpec((nb, CHUNK, B_COLS), bwd)]
    args = [pa, pb, pa, pb]
    if from_zero:
        out_specs += [pl.BlockSpec(sa_block, st), pl.BlockSpec(sb_block, st)]
        out_shape += [jax.ShapeDtypeStruct((bsz,) + sa_block[1:], F32),
                      jax.ShapeDtypeStruct((bsz,) + sb_block[1:], F32)]
    else:
        in_specs += [pl.BlockSpec(sa_block, st), pl.BlockSpec(sb_block, st)]
        args += list(states)
    in_specs += [full((2, D_A)), full((2, D_A)), full((2, LORA, 2 * D_A)),
                 full((1, D_A)), full((1, D_A)), full((2, D_B))]
    args += [w0, a0, wa2, k_k, k_a, lb]
    bd_block = (nb, 2, D_A // PACK, PACK, PACK)
    return pl.pallas_call(
        functools.partial(_scan_kernel, from_zero=from_zero),
        grid=(bsz // nb, nc),
        in_specs=in_specs,
        out_specs=out_specs,
        out_shape=out_shape,
        scratch_shapes=[pltpu.VMEM(bd_block, F32), pltpu.VMEM(bd_block, F32)]
                       + [pltpu.VMEM(shape, dtype) for shape, dtype in _MASK_SCRATCH],
        compiler_params=pltpu.CompilerParams(dimension_semantics=("arbitrary", "arbitrary"),
                                             vmem_limit_bytes=VMEM_LIMIT),
        name="scan_state" if from_zero else "scan",
    )(*args)


def _out_kernel(x_ref, pa_ref, zb_ref, yf_ref, yb_ref, of_ref, ob_ref, gate_ref, a0_ref, a2p_ref, ka_ref,
                rk_ref, lnw_ref, lnb_ref, og_ref, wout_ref, fg_ref, out_ref):
    pa = pa_ref[0]
    r = pa[:, 0:D_A]
    k = pa[:, D_A:2 * D_A]
    v = pa[:, 2 * D_A:3 * D_A]
    za = pa[:, 3 * D_A:4 * D_A]
    bm64_bf = _head_mask_a().astype(BF16)

    def seg_sum(z):
        zb16 = z.astype(BF16)
        return jnp.concatenate([_dot(zb16[:, :PACK], bm64_bf), _dot(zb16[:, PACK:], bm64_bf)], axis=1)

    y = yf_ref[0].astype(F32) + yb_ref[0].astype(F32)
    mu = seg_sum(y) * (1.0 / HEAD_A)
    dlt = y - mu
    var = seg_sum(dlt * dlt) * (1.0 / HEAD_A)
    yn = dlt * lax.rsqrt(var + GN_EPS) * lnw_ref[...] + lnb_ref[...]
    a_dirs = []
    for d in range(2):
        lo = pa[:, 4 * D_A + LORA * d:4 * D_A + LORA * (d + 1)]
        a_dirs.append(_sigmoid(a0_ref[d:d + 1] + _mm(lo, a2p_ref[d])))
    kmean = k * (1.0 + (0.5 * (a_dirs[0] + a_dirs[1]) - 1.0) * ka_ref[...])
    bonus = seg_sum(r * kmean * rk_ref[...]) * v
    out_a = (yn + bonus) * _silu(za)

    o = of_ref[0].astype(F32) + ob_ref[0].astype(F32)
    og = og_ref[...]
    zb = zb_ref[0]
    outs = [out_a]
    for h in range(D_B // HEAD_B):
        sl = slice(HEAD_B * h, HEAD_B * (h + 1))
        oh = o[:, sl]
        oh = oh * lax.rsqrt(jnp.mean(oh * oh, axis=-1, keepdims=True) + EPS) * og[:, sl]
        outs.append(oh * _silu(zb[:, sl]))
    mix = jnp.concatenate(outs, axis=1)
    proj = _dot(mix.astype(BF16), wout_ref[...])
    hs = x_ref[0] + gate_ref[0] * proj
    out_ref[0] = hs * lax.rsqrt(jnp.mean(hs * hs, axis=-1, keepdims=True) + EPS) * fg_ref[...]


def _out_call(x, pa, pb, yf, yb, of, ob, gate, a0, a2p, k_a, r_k, lnx_w, lnx_b, onorm_g, w_out_bf16, final_g):
    bsz, seq, _ = x.shape
    tm = min(OUT_ROWS, seq)
    tok = lambda b, i: (b, i, 0)
    full = lambda shape: pl.BlockSpec(shape, lambda b, i: (0,) * len(shape))
    return pl.pallas_call(
        _out_kernel,
        grid=(bsz, seq // tm),
        in_specs=[pl.BlockSpec((1, tm, D_MODEL), tok),
                  pl.BlockSpec((1, tm, A_COLS), tok),
                  pl.BlockSpec((1, tm, D_B), lambda b, i: (b, i, 4)),
                  pl.BlockSpec((1, tm, D_A), tok), pl.BlockSpec((1, tm, D_A), tok),
                  pl.BlockSpec((1, tm, D_B), tok), pl.BlockSpec((1, tm, D_B), tok),
                  pl.BlockSpec((1, 1, D_MODEL), lambda b, i: (b, 0, 0)),
                  full((2, D_A)), full((2, LORA, D_A)), full((1, D_A)), full((1, D_A)),
                  full((1, D_A)), full((1, D_A)), full((1, D_B)),
                  full((D_MODEL, D_MODEL)), full((1, D_MODEL))],
        out_specs=pl.BlockSpec((1, tm, D_MODEL), tok),
        out_shape=jax.ShapeDtypeStruct((bsz, seq, D_MODEL), F32),
        compiler_params=pltpu.CompilerParams(dimension_semantics=("arbitrary", "arbitrary"),
                                             vmem_limit_bytes=VMEM_LIMIT),
        name="out",
    )(x, pa, pb, yf, yb, of, ob, gate, a0, a2p, k_a, r_k, lnx_w, lnx_b, onorm_g, w_out_bf16, final_g)


def kernel(x_prompt, x_sample, state_rwkv, state_hgrn, c, c_ctx, norm_g, w_ada, b_ada, w_in, mu_h, mu_v, w0, w2,
           a0, a2, k_k, k_a, r_k, lnx_w, lnx_b, lb_logits, onorm_g, w_out, final_g):
    l = 0
    bp = x_prompt.shape[0]
    bs = x_sample.shape[0]
    lb_all = jnp.cumsum(jax.nn.softmax(lb_logits.astype(F32), axis=0), axis=0)
    lb = lb_all[l]

    wa_bf = w_in[l, :, :A_COLS].astype(BF16)
    wb_bf = w_in[l, :, A_COLS:].astype(BF16)
    w_out_bf = w_out[l].astype(BF16)
    zeros = jnp.zeros((2, LORA // 2, D_A), F32)
    wa2 = jnp.concatenate([jnp.concatenate([w2[l], zeros], axis=2),
                           jnp.concatenate([zeros, a2[l]], axis=2)], axis=1)
    a2p = jnp.concatenate([zeros, a2[l]], axis=1)
    row = lambda z: z.reshape(1, -1)

    cc = jnp.concatenate([c_ctx[None, :], c, jnp.zeros((16 - 1 - bs, D_MODEL), F32)], axis=0)
    m = _ada_call(cc, w_ada[l], row(b_ada[l]))
    shift, scale, gate = m[:, :D_MODEL], m[:, D_MODEL:2 * D_MODEL], m[:, 2 * D_MODEL:]
    ctx = lambda z: jnp.broadcast_to(z[0:1, None, :], (bp, 1, D_MODEL))
    lat = lambda z: z[1:1 + bs, None, :]

    def path(x, sh, sc, gt, states, grid_shift):
        pa, pb = _proj_call(x, sh, sc, row(norm_g[l]), wa_bf, wb_bf, row(mu_h[l]), row(mu_v[l]), grid_shift)
        res = _scan_call(pa, pb, states, w0[l], a0[l], wa2, row(k_k[l]), row(k_a[l]), lb)
        yf, yb, of, ob = res[:4]
        y = _out_call(x, pa, pb, yf, yb, of, ob, gt, a0[l], a2p, row(k_a[l]), row(r_k[l]), row(lnx_w[l]),
                      row(lnx_b[l]), row(onorm_g[l]), w_out_bf, row(final_g))
        return y, res[4:]

    y_prompt, (s_a, s_b) = path(x_prompt, ctx(shift), ctx(scale), ctx(gate), None, False)
    groups_a = D_A // PACK
    sa0 = state_rwkv[:, l].reshape(bs, 2, groups_a, PACK, HEAD_A)
    y_sample, _ = path(x_sample, lat(shift), lat(scale), lat(gate), (sa0, state_hgrn[:, l]), True)
    s_a = s_a.reshape(bp, 1, 2, D_A // HEAD_A, HEAD_A, HEAD_A)
    return y_prompt, y_sample, s_a, s_b[:, None]
```

```python
import functools

import jax
import jax.numpy as jnp
from jax import lax
from jax.experimental import pallas as pl
from jax.experimental.pallas import tpu as pltpu

F32 = jnp.float32
BF16 = jnp.bfloat16

D_MODEL = 1024
D_A = 512
D_B = 512
HEAD_A = 64
HEAD_B = 128
LOG2_HEAD_A = HEAD_A.bit_length() - 1
LORA = 128
A_COLS = 4 * D_A + 2 * LORA
B_COLS = 5 * D_B
GRID_W = 64
CHUNK = 64
SCAN_BATCH = 4
SUM_TERMS = 2
BLOCK_ROWS = 32
BLOCK_LANES = 128
EPS = 1e-6
GN_EPS = 64e-5
DECAY_SCALE = 0.6065306597126334
NEG_LOG2E = -1.4426950408889634
PACK = 256
TNA = 768
TNB = 512
NA = A_COLS // TNA
NB = B_COLS // TNB
PROJ_ROWS = 2048
OUT_ROWS = 512
VMEM_LIMIT = 56 * 1024 * 1024

NN = ((1,), (0,))
NT = ((1,), (1,))
TN_DIMS = ((0,), (0,))


def _dot(a, b, dims=NN):
    return lax.dot_general(a, b, (dims, ((), ())), preferred_element_type=F32)


def _split(x, passes=1):
    if isinstance(x, tuple):
        return x
    hi = x.astype(BF16)
    if passes == 1:
        return (hi,)
    return hi, (x - hi.astype(F32)).astype(BF16)


def _mm(a, b, dims=NN, passes=1):
    a = _split(a, passes)
    b = _split(b, passes)
    if len(a) == 1 or len(b) == 1:
        return _dot(a[0], b[0], dims)
    return _dot(a[0], b[0], dims) + (_dot(a[0], b[1], dims) + _dot(a[1], b[0], dims))


def _terms(x, n):
    out = []
    for _ in range(n - 1):
        h = x.astype(BF16)
        out.append(h)
        x = x - h.astype(F32)
    out.append(x.astype(BF16))
    return out


def _sum_small_first(parts):
    return functools.reduce(lambda acc, z: z + acc, reversed(parts))


def _mm_exact_rhs(a, b_bf16, n=3):
    return _sum_small_first([_dot(t, b_bf16) for t in _terms(a, n)])


def _exp_neg(x):
    return jnp.exp2(x * NEG_LOG2E)


def _sigmoid(x):
    return 0.5 * jnp.tanh(0.5 * x) + 0.5


def _silu(x):
    return x * _sigmoid(x)


def _iota(shape, dim):
    return lax.broadcasted_iota(jnp.int32, shape, dim)


def _block_mask(rows, cols, row_shift, col_shift):
    same = (_iota((rows, cols), 0) >> row_shift) == (_iota((rows, cols), 1) >> col_shift)
    return jnp.where(same, 1.0, 0.0).astype(F32)


def _head_mask_a():
    return _block_mask(PACK, PACK, LOG2_HEAD_A, LOG2_HEAD_A)


def _ada_kernel(c_ref, w_ref, b_ref, m_ref):
    m_ref[...] = _mm(_silu(c_ref[...]), w_ref[...], passes=3) + b_ref[...]


def _ada_call(cc, w_ada, b_ada):
    rows = cc.shape[0]
    return pl.pallas_call(
        _ada_kernel,
        grid=(3,),
        in_specs=[pl.BlockSpec((rows, D_MODEL), lambda j: (0, 0)),
                  pl.BlockSpec((D_MODEL, D_MODEL), lambda j: (0, j)),
                  pl.BlockSpec((1, D_MODEL), lambda j: (0, j))],
        out_specs=pl.BlockSpec((rows, D_MODEL), lambda j: (0, j)),
        out_shape=jax.ShapeDtypeStruct((rows, 3 * D_MODEL), F32),
        compiler_params=pltpu.CompilerParams(dimension_semantics=("arbitrary",),
                                             vmem_limit_bytes=VMEM_LIMIT),
        name="ada",
    )(cc, w_ada, b_ada)


def _proj_kernel(x_ref, sh_ref, sc_ref, g_ref, wa_ref, wb_ref, muh_ref, muv_ref, pa_ref, pb_ref, xm_ref, *,
                 nbp, seq, grid_shift):
    j = pl.program_id(1)
    rows = nbp * seq

    @pl.when(j == 0)
    def _():
        for b in range(nbp):
            x = x_ref[b]
            xn = x * lax.rsqrt(jnp.mean(x * x, axis=-1, keepdims=True) + EPS) * g_ref[...]
            xm_ref[b * seq:(b + 1) * seq, :] = (xn * (1.0 + sc_ref[b]) + sh_ref[b]).astype(BF16)

    @pl.when(j < NA)
    def _():
        p = _dot(xm_ref[...], wa_ref[...])
        w = GRID_W if grid_shift else seq
        nblk = rows // w
        pos = _iota((1, w, TNA), 1)
        to3 = lambda z: z.reshape(nblk, w, TNA)
        prev = jnp.where(pos == 0, 0.0, to3(pltpu.roll(p, 1, axis=0)))
        nxt = jnp.where(pos == w - 1, 0.0, to3(pltpu.roll(p, rows - 1, axis=0)))
        muh = muh_ref[...]
        out = (0.5 * muh) * (prev + nxt)
        if grid_shift:
            muv = muv_ref[...]
            p3 = to3(p)
            zblk = jnp.zeros((1, w, TNA), F32)
            up = jnp.concatenate([zblk, p3[:nblk - 1]], axis=0)
            dn = jnp.concatenate([p3[1:], zblk], axis=0)
            out = (1.0 - muh - muv) * p3 + out + (0.5 * muv) * (up + dn)
        else:
            out = (1.0 - muh) * to3(p) + out
        pa_ref[...] = out.reshape(nbp, seq, TNA)

    @pl.when(j >= NA)
    def _():
        pb_ref[...] = _dot(xm_ref[...], wb_ref[...]).reshape(nbp, seq, TNB)


def _proj_call(x, shift, scale, norm_g, wa_bf16, wb_bf16, mu_h, mu_v, grid_shift):
    bsz, seq, _ = x.shape
    nbp = 1 if grid_shift else PROJ_ROWS // seq
    kern = functools.partial(_proj_kernel, nbp=nbp, seq=seq, grid_shift=grid_shift)
    a_idx = lambda j: jnp.minimum(j, NA - 1)
    b_idx = lambda j: jnp.maximum(j - NA, 0)
    return pl.pallas_call(
        kern,
        grid=(bsz // nbp, NA + NB),
        in_specs=[pl.BlockSpec((nbp, seq, D_MODEL), lambda b, j: (b, 0, 0)),
                  pl.BlockSpec((nbp, 1, D_MODEL), lambda b, j: (b, 0, 0)),
                  pl.BlockSpec((nbp, 1, D_MODEL), lambda b, j: (b, 0, 0)),
                  pl.BlockSpec((1, D_MODEL), lambda b, j: (0, 0)),
                  pl.BlockSpec((D_MODEL, TNA), lambda b, j: (0, a_idx(j))),
                  pl.BlockSpec((D_MODEL, TNB), lambda b, j: (0, b_idx(j))),
                  pl.BlockSpec((1, TNA), lambda b, j: (0, a_idx(j))),
                  pl.BlockSpec((1, TNA), lambda b, j: (0, a_idx(j)))],
        out_specs=[pl.BlockSpec((nbp, seq, TNA), lambda b, j: (b, 0, a_idx(j))),
                   pl.BlockSpec((nbp, seq, TNB), lambda b, j: (b, 0, b_idx(j)))],
        out_shape=[jax.ShapeDtypeStruct((bsz, seq, A_COLS), F32),
                   jax.ShapeDtypeStruct((bsz, seq, B_COLS), F32)],
        scratch_shapes=[pltpu.VMEM((nbp * seq, D_MODEL), BF16)],
        compiler_params=pltpu.CompilerParams(dimension_semantics=("arbitrary", "arbitrary"),
                                             vmem_limit_bytes=VMEM_LIMIT),
        name="proj_grid" if grid_shift else "proj_seq",
    )(x, shift, scale, norm_g, wa_bf16, wb_bf16, mu_h, mu_v)


def _scan_masks():
    c = CHUNK
    t = _iota((c, PACK), 0)
    s = _iota((c, PACK), 1) & (c - 1)
    tt = _iota((c, c), 0)
    ss = _iota((c, c), 1)
    f01 = lambda cond: jnp.where(cond, 1.0, 0.0).astype(F32)
    return dict(
        bm64_bf=_head_mask_a().astype(BF16),
        eye=f01(t == s),
        incl=(f01(t >= s), f01(t <= s)),
        strict=(f01(t > s), f01(t < s)),
        tri=(f01(tt >= ss).astype(BF16), f01(tt <= ss).astype(BF16)),
    )


_MASK_KEYS = ("bm64_bf", "eye", "incl", "strict", "tri")
_MASK_SCRATCH = (((PACK, PACK), BF16), ((CHUNK, PACK), F32), ((2, CHUNK, PACK), F32), ((2, CHUNK, PACK), F32),
                 ((2, CHUNK, CHUNK), BF16))


def _store_masks(mask_refs):
    masks = _scan_masks()
    for ref, key in zip(mask_refs, _MASK_KEYS):
        val = masks[key]
        if isinstance(val, tuple):
            for i, z in enumerate(val):
                ref[i] = z
        else:
            ref[...] = val


class _Masks:
    def __init__(self, mask_refs):
        self._refs = dict(zip(_MASK_KEYS, mask_refs))

    def __getitem__(self, key):
        ref = self._refs[key]
        return (ref[0], ref[1]) if len(ref.shape) == 3 else ref[...]


def _half_lane_masks(dtype):
    lane = _iota((1, BLOCK_LANES), 1)
    return (jnp.where(lane < HEAD_A, 1.0, 0.0).astype(dtype), jnp.where(lane >= HEAD_A, 1.0, 0.0).astype(dtype))


def _block_diag_a(x):
    halves = _half_lane_masks(x.dtype)
    zero = jnp.zeros((x.shape[0], BLOCK_LANES), x.dtype)
    rows = []
    for h in range(PACK // HEAD_A):
        tile = h * HEAD_A // BLOCK_LANES
        piece = x[:, BLOCK_LANES * tile:BLOCK_LANES * (tile + 1)] * halves[h % 2]
        rows.append(jnp.concatenate([piece, zero] if tile == 0 else [zero, piece], axis=1))
    return jnp.concatenate(rows, axis=0)


def _block_diag_b(x):
    heads = D_B // HEAD_B
    zero = jnp.zeros((x.shape[0], HEAD_B), x.dtype)
    return jnp.concatenate(
        [jnp.concatenate([x[:, HEAD_B * h:HEAD_B * (h + 1)] if t == h else zero for t in range(heads)], axis=1)
         for h in range(heads)], axis=0)


def _lockstep(gens):
    results = [None] * len(gens)
    live = list(enumerate(gens))
    while live:
        still = []
        for i, g in live:
            try:
                next(g)
                still.append((i, g))
            except StopIteration as stop:
                results[i] = stop.value
        live = still
    return results


def _rwkv_group(rt, kt, kh, bh, kb, bb, v, s_ref, egc, mk, d):
    c = CHUNK
    bd = lambda x: _block_diag_a(x.astype(BF16))
    kr = _split(jnp.concatenate([kt, rt], axis=0))
    aa_k = _mm(kr, bd(kh), NT)
    aa_b = _mm(kr, bd(bh), NT)
    s_bd = s_ref[...]
    krs = _mm(kr, s_bd, NT)
    yield
    a_kk = aa_k[:c] * mk["strict"][d]
    a_rk = aa_k[c:] * mk["incl"][d]
    n = -(aa_b[:c] * mk["strict"][d])
    a_rb = aa_b[c:] * mk["incl"][d]
    x = mk["eye"] + n
    p = _mm(n, bd(n))
    akv = _mm(jnp.concatenate([a_kk, a_rk], axis=0), bd(v))
    yield
    for i in range(5):
        if i < 4:
            xp = _mm(jnp.concatenate([x, p], axis=0), bd(p))
            x = x + xp[:c]
            p = xp[c:]
        else:
            x = x + _mm(x, bd(p))
        yield
    u = _mm(x, bd(krs[:c] + akv[:c]))
    yield
    y = krs[c:] + akv[c:] - _mm(a_rb, bd(u))
    vu = jnp.concatenate([v, (-u).astype(BF16)], axis=0)
    kbb = jnp.concatenate([kb, bb], axis=0)
    upd = _mm(vu, kbb, TN_DIMS)
    halves = _half_lane_masks(F32)
    zero = jnp.zeros((HEAD_A, BLOCK_LANES), F32)
    rows = []
    for h in range(PACK // HEAD_A):
        tile = h * HEAD_A // BLOCK_LANES
        rs, ls = slice(HEAD_A * h, HEAD_A * (h + 1)), slice(BLOCK_LANES * tile, BLOCK_LANES * (tile + 1))
        piece = (s_bd[rs, ls] * egc[:, ls] + upd[rs, ls]) * halves[h % 2]
        rows.append(jnp.concatenate([piece, zero] if tile == 0 else [zero, piece], axis=1))
    s_ref[...] = jnp.concatenate(rows, axis=0)
    return y


def _hgrn_dir(qn, kn, qt, kbh, vb, eghc, s_refs, mk, d):
    a_p = _mm(qn, _block_diag_b(kn.astype(BF16)), NT) * mk["incl"][d]
    inter = []
    for grp, s_ref in enumerate(s_refs):
        sl = slice(PACK * grp, PACK * (grp + 1))
        st = s_ref[...]
        inter.append(_mm(qt[:, sl], st, NT))
        upd = _mm(vb[:, sl], kbh[:, sl], TN_DIMS)
        decay = eghc[:, sl]
        zero = jnp.zeros((HEAD_B, HEAD_B), F32)
        blocks = [st[b0:b0 + HEAD_B, b0:b0 + HEAD_B] * decay[:, b0:b0 + HEAD_B] + upd[b0:b0 + HEAD_B, b0:b0 + HEAD_B]
                  for b0 in range(0, PACK, HEAD_B)]
        s_ref[...] = jnp.concatenate([jnp.concatenate([blocks[0], zero], axis=1),
                                      jnp.concatenate([zero, blocks[1]], axis=1)], axis=0)
    yield
    o = _mm(a_p, _block_diag_b(vb.astype(BF16)))
    return o + jnp.concatenate(inter, axis=1)


def _blockwise(fn, operands, n_out, rows=CHUNK, width=D_A):
    cols = []
    for l0 in range(0, width, BLOCK_LANES):
        parts = []
        for r0 in range(0, rows, BLOCK_ROWS):
            blk = [z[(slice(None) if z.shape[0] == 1 else slice(r0, r0 + BLOCK_ROWS)), l0:l0 + BLOCK_LANES]
                   for z in operands]
            parts.append(fn(*blk))
        cols.append([jnp.concatenate([p[i] for p in parts], axis=0) for i in range(n_out)])
    return [jnp.concatenate([col[i] for col in cols], axis=1) for i in range(n_out)]


def _chunk_operands(pa_ref, pb_ref, bi, d, prm, mk):
    c = CHUNK
    n_t = SUM_TERMS
    w0, a0, wa2, k_k, k_a, r_k, lb = prm
    cols = lambda ref, i, width=D_A: ref.at[bi, :, i * width:(i + 1) * width]
    r_ref, k_ref, v_ref = cols(pa_ref, 0), cols(pa_ref, 1), cols(pa_ref, 2)
    lo = pa_ref[bi, :, 4 * D_A + LORA * d:4 * D_A + LORA * (d + 1)]
    lo = jnp.where(_iota((c, LORA), 1) < LORA // 2, jnp.tanh(lo), lo)
    wa = _mm(lo, wa2[d])
    yield

    def gates(wa_w, wa_a, k, fr, w0_, a0_, kk_, lb_):
        lw = -DECAY_SCALE * _sigmoid(w0_ + wa_w)
        a = _sigmoid(a0_ + wa_a)
        kk = k * kk_
        f = lb_ + (1.0 - lb_) * _sigmoid(fr)
        gf = jnp.log(f)
        return (lw, a, kk, 1.0 - f, (kk * kk).astype(BF16), *_terms(lw, n_t), *_terms(gf, n_t))

    res = _blockwise(gates, [wa[:, :D_A], wa[:, D_A:], k_ref, cols(pb_ref, 2 + d, D_B), w0[d:d + 1], a0[d:d + 1],
                             k_k, lb[d:d + 1]], 5 + 2 * n_t)
    lw, a, kk, kf, sq = res[:5]
    lw_t, gf_t = res[5:5 + n_t], res[5 + n_t:]
    seg = lambda lanes: _dot(sq[:, lanes], mk["bm64_bf"])
    n2 = jnp.concatenate([seg(slice(0, PACK)), seg(slice(PACK, D_A))], axis=1)
    gcum = _sum_small_first([_dot(mk["tri"][d], jnp.concatenate([tl, tg], axis=1))
                             for tl, tg in zip(lw_t, gf_t)])
    yield
    last = c - 1 if d == 0 else 0
    g, gh = gcum[:, :D_A], gcum[:, D_A:]
    gc, ghc, ghm = g[last:last + 1], gh[last:last + 1], gh[c // 2:c // 2 + 1]
    egc, eghc = jnp.exp(gc), jnp.exp(ghc)

    def rwkv_operands(r, k, kk_, n2_, a_, g_, lw_, ka_, rk_, egc_):
        kap = kk_ * lax.rsqrt(jnp.maximum(n2_, 1e-24))
        keff = k * (1.0 + (a_ - 1.0) * ka_)
        eng = _exp_neg(g_)
        kh = keff * eng
        bh = (kap * a_) * eng
        out = (r * jnp.exp(g_), kap * jnp.exp(g_ - lw_), kh, bh, kh * egc_, bh * egc_, r * keff * rk_)
        return tuple(z.astype(BF16) for z in out)

    rt, kt, kh, bh, kb, bb, rkr = _blockwise(rwkv_operands, [r_ref, k_ref, kk, n2, a, g, lw, k_a, r_k, egc], 7)
    bonus = jnp.concatenate([_dot(rkr[:, :PACK], mk["bm64_bf"]), _dot(rkr[:, PACK:], mk["bm64_bf"])], axis=1)

    def hgrn_operands(qr, kf_, gh_, ghm_, e_m, e_cm):
        q = _silu(qr)
        ghn = gh_ - ghm_
        qn = q * jnp.exp(ghn)
        kn = kf_ * _exp_neg(ghn)
        return tuple(z.astype(BF16) for z in (qn, kn, qn * e_m, kn * e_cm))

    qn, kn, qt, kbh = _blockwise(hgrn_operands, [cols(pb_ref, 0, D_B), kf, gh, ghm, jnp.exp(ghm),
                                                 jnp.exp(ghc - ghm)], 4, width=D_B)
    v = v_ref[...].astype(BF16)
    vb = pb_ref[bi, :, D_B:2 * D_B].astype(BF16)
    return (rt, kt, kh, bh, kb, bb, v, qn, kn, qt, kbh, vb), (egc, eghc), bonus


def _scan_dir(pa_ref, pb_ref, bi, d, prm, sa_scr, sb_scr, mk):
    ops, (egc, eghc), bonus = yield from _chunk_operands(pa_ref, pb_ref, bi, d, prm, mk)
    rt, kt, kh, bh, kb, bb, v, qn, kn, qt, kbh, vb = ops
    chains = []
    for grp in range(D_A // PACK):
        sl = slice(PACK * grp, PACK * (grp + 1))
        chains.append(_rwkv_group(rt[:, sl], kt[:, sl], kh[:, sl], bh[:, sl], kb[:, sl], bb[:, sl], v[:, sl],
                                  sa_scr.at[bi, d, grp], egc[:, sl], mk, d))
    chains.append(_hgrn_dir(qn, kn, qt, kbh, vb, eghc, [sb_scr.at[bi, d, grp] for grp in range(D_B // PACK)],
                            mk, d))
    results = [None] * len(chains)
    live = list(enumerate(chains))
    while live:
        still = []
        for i, g in live:
            try:
                next(g)
                still.append((i, g))
            except StopIteration as stop:
                results[i] = stop.value
        live = still
        yield
    return jnp.concatenate(results[:-1], axis=1), results[-1], bonus


def _state_slots():
    return [(bi, d, grp) for bi in range(SCAN_BATCH) for d in range(2) for grp in range(D_A // PACK)]


def _load_states(sa0_ref, sb0_ref, sa_scr, sb_scr):
    spread = jnp.where(_iota((HEAD_A, PACK), 0) == (_iota((HEAD_A, PACK), 1) & (HEAD_A - 1)), 1.0, 0.0)
    spread = spread.astype(BF16)
    bm64 = _head_mask_a()
    zero = jnp.zeros((HEAD_B, HEAD_B), F32)
    for bi, d, grp in _state_slots():
        sa_scr[bi, d, grp] = _mm_exact_rhs(sa0_ref[bi, d, grp], spread) * bm64
        h0 = sb0_ref[bi, d, 2 * grp].T
        h1 = sb0_ref[bi, d, 2 * grp + 1].T
        sb_scr[bi, d, grp] = jnp.concatenate([jnp.concatenate([h0, zero], axis=1),
                                              jnp.concatenate([zero, h1], axis=1)], axis=0)


def _store_states(sa_scr, sb_scr, sa_out, sb_out):
    gather = jnp.where((_iota((PACK, HEAD_A), 0) & (HEAD_A - 1)) == _iota((PACK, HEAD_A), 1), 1.0, 0.0)
    gather = gather.astype(BF16)
    for bi, d, grp in _state_slots():
        sa_out[bi, d, grp] = _mm_exact_rhs(sa_scr[bi, d, grp], gather)
        for hh in range(PACK // HEAD_B):
            blk = sb_scr[bi, d, grp, HEAD_B * hh:HEAD_B * (hh + 1), HEAD_B * hh:HEAD_B * (hh + 1)]
            sb_out[bi, d, 2 * grp + hh] = blk.T


def _scan_kernel(paf_ref, pbf_ref, pab_ref, pbb_ref, *rest, from_zero):
    rest, mask_refs = rest[:-len(_MASK_KEYS)], rest[-len(_MASK_KEYS):]
    if from_zero:
        (w0_ref, a0_ref, wa2_ref, kk_ref, ka_ref, rk_ref, lb_ref, yf_ref, yb_ref, of_ref, ob_ref, bf_ref, bb_ref,
         sa_out, sb_out, sa_scr, sb_scr) = rest
    else:
        (sa0_ref, sb0_ref, w0_ref, a0_ref, wa2_ref, kk_ref, ka_ref, rk_ref, lb_ref, yf_ref, yb_ref, of_ref, ob_ref,
         bf_ref, bb_ref, sa_scr, sb_scr) = rest
    ci = pl.program_id(1)

    @pl.when(ci == 0)
    def _():
        _store_masks(mask_refs)
        if from_zero:
            sa_scr[...] = jnp.zeros(sa_scr.shape, F32)
            sb_scr[...] = jnp.zeros(sb_scr.shape, F32)
        else:
            _load_states(sa0_ref, sb0_ref, sa_scr, sb_scr)

    mk = _Masks(mask_refs)
    prm = (w0_ref[...], a0_ref[...], wa2_ref, kk_ref[...], ka_ref[...], rk_ref[...], lb_ref[...])
    p_refs = ((paf_ref, pbf_ref), (pab_ref, pbb_ref))
    runs = [(bi, d) for bi in range(SCAN_BATCH) for d in range(2)]
    outs = _lockstep([_scan_dir(p_refs[d][0], p_refs[d][1], bi, d, prm, sa_scr, sb_scr, mk) for bi, d in runs])
    for (bi, d), (y, o, bonus) in zip(runs, outs):
        (yf_ref, yb_ref)[d][bi] = y.astype(BF16)
        (of_ref, ob_ref)[d][bi] = o.astype(BF16)
        (bf_ref, bb_ref)[d][bi] = bonus.astype(BF16)

    if from_zero:
        @pl.when(ci == pl.num_programs(1) - 1)
        def _():
            _store_states(sa_scr, sb_scr, sa_out, sb_out)


def _scan_call(pa, pb, states, w0, a0, wa2, k_k, k_a, r_k, lb):
    bsz, seq, _ = pa.shape
    nc = seq // CHUNK
    nb = SCAN_BATCH
    from_zero = states is None
    fwd = lambda b, c: (b, c, 0)
    bwd = lambda b, c: (b, nc - 1 - c, 0)
    st = lambda b, c: (b, 0, 0, 0, 0)
    full = lambda shape: pl.BlockSpec(shape, lambda b, c: (0,) * len(shape))
    sa_block = (nb, 2, D_A // PACK, PACK, HEAD_A)
    sb_block = (nb, 2, D_B // HEAD_B, HEAD_B, HEAD_B)
    y_shape = jax.ShapeDtypeStruct((bsz, seq, D_A), BF16)
    out_specs = [pl.BlockSpec((nb, CHUNK, D_A), fwd), pl.BlockSpec((nb, CHUNK, D_A), bwd),
                 pl.BlockSpec((nb, CHUNK, D_B), fwd), pl.BlockSpec((nb, CHUNK, D_B), bwd),
                 pl.BlockSpec((nb, CHUNK, D_A), fwd), pl.BlockSpec((nb, CHUNK, D_A), bwd)]
    out_shape = [y_shape] * len(out_specs)
    in_specs = [pl.BlockSpec((nb, CHUNK, A_COLS), fwd), pl.BlockSpec((nb, CHUNK, B_COLS), fwd),
                pl.BlockSpec((nb, CHUNK, A_COLS), bwd), pl.BlockSpec((nb, CHUNK, B_COLS), bwd)]
    args = [pa, pb, pa, pb]
    if from_zero:
        out_specs += [pl.BlockSpec(sa_block, st), pl.BlockSpec(sb_block, st)]
        out_shape += [jax.ShapeDtypeStruct((bsz,) + sa_block[1:], F32),
                      jax.ShapeDtypeStruct((bsz,) + sb_block[1:], F32)]
    else:
        in_specs += [pl.BlockSpec(sa_block, st), pl.BlockSpec(sb_block, st)]
        args += list(states)
    in_specs += [full((2, D_A)), full((2, D_A)), full((2, LORA, 2 * D_A)),
                 full((1, D_A)), full((1, D_A)), full((1, D_A)), full((2, D_B))]
    args += [w0, a0, wa2, k_k, k_a, r_k, lb]
    bd_block = (nb, 2, D_A // PACK, PACK, PACK)
    return pl.pallas_call(
        functools.partial(_scan_kernel, from_zero=from_zero),
        grid=(bsz // nb, nc),
        in_specs=in_specs,
        out_specs=out_specs,
        out_shape=out_shape,
        scratch_shapes=[pltpu.VMEM(bd_block, F32), pltpu.VMEM(bd_block, F32)]
                       + [pltpu.VMEM(shape, dtype) for shape, dtype in _MASK_SCRATCH],
        compiler_params=pltpu.CompilerParams(dimension_semantics=("arbitrary", "arbitrary"),
                                             vmem_limit_bytes=VMEM_LIMIT),
        name="scan_state" if from_zero else "scan",
    )(*args)


def _out_kernel(x_ref, vz_ref, zb_ref, yf_ref, yb_ref, of_ref, ob_ref, bf_ref, bb_ref, gate_ref, lnw_ref, lnb_ref,
                og_ref, wout_ref, fg_ref, out_ref):
    v = vz_ref[0, :, 0:D_A]
    za = vz_ref[0, :, D_A:2 * D_A]
    bm64_bf = _head_mask_a().astype(BF16)

    def seg_sum(z):
        zb16 = z.astype(BF16)
        return jnp.concatenate([_dot(zb16[:, :PACK], bm64_bf), _dot(zb16[:, PACK:], bm64_bf)], axis=1)

    y = yf_ref[0].astype(F32) + yb_ref[0].astype(F32)
    mu = seg_sum(y) * (1.0 / HEAD_A)
    dlt = y - mu
    var = seg_sum(dlt * dlt) * (1.0 / HEAD_A)
    yn = dlt * lax.rsqrt(var + GN_EPS) * lnw_ref[...] + lnb_ref[...]
    bonus = 0.5 * (bf_ref[0].astype(F32) + bb_ref[0].astype(F32)) * v
    out_a = (yn + bonus) * _silu(za)

    o = of_ref[0].astype(F32) + ob_ref[0].astype(F32)
    og = og_ref[...]
    zb = zb_ref[0]
    outs = [out_a]
    for h in range(D_B // HEAD_B):
        sl = slice(HEAD_B * h, HEAD_B * (h + 1))
        oh = o[:, sl]
        oh = oh * lax.rsqrt(jnp.mean(oh * oh, axis=-1, keepdims=True) + EPS) * og[:, sl]
        outs.append(oh * _silu(zb[:, sl]))
    mix = jnp.concatenate(outs, axis=1)
    proj = _dot(mix.astype(BF16), wout_ref[...])
    hs = x_ref[0] + gate_ref[0] * proj
    out_ref[0] = hs * lax.rsqrt(jnp.mean(hs * hs, axis=-1, keepdims=True) + EPS) * fg_ref[...]


def _out_call(x, pa, pb, scan_out, gate, lnx_w, lnx_b, onorm_g, w_out_bf16, final_g):
    bsz, seq, _ = x.shape
    tm = min(OUT_ROWS, seq)
    tok = lambda b, i: (b, i, 0)
    full = lambda shape: pl.BlockSpec(shape, lambda b, i: (0,) * len(shape))
    return pl.pallas_call(
        _out_kernel,
        grid=(bsz, seq // tm),
        in_specs=[pl.BlockSpec((1, tm, D_MODEL), tok),
                  pl.BlockSpec((1, tm, 2 * D_A), lambda b, i: (b, i, 1)),
                  pl.BlockSpec((1, tm, D_B), lambda b, i: (b, i, 4)),
                  pl.BlockSpec((1, tm, D_A), tok), pl.BlockSpec((1, tm, D_A), tok),
                  pl.BlockSpec((1, tm, D_B), tok), pl.BlockSpec((1, tm, D_B), tok),
                  pl.BlockSpec((1, tm, D_A), tok), pl.BlockSpec((1, tm, D_A), tok),
                  pl.BlockSpec((1, 1, D_MODEL), lambda b, i: (b, 0, 0)),
                  full((1, D_A)), full((1, D_A)), full((1, D_B)),
                  full((D_MODEL, D_MODEL)), full((1, D_MODEL))],
        out_specs=pl.BlockSpec((1, tm, D_MODEL), tok),
        out_shape=jax.ShapeDtypeStruct((bsz, seq, D_MODEL), F32),
        compiler_params=pltpu.CompilerParams(dimension_semantics=("arbitrary", "arbitrary"),
                                             vmem_limit_bytes=VMEM_LIMIT),
        name="out",
    )(x, pa, pb, *scan_out, gate, lnx_w, lnx_b, onorm_g, w_out_bf16, final_g)


def kernel(x_prompt, x_sample, state_rwkv, state_hgrn, c, c_ctx, norm_g, w_ada, b_ada, w_in, mu_h, mu_v, w0, w2,
           a0, a2, k_k, k_a, r_k, lnx_w, lnx_b, lb_logits, onorm_g, w_out, final_g):
    l = 0
    bp = x_prompt.shape[0]
    bs = x_sample.shape[0]
    lb_all = jnp.cumsum(jax.nn.softmax(lb_logits.astype(F32), axis=0), axis=0)
    lb = lb_all[l]

    wa_bf = w_in[l, :, :A_COLS].astype(BF16)
    wb_bf = w_in[l, :, A_COLS:].astype(BF16)
    w_out_bf = w_out[l].astype(BF16)
    zeros = jnp.zeros((2, LORA // 2, D_A), F32)
    wa2 = jnp.concatenate([jnp.concatenate([w2[l], zeros], axis=2),
                           jnp.concatenate([zeros, a2[l]], axis=2)], axis=1)
    row = lambda z: z.reshape(1, -1)

    cc = jnp.concatenate([c_ctx[None, :], c, jnp.zeros((16 - 1 - bs, D_MODEL), F32)], axis=0)
    m = _ada_call(cc, w_ada[l], row(b_ada[l]))
    shift, scale, gate = m[:, :D_MODEL], m[:, D_MODEL:2 * D_MODEL], m[:, 2 * D_MODEL:]
    ctx = lambda z: jnp.broadcast_to(z[0:1, None, :], (bp, 1, D_MODEL))
    lat = lambda z: z[1:1 + bs, None, :]

    def path(x, sh, sc, gt, states, grid_shift):
        pa, pb = _proj_call(x, sh, sc, row(norm_g[l]), wa_bf, wb_bf, row(mu_h[l]), row(mu_v[l]), grid_shift)
        res = _scan_call(pa, pb, states, w0[l], a0[l], wa2, row(k_k[l]), row(k_a[l]), row(r_k[l]), lb)
        y = _out_call(x, pa, pb, res[:6], gt, row(lnx_w[l]), row(lnx_b[l]), row(onorm_g[l]), w_out_bf,
                      row(final_g))
        return y, res[6:]

    y_prompt, (s_a, s_b) = path(x_prompt, ctx(shift), ctx(scale), ctx(gate), None, False)
    groups_a = D_A // PACK
    sa0 = state_rwkv[:, l].reshape(bs, 2, groups_a, PACK, HEAD_A)
    y_sample, _ = path(x_sample, lat(shift), lat(scale), lat(gate), (sa0, state_hgrn[:, l]), True)
    s_a = s_a.reshape(bp, 1, 2, D_A // HEAD_A, HEAD_A, HEAD_A)
    return y_prompt, y_sample, s_a, s_b[:, None]
```

```python
import functools

import jax
import jax.numpy as jnp
from jax import lax
from jax.experimental import pallas as pl
from jax.experimental.pallas import tpu as pltpu

F32 = jnp.float32
BF16 = jnp.bfloat16

D_MODEL = 1024
D_A = 512
D_B = 512
HEAD_A = 64
HEAD_B = 128
LOG2_HEAD_A = HEAD_A.bit_length() - 1
LORA = 128
A_COLS = 4 * D_A + 2 * LORA
B_COLS = 5 * D_B
GRID_W = 64
CHUNK = 64
SCAN_BATCH = 4
SUM_TERMS = 2
BLOCK_ROWS = 32
BLOCK_LANES = 128
EPS = 1e-6
GN_EPS = 64e-5
DECAY_SCALE = 0.6065306597126334
NEG_LOG2E = -1.4426950408889634
PACK = 256
TNA = 768
TNB = 512
NA = A_COLS // TNA
NB = B_COLS // TNB
PROJ_ROWS = 2048
OUT_ROWS = 512
VMEM_LIMIT = 56 * 1024 * 1024

NN = ((1,), (0,))
NT = ((1,), (1,))
TN_DIMS = ((0,), (0,))


def _dot(a, b, dims=NN):
    return lax.dot_general(a, b, (dims, ((), ())), preferred_element_type=F32)


def _split(x, passes=1):
    if isinstance(x, tuple):
        return x
    hi = x.astype(BF16)
    if passes == 1:
        return (hi,)
    return hi, (x - hi.astype(F32)).astype(BF16)


def _mm(a, b, dims=NN, passes=1):
    a = _split(a, passes)
    b = _split(b, passes)
    if len(a) == 1 or len(b) == 1:
        return _dot(a[0], b[0], dims)
    return _dot(a[0], b[0], dims) + (_dot(a[0], b[1], dims) + _dot(a[1], b[0], dims))


def _terms(x, n):
    out = []
    for _ in range(n - 1):
        h = x.astype(BF16)
        out.append(h)
        x = x - h.astype(F32)
    out.append(x.astype(BF16))
    return out


def _sum_small_first(parts):
    return functools.reduce(lambda acc, z: z + acc, reversed(parts))


def _mm_exact_rhs(a, b_bf16, n=3):
    return _sum_small_first([_dot(t, b_bf16) for t in _terms(a, n)])


def _exp_neg(x):
    return jnp.exp2(x * NEG_LOG2E)


def _sigmoid(x):
    return 0.5 * jnp.tanh(0.5 * x) + 0.5


def _silu(x):
    return x * _sigmoid(x)


def _iota(shape, dim):
    return lax.broadcasted_iota(jnp.int32, shape, dim)


def _block_mask(rows, cols, row_shift, col_shift):
    same = (_iota((rows, cols), 0) >> row_shift) == (_iota((rows, cols), 1) >> col_shift)
    return jnp.where(same, 1.0, 0.0).astype(F32)


def _head_mask_a():
    return _block_mask(PACK, PACK, LOG2_HEAD_A, LOG2_HEAD_A)


def _ada_kernel(c_ref, w_ref, b_ref, m_ref):
    m_ref[...] = _mm(_silu(c_ref[...]), w_ref[...], passes=3) + b_ref[...]


def _ada_call(cc, w_ada, b_ada):
    rows = cc.shape[0]
    return pl.pallas_call(
        _ada_kernel,
        grid=(3,),
        in_specs=[pl.BlockSpec((rows, D_MODEL), lambda j: (0, 0)),
                  pl.BlockSpec((D_MODEL, D_MODEL), lambda j: (0, j)),
                  pl.BlockSpec((1, D_MODEL), lambda j: (0, j))],
        out_specs=pl.BlockSpec((rows, D_MODEL), lambda j: (0, j)),
        out_shape=jax.ShapeDtypeStruct((rows, 3 * D_MODEL), F32),
        compiler_params=pltpu.CompilerParams(dimension_semantics=("arbitrary",),
                                             vmem_limit_bytes=VMEM_LIMIT),
        name="ada",
    )(cc, w_ada, b_ada)


def _proj_kernel(x_ref, sh_ref, sc_ref, g_ref, wa_ref, wb_ref, muh_ref, muv_ref, pa_ref, pb_ref, xm_ref, *,
                 nbp, seq, grid_shift):
    j = pl.program_id(1)
    rows = nbp * seq

    @pl.when(j == 0)
    def _():
        for b in range(nbp):
            x = x_ref[b]
            xn = x * lax.rsqrt(jnp.mean(x * x, axis=-1, keepdims=True) + EPS) * g_ref[...]
            xm_ref[b * seq:(b + 1) * seq, :] = (xn * (1.0 + sc_ref[b]) + sh_ref[b]).astype(BF16)

    @pl.when(j < NA)
    def _():
        p = _dot(xm_ref[...], wa_ref[...])
        w = GRID_W if grid_shift else seq
        nblk = rows // w
        pos = _iota((1, w, TNA), 1)
        to3 = lambda z: z.reshape(nblk, w, TNA)
        prev = jnp.where(pos == 0, 0.0, to3(pltpu.roll(p, 1, axis=0)))
        nxt = jnp.where(pos == w - 1, 0.0, to3(pltpu.roll(p, rows - 1, axis=0)))
        muh = muh_ref[...]
        out = (0.5 * muh) * (prev + nxt)
        if grid_shift:
            muv = muv_ref[...]
            p3 = to3(p)
            zblk = jnp.zeros((1, w, TNA), F32)
            up = jnp.concatenate([zblk, p3[:nblk - 1]], axis=0)
            dn = jnp.concatenate([p3[1:], zblk], axis=0)
            out = (1.0 - muh - muv) * p3 + out + (0.5 * muv) * (up + dn)
        else:
            out = (1.0 - muh) * to3(p) + out
        pa_ref[...] = out.reshape(nbp, seq, TNA)

    @pl.when(j >= NA)
    def _():
        pb_ref[...] = _dot(xm_ref[...], wb_ref[...]).reshape(nbp, seq, TNB)


def _proj_call(x, shift, scale, norm_g, wa_bf16, wb_bf16, mu_h, mu_v, grid_shift):
    bsz, seq, _ = x.shape
    nbp = 1 if grid_shift else PROJ_ROWS // seq
    kern = functools.partial(_proj_kernel, nbp=nbp, seq=seq, grid_shift=grid_shift)
    a_idx = lambda j: jnp.minimum(j, NA - 1)
    b_idx = lambda j: jnp.maximum(j - NA, 0)
    return pl.pallas_call(
        kern,
        grid=(bsz // nbp, NA + NB),
        in_specs=[pl.BlockSpec((nbp, seq, D_MODEL), lambda b, j: (b, 0, 0)),
                  pl.BlockSpec((nbp, 1, D_MODEL), lambda b, j: (b, 0, 0)),
                  pl.BlockSpec((nbp, 1, D_MODEL), lambda b, j: (b, 0, 0)),
                  pl.BlockSpec((1, D_MODEL), lambda b, j: (0, 0)),
                  pl.BlockSpec((D_MODEL, TNA), lambda b, j: (0, a_idx(j))),
                  pl.BlockSpec((D_MODEL, TNB), lambda b, j: (0, b_idx(j))),
                  pl.BlockSpec((1, TNA), lambda b, j: (0, a_idx(j))),
                  pl.BlockSpec((1, TNA), lambda b, j: (0, a_idx(j)))],
        out_specs=[pl.BlockSpec((nbp, seq, TNA), lambda b, j: (b, 0, a_idx(j))),
                   pl.BlockSpec((nbp, seq, TNB), lambda b, j: (b, 0, b_idx(j)))],
        out_shape=[jax.ShapeDtypeStruct((bsz, seq, A_COLS), F32),
                   jax.ShapeDtypeStruct((bsz, seq, B_COLS), F32)],
        scratch_shapes=[pltpu.VMEM((nbp * seq, D_MODEL), BF16)],
        compiler_params=pltpu.CompilerParams(dimension_semantics=("arbitrary", "arbitrary"),
                                             vmem_limit_bytes=VMEM_LIMIT),
        name="proj_grid" if grid_shift else "proj_seq",
    )(x, shift, scale, norm_g, wa_bf16, wb_bf16, mu_h, mu_v)


def _scan_masks():
    c = CHUNK
    t = _iota((c, PACK), 0)
    s = _iota((c, PACK), 1) & (c - 1)
    tt = _iota((c, c), 0)
    ss = _iota((c, c), 1)
    f01 = lambda cond: jnp.where(cond, 1.0, 0.0).astype(F32)
    return dict(
        bm64_bf=_head_mask_a().astype(BF16),
        eye=f01(t == s),
        incl=(f01(t >= s), f01(t <= s)),
        strict=(f01(t > s), f01(t < s)),
        tri=(f01(tt >= ss).astype(BF16), f01(tt <= ss).astype(BF16)),
    )


_MASK_KEYS = ("bm64_bf", "eye", "incl", "strict", "tri")
_MASK_SCRATCH = (((PACK, PACK), BF16), ((CHUNK, PACK), F32), ((2, CHUNK, PACK), F32), ((2, CHUNK, PACK), F32),
                 ((2, CHUNK, CHUNK), BF16))


def _store_masks(mask_refs):
    masks = _scan_masks()
    for ref, key in zip(mask_refs, _MASK_KEYS):
        val = masks[key]
        if isinstance(val, tuple):
            for i, z in enumerate(val):
                ref[i] = z
        else:
            ref[...] = val


class _Masks:
    def __init__(self, mask_refs):
        self._refs = dict(zip(_MASK_KEYS, mask_refs))

    def __getitem__(self, key):
        ref = self._refs[key]
        return (ref[0], ref[1]) if len(ref.shape) == 3 else ref[...]


def _half_lane_masks(dtype):
    lane = _iota((1, BLOCK_LANES), 1)
    return (jnp.where(lane < HEAD_A, 1.0, 0.0).astype(dtype), jnp.where(lane >= HEAD_A, 1.0, 0.0).astype(dtype))


def _block_diag_a(x):
    halves = _half_lane_masks(x.dtype)
    zero = jnp.zeros((x.shape[0], BLOCK_LANES), x.dtype)
    rows = []
    for h in range(PACK // HEAD_A):
        tile = h * HEAD_A // BLOCK_LANES
        piece = x[:, BLOCK_LANES * tile:BLOCK_LANES * (tile + 1)] * halves[h % 2]
        rows.append(jnp.concatenate([piece, zero] if tile == 0 else [zero, piece], axis=1))
    return jnp.concatenate(rows, axis=0)


def _block_diag_b(x):
    heads = D_B // HEAD_B
    zero = jnp.zeros((x.shape[0], HEAD_B), x.dtype)
    return jnp.concatenate(
        [jnp.concatenate([x[:, HEAD_B * h:HEAD_B * (h + 1)] if t == h else zero for t in range(heads)], axis=1)
         for h in range(heads)], axis=0)


def _lockstep(gens):
    results = [None] * len(gens)
    live = list(enumerate(gens))
    while live:
        still = []
        for i, g in live:
            try:
                next(g)
                still.append((i, g))
            except StopIteration as stop:
                results[i] = stop.value
        live = still
    return results


def _rwkv_group(rt, kt, kh, bh, kb, bb, v, s_ref, egc, mk, d):
    c = CHUNK
    bd = lambda x: _block_diag_a(x.astype(BF16))
    kr = _split(jnp.concatenate([kt, rt], axis=0))
    aa_k = _mm(kr, bd(kh), NT)
    aa_b = _mm(kr, bd(bh), NT)
    s_bd = s_ref[...]
    krs = _mm(kr, s_bd, NT)
    yield
    a_kk = aa_k[:c] * mk["strict"][d]
    a_rk = aa_k[c:] * mk["incl"][d]
    n = -(aa_b[:c] * mk["strict"][d])
    a_rb = aa_b[c:] * mk["incl"][d]
    x = mk["eye"] + n
    p = _mm(n, bd(n))
    akv = _mm(jnp.concatenate([a_kk, a_rk], axis=0), bd(v))
    yield
    for i in range(5):
        if i < 4:
            xp = _mm(jnp.concatenate([x, p], axis=0), bd(p))
            x = x + xp[:c]
            p = xp[c:]
        else:
            x = x + _mm(x, bd(p))
        yield
    u = _mm(x, bd(krs[:c] + akv[:c]))
    yield
    y = krs[c:] + akv[c:] - _mm(a_rb, bd(u))
    vu = jnp.concatenate([v, (-u).astype(BF16)], axis=0)
    kbb = jnp.concatenate([kb, bb], axis=0)
    upd = _mm(vu, kbb, TN_DIMS)
    halves = _half_lane_masks(F32)
    zero = jnp.zeros((HEAD_A, BLOCK_LANES), F32)
    rows = []
    for h in range(PACK // HEAD_A):
        tile = h * HEAD_A // BLOCK_LANES
        rs, ls = slice(HEAD_A * h, HEAD_A * (h + 1)), slice(BLOCK_LANES * tile, BLOCK_LANES * (tile + 1))
        piece = (s_bd[rs, ls] * egc[:, ls] + upd[rs, ls]) * halves[h % 2]
        rows.append(jnp.concatenate([piece, zero] if tile == 0 else [zero, piece], axis=1))
    s_ref[...] = jnp.concatenate(rows, axis=0)
    return y


def _hgrn_dir(qn, kn, qt, kbh, vb, eghc, s_refs, mk, d):
    a_p = _mm(qn, _block_diag_b(kn.astype(BF16)), NT) * mk["incl"][d]
    inter = []
    for grp, s_ref in enumerate(s_refs):
        sl = slice(PACK * grp, PACK * (grp + 1))
        st = s_ref[...]
        inter.append(_mm(qt[:, sl], st, NT))
        upd = _mm(vb[:, sl], kbh[:, sl], TN_DIMS)
        decay = eghc[:, sl]
        zero = jnp.zeros((HEAD_B, HEAD_B), F32)
        blocks = [st[b0:b0 + HEAD_B, b0:b0 + HEAD_B] * decay[:, b0:b0 + HEAD_B] + upd[b0:b0 + HEAD_B, b0:b0 + HEAD_B]
                  for b0 in range(0, PACK, HEAD_B)]
        s_ref[...] = jnp.concatenate([jnp.concatenate([blocks[0], zero], axis=1),
                                      jnp.concatenate([zero, blocks[1]], axis=1)], axis=0)
    yield
    o = _mm(a_p, _block_diag_b(vb.astype(BF16)))
    return o + jnp.concatenate(inter, axis=1)


def _blockwise(fn, operands, n_out, rows=CHUNK, width=D_A):
    cols = []
    for l0 in range(0, width, BLOCK_LANES):
        parts = []
        for r0 in range(0, rows, BLOCK_ROWS):
            blk = [z[(slice(None) if z.shape[0] == 1 else slice(r0, r0 + BLOCK_ROWS)), l0:l0 + BLOCK_LANES]
                   for z in operands]
            parts.append(fn(*blk))
        cols.append([jnp.concatenate([p[i] for p in parts], axis=0) for i in range(n_out)])
    return [jnp.concatenate([col[i] for col in cols], axis=1) for i in range(n_out)]


def _chunk_operands(pa_ref, pb_ref, bi, d, prm, mk):
    c = CHUNK
    n_t = SUM_TERMS
    w0, a0, wa2, k_k, k_a, lb = prm
    cols = lambda ref, i, width=D_A: ref.at[bi, :, i * width:(i + 1) * width]
    r_ref, k_ref, v_ref = cols(pa_ref, 0), cols(pa_ref, 1), cols(pa_ref, 2)
    lo = pa_ref[bi, :, 4 * D_A + LORA * d:4 * D_A + LORA * (d + 1)]
    lo = jnp.where(_iota((c, LORA), 1) < LORA // 2, jnp.tanh(lo), lo)
    wa = _mm(lo, wa2[d])
    yield

    def gates(wa_w, wa_a, k, fr, w0_, a0_, kk_, lb_):
        lw = -DECAY_SCALE * _sigmoid(w0_ + wa_w)
        a = _sigmoid(a0_ + wa_a)
        kk = k * kk_
        f = lb_ + (1.0 - lb_) * _sigmoid(fr)
        gf = jnp.log(f)
        return (lw, a, kk, 1.0 - f, (kk * kk).astype(BF16), *_terms(lw, n_t), gf.astype(BF16))

    res = _blockwise(gates, [wa[:, :D_A], wa[:, D_A:], k_ref, cols(pb_ref, 2 + d, D_B), w0[d:d + 1], a0[d:d + 1],
                             k_k, lb[d:d + 1]], 6 + n_t)
    lw, a, kk, kf, sq = res[:5]
    lw_t, gf_t = res[5:5 + n_t], res[5 + n_t]
    seg = lambda lanes: _dot(sq[:, lanes], mk["bm64_bf"])
    n2 = jnp.concatenate([seg(slice(0, PACK)), seg(slice(PACK, D_A))], axis=1)
    g = _sum_small_first([_dot(mk["tri"][d], t) for t in lw_t])
    gh = _dot(mk["tri"][d], gf_t)
    yield
    last = c - 1 if d == 0 else 0
    gc, ghc, ghm = g[last:last + 1], gh[last:last + 1], gh[c // 2:c // 2 + 1]
    egc, eghc = jnp.exp(gc), jnp.exp(ghc)

    def rwkv_operands(r, k, kk_, n2_, a_, g_, lw_, ka_, egc_):
        kap = kk_ * lax.rsqrt(jnp.maximum(n2_, 1e-24))
        keff = k * (1.0 + (a_ - 1.0) * ka_)
        eng = _exp_neg(g_)
        kh = keff * eng
        bh = (kap * a_) * eng
        out = (r * jnp.exp(g_), kap * jnp.exp(g_ - lw_), kh, bh, kh * egc_, bh * egc_)
        return tuple(z.astype(BF16) for z in out)

    rt, kt, kh, bh, kb, bb = _blockwise(rwkv_operands, [r_ref, k_ref, kk, n2, a, g, lw, k_a, egc], 6)

    def hgrn_operands(qr, kf_, gh_, ghm_, e_m, e_cm):
        q = _silu(qr)
        ghn = gh_ - ghm_
        qn = q * jnp.exp(ghn)
        kn = kf_ * _exp_neg(ghn)
        return tuple(z.astype(BF16) for z in (qn, kn, qn * e_m, kn * e_cm))

    qn, kn, qt, kbh = _blockwise(hgrn_operands, [cols(pb_ref, 0, D_B), kf, gh, ghm, jnp.exp(ghm),
                                                 jnp.exp(ghc - ghm)], 4, width=D_B)
    v = v_ref[...].astype(BF16)
    vb = pb_ref[bi, :, D_B:2 * D_B].astype(BF16)
    return (rt, kt, kh, bh, kb, bb, v, qn, kn, qt, kbh, vb), (egc, eghc)


def _scan_dir(pa_ref, pb_ref, bi, d, prm, sa_scr, sb_scr, mk):
    ops, (egc, eghc) = yield from _chunk_operands(pa_ref, pb_ref, bi, d, prm, mk)
    rt, kt, kh, bh, kb, bb, v, qn, kn, qt, kbh, vb = ops
    chains = []
    for grp in range(D_A // PACK):
        sl = slice(PACK * grp, PACK * (grp + 1))
        chains.append(_rwkv_group(rt[:, sl], kt[:, sl], kh[:, sl], bh[:, sl], kb[:, sl], bb[:, sl], v[:, sl],
                                  sa_scr.at[bi, d, grp], egc[:, sl], mk, d))
    chains.append(_hgrn_dir(qn, kn, qt, kbh, vb, eghc, [sb_scr.at[bi, d, grp] for grp in range(D_B // PACK)],
                            mk, d))
    results = [None] * len(chains)
    live = list(enumerate(chains))
    while live:
        still = []
        for i, g in live:
            try:
                next(g)
                still.append((i, g))
            except StopIteration as stop:
                results[i] = stop.value
        live = still
        yield
    return jnp.concatenate(results[:-1], axis=1), results[-1]


def _state_slots():
    return [(bi, d, grp) for bi in range(SCAN_BATCH) for d in range(2) for grp in range(D_A // PACK)]


def _load_states(sa0_ref, sb0_ref, sa_scr, sb_scr):
    spread = jnp.where(_iota((HEAD_A, PACK), 0) == (_iota((HEAD_A, PACK), 1) & (HEAD_A - 1)), 1.0, 0.0)
    spread = spread.astype(BF16)
    bm64 = _head_mask_a()
    zero = jnp.zeros((HEAD_B, HEAD_B), F32)
    for bi, d, grp in _state_slots():
        sa_scr[bi, d, grp] = _mm_exact_rhs(sa0_ref[bi, d, grp], spread) * bm64
        h0 = sb0_ref[bi, d, 2 * grp].T
        h1 = sb0_ref[bi, d, 2 * grp + 1].T
        sb_scr[bi, d, grp] = jnp.concatenate([jnp.concatenate([h0, zero], axis=1),
                                              jnp.concatenate([zero, h1], axis=1)], axis=0)


def _store_states(sa_scr, sb_scr, sa_out, sb_out):
    gather = jnp.where((_iota((PACK, HEAD_A), 0) & (HEAD_A - 1)) == _iota((PACK, HEAD_A), 1), 1.0, 0.0)
    gather = gather.astype(BF16)
    for bi, d, grp in _state_slots():
        sa_out[bi, d, grp] = _mm_exact_rhs(sa_scr[bi, d, grp], gather)
        for hh in range(PACK // HEAD_B):
            blk = sb_scr[bi, d, grp, HEAD_B * hh:HEAD_B * (hh + 1), HEAD_B * hh:HEAD_B * (hh + 1)]
            sb_out[bi, d, 2 * grp + hh] = blk.T


def _scan_kernel(paf_ref, pbf_ref, pab_ref, pbb_ref, *rest, from_zero):
    rest, mask_refs = rest[:-len(_MASK_KEYS)], rest[-len(_MASK_KEYS):]
    if from_zero:
        (w0_ref, a0_ref, wa2_ref, kk_ref, ka_ref, lb_ref, yf_ref, yb_ref, of_ref, ob_ref, sa_out, sb_out,
         sa_scr, sb_scr) = rest
    else:
        (sa0_ref, sb0_ref, w0_ref, a0_ref, wa2_ref, kk_ref, ka_ref, lb_ref, yf_ref, yb_ref, of_ref, ob_ref,
         sa_scr, sb_scr) = rest
    ci = pl.program_id(1)

    @pl.when(ci == 0)
    def _():
        _store_masks(mask_refs)
        if from_zero:
            sa_scr[...] = jnp.zeros(sa_scr.shape, F32)
            sb_scr[...] = jnp.zeros(sb_scr.shape, F32)
        else:
            _load_states(sa0_ref, sb0_ref, sa_scr, sb_scr)

    mk = _Masks(mask_refs)
    prm = (w0_ref[...], a0_ref[...], wa2_ref, kk_ref[...], ka_ref[...], lb_ref[...])
    p_refs = ((paf_ref, pbf_ref), (pab_ref, pbb_ref))
    runs = [(bi, d) for bi in range(SCAN_BATCH) for d in range(2)]
    outs = _lockstep([_scan_dir(p_refs[d][0], p_refs[d][1], bi, d, prm, sa_scr, sb_scr, mk) for bi, d in runs])
    for (bi, d), (y, o) in zip(runs, outs):
        (yf_ref, yb_ref)[d][bi] = y.astype(BF16)
        (of_ref, ob_ref)[d][bi] = o.astype(BF16)

    if from_zero:
        @pl.when(ci == pl.num_programs(1) - 1)
        def _():
            _store_states(sa_scr, sb_scr, sa_out, sb_out)


def _scan_call(pa, pb, states, w0, a0, wa2, k_k, k_a, lb):
    bsz, seq, _ = pa.shape
    nc = seq // CHUNK
    nb = SCAN_BATCH
    from_zero = states is None
    fwd = lambda b, c: (b, c, 0)
    bwd = lambda b, c: (b, nc - 1 - c, 0)
    st = lambda b, c: (b, 0, 0, 0, 0)
    full = lambda shape: pl.BlockSpec(shape, lambda b, c: (0,) * len(shape))
    sa_block = (nb, 2, D_A // PACK, PACK, HEAD_A)
    sb_block = (nb, 2, D_B // HEAD_B, HEAD_B, HEAD_B)
    y_shape = jax.ShapeDtypeStruct((bsz, seq, D_A), BF16)
    out_specs = [pl.BlockSpec((nb, CHUNK, D_A), fwd), pl.BlockSpec((nb, CHUNK, D_A), bwd),
                 pl.BlockSpec((nb, CHUNK, D_B), fwd), pl.BlockSpec((nb, CHUNK, D_B), bwd)]
    out_shape = [y_shape, y_shape, y_shape, y_shape]
    in_specs = [pl.BlockSpec((nb, CHUNK, A_COLS), fwd), pl.BlockSpec((nb, CHUNK, B_COLS), fwd),
                pl.BlockSpec((nb, CHUNK, A_COLS), bwd), pl.BlockSpec((nb, CHUNK, B_COLS), bwd)]
    args = [pa, pb, pa, pb]
    if from_zero:
        out_specs += [pl.BlockSpec(sa_block, st), pl.BlockSpec(sb_block, st)]
        out_shape += [jax.ShapeDtypeStruct((bsz,) + sa_block[1:], F32),
                      jax.ShapeDtypeStruct((bsz,) + sb_block[1:], F32)]
    else:
        in_specs += [pl.BlockSpec(sa_block, st), pl.BlockSpec(sb_block, st)]
        args += list(states)
    in_specs += [full((2, D_A)), full((2, D_A)), full((2, LORA, 2 * D_A)),
                 full((1, D_A)), full((1, D_A)), full((2, D_B))]
    args += [w0, a0, wa2, k_k, k_a, lb]
    bd_block = (nb, 2, D_A // PACK, PACK, PACK)
    return pl.pallas_call(
        functools.partial(_scan_kernel, from_zero=from_zero),
        grid=(bsz // nb, nc),
        in_specs=in_specs,
        out_specs=out_specs,
        out_shape=out_shape,
        scratch_shapes=[pltpu.VMEM(bd_block, F32), pltpu.VMEM(bd_block, F32)]
                       + [pltpu.VMEM(shape, dtype) for shape, dtype in _MASK_SCRATCH],
        compiler_params=pltpu.CompilerParams(dimension_semantics=("arbitrary", "arbitrary"),
                                             vmem_limit_bytes=VMEM_LIMIT),
        name="scan_state" if from_zero else "scan",
    )(*args)


def _out_kernel(x_ref, pa_ref, zb_ref, yf_ref, yb_ref, of_ref, ob_ref, gate_ref, a0_ref, a2p_ref, ka_ref,
                rk_ref, lnw_ref, lnb_ref, og_ref, wout_ref, fg_ref, out_ref):
    pa = pa_ref[0]
    r = pa[:, 0:D_A]
    k = pa[:, D_A:2 * D_A]
    v = pa[:, 2 * D_A:3 * D_A]
    za = pa[:, 3 * D_A:4 * D_A]
    bm64_bf = _head_mask_a().astype(BF16)

    def seg_sum(z):
        zb16 = z.astype(BF16)
        return jnp.concatenate([_dot(zb16[:, :PACK], bm64_bf), _dot(zb16[:, PACK:], bm64_bf)], axis=1)

    y = yf_ref[0].astype(F32) + yb_ref[0].astype(F32)
    mu = seg_sum(y) * (1.0 / HEAD_A)
    dlt = y - mu
    var = seg_sum(dlt * dlt) * (1.0 / HEAD_A)
    yn = dlt * lax.rsqrt(var + GN_EPS) * lnw_ref[...] + lnb_ref[...]
    a_dirs = []
    for d in range(2):
        lo = pa[:, 4 * D_A + LORA * d:4 * D_A + LORA * (d + 1)]
        a_dirs.append(_sigmoid(a0_ref[d:d + 1] + _mm(lo, a2p_ref[d])))
    kmean = k * (1.0 + (0.5 * (a_dirs[0] + a_dirs[1]) - 1.0) * ka_ref[...])
    bonus = seg_sum(r * kmean * rk_ref[...]) * v
    out_a = (yn + bonus) * _silu(za)

    o = of_ref[0].astype(F32) + ob_ref[0].astype(F32)
    og = og_ref[...]
    zb = zb_ref[0]
    outs = [out_a]
    for h in range(D_B // HEAD_B):
        sl = slice(HEAD_B * h, HEAD_B * (h + 1))
        oh = o[:, sl]
        oh = oh * lax.rsqrt(jnp.mean(oh * oh, axis=-1, keepdims=True) + EPS) * og[:, sl]
        outs.append(oh * _silu(zb[:, sl]))
    mix = jnp.concatenate(outs, axis=1)
    proj = _dot(mix.astype(BF16), wout_ref[...])
    hs = x_ref[0] + gate_ref[0] * proj
    out_ref[0] = hs * lax.rsqrt(jnp.mean(hs * hs, axis=-1, keepdims=True) + EPS) * fg_ref[...]


def _out_call(x, pa, pb, yf, yb, of, ob, gate, a0, a2p, k_a, r_k, lnx_w, lnx_b, onorm_g, w_out_bf16, final_g):
    bsz, seq, _ = x.shape
    tm = min(OUT_ROWS, seq)
    tok = lambda b, i: (b, i, 0)
    full = lambda shape: pl.BlockSpec(shape, lambda b, i: (0,) * len(shape))
    return pl.pallas_call(
        _out_kernel,
        grid=(bsz, seq // tm),
        in_specs=[pl.BlockSpec((1, tm, D_MODEL), tok),
                  pl.BlockSpec((1, tm, A_COLS), tok),
                  pl.BlockSpec((1, tm, D_B), lambda b, i: (b, i, 4)),
                  pl.BlockSpec((1, tm, D_A), tok), pl.BlockSpec((1, tm, D_A), tok),
                  pl.BlockSpec((1, tm, D_B), tok), pl.BlockSpec((1, tm, D_B), tok),
                  pl.BlockSpec((1, 1, D_MODEL), lambda b, i: (b, 0, 0)),
                  full((2, D_A)), full((2, LORA, D_A)), full((1, D_A)), full((1, D_A)),
                  full((1, D_A)), full((1, D_A)), full((1, D_B)),
                  full((D_MODEL, D_MODEL)), full((1, D_MODEL))],
        out_specs=pl.BlockSpec((1, tm, D_MODEL), tok),
        out_shape=jax.ShapeDtypeStruct((bsz, seq, D_MODEL), F32),
        compiler_params=pltpu.CompilerParams(dimension_semantics=("arbitrary", "arbitrary"),
                                             vmem_limit_bytes=VMEM_LIMIT),
        name="out",
    )(x, pa, pb, yf, yb, of, ob, gate, a0, a2p, k_a, r_k, lnx_w, lnx_b, onorm_g, w_out_bf16, final_g)


def kernel(x_prompt, x_sample, state_rwkv, state_hgrn, c, c_ctx, norm_g, w_ada, b_ada, w_in, mu_h, mu_v, w0, w2,
           a0, a2, k_k, k_a, r_k, lnx_w, lnx_b, lb_logits, onorm_g, w_out, final_g):
    l = 0
    bp = x_prompt.shape[0]
    bs = x_sample.shape[0]
    lb_all = jnp.cumsum(jax.nn.softmax(lb_logits.astype(F32), axis=0), axis=0)
    lb = lb_all[l]

    wa_bf = w_in[l, :, :A_COLS].astype(BF16)
    wb_bf = w_in[l, :, A_COLS:].astype(BF16)
    w_out_bf = w_out[l].astype(BF16)
    zeros = jnp.zeros((2, LORA // 2, D_A), F32)
    wa2 = jnp.concatenate([jnp.concatenate([w2[l], zeros], axis=2),
                           jnp.concatenate([zeros, a2[l]], axis=2)], axis=1)
    a2p = jnp.concatenate([zeros, a2[l]], axis=1)
    row = lambda z: z.reshape(1, -1)

    cc = jnp.concatenate([c_ctx[None, :], c, jnp.zeros((16 - 1 - bs, D_MODEL), F32)], axis=0)
    m = _ada_call(cc, w_ada[l], row(b_ada[l]))
    shift, scale, gate = m[:, :D_MODEL], m[:, D_MODEL:2 * D_MODEL], m[:, 2 * D_MODEL:]
    ctx = lambda z: jnp.broadcast_to(z[0:1, None, :], (bp, 1, D_MODEL))
    lat = lambda z: z[1:1 + bs, None, :]

    def path(x, sh, sc, gt, states, grid_shift):
        pa, pb = _proj_call(x, sh, sc, row(norm_g[l]), wa_bf, wb_bf, row(mu_h[l]), row(mu_v[l]), grid_shift)
        res = _scan_call(pa, pb, states, w0[l], a0[l], wa2, row(k_k[l]), row(k_a[l]), lb)
        yf, yb, of, ob = res[:4]
        y = _out_call(x, pa, pb, yf, yb, of, ob, gt, a0[l], a2p, row(k_a[l]), row(r_k[l]), row(lnx_w[l]),
                      row(lnx_b[l]), row(onorm_g[l]), w_out_bf, row(final_g))
        return y, res[4:]

    y_prompt, (s_a, s_b) = path(x_prompt, ctx(shift), ctx(scale), ctx(gate), None, False)
    groups_a = D_A // PACK
    sa0 = state_rwkv[:, l].reshape(bs, 2, groups_a, PACK, HEAD_A)
    y_sample, _ = path(x_sample, lat(shift), lat(scale), lat(gate), (sa0, state_hgrn[:, l]), True)
    s_a = s_a.reshape(bp, 1, 2, D_A // HEAD_A, HEAD_A, HEAD_A)
    return y_prompt, y_sample, s_a, s_b[:, None]
```

```python
import functools

import jax
import jax.numpy as jnp
from jax import lax
from jax.experimental import pallas as pl
from jax.experimental.pallas import tpu as pltpu

F32 = jnp.float32
BF16 = jnp.bfloat16

D_MODEL = 1024
D_A = 512
D_B = 512
HEAD_A = 64
HEAD_B = 128
LOG2_HEAD_A = HEAD_A.bit_length() - 1
LORA = 128
A_COLS = 4 * D_A + 2 * LORA
B_COLS = 5 * D_B
GRID_W = 64
CHUNK = 64
SCAN_BATCH = 4
SUM_TERMS = 1
BLOCK_ROWS = 32
BLOCK_LANES = 128
EPS = 1e-6
GN_EPS = 64e-5
DECAY_SCALE = 0.6065306597126334
NEG_LOG2E = -1.4426950408889634
PACK = 256
TNA = 768
TNB = 512
NA = A_COLS // TNA
NB = B_COLS // TNB
PROJ_ROWS = 2048
OUT_ROWS = 512
VMEM_LIMIT = 56 * 1024 * 1024

NN = ((1,), (0,))
NT = ((1,), (1,))
TN_DIMS = ((0,), (0,))


def _dot(a, b, dims=NN):
    return lax.dot_general(a, b, (dims, ((), ())), preferred_element_type=F32)


def _split(x, passes=1):
    if isinstance(x, tuple):
        return x
    hi = x.astype(BF16)
    if passes == 1:
        return (hi,)
    return hi, (x - hi.astype(F32)).astype(BF16)


def _mm(a, b, dims=NN, passes=1):
    a = _split(a, passes)
    b = _split(b, passes)
    if len(a) == 1 or len(b) == 1:
        return _dot(a[0], b[0], dims)
    return _dot(a[0], b[0], dims) + (_dot(a[0], b[1], dims) + _dot(a[1], b[0], dims))


def _terms(x, n):
    out = []
    for _ in range(n - 1):
        h = x.astype(BF16)
        out.append(h)
        x = x - h.astype(F32)
    out.append(x.astype(BF16))
    return out


def _sum_small_first(parts):
    return functools.reduce(lambda acc, z: z + acc, reversed(parts))


def _mm_exact_rhs(a, b_bf16, n=3):
    return _sum_small_first([_dot(t, b_bf16) for t in _terms(a, n)])


def _exp_neg(x):
    return jnp.exp2(x * NEG_LOG2E)


def _sigmoid(x):
    return 0.5 * jnp.tanh(0.5 * x) + 0.5


def _silu(x):
    return x * _sigmoid(x)


def _iota(shape, dim):
    return lax.broadcasted_iota(jnp.int32, shape, dim)


def _block_mask(rows, cols, row_shift, col_shift):
    same = (_iota((rows, cols), 0) >> row_shift) == (_iota((rows, cols), 1) >> col_shift)
    return jnp.where(same, 1.0, 0.0).astype(F32)


def _head_mask_a():
    return _block_mask(PACK, PACK, LOG2_HEAD_A, LOG2_HEAD_A)


def _ada_kernel(c_ref, w_ref, b_ref, m_ref):
    m_ref[...] = _mm(_silu(c_ref[...]), w_ref[...], passes=3) + b_ref[...]


def _ada_call(cc, w_ada, b_ada):
    rows = cc.shape[0]
    return pl.pallas_call(
        _ada_kernel,
        grid=(3,),
        in_specs=[pl.BlockSpec((rows, D_MODEL), lambda j: (0, 0)),
                  pl.BlockSpec((D_MODEL, D_MODEL), lambda j: (0, j)),
                  pl.BlockSpec((1, D_MODEL), lambda j: (0, j))],
        out_specs=pl.BlockSpec((rows, D_MODEL), lambda j: (0, j)),
        out_shape=jax.ShapeDtypeStruct((rows, 3 * D_MODEL), F32),
        compiler_params=pltpu.CompilerParams(dimension_semantics=("arbitrary",),
                                             vmem_limit_bytes=VMEM_LIMIT),
        name="ada",
    )(cc, w_ada, b_ada)


def _proj_kernel(x_ref, sh_ref, sc_ref, g_ref, wa_ref, wb_ref, muh_ref, muv_ref, pa_ref, pb_ref, xm_ref, *,
                 nbp, seq, grid_shift):
    j = pl.program_id(1)
    rows = nbp * seq

    @pl.when(j == 0)
    def _():
        for b in range(nbp):
            x = x_ref[b]
            xn = x * lax.rsqrt(jnp.mean(x * x, axis=-1, keepdims=True) + EPS) * g_ref[...]
            xm_ref[b * seq:(b + 1) * seq, :] = (xn * (1.0 + sc_ref[b]) + sh_ref[b]).astype(BF16)

    @pl.when(j < NA)
    def _():
        p = _dot(xm_ref[...], wa_ref[...])
        w = GRID_W if grid_shift else seq
        nblk = rows // w
        pos = _iota((1, w, TNA), 1)
        to3 = lambda z: z.reshape(nblk, w, TNA)
        prev = jnp.where(pos == 0, 0.0, to3(pltpu.roll(p, 1, axis=0)))
        nxt = jnp.where(pos == w - 1, 0.0, to3(pltpu.roll(p, rows - 1, axis=0)))
        muh = muh_ref[...]
        out = (0.5 * muh) * (prev + nxt)
        if grid_shift:
            muv = muv_ref[...]
            p3 = to3(p)
            zblk = jnp.zeros((1, w, TNA), F32)
            up = jnp.concatenate([zblk, p3[:nblk - 1]], axis=0)
            dn = jnp.concatenate([p3[1:], zblk], axis=0)
            out = (1.0 - muh - muv) * p3 + out + (0.5 * muv) * (up + dn)
        else:
            out = (1.0 - muh) * to3(p) + out
        pa_ref[...] = out.reshape(nbp, seq, TNA)

    @pl.when(j >= NA)
    def _():
        pb_ref[...] = _dot(xm_ref[...], wb_ref[...]).reshape(nbp, seq, TNB)


def _proj_call(x, shift, scale, norm_g, wa_bf16, wb_bf16, mu_h, mu_v, grid_shift):
    bsz, seq, _ = x.shape
    nbp = 1 if grid_shift else PROJ_ROWS // seq
    kern = functools.partial(_proj_kernel, nbp=nbp, seq=seq, grid_shift=grid_shift)
    a_idx = lambda j: jnp.minimum(j, NA - 1)
    b_idx = lambda j: jnp.maximum(j - NA, 0)
    return pl.pallas_call(
        kern,
        grid=(bsz // nbp, NA + NB),
        in_specs=[pl.BlockSpec((nbp, seq, D_MODEL), lambda b, j: (b, 0, 0)),
                  pl.BlockSpec((nbp, 1, D_MODEL), lambda b, j: (b, 0, 0)),
                  pl.BlockSpec((nbp, 1, D_MODEL), lambda b, j: (b, 0, 0)),
                  pl.BlockSpec((1, D_MODEL), lambda b, j: (0, 0)),
                  pl.BlockSpec((D_MODEL, TNA), lambda b, j: (0, a_idx(j))),
                  pl.BlockSpec((D_MODEL, TNB), lambda b, j: (0, b_idx(j))),
                  pl.BlockSpec((1, TNA), lambda b, j: (0, a_idx(j))),
                  pl.BlockSpec((1, TNA), lambda b, j: (0, a_idx(j)))],
        out_specs=[pl.BlockSpec((nbp, seq, TNA), lambda b, j: (b, 0, a_idx(j))),
                   pl.BlockSpec((nbp, seq, TNB), lambda b, j: (b, 0, b_idx(j)))],
        out_shape=[jax.ShapeDtypeStruct((bsz, seq, A_COLS), F32),
                   jax.ShapeDtypeStruct((bsz, seq, B_COLS), F32)],
        scratch_shapes=[pltpu.VMEM((nbp * seq, D_MODEL), BF16)],
        compiler_params=pltpu.CompilerParams(dimension_semantics=("arbitrary", "arbitrary"),
                                             vmem_limit_bytes=VMEM_LIMIT),
        name="proj_grid" if grid_shift else "proj_seq",
    )(x, shift, scale, norm_g, wa_bf16, wb_bf16, mu_h, mu_v)


def _scan_masks():
    c = CHUNK
    t = _iota((c, PACK), 0)
    s = _iota((c, PACK), 1) & (c - 1)
    tt = _iota((c, c), 0)
    ss = _iota((c, c), 1)
    f01 = lambda cond: jnp.where(cond, 1.0, 0.0).astype(F32)
    return dict(
        bm64_bf=_head_mask_a().astype(BF16),
        eye=f01(t == s),
        incl=(f01(t >= s), f01(t <= s)),
        strict=(f01(t > s), f01(t < s)),
        tri=(f01(tt >= ss).astype(BF16), f01(tt <= ss).astype(BF16)),
    )


_MASK_KEYS = ("bm64_bf", "eye", "incl", "strict", "tri")
_MASK_SCRATCH = (((PACK, PACK), BF16), ((CHUNK, PACK), F32), ((2, CHUNK, PACK), F32), ((2, CHUNK, PACK), F32),
                 ((2, CHUNK, CHUNK), BF16))


def _store_masks(mask_refs):
    masks = _scan_masks()
    for ref, key in zip(mask_refs, _MASK_KEYS):
        val = masks[key]
        if isinstance(val, tuple):
            for i, z in enumerate(val):
                ref[i] = z
        else:
            ref[...] = val


class _Masks:
    def __init__(self, mask_refs):
        self._refs = dict(zip(_MASK_KEYS, mask_refs))

    def __getitem__(self, key):
        ref = self._refs[key]
        return (ref[0], ref[1]) if len(ref.shape) == 3 else ref[...]


def _half_lane_masks(dtype):
    lane = _iota((1, BLOCK_LANES), 1)
    return (jnp.where(lane < HEAD_A, 1.0, 0.0).astype(dtype), jnp.where(lane >= HEAD_A, 1.0, 0.0).astype(dtype))


def _block_diag_a(x):
    halves = _half_lane_masks(x.dtype)
    zero = jnp.zeros((x.shape[0], BLOCK_LANES), x.dtype)
    rows = []
    for h in range(PACK // HEAD_A):
        tile = h * HEAD_A // BLOCK_LANES
        piece = x[:, BLOCK_LANES * tile:BLOCK_LANES * (tile + 1)] * halves[h % 2]
        rows.append(jnp.concatenate([piece, zero] if tile == 0 else [zero, piece], axis=1))
    return jnp.concatenate(rows, axis=0)


def _block_diag_b(x):
    heads = D_B // HEAD_B
    zero = jnp.zeros((x.shape[0], HEAD_B), x.dtype)
    return jnp.concatenate(
        [jnp.concatenate([x[:, HEAD_B * h:HEAD_B * (h + 1)] if t == h else zero for t in range(heads)], axis=1)
         for h in range(heads)], axis=0)


def _lockstep(gens):
    results = [None] * len(gens)
    live = list(enumerate(gens))
    while live:
        still = []
        for i, g in live:
            try:
                next(g)
                still.append((i, g))
            except StopIteration as stop:
                results[i] = stop.value
        live = still
    return results


def _rwkv_group(rt, kt, kh, bh, kb, bb, v, s_ref, egc, mk, d):
    c = CHUNK
    bd = lambda x: _block_diag_a(x.astype(BF16))
    kr = _split(jnp.concatenate([kt, rt], axis=0))
    aa_k = _mm(kr, bd(kh), NT)
    aa_b = _mm(kr, bd(bh), NT)
    s_bd = s_ref[...]
    krs = _mm(kr, s_bd, NT)
    yield
    a_kk = aa_k[:c] * mk["strict"][d]
    a_rk = aa_k[c:] * mk["incl"][d]
    n = -(aa_b[:c] * mk["strict"][d])
    a_rb = aa_b[c:] * mk["incl"][d]
    x = mk["eye"] + n
    p = _mm(n, bd(n))
    akv = _mm(jnp.concatenate([a_kk, a_rk], axis=0), bd(v))
    yield
    for i in range(5):
        if i < 4:
            xp = _mm(jnp.concatenate([x, p], axis=0), bd(p))
            x = x + xp[:c]
            p = xp[c:]
        else:
            x = x + _mm(x, bd(p))
        yield
    u = _mm(x, bd(krs[:c] + akv[:c]))
    yield
    y = krs[c:] + akv[c:] - _mm(a_rb, bd(u))
    vu = jnp.concatenate([v, (-u).astype(BF16)], axis=0)
    kbb = jnp.concatenate([kb, bb], axis=0)
    upd = _mm(vu, kbb, TN_DIMS)
    halves = _half_lane_masks(F32)
    zero = jnp.zeros((HEAD_A, BLOCK_LANES), F32)
    rows = []
    for h in range(PACK // HEAD_A):
        tile = h * HEAD_A // BLOCK_LANES
        rs, ls = slice(HEAD_A * h, HEAD_A * (h + 1)), slice(BLOCK_LANES * tile, BLOCK_LANES * (tile + 1))
        piece = (s_bd[rs, ls] * egc[:, ls] + upd[rs, ls]) * halves[h % 2]
        rows.append(jnp.concatenate([piece, zero] if tile == 0 else [zero, piece], axis=1))
    s_ref[...] = jnp.concatenate(rows, axis=0)
    return y


def _hgrn_dir(qn, kn, qt, kbh, vb, eghc, s_refs, mk, d):
    a_p = _mm(qn, _block_diag_b(kn.astype(BF16)), NT) * mk["incl"][d]
    inter = []
    for grp, s_ref in enumerate(s_refs):
        sl = slice(PACK * grp, PACK * (grp + 1))
        st = s_ref[...]
        inter.append(_mm(qt[:, sl], st, NT))
        upd = _mm(vb[:, sl], kbh[:, sl], TN_DIMS)
        decay = eghc[:, sl]
        zero = jnp.zeros((HEAD_B, HEAD_B), F32)
        blocks = [st[b0:b0 + HEAD_B, b0:b0 + HEAD_B] * decay[:, b0:b0 + HEAD_B] + upd[b0:b0 + HEAD_B, b0:b0 + HEAD_B]
                  for b0 in range(0, PACK, HEAD_B)]
        s_ref[...] = jnp.concatenate([jnp.concatenate([blocks[0], zero], axis=1),
                                      jnp.concatenate([zero, blocks[1]], axis=1)], axis=0)
    yield
    o = _mm(a_p, _block_diag_b(vb.astype(BF16)))
    return o + jnp.concatenate(inter, axis=1)


def _blockwise(fn, operands, n_out, rows=CHUNK, width=D_A):
    cols = []
    for l0 in range(0, width, BLOCK_LANES):
        parts = []
        for r0 in range(0, rows, BLOCK_ROWS):
            blk = [z[(slice(None) if z.shape[0] == 1 else slice(r0, r0 + BLOCK_ROWS)), l0:l0 + BLOCK_LANES]
                   for z in operands]
            parts.append(fn(*blk))
        cols.append([jnp.concatenate([p[i] for p in parts], axis=0) for i in range(n_out)])
    return [jnp.concatenate([col[i] for col in cols], axis=1) for i in range(n_out)]


def _chunk_operands(pa_ref, pb_ref, bi, d, prm, mk):
    c = CHUNK
    n_t = SUM_TERMS
    w0, a0, wa2, k_k, k_a, lb = prm
    cols = lambda ref, i, width=D_A: ref.at[bi, :, i * width:(i + 1) * width]
    r_ref, k_ref, v_ref = cols(pa_ref, 0), cols(pa_ref, 1), cols(pa_ref, 2)
    lo = pa_ref[bi, :, 4 * D_A + LORA * d:4 * D_A + LORA * (d + 1)]
    lo = jnp.where(_iota((c, LORA), 1) < LORA // 2, jnp.tanh(lo), lo)
    wa = _mm(lo, wa2[d])
    yield

    def gates(wa_w, wa_a, k, fr, w0_, a0_, kk_, lb_):
        lw = -DECAY_SCALE * _sigmoid(w0_ + wa_w)
        a = _sigmoid(a0_ + wa_a)
        kk = k * kk_
        f = lb_ + (1.0 - lb_) * _sigmoid(fr)
        gf = jnp.log(f)
        return (lw, a, kk, 1.0 - f, (kk * kk).astype(BF16), *_terms(lw, n_t), gf.astype(BF16))

    res = _blockwise(gates, [wa[:, :D_A], wa[:, D_A:], k_ref, cols(pb_ref, 2 + d, D_B), w0[d:d + 1], a0[d:d + 1],
                             k_k, lb[d:d + 1]], 6 + n_t)
    lw, a, kk, kf, sq = res[:5]
    lw_t, gf_t = res[5:5 + n_t], res[5 + n_t]
    seg = lambda lanes: _dot(sq[:, lanes], mk["bm64_bf"])
    n2 = jnp.concatenate([seg(slice(0, PACK)), seg(slice(PACK, D_A))], axis=1)
    g = _sum_small_first([_dot(mk["tri"][d], t) for t in lw_t])
    gh = _dot(mk["tri"][d], gf_t)
    yield
    last = c - 1 if d == 0 else 0
    gc, ghc, ghm = g[last:last + 1], gh[last:last + 1], gh[c // 2:c // 2 + 1]
    egc, eghc = jnp.exp(gc), jnp.exp(ghc)

    def rwkv_operands(r, k, kk_, n2_, a_, g_, lw_, ka_, egc_):
        kap = kk_ * lax.rsqrt(jnp.maximum(n2_, 1e-24))
        keff = k * (1.0 + (a_ - 1.0) * ka_)
        eng = _exp_neg(g_)
        kh = keff * eng
        bh = (kap * a_) * eng
        out = (r * jnp.exp(g_), kap * jnp.exp(g_ - lw_), kh, bh, kh * egc_, bh * egc_)
        return tuple(z.astype(BF16) for z in out)

    rt, kt, kh, bh, kb, bb = _blockwise(rwkv_operands, [r_ref, k_ref, kk, n2, a, g, lw, k_a, egc], 6)

    def hgrn_operands(qr, kf_, gh_, ghm_, e_m, e_cm):
        q = _silu(qr)
        ghn = gh_ - ghm_
        qn = q * jnp.exp(ghn)
        kn = kf_ * _exp_neg(ghn)
        return tuple(z.astype(BF16) for z in (qn, kn, qn * e_m, kn * e_cm))

    qn, kn, qt, kbh = _blockwise(hgrn_operands, [cols(pb_ref, 0, D_B), kf, gh, ghm, jnp.exp(ghm),
                                                 jnp.exp(ghc - ghm)], 4, width=D_B)
    v = v_ref[...].astype(BF16)
    vb = pb_ref[bi, :, D_B:2 * D_B].astype(BF16)
    return (rt, kt, kh, bh, kb, bb, v, qn, kn, qt, kbh, vb), (egc, eghc)


def _scan_dir(pa_ref, pb_ref, bi, d, prm, sa_scr, sb_scr, mk):
    ops, (egc, eghc) = yield from _chunk_operands(pa_ref, pb_ref, bi, d, prm, mk)
    rt, kt, kh, bh, kb, bb, v, qn, kn, qt, kbh, vb = ops
    chains = []
    for grp in range(D_A // PACK):
        sl = slice(PACK * grp, PACK * (grp + 1))
        chains.append(_rwkv_group(rt[:, sl], kt[:, sl], kh[:, sl], bh[:, sl], kb[:, sl], bb[:, sl], v[:, sl],
                                  sa_scr.at[bi, d, grp], egc[:, sl], mk, d))
    chains.append(_hgrn_dir(qn, kn, qt, kbh, vb, eghc, [sb_scr.at[bi, d, grp] for grp in range(D_B // PACK)],
                            mk, d))
    results = [None] * len(chains)
    live = list(enumerate(chains))
    while live:
        still = []
        for i, g in live:
            try:
                next(g)
                still.append((i, g))
            except StopIteration as stop:
                results[i] = stop.value
        live = still
        yield
    return jnp.concatenate(results[:-1], axis=1), results[-1]


def _state_slots():
    return [(bi, d, grp) for bi in range(SCAN_BATCH) for d in range(2) for grp in range(D_A // PACK)]


def _load_states(sa0_ref, sb0_ref, sa_scr, sb_scr):
    spread = jnp.where(_iota((HEAD_A, PACK), 0) == (_iota((HEAD_A, PACK), 1) & (HEAD_A - 1)), 1.0, 0.0)
    spread = spread.astype(BF16)
    bm64 = _head_mask_a()
    zero = jnp.zeros((HEAD_B, HEAD_B), F32)
    for bi, d, grp in _state_slots():
        sa_scr[bi, d, grp] = _mm_exact_rhs(sa0_ref[bi, d, grp], spread) * bm64
        h0 = sb0_ref[bi, d, 2 * grp].T
        h1 = sb0_ref[bi, d, 2 * grp + 1].T
        sb_scr[bi, d, grp] = jnp.concatenate([jnp.concatenate([h0, zero], axis=1),
                                              jnp.concatenate([zero, h1], axis=1)], axis=0)


def _store_states(sa_scr, sb_scr, sa_out, sb_out):
    gather = jnp.where((_iota((PACK, HEAD_A), 0) & (HEAD_A - 1)) == _iota((PACK, HEAD_A), 1), 1.0, 0.0)
    gather = gather.astype(BF16)
    for bi, d, grp in _state_slots():
        sa_out[bi, d, grp] = _mm_exact_rhs(sa_scr[bi, d, grp], gather)
        for hh in range(PACK // HEAD_B):
            blk = sb_scr[bi, d, grp, HEAD_B * hh:HEAD_B * (hh + 1), HEAD_B * hh:HEAD_B * (hh + 1)]
            sb_out[bi, d, 2 * grp + hh] = blk.T


def _scan_kernel(paf_ref, pbf_ref, pab_ref, pbb_ref, *rest, from_zero):
    rest, mask_refs = rest[:-len(_MASK_KEYS)], rest[-len(_MASK_KEYS):]
    if from_zero:
        (w0_ref, a0_ref, wa2_ref, kk_ref, ka_ref, lb_ref, yf_ref, yb_ref, of_ref, ob_ref, sa_out, sb_out,
         sa_scr, sb_scr) = rest
    else:
        (sa0_ref, sb0_ref, w0_ref, a0_ref, wa2_ref, kk_ref, ka_ref, lb_ref, yf_ref, yb_ref, of_ref, ob_ref,
         sa_scr, sb_scr) = rest
    ci = pl.program_id(1)

    @pl.when(ci == 0)
    def _():
        _store_masks(mask_refs)
        if from_zero:
            sa_scr[...] = jnp.zeros(sa_scr.shape, F32)
            sb_scr[...] = jnp.zeros(sb_scr.shape, F32)
        else:
            _load_states(sa0_ref, sb0_ref, sa_scr, sb_scr)

    mk = _Masks(mask_refs)
    prm = (w0_ref[...], a0_ref[...], wa2_ref, kk_ref[...], ka_ref[...], lb_ref[...])
    p_refs = ((paf_ref, pbf_ref), (pab_ref, pbb_ref))
    runs = [(bi, d) for bi in range(SCAN_BATCH) for d in range(2)]
    outs = _lockstep([_scan_dir(p_refs[d][0], p_refs[d][1], bi, d, prm, sa_scr, sb_scr, mk) for bi, d in runs])
    for (bi, d), (y, o) in zip(runs, outs):
        (yf_ref, yb_ref)[d][bi] = y.astype(BF16)
        (of_ref, ob_ref)[d][bi] = o.astype(BF16)

    if from_zero:
        @pl.when(ci == pl.num_programs(1) - 1)
        def _():
            _store_states(sa_scr, sb_scr, sa_out, sb_out)


def _scan_call(pa, pb, states, w0, a0, wa2, k_k, k_a, lb):
    bsz, seq, _ = pa.shape
    nc = seq // CHUNK
    nb = SCAN_BATCH
    from_zero = states is None
    fwd = lambda b, c: (b, c, 0)
    bwd = lambda b, c: (b, nc - 1 - c, 0)
    st = lambda b, c: (b, 0, 0, 0, 0)
    full = lambda shape: pl.BlockSpec(shape, lambda b, c: (0,) * len(shape))
    sa_block = (nb, 2, D_A // PACK, PACK, HEAD_A)
    sb_block = (nb, 2, D_B // HEAD_B, HEAD_B, HEAD_B)
    y_shape = jax.ShapeDtypeStruct((bsz, seq, D_A), BF16)
    out_specs = [pl.BlockSpec((nb, CHUNK, D_A), fwd), pl.BlockSpec((nb, CHUNK, D_A), bwd),
                 pl.BlockSpec((nb, CHUNK, D_B), fwd), pl.BlockSpec((nb, CHUNK, D_B), bwd)]
    out_shape = [y_shape, y_shape, y_shape, y_shape]
    in_specs = [pl.BlockSpec((nb, CHUNK, A_COLS), fwd), pl.BlockSpec((nb, CHUNK, B_COLS), fwd),
                pl.BlockSpec((nb, CHUNK, A_COLS), bwd), pl.BlockSpec((nb, CHUNK, B_COLS), bwd)]
    args = [pa, pb, pa, pb]
    if from_zero:
        out_specs += [pl.BlockSpec(sa_block, st), pl.BlockSpec(sb_block, st)]
        out_shape += [jax.ShapeDtypeStruct((bsz,) + sa_block[1:], F32),
                      jax.ShapeDtypeStruct((bsz,) + sb_block[1:], F32)]
    else:
        in_specs += [pl.BlockSpec(sa_block, st), pl.BlockSpec(sb_block, st)]
        args += list(states)
    in_specs += [full((2, D_A)), full((2, D_A)), full((2, LORA, 2 * D_A)),
                 full((1, D_A)), full((1, D_A)), full((2, D_B))]
    args += [w0, a0, wa2, k_k, k_a, lb]
    bd_block = (nb, 2, D_A // PACK, PACK, PACK)
    return pl.pallas_call(
        functools.partial(_scan_kernel, from_zero=from_zero),
        grid=(bsz // nb, nc),
        in_specs=in_specs,
        out_specs=out_specs,
        out_shape=out_shape,
        scratch_shapes=[pltpu.VMEM(bd_block, F32), pltpu.VMEM(bd_block, F32)]
                       + [pltpu.VMEM(shape, dtype) for shape, dtype in _MASK_SCRATCH],
        compiler_params=pltpu.CompilerParams(dimension_semantics=("arbitrary", "arbitrary"),
                                             vmem_limit_bytes=VMEM_LIMIT),
        name="scan_state" if from_zero else "scan",
    )(*args)


def _out_kernel(x_ref, pa_ref, zb_ref, yf_ref, yb_ref, of_ref, ob_ref, gate_ref, a0_ref, a2p_ref, ka_ref,
                rk_ref, lnw_ref, lnb_ref, og_ref, wout_ref, fg_ref, out_ref):
    pa = pa_ref[0]
    r = pa[:, 0:D_A]
    k = pa[:, D_A:2 * D_A]
    v = pa[:, 2 * D_A:3 * D_A]
    za = pa[:, 3 * D_A:4 * D_A]
    bm64_bf = _head_mask_a().astype(BF16)

    def seg_sum(z):
        zb16 = z.astype(BF16)
        return jnp.concatenate([_dot(zb16[:, :PACK], bm64_bf), _dot(zb16[:, PACK:], bm64_bf)], axis=1)

    y = yf_ref[0].astype(F32) + yb_ref[0].astype(F32)
    mu = seg_sum(y) * (1.0 / HEAD_A)
    dlt = y - mu
    var = seg_sum(dlt * dlt) * (1.0 / HEAD_A)
    yn = dlt * lax.rsqrt(var + GN_EPS) * lnw_ref[...] + lnb_ref[...]
    a_dirs = []
    for d in range(2):
        lo = pa[:, 4 * D_A + LORA * d:4 * D_A + LORA * (d + 1)]
        a_dirs.append(_sigmoid(a0_ref[d:d + 1] + _mm(lo, a2p_ref[d])))
    kmean = k * (1.0 + (0.5 * (a_dirs[0] + a_dirs[1]) - 1.0) * ka_ref[...])
    bonus = seg_sum(r * kmean * rk_ref[...]) * v
    out_a = (yn + bonus) * _silu(za)

    o = of_ref[0].astype(F32) + ob_ref[0].astype(F32)
    og = og_ref[...]
    zb = zb_ref[0]
    outs = [out_a]
    for h in range(D_B // HEAD_B):
        sl = slice(HEAD_B * h, HEAD_B * (h + 1))
        oh = o[:, sl]
        oh = oh * lax.rsqrt(jnp.mean(oh * oh, axis=-1, keepdims=True) + EPS) * og[:, sl]
        outs.append(oh * _silu(zb[:, sl]))
    mix = jnp.concatenate(outs, axis=1)
    proj = _dot(mix.astype(BF16), wout_ref[...])
    hs = x_ref[0] + gate_ref[0] * proj
    out_ref[0] = hs * lax.rsqrt(jnp.mean(hs * hs, axis=-1, keepdims=True) + EPS) * fg_ref[...]


def _out_call(x, pa, pb, yf, yb, of, ob, gate, a0, a2p, k_a, r_k, lnx_w, lnx_b, onorm_g, w_out_bf16, final_g):
    bsz, seq, _ = x.shape
    tm = min(OUT_ROWS, seq)
    tok = lambda b, i: (b, i, 0)
    full = lambda shape: pl.BlockSpec(shape, lambda b, i: (0,) * len(shape))
    return pl.pallas_call(
        _out_kernel,
        grid=(bsz, seq // tm),
        in_specs=[pl.BlockSpec((1, tm, D_MODEL), tok),
                  pl.BlockSpec((1, tm, A_COLS), tok),
                  pl.BlockSpec((1, tm, D_B), lambda b, i: (b, i, 4)),
                  pl.BlockSpec((1, tm, D_A), tok), pl.BlockSpec((1, tm, D_A), tok),
                  pl.BlockSpec((1, tm, D_B), tok), pl.BlockSpec((1, tm, D_B), tok),
                  pl.BlockSpec((1, 1, D_MODEL), lambda b, i: (b, 0, 0)),
                  full((2, D_A)), full((2, LORA, D_A)), full((1, D_A)), full((1, D_A)),
                  full((1, D_A)), full((1, D_A)), full((1, D_B)),
                  full((D_MODEL, D_MODEL)), full((1, D_MODEL))],
        out_specs=pl.BlockSpec((1, tm, D_MODEL), tok),
        out_shape=jax.ShapeDtypeStruct((bsz, seq, D_MODEL), F32),
        compiler_params=pltpu.CompilerParams(dimension_semantics=("arbitrary", "arbitrary"),
                                             vmem_limit_bytes=VMEM_LIMIT),
        name="out",
    )(x, pa, pb, yf, yb, of, ob, gate, a0, a2p, k_a, r_k, lnx_w, lnx_b, onorm_g, w_out_bf16, final_g)


def kernel(x_prompt, x_sample, state_rwkv, state_hgrn, c, c_ctx, norm_g, w_ada, b_ada, w_in, mu_h, mu_v, w0, w2,
           a0, a2, k_k, k_a, r_k, lnx_w, lnx_b, lb_logits, onorm_g, w_out, final_g):
    l = 0
    bp = x_prompt.shape[0]
    bs = x_sample.shape[0]
    lb_all = jnp.cumsum(jax.nn.softmax(lb_logits.astype(F32), axis=0), axis=0)
    lb = lb_all[l]

    wa_bf = w_in[l, :, :A_COLS].astype(BF16)
    wb_bf = w_in[l, :, A_COLS:].astype(BF16)
    w_out_bf = w_out[l].astype(BF16)
    zeros = jnp.zeros((2, LORA // 2, D_A), F32)
    wa2 = jnp.concatenate([jnp.concatenate([w2[l], zeros], axis=2),
                           jnp.concatenate([zeros, a2[l]], axis=2)], axis=1)
    a2p = jnp.concatenate([zeros, a2[l]], axis=1)
    row = lambda z: z.reshape(1, -1)

    cc = jnp.concatenate([c_ctx[None, :], c, jnp.zeros((16 - 1 - bs, D_MODEL), F32)], axis=0)
    m = _ada_call(cc, w_ada[l], row(b_ada[l]))
    shift, scale, gate = m[:, :D_MODEL], m[:, D_MODEL:2 * D_MODEL], m[:, 2 * D_MODEL:]
    ctx = lambda z: jnp.broadcast_to(z[0:1, None, :], (bp, 1, D_MODEL))
    lat = lambda z: z[1:1 + bs, None, :]

    def path(x, sh, sc, gt, states, grid_shift):
        pa, pb = _proj_call(x, sh, sc, row(norm_g[l]), wa_bf, wb_bf, row(mu_h[l]), row(mu_v[l]), grid_shift)
        res = _scan_call(pa, pb, states, w0[l], a0[l], wa2, row(k_k[l]), row(k_a[l]), lb)
        yf, yb, of, ob = res[:4]
        y = _out_call(x, pa, pb, yf, yb, of, ob, gt, a0[l], a2p, row(k_a[l]), row(r_k[l]), row(lnx_w[l]),
                      row(lnx_b[l]), row(onorm_g[l]), w_out_bf, row(final_g))
        return y, res[4:]

    y_prompt, (s_a, s_b) = path(x_prompt, ctx(shift), ctx(scale), ctx(gate), None, False)
    groups_a = D_A // PACK
    sa0 = state_rwkv[:, l].reshape(bs, 2, groups_a, PACK, HEAD_A)
    y_sample, _ = path(x_sample, lat(shift), lat(scale), lat(gate), (sa0, state_hgrn[:, l]), True)
    s_a = s_a.reshape(bp, 1, 2, D_A // HEAD_A, HEAD_A, HEAD_A)
    return y_prompt, y_sample, s_a, s_b[:, None]
```

```python
import functools

import jax
import jax.numpy as jnp
from jax import lax
from jax.experimental import pallas as pl
from jax.experimental.pallas import tpu as pltpu

F32 = jnp.float32
BF16 = jnp.bfloat16

D_MODEL = 1024
D_A = 512
D_B = 512
HEAD_A = 64
HEAD_B = 128
LOG2_HEAD_A = HEAD_A.bit_length() - 1
LORA = 128
A_COLS = 4 * D_A + 2 * LORA
B_COLS = 5 * D_B
GRID_W = 64
CHUNK = 64
SCAN_BATCH = 4
SUM_TERMS = 2
BLOCK_ROWS = 32
BLOCK_LANES = 128
EPS = 1e-6
GN_EPS = 64e-5
DECAY_SCALE = 0.6065306597126334
NEG_LOG2E = -1.4426950408889634
PACK = 256
TNA = 768
TNB = 512
NA = A_COLS // TNA
NB = B_COLS // TNB
PROJ_ROWS = 2048
OUT_ROWS = 512
VMEM_LIMIT = 56 * 1024 * 1024

NN = ((1,), (0,))
NT = ((1,), (1,))
TN_DIMS = ((0,), (0,))


def _dot(a, b, dims=NN):
    return lax.dot_general(a, b, (dims, ((), ())), preferred_element_type=F32)


def _split(x, passes=1):
    if isinstance(x, tuple):
        return x
    hi = x.astype(BF16)
    if passes == 1:
        return (hi,)
    return hi, (x - hi.astype(F32)).astype(BF16)


def _mm(a, b, dims=NN, passes=1):
    a = _split(a, passes)
    b = _split(b, passes)
    if len(a) == 1 or len(b) == 1:
        return _dot(a[0], b[0], dims)
    return _dot(a[0], b[0], dims) + (_dot(a[0], b[1], dims) + _dot(a[1], b[0], dims))


def _terms(x, n):
    out = []
    for _ in range(n - 1):
        h = x.astype(BF16)
        out.append(h)
        x = x - h.astype(F32)
    out.append(x.astype(BF16))
    return out


def _sum_small_first(parts):
    return functools.reduce(lambda acc, z: z + acc, reversed(parts))


def _mm_exact_rhs(a, b_bf16, n=3):
    return _sum_small_first([_dot(t, b_bf16) for t in _terms(a, n)])


def _exp_neg(x):
    return jnp.exp2(x * NEG_LOG2E)


def _sigmoid(x):
    return 0.5 * jnp.tanh(0.5 * x) + 0.5


def _silu(x):
    return x * _sigmoid(x)


def _iota(shape, dim):
    return lax.broadcasted_iota(jnp.int32, shape, dim)


def _block_mask(rows, cols, row_shift, col_shift):
    same = (_iota((rows, cols), 0) >> row_shift) == (_iota((rows, cols), 1) >> col_shift)
    return jnp.where(same, 1.0, 0.0).astype(F32)


def _head_mask_a():
    return _block_mask(PACK, PACK, LOG2_HEAD_A, LOG2_HEAD_A)


def _ada_kernel(c_ref, w_ref, b_ref, m_ref):
    m_ref[...] = _mm(_silu(c_ref[...]), w_ref[...], passes=3) + b_ref[...]


def _ada_call(cc, w_ada, b_ada):
    rows = cc.shape[0]
    return pl.pallas_call(
        _ada_kernel,
        grid=(3,),
        in_specs=[pl.BlockSpec((rows, D_MODEL), lambda j: (0, 0)),
                  pl.BlockSpec((D_MODEL, D_MODEL), lambda j: (0, j)),
                  pl.BlockSpec((1, D_MODEL), lambda j: (0, j))],
        out_specs=pl.BlockSpec((rows, D_MODEL), lambda j: (0, j)),
        out_shape=jax.ShapeDtypeStruct((rows, 3 * D_MODEL), F32),
        compiler_params=pltpu.CompilerParams(dimension_semantics=("arbitrary",),
                                             vmem_limit_bytes=VMEM_LIMIT),
        name="ada",
    )(cc, w_ada, b_ada)


def _proj_kernel(x_ref, sh_ref, sc_ref, g_ref, wa_ref, wb_ref, muh_ref, muv_ref, pa_ref, pb_ref, xm_ref, *,
                 nbp, seq, grid_shift):
    j = pl.program_id(1)
    rows = nbp * seq

    @pl.when(j == 0)
    def _():
        for b in range(nbp):
            x = x_ref[b]
            xn = x * lax.rsqrt(jnp.mean(x * x, axis=-1, keepdims=True) + EPS) * g_ref[...]
            xm_ref[b * seq:(b + 1) * seq, :] = (xn * (1.0 + sc_ref[b]) + sh_ref[b]).astype(BF16)

    @pl.when(j < NA)
    def _():
        p = _dot(xm_ref[...], wa_ref[...])
        w = GRID_W if grid_shift else seq
        nblk = rows // w
        pos = _iota((1, w, TNA), 1)
        to3 = lambda z: z.reshape(nblk, w, TNA)
        prev = jnp.where(pos == 0, 0.0, to3(pltpu.roll(p, 1, axis=0)))
        nxt = jnp.where(pos == w - 1, 0.0, to3(pltpu.roll(p, rows - 1, axis=0)))
        muh = muh_ref[...]
        out = (0.5 * muh) * (prev + nxt)
        if grid_shift:
            muv = muv_ref[...]
            p3 = to3(p)
            zblk = jnp.zeros((1, w, TNA), F32)
            up = jnp.concatenate([zblk, p3[:nblk - 1]], axis=0)
            dn = jnp.concatenate([p3[1:], zblk], axis=0)
            out = (1.0 - muh - muv) * p3 + out + (0.5 * muv) * (up + dn)
        else:
            out = (1.0 - muh) * to3(p) + out
        pa_ref[...] = out.reshape(nbp, seq, TNA)

    @pl.when(j >= NA)
    def _():
        pb_ref[...] = _dot(xm_ref[...], wb_ref[...]).reshape(nbp, seq, TNB)


def _proj_call(x, shift, scale, norm_g, wa_bf16, wb_bf16, mu_h, mu_v, grid_shift):
    bsz, seq, _ = x.shape
    nbp = 1 if grid_shift else PROJ_ROWS // seq
    kern = functools.partial(_proj_kernel, nbp=nbp, seq=seq, grid_shift=grid_shift)
    a_idx = lambda j: jnp.minimum(j, NA - 1)
    b_idx = lambda j: jnp.maximum(j - NA, 0)
    return pl.pallas_call(
        kern,
        grid=(bsz // nbp, NA + NB),
        in_specs=[pl.BlockSpec((nbp, seq, D_MODEL), lambda b, j: (b, 0, 0)),
                  pl.BlockSpec((nbp, 1, D_MODEL), lambda b, j: (b, 0, 0)),
                  pl.BlockSpec((nbp, 1, D_MODEL), lambda b, j: (b, 0, 0)),
                  pl.BlockSpec((1, D_MODEL), lambda b, j: (0, 0)),
                  pl.BlockSpec((D_MODEL, TNA), lambda b, j: (0, a_idx(j))),
                  pl.BlockSpec((D_MODEL, TNB), lambda b, j: (0, b_idx(j))),
                  pl.BlockSpec((1, TNA), lambda b, j: (0, a_idx(j))),
                  pl.BlockSpec((1, TNA), lambda b, j: (0, a_idx(j)))],
        out_specs=[pl.BlockSpec((nbp, seq, TNA), lambda b, j: (b, 0, a_idx(j))),
                   pl.BlockSpec((nbp, seq, TNB), lambda b, j: (b, 0, b_idx(j)))],
        out_shape=[jax.ShapeDtypeStruct((bsz, seq, A_COLS), F32),
                   jax.ShapeDtypeStruct((bsz, seq, B_COLS), F32)],
        scratch_shapes=[pltpu.VMEM((nbp * seq, D_MODEL), BF16)],
        compiler_params=pltpu.CompilerParams(dimension_semantics=("arbitrary", "arbitrary"),
                                             vmem_limit_bytes=VMEM_LIMIT),
        name="proj_grid" if grid_shift else "proj_seq",
    )(x, shift, scale, norm_g, wa_bf16, wb_bf16, mu_h, mu_v)


def _scan_masks():
    c = CHUNK
    t = _iota((c, PACK), 0)
    s = _iota((c, PACK), 1) & (c - 1)
    tt = _iota((c, c), 0)
    ss = _iota((c, c), 1)
    f01 = lambda cond: jnp.where(cond, 1.0, 0.0).astype(F32)
    return dict(
        bm64_bf=_head_mask_a().astype(BF16),
        eye=f01(t == s),
        incl=(f01(t >= s), f01(t <= s)),
        strict=(f01(t > s), f01(t < s)),
        tri=(f01(tt >= ss).astype(BF16), f01(tt <= ss).astype(BF16)),
    )


_MASK_KEYS = ("bm64_bf", "eye", "incl", "strict", "tri")
_MASK_SCRATCH = (((PACK, PACK), BF16), ((CHUNK, PACK), F32), ((2, CHUNK, PACK), F32), ((2, CHUNK, PACK), F32),
                 ((2, CHUNK, CHUNK), BF16))


def _store_masks(mask_refs):
    masks = _scan_masks()
    for ref, key in zip(mask_refs, _MASK_KEYS):
        val = masks[key]
        if isinstance(val, tuple):
            for i, z in enumerate(val):
                ref[i] = z
        else:
            ref[...] = val


class _Masks:
    def __init__(self, mask_refs):
        self._refs = dict(zip(_MASK_KEYS, mask_refs))

    def __getitem__(self, key):
        ref = self._refs[key]
        return (ref[0], ref[1]) if len(ref.shape) == 3 else ref[...]


def _half_lane_masks(dtype):
    lane = _iota((1, BLOCK_LANES), 1)
    return (jnp.where(lane < HEAD_A, 1.0, 0.0).astype(dtype), jnp.where(lane >= HEAD_A, 1.0, 0.0).astype(dtype))


def _block_diag_a(x):
    halves = _half_lane_masks(x.dtype)
    zero = jnp.zeros((x.shape[0], BLOCK_LANES), x.dtype)
    rows = []
    for h in range(PACK // HEAD_A):
        tile = h * HEAD_A // BLOCK_LANES
        piece = x[:, BLOCK_LANES * tile:BLOCK_LANES * (tile + 1)] * halves[h % 2]
        rows.append(jnp.concatenate([piece, zero] if tile == 0 else [zero, piece], axis=1))
    return jnp.concatenate(rows, axis=0)


def _block_diag_b(x):
    heads = D_B // HEAD_B
    zero = jnp.zeros((x.shape[0], HEAD_B), x.dtype)
    return jnp.concatenate(
        [jnp.concatenate([x[:, HEAD_B * h:HEAD_B * (h + 1)] if t == h else zero for t in range(heads)], axis=1)
         for h in range(heads)], axis=0)


def _lockstep(gens):
    results = [None] * len(gens)
    live = list(enumerate(gens))
    while live:
        still = []
        for i, g in live:
            try:
                next(g)
                still.append((i, g))
            except StopIteration as stop:
                results[i] = stop.value
        live = still
    return results


def _rwkv_group(rt, kt, kh, bh, kb, bb, v, s_ref, egc, mk, d):
    c = CHUNK
    bd = lambda x: _block_diag_a(x.astype(BF16))
    kr = _split(jnp.concatenate([kt, rt], axis=0))
    aa_k = _mm(kr, bd(kh), NT)
    aa_b = _mm(kr, bd(bh), NT)
    s_bd = s_ref[...]
    krs = _mm(kr, s_bd, NT)
    yield
    a_kk = aa_k[:c] * mk["strict"][d]
    a_rk = aa_k[c:] * mk["incl"][d]
    n = -(aa_b[:c] * mk["strict"][d])
    a_rb = aa_b[c:] * mk["incl"][d]
    x = mk["eye"] + n
    p = _mm(n, bd(n))
    akv = _mm(jnp.concatenate([a_kk, a_rk], axis=0), bd(v))
    yield
    for i in range(5):
        if i < 4:
            xp = _mm(jnp.concatenate([x, p], axis=0), bd(p))
            x = x + xp[:c]
            p = xp[c:]
        else:
            x = x + _mm(x, bd(p))
        yield
    u = _mm(x, bd(krs[:c] + akv[:c]))
    yield
    y = krs[c:] + akv[c:] - _mm(a_rb, bd(u))
    vu = jnp.concatenate([v, (-u).astype(BF16)], axis=0)
    kbb = jnp.concatenate([kb, bb], axis=0)
    upd = _mm(vu, kbb, TN_DIMS)
    halves = _half_lane_masks(F32)
    zero = jnp.zeros((HEAD_A, BLOCK_LANES), F32)
    rows = []
    for h in range(PACK // HEAD_A):
        tile = h * HEAD_A // BLOCK_LANES
        rs, ls = slice(HEAD_A * h, HEAD_A * (h + 1)), slice(BLOCK_LANES * tile, BLOCK_LANES * (tile + 1))
        piece = (s_bd[rs, ls] * egc[:, ls] + upd[rs, ls]) * halves[h % 2]
        rows.append(jnp.concatenate([piece, zero] if tile == 0 else [zero, piece], axis=1))
    s_ref[...] = jnp.concatenate(rows, axis=0)
    return y


def _hgrn_dir(qn, kn, qt, kbh, vb, eghc, s_refs, mk, d):
    a_p = _mm(qn, _block_diag_b(kn.astype(BF16)), NT) * mk["incl"][d]
    inter = []
    for grp, s_ref in enumerate(s_refs):
        sl = slice(PACK * grp, PACK * (grp + 1))
        st = s_ref[...]
        inter.append(_mm(qt[:, sl], st, NT))
        upd = _mm(vb[:, sl], kbh[:, sl], TN_DIMS)
        decay = eghc[:, sl]
        zero = jnp.zeros((HEAD_B, HEAD_B), F32)
        blocks = [st[b0:b0 + HEAD_B, b0:b0 + HEAD_B] * decay[:, b0:b0 + HEAD_B] + upd[b0:b0 + HEAD_B, b0:b0 + HEAD_B]
                  for b0 in range(0, PACK, HEAD_B)]
        s_ref[...] = jnp.concatenate([jnp.concatenate([blocks[0], zero], axis=1),
                                      jnp.concatenate([zero, blocks[1]], axis=1)], axis=0)
    yield
    o = _mm(a_p, _block_diag_b(vb.astype(BF16)))
    return o + jnp.concatenate(inter, axis=1)


def _blockwise(fn, operands, n_out, rows=CHUNK, width=D_A):
    cols = []
    for l0 in range(0, width, BLOCK_LANES):
        parts = []
        for r0 in range(0, rows, BLOCK_ROWS):
            blk = [z[(slice(None) if z.shape[0] == 1 else slice(r0, r0 + BLOCK_ROWS)), l0:l0 + BLOCK_LANES]
                   for z in operands]
            parts.append(fn(*blk))
        cols.append([jnp.concatenate([p[i] for p in parts], axis=0) for i in range(n_out)])
    return [jnp.concatenate([col[i] for col in cols], axis=1) for i in range(n_out)]


def _chunk_operands(pa_ref, pb_ref, bi, d, prm, mk):
    c = CHUNK
    n_t = SUM_TERMS
    w0, a0, wa2, k_k, k_a, lb = prm
    cols = lambda ref, i, width=D_A: ref.at[bi, :, i * width:(i + 1) * width]
    r_ref, k_ref, v_ref = cols(pa_ref, 0), cols(pa_ref, 1), cols(pa_ref, 2)
    lo = pa_ref[bi, :, 4 * D_A + LORA * d:4 * D_A + LORA * (d + 1)]
    lo = jnp.where(_iota((c, LORA), 1) < LORA // 2, jnp.tanh(lo), lo)
    wa = _mm(lo, wa2[d])
    yield

    def gates(wa_w, wa_a, k, fr, w0_, a0_, kk_, lb_):
        lw = -DECAY_SCALE * _sigmoid(w0_ + wa_w)
        a = _sigmoid(a0_ + wa_a)
        kk = k * kk_
        f = lb_ + (1.0 - lb_) * _sigmoid(fr)
        gf = jnp.log(f)
        return (lw, a, kk, 1.0 - f, (kk * kk).astype(BF16), *_terms(lw, n_t), gf.astype(BF16))

    res = _blockwise(gates, [wa[:, :D_A], wa[:, D_A:], k_ref, cols(pb_ref, 2 + d, D_B), w0[d:d + 1], a0[d:d + 1],
                             k_k, lb[d:d + 1]], 6 + n_t)
    lw, a, kk, kf, sq = res[:5]
    lw_t, gf_t = res[5:5 + n_t], res[5 + n_t]
    seg = lambda lanes: _dot(sq[:, lanes], mk["bm64_bf"])
    n2 = jnp.concatenate([seg(slice(0, PACK)), seg(slice(PACK, D_A))], axis=1)
    g = _sum_small_first([_dot(mk["tri"][d], t) for t in lw_t])
    gh = _dot(mk["tri"][d], gf_t)
    yield
    last = c - 1 if d == 0 else 0
    gc, ghc, ghm = g[last:last + 1], gh[last:last + 1], gh[c // 2:c // 2 + 1]
    egc, eghc = jnp.exp(gc), jnp.exp(ghc)

    def rwkv_operands(r, k, kk_, n2_, a_, g_, lw_, ka_, egc_):
        kap = kk_ * lax.rsqrt(jnp.maximum(n2_, 1e-24))
        keff = k * (1.0 + (a_ - 1.0) * ka_)
        eg = jnp.exp(g_)
        eng = 1.0 / eg
        kh = keff * eng
        bh = (kap * a_) * eng
        out = (r * eg, kap * jnp.exp(g_ - lw_), kh, bh, kh * egc_, bh * egc_)
        return tuple(z.astype(BF16) for z in out)

    rt, kt, kh, bh, kb, bb = _blockwise(rwkv_operands, [r_ref, k_ref, kk, n2, a, g, lw, k_a, egc], 6)

    def hgrn_operands(qr, kf_, gh_, ghm_, e_m, e_cm):
        q = _silu(qr)
        ghn = gh_ - ghm_
        en = jnp.exp(ghn)
        qn = q * en
        kn = kf_ * (1.0 / en)
        return tuple(z.astype(BF16) for z in (qn, kn, qn * e_m, kn * e_cm))

    qn, kn, qt, kbh = _blockwise(hgrn_operands, [cols(pb_ref, 0, D_B), kf, gh, ghm, jnp.exp(ghm),
                                                 jnp.exp(ghc - ghm)], 4, width=D_B)
    v = v_ref[...].astype(BF16)
    vb = pb_ref[bi, :, D_B:2 * D_B].astype(BF16)
    return (rt, kt, kh, bh, kb, bb, v, qn, kn, qt, kbh, vb), (egc, eghc)


def _scan_dir(pa_ref, pb_ref, bi, d, prm, sa_scr, sb_scr, mk):
    ops, (egc, eghc) = yield from _chunk_operands(pa_ref, pb_ref, bi, d, prm, mk)
    rt, kt, kh, bh, kb, bb, v, qn, kn, qt, kbh, vb = ops
    chains = []
    for grp in range(D_A // PACK):
        sl = slice(PACK * grp, PACK * (grp + 1))
        chains.append(_rwkv_group(rt[:, sl], kt[:, sl], kh[:, sl], bh[:, sl], kb[:, sl], bb[:, sl], v[:, sl],
                                  sa_scr.at[bi, d, grp], egc[:, sl], mk, d))
    chains.append(_hgrn_dir(qn, kn, qt, kbh, vb, eghc, [sb_scr.at[bi, d, grp] for grp in range(D_B // PACK)],
                            mk, d))
    results = [None] * len(chains)
    live = list(enumerate(chains))
    while live:
        still = []
        for i, g in live:
            try:
                next(g)
                still.append((i, g))
            except StopIteration as stop:
                results[i] = stop.value
        live = still
        yield
    return jnp.concatenate(results[:-1], axis=1), results[-1]


def _state_slots():
    return [(bi, d, grp) for bi in range(SCAN_BATCH) for d in range(2) for grp in range(D_A // PACK)]


def _load_states(sa0_ref, sb0_ref, sa_scr, sb_scr):
    spread = jnp.where(_iota((HEAD_A, PACK), 0) == (_iota((HEAD_A, PACK), 1) & (HEAD_A - 1)), 1.0, 0.0)
    spread = spread.astype(BF16)
    bm64 = _head_mask_a()
    zero = jnp.zeros((HEAD_B, HEAD_B), F32)
    for bi, d, grp in _state_slots():
        sa_scr[bi, d, grp] = _mm_exact_rhs(sa0_ref[bi, d, grp], spread) * bm64
        h0 = sb0_ref[bi, d, 2 * grp].T
        h1 = sb0_ref[bi, d, 2 * grp + 1].T
        sb_scr[bi, d, grp] = jnp.concatenate([jnp.concatenate([h0, zero], axis=1),
                                              jnp.concatenate([zero, h1], axis=1)], axis=0)


def _store_states(sa_scr, sb_scr, sa_out, sb_out):
    gather = jnp.where((_iota((PACK, HEAD_A), 0) & (HEAD_A - 1)) == _iota((PACK, HEAD_A), 1), 1.0, 0.0)
    gather = gather.astype(BF16)
    for bi, d, grp in _state_slots():
        sa_out[bi, d, grp] = _mm_exact_rhs(sa_scr[bi, d, grp], gather)
        for hh in range(PACK // HEAD_B):
            blk = sb_scr[bi, d, grp, HEAD_B * hh:HEAD_B * (hh + 1), HEAD_B * hh:HEAD_B * (hh + 1)]
            sb_out[bi, d, 2 * grp + hh] = blk.T


def _scan_kernel(paf_ref, pbf_ref, pab_ref, pbb_ref, *rest, from_zero):
    rest, mask_refs = rest[:-len(_MASK_KEYS)], rest[-len(_MASK_KEYS):]
    if from_zero:
        (w0_ref, a0_ref, wa2_ref, kk_ref, ka_ref, lb_ref, yf_ref, yb_ref, of_ref, ob_ref, sa_out, sb_out,
         sa_scr, sb_scr) = rest
    else:
        (sa0_ref, sb0_ref, w0_ref, a0_ref, wa2_ref, kk_ref, ka_ref, lb_ref, yf_ref, yb_ref, of_ref, ob_ref,
         sa_scr, sb_scr) = rest
    ci = pl.program_id(1)

    @pl.when(ci == 0)
    def _():
        _store_masks(mask_refs)
        if from_zero:
            sa_scr[...] = jnp.zeros(sa_scr.shape, F32)
            sb_scr[...] = jnp.zeros(sb_scr.shape, F32)
        else:
            _load_states(sa0_ref, sb0_ref, sa_scr, sb_scr)

    mk = _Masks(mask_refs)
    prm = (w0_ref[...], a0_ref[...], wa2_ref, kk_ref[...], ka_ref[...], lb_ref[...])
    p_refs = ((paf_ref, pbf_ref), (pab_ref, pbb_ref))
    runs = [(bi, d) for bi in range(SCAN_BATCH) for d in range(2)]
    outs = _lockstep([_scan_dir(p_refs[d][0], p_refs[d][1], bi, d, prm, sa_scr, sb_scr, mk) for bi, d in runs])
    for (bi, d), (y, o) in zip(runs, outs):
        (yf_ref, yb_ref)[d][bi] = y.astype(BF16)
        (of_ref, ob_ref)[d][bi] = o.astype(BF16)

    if from_zero:
        @pl.when(ci == pl.num_programs(1) - 1)
        def _():
            _store_states(sa_scr, sb_scr, sa_out, sb_out)


def _scan_call(pa, pb, states, w0, a0, wa2, k_k, k_a, lb):
    bsz, seq, _ = pa.shape
    nc = seq // CHUNK
    nb = SCAN_BATCH
    from_zero = states is None
    fwd = lambda b, c: (b, c, 0)
    bwd = lambda b, c: (b, nc - 1 - c, 0)
    st = lambda b, c: (b, 0, 0, 0, 0)
    full = lambda shape: pl.BlockSpec(shape, lambda b, c: (0,) * len(shape))
    sa_block = (nb, 2, D_A // PACK, PACK, HEAD_A)
    sb_block = (nb, 2, D_B // HEAD_B, HEAD_B, HEAD_B)
    y_shape = jax.ShapeDtypeStruct((bsz, seq, D_A), BF16)
    out_specs = [pl.BlockSpec((nb, CHUNK, D_A), fwd), pl.BlockSpec((nb, CHUNK, D_A), bwd),
                 pl.BlockSpec((nb, CHUNK, D_B), fwd), pl.BlockSpec((nb, CHUNK, D_B), bwd)]
    out_shape = [y_shape, y_shape, y_shape, y_shape]
    in_specs = [pl.BlockSpec((nb, CHUNK, A_COLS), fwd), pl.BlockSpec((nb, CHUNK, B_COLS), fwd),
                pl.BlockSpec((nb, CHUNK, A_COLS), bwd), pl.BlockSpec((nb, CHUNK, B_COLS), bwd)]
    args = [pa, pb, pa, pb]
    if from_zero:
        out_specs += [pl.BlockSpec(sa_block, st), pl.BlockSpec(sb_block, st)]
        out_shape += [jax.ShapeDtypeStruct((bsz,) + sa_block[1:], F32),
                      jax.ShapeDtypeStruct((bsz,) + sb_block[1:], F32)]
    else:
        in_specs += [pl.BlockSpec(sa_block, st), pl.BlockSpec(sb_block, st)]
        args += list(states)
    in_specs += [full((2, D_A)), full((2, D_A)), full((2, LORA, 2 * D_A)),
                 full((1, D_A)), full((1, D_A)), full((2, D_B))]
    args += [w0, a0, wa2, k_k, k_a, lb]
    bd_block = (nb, 2, D_A // PACK, PACK, PACK)
    return pl.pallas_call(
        functools.partial(_scan_kernel, from_zero=from_zero),
        grid=(bsz // nb, nc),
        in_specs=in_specs,
        out_specs=out_specs,
        out_shape=out_shape,
        scratch_shapes=[pltpu.VMEM(bd_block, F32), pltpu.VMEM(bd_block, F32)]
                       + [pltpu.VMEM(shape, dtype) for shape, dtype in _MASK_SCRATCH],
        compiler_params=pltpu.CompilerParams(dimension_semantics=("arbitrary", "arbitrary"),
                                             vmem_limit_bytes=VMEM_LIMIT),
        name="scan_state" if from_zero else "scan",
    )(*args)


def _out_kernel(x_ref, pa_ref, zb_ref, yf_ref, yb_ref, of_ref, ob_ref, gate_ref, a0_ref, a2p_ref, ka_ref,
                rk_ref, lnw_ref, lnb_ref, og_ref, wout_ref, fg_ref, out_ref):
    pa = pa_ref[0]
    r = pa[:, 0:D_A]
    k = pa[:, D_A:2 * D_A]
    v = pa[:, 2 * D_A:3 * D_A]
    za = pa[:, 3 * D_A:4 * D_A]
    bm64_bf = _head_mask_a().astype(BF16)

    def seg_sum(z):
        zb16 = z.astype(BF16)
        return jnp.concatenate([_dot(zb16[:, :PACK], bm64_bf), _dot(zb16[:, PACK:], bm64_bf)], axis=1)

    y = yf_ref[0].astype(F32) + yb_ref[0].astype(F32)
    mu = seg_sum(y) * (1.0 / HEAD_A)
    dlt = y - mu
    var = seg_sum(dlt * dlt) * (1.0 / HEAD_A)
    yn = dlt * lax.rsqrt(var + GN_EPS) * lnw_ref[...] + lnb_ref[...]
    a_dirs = []
    for d in range(2):
        lo = pa[:, 4 * D_A + LORA * d:4 * D_A + LORA * (d + 1)]
        a_dirs.append(_sigmoid(a0_ref[d:d + 1] + _mm(lo, a2p_ref[d])))
    kmean = k * (1.0 + (0.5 * (a_dirs[0] + a_dirs[1]) - 1.0) * ka_ref[...])
    bonus = seg_sum(r * kmean * rk_ref[...]) * v
    out_a = (yn + bonus) * _silu(za)

    o = of_ref[0].astype(F32) + ob_ref[0].astype(F32)
    og = og_ref[...]
    zb = zb_ref[0]
    outs = [out_a]
    for h in range(D_B // HEAD_B):
        sl = slice(HEAD_B * h, HEAD_B * (h + 1))
        oh = o[:, sl]
        oh = oh * lax.rsqrt(jnp.mean(oh * oh, axis=-1, keepdims=True) + EPS) * og[:, sl]
        outs.append(oh * _silu(zb[:, sl]))
    mix = jnp.concatenate(outs, axis=1)
    proj = _dot(mix.astype(BF16), wout_ref[...])
    hs = x_ref[0] + gate_ref[0] * proj
    out_ref[0] = hs * lax.rsqrt(jnp.mean(hs * hs, axis=-1, keepdims=True) + EPS) * fg_ref[...]


def _out_call(x, pa, pb, yf, yb, of, ob, gate, a0, a2p, k_a, r_k, lnx_w, lnx_b, onorm_g, w_out_bf16, final_g):
    bsz, seq, _ = x.shape
    tm = min(OUT_ROWS, seq)
    tok = lambda b, i: (b, i, 0)
    full = lambda shape: pl.BlockSpec(shape, lambda b, i: (0,) * len(shape))
    return pl.pallas_call(
        _out_kernel,
        grid=(bsz, seq // tm),
        in_specs=[pl.BlockSpec((1, tm, D_MODEL), tok),
                  pl.BlockSpec((1, tm, A_COLS), tok),
                  pl.BlockSpec((1, tm, D_B), lambda b, i: (b, i, 4)),
                  pl.BlockSpec((1, tm, D_A), tok), pl.BlockSpec((1, tm, D_A), tok),
                  pl.BlockSpec((1, tm, D_B), tok), pl.BlockSpec((1, tm, D_B), tok),
                  pl.BlockSpec((1, 1, D_MODEL), lambda b, i: (b, 0, 0)),
                  full((2, D_A)), full((2, LORA, D_A)), full((1, D_A)), full((1, D_A)),
                  full((1, D_A)), full((1, D_A)), full((1, D_B)),
                  full((D_MODEL, D_MODEL)), full((1, D_MODEL))],
        out_specs=pl.BlockSpec((1, tm, D_MODEL), tok),
        out_shape=jax.ShapeDtypeStruct((bsz, seq, D_MODEL), F32),
        compiler_params=pltpu.CompilerParams(dimension_semantics=("arbitrary", "arbitrary"),
                                             vmem_limit_bytes=VMEM_LIMIT),
        name="out",
    )(x, pa, pb, yf, yb, of, ob, gate, a0, a2p, k_a, r_k, lnx_w, lnx_b, onorm_g, w_out_bf16, final_g)


def kernel(x_prompt, x_sample, state_rwkv, state_hgrn, c, c_ctx, norm_g, w_ada, b_ada, w_in, mu_h, mu_v, w0, w2,
           a0, a2, k_k, k_a, r_k, lnx_w, lnx_b, lb_logits, onorm_g, w_out, final_g):
    l = 0
    bp = x_prompt.shape[0]
    bs = x_sample.shape[0]
    lb_all = jnp.cumsum(jax.nn.softmax(lb_logits.astype(F32), axis=0), axis=0)
    lb = lb_all[l]

    wa_bf = w_in[l, :, :A_COLS].astype(BF16)
    wb_bf = w_in[l, :, A_COLS:].astype(BF16)
    w_out_bf = w_out[l].astype(BF16)
    zeros = jnp.zeros((2, LORA // 2, D_A), F32)
    wa2 = jnp.concatenate([jnp.concatenate([w2[l], zeros], axis=2),
                           jnp.concatenate([zeros, a2[l]], axis=2)], axis=1)
    a2p = jnp.concatenate([zeros, a2[l]], axis=1)
    row = lambda z: z.reshape(1, -1)

    cc = jnp.concatenate([c_ctx[None, :], c, jnp.zeros((16 - 1 - bs, D_MODEL), F32)], axis=0)
    m = _ada_call(cc, w_ada[l], row(b_ada[l]))
    shift, scale, gate = m[:, :D_MODEL], m[:, D_MODEL:2 * D_MODEL], m[:, 2 * D_MODEL:]
    ctx = lambda z: jnp.broadcast_to(z[0:1, None, :], (bp, 1, D_MODEL))
    lat = lambda z: z[1:1 + bs, None, :]

    def path(x, sh, sc, gt, states, grid_shift):
        pa, pb = _proj_call(x, sh, sc, row(norm_g[l]), wa_bf, wb_bf, row(mu_h[l]), row(mu_v[l]), grid_shift)
        res = _scan_call(pa, pb, states, w0[l], a0[l], wa2, row(k_k[l]), row(k_a[l]), lb)
        yf, yb, of, ob = res[:4]
        y = _out_call(x, pa, pb, yf, yb, of, ob, gt, a0[l], a2p, row(k_a[l]), row(r_k[l]), row(lnx_w[l]),
                      row(lnx_b[l]), row(onorm_g[l]), w_out_bf, row(final_g))
        return y, res[4:]

    y_prompt, (s_a, s_b) = path(x_prompt, ctx(shift), ctx(scale), ctx(gate), None, False)
    groups_a = D_A // PACK
    sa0 = state_rwkv[:, l].reshape(bs, 2, groups_a, PACK, HEAD_A)
    y_sample, _ = path(x_sample, lat(shift), lat(scale), lat(gate), (sa0, state_hgrn[:, l]), True)
    s_a = s_a.reshape(bp, 1, 2, D_A // HEAD_A, HEAD_A, HEAD_A)
    return y_prompt, y_sample, s_a, s_b[:, None]
```

```python
import functools

import jax
import jax.numpy as jnp
from jax import lax
from jax.experimental import pallas as pl
from jax.experimental.pallas import tpu as pltpu

F32 = jnp.float32
BF16 = jnp.bfloat16

D_MODEL = 1024
D_A = 512
D_B = 512
HEAD_A = 64
HEAD_B = 128
LOG2_HEAD_A = HEAD_A.bit_length() - 1
LORA = 128
A_COLS = 4 * D_A + 2 * LORA
B_COLS = 5 * D_B
GRID_W = 64
CHUNK = 64
SCAN_BATCH = 4
SUM_TERMS = 2
BLOCK_ROWS = 32
BLOCK_LANES = 128
EPS = 1e-6
GN_EPS = 64e-5
DECAY_SCALE = 0.6065306597126334
NEG_LOG2E = -1.4426950408889634
PACK = 256
TNA = 768
TNB = 512
NA = A_COLS // TNA
NB = B_COLS // TNB
PROJ_ROWS = 2048
OUT_ROWS = 512
VMEM_LIMIT = 56 * 1024 * 1024

NN = ((1,), (0,))
NT = ((1,), (1,))
TN_DIMS = ((0,), (0,))


def _dot(a, b, dims=NN):
    return lax.dot_general(a, b, (dims, ((), ())), preferred_element_type=F32)


def _split(x, passes=1):
    if isinstance(x, tuple):
        return x
    hi = x.astype(BF16)
    if passes == 1:
        return (hi,)
    return hi, (x - hi.astype(F32)).astype(BF16)


def _mm(a, b, dims=NN, passes=1):
    a = _split(a, passes)
    b = _split(b, passes)
    if len(a) == 1 or len(b) == 1:
        return _dot(a[0], b[0], dims)
    return _dot(a[0], b[0], dims) + (_dot(a[0], b[1], dims) + _dot(a[1], b[0], dims))


def _terms(x, n):
    out = []
    for _ in range(n - 1):
        h = x.astype(BF16)
        out.append(h)
        x = x - h.astype(F32)
    out.append(x.astype(BF16))
    return out


def _sum_small_first(parts):
    return functools.reduce(lambda acc, z: z + acc, reversed(parts))


def _mm_exact_rhs(a, b_bf16, n=3):
    return _sum_small_first([_dot(t, b_bf16) for t in _terms(a, n)])


def _exp_neg(x):
    return jnp.exp2(x * NEG_LOG2E)


def _sigmoid(x):
    return 0.5 * jnp.tanh(0.5 * x) + 0.5


def _silu(x):
    h = 0.5 * x
    return h * jnp.tanh(h) + h


def _iota(shape, dim):
    return lax.broadcasted_iota(jnp.int32, shape, dim)


def _block_mask(rows, cols, row_shift, col_shift):
    same = (_iota((rows, cols), 0) >> row_shift) == (_iota((rows, cols), 1) >> col_shift)
    return jnp.where(same, 1.0, 0.0).astype(F32)


def _head_mask_a():
    return _block_mask(PACK, PACK, LOG2_HEAD_A, LOG2_HEAD_A)


def _ada_kernel(c_ref, w_ref, b_ref, m_ref):
    m_ref[...] = _mm(_silu(c_ref[...]), w_ref[...], passes=3) + b_ref[...]


def _ada_call(cc, w_ada, b_ada):
    rows = cc.shape[0]
    return pl.pallas_call(
        _ada_kernel,
        grid=(3,),
        in_specs=[pl.BlockSpec((rows, D_MODEL), lambda j: (0, 0)),
                  pl.BlockSpec((D_MODEL, D_MODEL), lambda j: (0, j)),
                  pl.BlockSpec((1, D_MODEL), lambda j: (0, j))],
        out_specs=pl.BlockSpec((rows, D_MODEL), lambda j: (0, j)),
        out_shape=jax.ShapeDtypeStruct((rows, 3 * D_MODEL), F32),
        compiler_params=pltpu.CompilerParams(dimension_semantics=("arbitrary",),
                                             vmem_limit_bytes=VMEM_LIMIT),
        name="ada",
    )(cc, w_ada, b_ada)


def _proj_kernel(x_ref, sh_ref, sc_ref, g_ref, wa_ref, wb_ref, muh_ref, muv_ref, pa_ref, pb_ref, xm_ref, *,
                 nbp, seq, grid_shift):
    j = pl.program_id(1)
    rows = nbp * seq

    @pl.when(j == 0)
    def _():
        for b in range(nbp):
            x = x_ref[b]
            xn = x * lax.rsqrt(jnp.mean(x * x, axis=-1, keepdims=True) + EPS) * g_ref[...]
            xm_ref[b * seq:(b + 1) * seq, :] = (xn * (1.0 + sc_ref[b]) + sh_ref[b]).astype(BF16)

    @pl.when(j < NA)
    def _():
        p = _dot(xm_ref[...], wa_ref[...])
        w = GRID_W if grid_shift else seq
        nblk = rows // w
        pos = _iota((1, w, TNA), 1)
        to3 = lambda z: z.reshape(nblk, w, TNA)
        prev = jnp.where(pos == 0, 0.0, to3(pltpu.roll(p, 1, axis=0)))
        nxt = jnp.where(pos == w - 1, 0.0, to3(pltpu.roll(p, rows - 1, axis=0)))
        muh = muh_ref[...]
        out = (0.5 * muh) * (prev + nxt)
        if grid_shift:
            muv = muv_ref[...]
            p3 = to3(p)
            zblk = jnp.zeros((1, w, TNA), F32)
            up = jnp.concatenate([zblk, p3[:nblk - 1]], axis=0)
            dn = jnp.concatenate([p3[1:], zblk], axis=0)
            out = (1.0 - muh - muv) * p3 + out + (0.5 * muv) * (up + dn)
        else:
            out = (1.0 - muh) * to3(p) + out
        pa_ref[...] = out.reshape(nbp, seq, TNA)

    @pl.when(j >= NA)
    def _():
        pb_ref[...] = _dot(xm_ref[...], wb_ref[...]).reshape(nbp, seq, TNB)


def _proj_call(x, shift, scale, norm_g, wa_bf16, wb_bf16, mu_h, mu_v, grid_shift):
    bsz, seq, _ = x.shape
    nbp = 1 if grid_shift else PROJ_ROWS // seq
    kern = functools.partial(_proj_kernel, nbp=nbp, seq=seq, grid_shift=grid_shift)
    a_idx = lambda j: jnp.minimum(j, NA - 1)
    b_idx = lambda j: jnp.maximum(j - NA, 0)
    return pl.pallas_call(
        kern,
        grid=(bsz // nbp, NA + NB),
        in_specs=[pl.BlockSpec((nbp, seq, D_MODEL), lambda b, j: (b, 0, 0)),
                  pl.BlockSpec((nbp, 1, D_MODEL), lambda b, j: (b, 0, 0)),
                  pl.BlockSpec((nbp, 1, D_MODEL), lambda b, j: (b, 0, 0)),
                  pl.BlockSpec((1, D_MODEL), lambda b, j: (0, 0)),
                  pl.BlockSpec((D_MODEL, TNA), lambda b, j: (0, a_idx(j))),
                  pl.BlockSpec((D_MODEL, TNB), lambda b, j: (0, b_idx(j))),
                  pl.BlockSpec((1, TNA), lambda b, j: (0, a_idx(j))),
                  pl.BlockSpec((1, TNA), lambda b, j: (0, a_idx(j)))],
        out_specs=[pl.BlockSpec((nbp, seq, TNA), lambda b, j: (b, 0, a_idx(j))),
                   pl.BlockSpec((nbp, seq, TNB), lambda b, j: (b, 0, b_idx(j)))],
        out_shape=[jax.ShapeDtypeStruct((bsz, seq, A_COLS), F32),
                   jax.ShapeDtypeStruct((bsz, seq, B_COLS), F32)],
        scratch_shapes=[pltpu.VMEM((nbp * seq, D_MODEL), BF16)],
        compiler_params=pltpu.CompilerParams(dimension_semantics=("arbitrary", "arbitrary"),
                                             vmem_limit_bytes=VMEM_LIMIT),
        name="proj_grid" if grid_shift else "proj_seq",
    )(x, shift, scale, norm_g, wa_bf16, wb_bf16, mu_h, mu_v)


def _scan_masks():
    c = CHUNK
    t = _iota((c, PACK), 0)
    s = _iota((c, PACK), 1) & (c - 1)
    tt = _iota((c, c), 0)
    ss = _iota((c, c), 1)
    f01 = lambda cond: jnp.where(cond, 1.0, 0.0).astype(F32)
    return dict(
        bm64_bf=_head_mask_a().astype(BF16),
        eye=f01(t == s),
        incl=(f01(t >= s), f01(t <= s)),
        strict=(f01(t > s), f01(t < s)),
        tri=(f01(tt >= ss).astype(BF16), f01(tt <= ss).astype(BF16)),
    )


_MASK_KEYS = ("bm64_bf", "eye", "incl", "strict", "tri")
_MASK_SCRATCH = (((PACK, PACK), BF16), ((CHUNK, PACK), F32), ((2, CHUNK, PACK), F32), ((2, CHUNK, PACK), F32),
                 ((2, CHUNK, CHUNK), BF16))


def _store_masks(mask_refs):
    masks = _scan_masks()
    for ref, key in zip(mask_refs, _MASK_KEYS):
        val = masks[key]
        if isinstance(val, tuple):
            for i, z in enumerate(val):
                ref[i] = z
        else:
            ref[...] = val


class _Masks:
    def __init__(self, mask_refs):
        self._refs = dict(zip(_MASK_KEYS, mask_refs))

    def __getitem__(self, key):
        ref = self._refs[key]
        return (ref[0], ref[1]) if len(ref.shape) == 3 else ref[...]


def _half_lane_masks(dtype):
    lane = _iota((1, BLOCK_LANES), 1)
    return (jnp.where(lane < HEAD_A, 1.0, 0.0).astype(dtype), jnp.where(lane >= HEAD_A, 1.0, 0.0).astype(dtype))


def _block_diag_a(x):
    halves = _half_lane_masks(x.dtype)
    zero = jnp.zeros((x.shape[0], BLOCK_LANES), x.dtype)
    rows = []
    for h in range(PACK // HEAD_A):
        tile = h * HEAD_A // BLOCK_LANES
        piece = x[:, BLOCK_LANES * tile:BLOCK_LANES * (tile + 1)] * halves[h % 2]
        rows.append(jnp.concatenate([piece, zero] if tile == 0 else [zero, piece], axis=1))
    return jnp.concatenate(rows, axis=0)


def _block_diag_b(x):
    heads = D_B // HEAD_B
    zero = jnp.zeros((x.shape[0], HEAD_B), x.dtype)
    return jnp.concatenate(
        [jnp.concatenate([x[:, HEAD_B * h:HEAD_B * (h + 1)] if t == h else zero for t in range(heads)], axis=1)
         for h in range(heads)], axis=0)


def _lockstep(gens):
    results = [None] * len(gens)
    live = list(enumerate(gens))
    while live:
        still = []
        for i, g in live:
            try:
                next(g)
                still.append((i, g))
            except StopIteration as stop:
                results[i] = stop.value
        live = still
    return results


def _rwkv_group(rt, kt, kh, bh, kb, bb, v, s_ref, egc, mk, d):
    c = CHUNK
    bd = lambda x: _block_diag_a(x.astype(BF16))
    kr = _split(jnp.concatenate([kt, rt], axis=0))
    aa_k = _mm(kr, bd(kh), NT)
    aa_b = _mm(kr, bd(bh), NT)
    s_bd = s_ref[...]
    krs = _mm(kr, s_bd, NT)
    yield
    a_kk = aa_k[:c] * mk["strict"][d]
    a_rk = aa_k[c:] * mk["incl"][d]
    n = -(aa_b[:c] * mk["strict"][d])
    a_rb = aa_b[c:] * mk["incl"][d]
    x = mk["eye"] + n
    p = _mm(n, bd(n))
    akv = _mm(jnp.concatenate([a_kk, a_rk], axis=0), bd(v))
    yield
    for i in range(5):
        if i < 4:
            xp = _mm(jnp.concatenate([x, p], axis=0), bd(p))
            x = x + xp[:c]
            p = xp[c:]
        else:
            x = x + _mm(x, bd(p))
        yield
    u = _mm(x, bd(krs[:c] + akv[:c]))
    yield
    y = krs[c:] + akv[c:] - _mm(a_rb, bd(u))
    vu = jnp.concatenate([v, (-u).astype(BF16)], axis=0)
    kbb = jnp.concatenate([kb, bb], axis=0)
    upd = _mm(vu, kbb, TN_DIMS)
    halves = _half_lane_masks(F32)
    zero = jnp.zeros((HEAD_A, BLOCK_LANES), F32)
    rows = []
    for h in range(PACK // HEAD_A):
        tile = h * HEAD_A // BLOCK_LANES
        rs, ls = slice(HEAD_A * h, HEAD_A * (h + 1)), slice(BLOCK_LANES * tile, BLOCK_LANES * (tile + 1))
        piece = (s_bd[rs, ls] * egc[:, ls] + upd[rs, ls]) * halves[h % 2]
        rows.append(jnp.concatenate([piece, zero] if tile == 0 else [zero, piece], axis=1))
    s_ref[...] = jnp.concatenate(rows, axis=0)
    return y


def _hgrn_dir(qn, kn, qt, kbh, vb, eghc, s_refs, mk, d):
    a_p = _mm(qn, _block_diag_b(kn.astype(BF16)), NT) * mk["incl"][d]
    inter = []
    for grp, s_ref in enumerate(s_refs):
        sl = slice(PACK * grp, PACK * (grp + 1))
        st = s_ref[...]
        inter.append(_mm(qt[:, sl], st, NT))
        upd = _mm(vb[:, sl], kbh[:, sl], TN_DIMS)
        decay = eghc[:, sl]
        zero = jnp.zeros((HEAD_B, HEAD_B), F32)
        blocks = [st[b0:b0 + HEAD_B, b0:b0 + HEAD_B] * decay[:, b0:b0 + HEAD_B] + upd[b0:b0 + HEAD_B, b0:b0 + HEAD_B]
                  for b0 in range(0, PACK, HEAD_B)]
        s_ref[...] = jnp.concatenate([jnp.concatenate([blocks[0], zero], axis=1),
                                      jnp.concatenate([zero, blocks[1]], axis=1)], axis=0)
    yield
    o = _mm(a_p, _block_diag_b(vb.astype(BF16)))
    return o + jnp.concatenate(inter, axis=1)


def _blockwise(fn, operands, n_out, rows=CHUNK, width=D_A):
    cols = []
    for l0 in range(0, width, BLOCK_LANES):
        parts = []
        for r0 in range(0, rows, BLOCK_ROWS):
            blk = [z[(slice(None) if z.shape[0] == 1 else slice(r0, r0 + BLOCK_ROWS)), l0:l0 + BLOCK_LANES]
                   for z in operands]
            parts.append(fn(*blk))
        cols.append([jnp.concatenate([p[i] for p in parts], axis=0) for i in range(n_out)])
    return [jnp.concatenate([col[i] for col in cols], axis=1) for i in range(n_out)]


def _chunk_operands(pa_ref, pb_ref, bi, d, prm, mk):
    c = CHUNK
    n_t = SUM_TERMS
    w0, a0, wa2, k_k, k_a, lb = prm
    cols = lambda ref, i, width=D_A: ref.at[bi, :, i * width:(i + 1) * width]
    r_ref, k_ref, v_ref = cols(pa_ref, 0), cols(pa_ref, 1), cols(pa_ref, 2)
    lo = pa_ref[bi, :, 4 * D_A + LORA * d:4 * D_A + LORA * (d + 1)]
    lo = jnp.where(_iota((c, LORA), 1) < LORA // 2, jnp.tanh(lo), lo)
    wa = _mm(lo, wa2[d])
    yield

    def gates(wa_w, wa_a, k, fr, w0_, a0_, kk_, lb_):
        half_decay = 0.5 * DECAY_SCALE
        lw = -half_decay * jnp.tanh(0.5 * (w0_ + wa_w)) - half_decay
        a = _sigmoid(a0_ + wa_a)
        kk = k * kk_
        half_gap = 0.5 * (1.0 - lb_)
        f = half_gap * jnp.tanh(0.5 * fr) + (lb_ + half_gap)
        gf = jnp.log(f)
        return (lw, a, kk, 1.0 - f, (kk * kk).astype(BF16), *_terms(lw, n_t), gf.astype(BF16))

    res = _blockwise(gates, [wa[:, :D_A], wa[:, D_A:], k_ref, cols(pb_ref, 2 + d, D_B), w0[d:d + 1], a0[d:d + 1],
                             k_k, lb[d:d + 1]], 6 + n_t)
    lw, a, kk, kf, sq = res[:5]
    lw_t, gf_t = res[5:5 + n_t], res[5 + n_t]
    seg = lambda lanes: _dot(sq[:, lanes], mk["bm64_bf"])
    n2 = jnp.concatenate([seg(slice(0, PACK)), seg(slice(PACK, D_A))], axis=1)
    g = _sum_small_first([_dot(mk["tri"][d], t) for t in lw_t])
    gh = _dot(mk["tri"][d], gf_t)
    yield
    last = c - 1 if d == 0 else 0
    gc, ghc, ghm = g[last:last + 1], gh[last:last + 1], gh[c // 2:c // 2 + 1]
    egc, eghc = jnp.exp(gc), jnp.exp(ghc)

    def rwkv_operands(r, k, kk_, n2_, a_, g_, lw_, ka_, egc_):
        kap = kk_ * lax.rsqrt(jnp.maximum(n2_, 1e-24))
        keff = k * ((1.0 - ka_) + a_ * ka_)
        eng = _exp_neg(g_)
        kh = keff * eng
        bh = (kap * a_) * eng
        out = (r * jnp.exp(g_), kap * jnp.exp(g_ - lw_), kh, bh, kh * egc_, bh * egc_)
        return tuple(z.astype(BF16) for z in out)

    rt, kt, kh, bh, kb, bb = _blockwise(rwkv_operands, [r_ref, k_ref, kk, n2, a, g, lw, k_a, egc], 6)

    def hgrn_operands(qr, kf_, gh_, ghm_, e_m, e_cm):
        q = _silu(qr)
        ghn = gh_ - ghm_
        qn = q * jnp.exp(ghn)
        kn = kf_ * _exp_neg(ghn)
        return tuple(z.astype(BF16) for z in (qn, kn, qn * e_m, kn * e_cm))

    qn, kn, qt, kbh = _blockwise(hgrn_operands, [cols(pb_ref, 0, D_B), kf, gh, ghm, jnp.exp(ghm),
                                                 jnp.exp(ghc - ghm)], 4, width=D_B)
    v = v_ref[...].astype(BF16)
    vb = pb_ref[bi, :, D_B:2 * D_B].astype(BF16)
    return (rt, kt, kh, bh, kb, bb, v, qn, kn, qt, kbh, vb), (egc, eghc)


def _scan_dir(pa_ref, pb_ref, bi, d, prm, sa_scr, sb_scr, mk):
    ops, (egc, eghc) = yield from _chunk_operands(pa_ref, pb_ref, bi, d, prm, mk)
    rt, kt, kh, bh, kb, bb, v, qn, kn, qt, kbh, vb = ops
    chains = []
    for grp in range(D_A // PACK):
        sl = slice(PACK * grp, PACK * (grp + 1))
        chains.append(_rwkv_group(rt[:, sl], kt[:, sl], kh[:, sl], bh[:, sl], kb[:, sl], bb[:, sl], v[:, sl],
                                  sa_scr.at[bi, d, grp], egc[:, sl], mk, d))
    chains.append(_hgrn_dir(qn, kn, qt, kbh, vb, eghc, [sb_scr.at[bi, d, grp] for grp in range(D_B // PACK)],
                            mk, d))
    results = [None] * len(chains)
    live = list(enumerate(chains))
    while live:
        still = []
        for i, g in live:
            try:
                next(g)
                still.append((i, g))
            except StopIteration as stop:
                results[i] = stop.value
        live = still
        yield
    return jnp.concatenate(results[:-1], axis=1), results[-1]


def _state_slots():
    return [(bi, d, grp) for bi in range(SCAN_BATCH) for d in range(2) for grp in range(D_A // PACK)]


def _load_states(sa0_ref, sb0_ref, sa_scr, sb_scr):
    spread = jnp.where(_iota((HEAD_A, PACK), 0) == (_iota((HEAD_A, PACK), 1) & (HEAD_A - 1)), 1.0, 0.0)
    spread = spread.astype(BF16)
    bm64 = _head_mask_a()
    zero = jnp.zeros((HEAD_B, HEAD_B), F32)
    for bi, d, grp in _state_slots():
        sa_scr[bi, d, grp] = _mm_exact_rhs(sa0_ref[bi, d, grp], spread) * bm64
        h0 = sb0_ref[bi, d, 2 * grp].T
        h1 = sb0_ref[bi, d, 2 * grp + 1].T
        sb_scr[bi, d, grp] = jnp.concatenate([jnp.concatenate([h0, zero], axis=1),
                                              jnp.concatenate([zero, h1], axis=1)], axis=0)


def _store_states(sa_scr, sb_scr, sa_out, sb_out):
    gather = jnp.where((_iota((PACK, HEAD_A), 0) & (HEAD_A - 1)) == _iota((PACK, HEAD_A), 1), 1.0, 0.0)
    gather = gather.astype(BF16)
    for bi, d, grp in _state_slots():
        sa_out[bi, d, grp] = _mm_exact_rhs(sa_scr[bi, d, grp], gather)
        for hh in range(PACK // HEAD_B):
            blk = sb_scr[bi, d, grp, HEAD_B * hh:HEAD_B * (hh + 1), HEAD_B * hh:HEAD_B * (hh + 1)]
            sb_out[bi, d, 2 * grp + hh] = blk.T


def _scan_kernel(paf_ref, pbf_ref, pab_ref, pbb_ref, *rest, from_zero):
    rest, mask_refs = rest[:-len(_MASK_KEYS)], rest[-len(_MASK_KEYS):]
    if from_zero:
        (w0_ref, a0_ref, wa2_ref, kk_ref, ka_ref, lb_ref, yf_ref, yb_ref, of_ref, ob_ref, sa_out, sb_out,
         sa_scr, sb_scr) = rest
    else:
        (sa0_ref, sb0_ref, w0_ref, a0_ref, wa2_ref, kk_ref, ka_ref, lb_ref, yf_ref, yb_ref, of_ref, ob_ref,
         sa_scr, sb_scr) = rest
    ci = pl.program_id(1)

    @pl.when(ci == 0)
    def _():
        _store_masks(mask_refs)
        if from_zero:
            sa_scr[...] = jnp.zeros(sa_scr.shape, F32)
            sb_scr[...] = jnp.zeros(sb_scr.shape, F32)
        else:
            _load_states(sa0_ref, sb0_ref, sa_scr, sb_scr)

    mk = _Masks(mask_refs)
    prm = (w0_ref[...], a0_ref[...], wa2_ref, kk_ref[...], ka_ref[...], lb_ref[...])
    p_refs = ((paf_ref, pbf_ref), (pab_ref, pbb_ref))
    runs = [(bi, d) for bi in range(SCAN_BATCH) for d in range(2)]
    outs = _lockstep([_scan_dir(p_refs[d][0], p_refs[d][1], bi, d, prm, sa_scr, sb_scr, mk) for bi, d in runs])
    for (bi, d), (y, o) in zip(runs, outs):
        (yf_ref, yb_ref)[d][bi] = y.astype(BF16)
        (of_ref, ob_ref)[d][bi] = o.astype(BF16)

    if from_zero:
        @pl.when(ci == pl.num_programs(1) - 1)
        def _():
            _store_states(sa_scr, sb_scr, sa_out, sb_out)


def _scan_call(pa, pb, states, w0, a0, wa2, k_k, k_a, lb):
    bsz, seq, _ = pa.shape
    nc = seq // CHUNK
    nb = SCAN_BATCH
    from_zero = states is None
    fwd = lambda b, c: (b, c, 0)
    bwd = lambda b, c: (b, nc - 1 - c, 0)
    st = lambda b, c: (b, 0, 0, 0, 0)
    full = lambda shape: pl.BlockSpec(shape, lambda b, c: (0,) * len(shape))
    sa_block = (nb, 2, D_A // PACK, PACK, HEAD_A)
    sb_block = (nb, 2, D_B // HEAD_B, HEAD_B, HEAD_B)
    y_shape = jax.ShapeDtypeStruct((bsz, seq, D_A), BF16)
    out_specs = [pl.BlockSpec((nb, CHUNK, D_A), fwd), pl.BlockSpec((nb, CHUNK, D_A), bwd),
                 pl.BlockSpec((nb, CHUNK, D_B), fwd), pl.BlockSpec((nb, CHUNK, D_B), bwd)]
    out_shape = [y_shape, y_shape, y_shape, y_shape]
    in_specs = [pl.BlockSpec((nb, CHUNK, A_COLS), fwd), pl.BlockSpec((nb, CHUNK, B_COLS), fwd),
                pl.BlockSpec((nb, CHUNK, A_COLS), bwd), pl.BlockSpec((nb, CHUNK, B_COLS), bwd)]
    args = [pa, pb, pa, pb]
    if from_zero:
        out_specs += [pl.BlockSpec(sa_block, st), pl.BlockSpec(sb_block, st)]
        out_shape += [jax.ShapeDtypeStruct((bsz,) + sa_block[1:], F32),
                      jax.ShapeDtypeStruct((bsz,) + sb_block[1:], F32)]
    else:
        in_specs += [pl.BlockSpec(sa_block, st), pl.BlockSpec(sb_block, st)]
        args += list(states)
    in_specs += [full((2, D_A)), full((2, D_A)), full((2, LORA, 2 * D_A)),
                 full((1, D_A)), full((1, D_A)), full((2, D_B))]
    args += [w0, a0, wa2, k_k, k_a, lb]
    bd_block = (nb, 2, D_A // PACK, PACK, PACK)
    return pl.pallas_call(
        functools.partial(_scan_kernel, from_zero=from_zero),
        grid=(bsz // nb, nc),
        in_specs=in_specs,
        out_specs=out_specs,
        out_shape=out_shape,
        scratch_shapes=[pltpu.VMEM(bd_block, F32), pltpu.VMEM(bd_block, F32)]
                       + [pltpu.VMEM(shape, dtype) for shape, dtype in _MASK_SCRATCH],
        compiler_params=pltpu.CompilerParams(dimension_semantics=("arbitrary", "arbitrary"),
                                             vmem_limit_bytes=VMEM_LIMIT),
        name="scan_state" if from_zero else "scan",
    )(*args)


def _out_kernel(x_ref, pa_ref, zb_ref, yf_ref, yb_ref, of_ref, ob_ref, gate_ref, a0_ref, a2p_ref, ka_ref,
                rk_ref, lnw_ref, lnb_ref, og_ref, wout_ref, fg_ref, out_ref):
    pa = pa_ref[0]
    r = pa[:, 0:D_A]
    k = pa[:, D_A:2 * D_A]
    v = pa[:, 2 * D_A:3 * D_A]
    za = pa[:, 3 * D_A:4 * D_A]
    bm64_bf = _head_mask_a().astype(BF16)

    def seg_sum(z):
        zb16 = z.astype(BF16)
        return jnp.concatenate([_dot(zb16[:, :PACK], bm64_bf), _dot(zb16[:, PACK:], bm64_bf)], axis=1)

    y = yf_ref[0].astype(F32) + yb_ref[0].astype(F32)
    mu = seg_sum(y) * (1.0 / HEAD_A)
    dlt = y - mu
    var = seg_sum(dlt * dlt) * (1.0 / HEAD_A)
    yn = dlt * lax.rsqrt(var + GN_EPS) * lnw_ref[...] + lnb_ref[...]
    a_dirs = []
    for d in range(2):
        lo = pa[:, 4 * D_A + LORA * d:4 * D_A + LORA * (d + 1)]
        a_dirs.append(_sigmoid(a0_ref[d:d + 1] + _mm(lo, a2p_ref[d])))
    kmean = k * (1.0 + (0.5 * (a_dirs[0] + a_dirs[1]) - 1.0) * ka_ref[...])
    bonus = seg_sum(r * kmean * rk_ref[...]) * v
    out_a = (yn + bonus) * _silu(za)

    o = of_ref[0].astype(F32) + ob_ref[0].astype(F32)
    og = og_ref[...]
    zb = zb_ref[0]
    outs = [out_a]
    for h in range(D_B // HEAD_B):
        sl = slice(HEAD_B * h, HEAD_B * (h + 1))
        oh = o[:, sl]
        oh = oh * lax.rsqrt(jnp.mean(oh * oh, axis=-1, keepdims=True) + EPS) * og[:, sl]
        outs.append(oh * _silu(zb[:, sl]))
    mix = jnp.concatenate(outs, axis=1)
    proj = _dot(mix.astype(BF16), wout_ref[...])
    hs = x_ref[0] + gate_ref[0] * proj
    out_ref[0] = hs * lax.rsqrt(jnp.mean(hs * hs, axis=-1, keepdims=True) + EPS) * fg_ref[...]


def _out_call(x, pa, pb, yf, yb, of, ob, gate, a0, a2p, k_a, r_k, lnx_w, lnx_b, onorm_g, w_out_bf16, final_g):
    bsz, seq, _ = x.shape
    tm = min(OUT_ROWS, seq)
    tok = lambda b, i: (b, i, 0)
    full = lambda shape: pl.BlockSpec(shape, lambda b, i: (0,) * len(shape))
    return pl.pallas_call(
        _out_kernel,
        grid=(bsz, seq // tm),
        in_specs=[pl.BlockSpec((1, tm, D_MODEL), tok),
                  pl.BlockSpec((1, tm, A_COLS), tok),
                  pl.BlockSpec((1, tm, D_B), lambda b, i: (b, i, 4)),
                  pl.BlockSpec((1, tm, D_A), tok), pl.BlockSpec((1, tm, D_A), tok),
                  pl.BlockSpec((1, tm, D_B), tok), pl.BlockSpec((1, tm, D_B), tok),
                  pl.BlockSpec((1, 1, D_MODEL), lambda b, i: (b, 0, 0)),
                  full((2, D_A)), full((2, LORA, D_A)), full((1, D_A)), full((1, D_A)),
                  full((1, D_A)), full((1, D_A)), full((1, D_B)),
                  full((D_MODEL, D_MODEL)), full((1, D_MODEL))],
        out_specs=pl.BlockSpec((1, tm, D_MODEL), tok),
        out_shape=jax.ShapeDtypeStruct((bsz, seq, D_MODEL), F32),
        compiler_params=pltpu.CompilerParams(dimension_semantics=("arbitrary", "arbitrary"),
                                             vmem_limit_bytes=VMEM_LIMIT),
        name="out",
    )(x, pa, pb, yf, yb, of, ob, gate, a0, a2p, k_a, r_k, lnx_w, lnx_b, onorm_g, w_out_bf16, final_g)


def kernel(x_prompt, x_sample, state_rwkv, state_hgrn, c, c_ctx, norm_g, w_ada, b_ada, w_in, mu_h, mu_v, w0, w2,
           a0, a2, k_k, k_a, r_k, lnx_w, lnx_b, lb_logits, onorm_g, w_out, final_g):
    l = 0
    bp = x_prompt.shape[0]
    bs = x_sample.shape[0]
    lb_all = jnp.cumsum(jax.nn.softmax(lb_logits.astype(F32), axis=0), axis=0)
    lb = lb_all[l]

    wa_bf = w_in[l, :, :A_COLS].astype(BF16)
    wb_bf = w_in[l, :, A_COLS:].astype(BF16)
    w_out_bf = w_out[l].astype(BF16)
    zeros = jnp.zeros((2, LORA // 2, D_A), F32)
    wa2 = jnp.concatenate([jnp.concatenate([w2[l], zeros], axis=2),
                           jnp.concatenate([zeros, a2[l]], axis=2)], axis=1)
    a2p = jnp.concatenate([zeros, a2[l]], axis=1)
    row = lambda z: z.reshape(1, -1)

    cc = jnp.concatenate([c_ctx[None, :], c, jnp.zeros((16 - 1 - bs, D_MODEL), F32)], axis=0)
    m = _ada_call(cc, w_ada[l], row(b_ada[l]))
    shift, scale, gate = m[:, :D_MODEL], m[:, D_MODEL:2 * D_MODEL], m[:, 2 * D_MODEL:]
    ctx = lambda z: jnp.broadcast_to(z[0:1, None, :], (bp, 1, D_MODEL))
    lat = lambda z: z[1:1 + bs, None, :]

    def path(x, sh, sc, gt, states, grid_shift):
        pa, pb = _proj_call(x, sh, sc, row(norm_g[l]), wa_bf, wb_bf, row(mu_h[l]), row(mu_v[l]), grid_shift)
        res = _scan_call(pa, pb, states, w0[l], a0[l], wa2, row(k_k[l]), row(k_a[l]), lb)
        yf, yb, of, ob = res[:4]
        y = _out_call(x, pa, pb, yf, yb, of, ob, gt, a0[l], a2p, row(k_a[l]), row(r_k[l]), row(lnx_w[l]),
                      row(lnx_b[l]), row(onorm_g[l]), w_out_bf, row(final_g))
        return y, res[4:]

    y_prompt, (s_a, s_b) = path(x_prompt, ctx(shift), ctx(scale), ctx(gate), None, False)
    groups_a = D_A // PACK
    sa0 = state_rwkv[:, l].reshape(bs, 2, groups_a, PACK, HEAD_A)
    y_sample, _ = path(x_sample, lat(shift), lat(scale), lat(gate), (sa0, state_hgrn[:, l]), True)
    s_a = s_a.reshape(bp, 1, 2, D_A // HEAD_A, HEAD_A, HEAD_A)
    return y_prompt, y_sample, s_a, s_b[:, None]
```

```python
import functools

import jax
import jax.numpy as jnp
from jax import lax
from jax.experimental import pallas as pl
from jax.experimental.pallas import tpu as pltpu

F32 = jnp.float32
BF16 = jnp.bfloat16

D_MODEL = 1024
D_A = 512
D_B = 512
HEAD_A = 64
HEAD_B = 128
LOG2_HEAD_A = HEAD_A.bit_length() - 1
LORA = 128
A_COLS = 4 * D_A + 2 * LORA
B_COLS = 5 * D_B
GRID_W = 64
CHUNK = 64
SCAN_BATCH = 4
SUM_TERMS = 2
BLOCK_ROWS = 32
BLOCK_LANES = 128
EPS = 1e-6
GN_EPS = 64e-5
DECAY_SCALE = 0.6065306597126334
NEG_LOG2E = -1.4426950408889634
PACK = 256
TNA = 768
TNB = 512
NA = A_COLS // TNA
NB = B_COLS // TNB
PROJ_ROWS = 2048
OUT_ROWS = 512
VMEM_LIMIT = 56 * 1024 * 1024

NN = ((1,), (0,))
NT = ((1,), (1,))
TN_DIMS = ((0,), (0,))


def _dot(a, b, dims=NN):
    return lax.dot_general(a, b, (dims, ((), ())), preferred_element_type=F32)


def _split(x, passes=1):
    if isinstance(x, tuple):
        return x
    hi = x.astype(BF16)
    if passes == 1:
        return (hi,)
    return hi, (x - hi.astype(F32)).astype(BF16)


def _mm(a, b, dims=NN, passes=1):
    a = _split(a, passes)
    b = _split(b, passes)
    if len(a) == 1 or len(b) == 1:
        return _dot(a[0], b[0], dims)
    return _dot(a[0], b[0], dims) + (_dot(a[0], b[1], dims) + _dot(a[1], b[0], dims))


def _terms(x, n):
    out = []
    for _ in range(n - 1):
        h = x.astype(BF16)
        out.append(h)
        x = x - h.astype(F32)
    out.append(x.astype(BF16))
    return out


def _sum_small_first(parts):
    return functools.reduce(lambda acc, z: z + acc, reversed(parts))


def _mm_exact_rhs(a, b_bf16, n=3):
    return _sum_small_first([_dot(t, b_bf16) for t in _terms(a, n)])


def _exp_neg(x):
    return jnp.exp2(x * NEG_LOG2E)


def _sigmoid(x):
    return 0.5 * jnp.tanh(0.5 * x) + 0.5


def _silu(x):
    return x * _sigmoid(x)


def _iota(shape, dim):
    return lax.broadcasted_iota(jnp.int32, shape, dim)


def _block_mask(rows, cols, row_shift, col_shift):
    same = (_iota((rows, cols), 0) >> row_shift) == (_iota((rows, cols), 1) >> col_shift)
    return jnp.where(same, 1.0, 0.0).astype(F32)


def _head_mask_a():
    return _block_mask(PACK, PACK, LOG2_HEAD_A, LOG2_HEAD_A)


def _ada_kernel(c_ref, w_ref, b_ref, m_ref):
    m_ref[...] = _mm(_silu(c_ref[...]), w_ref[...], passes=3) + b_ref[...]


def _ada_call(cc, w_ada, b_ada):
    rows = cc.shape[0]
    return pl.pallas_call(
        _ada_kernel,
        grid=(3,),
        in_specs=[pl.BlockSpec((rows, D_MODEL), lambda j: (0, 0)),
                  pl.BlockSpec((D_MODEL, D_MODEL), lambda j: (0, j)),
                  pl.BlockSpec((1, D_MODEL), lambda j: (0, j))],
        out_specs=pl.BlockSpec((rows, D_MODEL), lambda j: (0, j)),
        out_shape=jax.ShapeDtypeStruct((rows, 3 * D_MODEL), F32),
        compiler_params=pltpu.CompilerParams(dimension_semantics=("arbitrary",),
                                             vmem_limit_bytes=VMEM_LIMIT),
        name="ada",
    )(cc, w_ada, b_ada)


def _proj_kernel(x_ref, sh_ref, sc_ref, g_ref, wa_ref, wb_ref, muh_ref, muv_ref, pa_ref, pb_ref, xm_ref, *,
                 nbp, seq, grid_shift):
    j = pl.program_id(1)
    rows = nbp * seq

    @pl.when(j == 0)
    def _():
        for b in range(nbp):
            x = x_ref[b]
            xn = x * lax.rsqrt(jnp.mean(x * x, axis=-1, keepdims=True) + EPS) * g_ref[...]
            xm_ref[b * seq:(b + 1) * seq, :] = (xn * (1.0 + sc_ref[b]) + sh_ref[b]).astype(BF16)

    @pl.when(j < NA)
    def _():
        p = _dot(xm_ref[...], wa_ref[...])
        w = GRID_W if grid_shift else seq
        nblk = rows // w
        pos = _iota((1, w, TNA), 1)
        to3 = lambda z: z.reshape(nblk, w, TNA)
        prev = jnp.where(pos == 0, 0.0, to3(pltpu.roll(p, 1, axis=0)))
        nxt = jnp.where(pos == w - 1, 0.0, to3(pltpu.roll(p, rows - 1, axis=0)))
        muh = muh_ref[...]
        out = (0.5 * muh) * (prev + nxt)
        if grid_shift:
            muv = muv_ref[...]
            p3 = to3(p)
            zblk = jnp.zeros((1, w, TNA), F32)
            up = jnp.concatenate([zblk, p3[:nblk - 1]], axis=0)
            dn = jnp.concatenate([p3[1:], zblk], axis=0)
            out = (1.0 - muh - muv) * p3 + out + (0.5 * muv) * (up + dn)
        else:
            out = (1.0 - muh) * to3(p) + out
        pa_ref[...] = out.reshape(nbp, seq, TNA)

    @pl.when(j >= NA)
    def _():
        pb_ref[...] = _dot(xm_ref[...], wb_ref[...]).reshape(nbp, seq, TNB)


def _proj_call(x, shift, scale, norm_g, wa_bf16, wb_bf16, mu_h, mu_v, grid_shift):
    bsz, seq, _ = x.shape
    nbp = 1 if grid_shift else PROJ_ROWS // seq
    kern = functools.partial(_proj_kernel, nbp=nbp, seq=seq, grid_shift=grid_shift)
    a_idx = lambda j: jnp.minimum(j, NA - 1)
    b_idx = lambda j: jnp.maximum(j - NA, 0)
    return pl.pallas_call(
        kern,
        grid=(bsz // nbp, NA + NB),
        in_specs=[pl.BlockSpec((nbp, seq, D_MODEL), lambda b, j: (b, 0, 0)),
                  pl.BlockSpec((nbp, 1, D_MODEL), lambda b, j: (b, 0, 0)),
                  pl.BlockSpec((nbp, 1, D_MODEL), lambda b, j: (b, 0, 0)),
                  pl.BlockSpec((1, D_MODEL), lambda b, j: (0, 0)),
                  pl.BlockSpec((D_MODEL, TNA), lambda b, j: (0, a_idx(j))),
                  pl.BlockSpec((D_MODEL, TNB), lambda b, j: (0, b_idx(j))),
                  pl.BlockSpec((1, TNA), lambda b, j: (0, a_idx(j))),
                  pl.BlockSpec((1, TNA), lambda b, j: (0, a_idx(j)))],
        out_specs=[pl.BlockSpec((nbp, seq, TNA), lambda b, j: (b, 0, a_idx(j))),
                   pl.BlockSpec((nbp, seq, TNB), lambda b, j: (b, 0, b_idx(j)))],
        out_shape=[jax.ShapeDtypeStruct((bsz, seq, A_COLS), F32),
                   jax.ShapeDtypeStruct((bsz, seq, B_COLS), F32)],
        scratch_shapes=[pltpu.VMEM((nbp * seq, D_MODEL), BF16)],
        compiler_params=pltpu.CompilerParams(dimension_semantics=("arbitrary", "arbitrary"),
                                             vmem_limit_bytes=VMEM_LIMIT),
        name="proj_grid" if grid_shift else "proj_seq",
    )(x, shift, scale, norm_g, wa_bf16, wb_bf16, mu_h, mu_v)


def _scan_masks():
    c = CHUNK
    t = _iota((c, PACK), 0)
    s = _iota((c, PACK), 1) & (c - 1)
    tt = _iota((c, c), 0)
    ss = _iota((c, c), 1)
    f01 = lambda cond: jnp.where(cond, 1.0, 0.0).astype(F32)
    return dict(
        bm64_bf=_head_mask_a().astype(BF16),
        eye=f01(t == s),
        incl=(f01(t >= s), f01(t <= s)),
        strict=(f01(t > s), f01(t < s)),
        tri=(f01(tt >= ss).astype(BF16), f01(tt <= ss).astype(BF16)),
    )


_MASK_KEYS = ("bm64_bf", "eye", "incl", "strict", "tri")
_MASK_SCRATCH = (((PACK, PACK), BF16), ((CHUNK, PACK), F32), ((2, CHUNK, PACK), F32), ((2, CHUNK, PACK), F32),
                 ((2, CHUNK, CHUNK), BF16))


def _store_masks(mask_refs):
    masks = _scan_masks()
    for ref, key in zip(mask_refs, _MASK_KEYS):
        val = masks[key]
        if isinstance(val, tuple):
            for i, z in enumerate(val):
                ref[i] = z
        else:
            ref[...] = val


class _Masks:
    def __init__(self, mask_refs):
        self._refs = dict(zip(_MASK_KEYS, mask_refs))

    def __getitem__(self, key):
        ref = self._refs[key]
        return (ref[0], ref[1]) if len(ref.shape) == 3 else ref[...]


def _half_lane_masks(dtype):
    lane = _iota((1, BLOCK_LANES), 1)
    return (jnp.where(lane < HEAD_A, 1.0, 0.0).astype(dtype), jnp.where(lane >= HEAD_A, 1.0, 0.0).astype(dtype))


def _block_diag_a(x):
    halves = _half_lane_masks(x.dtype)
    zero = jnp.zeros((x.shape[0], BLOCK_LANES), x.dtype)
    rows = []
    for h in range(PACK // HEAD_A):
        tile = h * HEAD_A // BLOCK_LANES
        piece = x[:, BLOCK_LANES * tile:BLOCK_LANES * (tile + 1)] * halves[h % 2]
        rows.append(jnp.concatenate([piece, zero] if tile == 0 else [zero, piece], axis=1))
    return jnp.concatenate(rows, axis=0)


def _block_diag_b(x):
    heads = D_B // HEAD_B
    zero = jnp.zeros((x.shape[0], HEAD_B), x.dtype)
    return jnp.concatenate(
        [jnp.concatenate([x[:, HEAD_B * h:HEAD_B * (h + 1)] if t == h else zero for t in range(heads)], axis=1)
         for h in range(heads)], axis=0)


def _lockstep(gens):
    results = [None] * len(gens)
    live = list(enumerate(gens))
    while live:
        still = []
        for i, g in live:
            try:
                next(g)
                still.append((i, g))
            except StopIteration as stop:
                results[i] = stop.value
        live = still
    return results


def _rwkv_group(rt, kt, kh, bh, kb, bb, v, s_ref, egc, mk, d):
    c = CHUNK
    bd = lambda x: _block_diag_a(x.astype(BF16))
    kr = _split(jnp.concatenate([kt, rt], axis=0))
    aa_k = _mm(kr, bd(kh), NT)
    aa_b = _mm(kr, bd(bh), NT)
    s_bd = s_ref[...]
    krs = _mm(kr, s_bd, NT)
    yield
    a_kk = aa_k[:c] * mk["strict"][d]
    a_rk = aa_k[c:] * mk["incl"][d]
    n = -(aa_b[:c] * mk["strict"][d])
    a_rb = aa_b[c:] * mk["incl"][d]
    x = mk["eye"] + n
    p = _mm(n, bd(n))
    akv = _mm(jnp.concatenate([a_kk, a_rk], axis=0), bd(v))
    yield
    for i in range(5):
        if i < 4:
            xp = _mm(jnp.concatenate([x, p], axis=0), bd(p))
            x = x + xp[:c]
            p = xp[c:]
        else:
            x = x + _mm(x, bd(p))
        yield
    u = _mm(x, bd(krs[:c] + akv[:c]))
    yield
    y = krs[c:] + akv[c:] - _mm(a_rb, bd(u))
    vu = jnp.concatenate([v, (-u).astype(BF16)], axis=0)
    kbb = jnp.concatenate([kb, bb], axis=0)
    upd = _mm(vu, kbb, TN_DIMS)
    halves = _half_lane_masks(F32)
    zero = jnp.zeros((HEAD_A, BLOCK_LANES), F32)
    rows = []
    for h in range(PACK // HEAD_A):
        tile = h * HEAD_A // BLOCK_LANES
        rs, ls = slice(HEAD_A * h, HEAD_A * (h + 1)), slice(BLOCK_LANES * tile, BLOCK_LANES * (tile + 1))
        piece = (s_bd[rs, ls] * egc[:, ls] + upd[rs, ls]) * halves[h % 2]
        rows.append(jnp.concatenate([piece, zero] if tile == 0 else [zero, piece], axis=1))
    s_ref[...] = jnp.concatenate(rows, axis=0)
    return y


def _hgrn_dir(qn, kn, qt, kbh, vb, eghc, s_refs, mk, d):
    a_p = _mm(qn, _block_diag_b(kn.astype(BF16)), NT) * mk["incl"][d]
    inter = []
    for grp, s_ref in enumerate(s_refs):
        sl = slice(PACK * grp, PACK * (grp + 1))
        st = s_ref[...]
        inter.append(_mm(qt[:, sl], st, NT))
        upd = _mm(vb[:, sl], kbh[:, sl], TN_DIMS)
        decay = eghc[:, sl]
        zero = jnp.zeros((HEAD_B, HEAD_B), F32)
        blocks = [st[b0:b0 + HEAD_B, b0:b0 + HEAD_B] * decay[:, b0:b0 + HEAD_B] + upd[b0:b0 + HEAD_B, b0:b0 + HEAD_B]
                  for b0 in range(0, PACK, HEAD_B)]
        s_ref[...] = jnp.concatenate([jnp.concatenate([blocks[0], zero], axis=1),
                                      jnp.concatenate([zero, blocks[1]], axis=1)], axis=0)
    yield
    o = _mm(a_p, _block_diag_b(vb.astype(BF16)))
    return o + jnp.concatenate(inter, axis=1)


def _blockwise(fn, operands, n_out, rows=CHUNK, width=D_A):
    cols = []
    for l0 in range(0, width, BLOCK_LANES):
        parts = []
        for r0 in range(0, rows, BLOCK_ROWS):
            blk = [z[(slice(None) if z.shape[0] == 1 else slice(r0, r0 + BLOCK_ROWS)), l0:l0 + BLOCK_LANES]
                   for z in operands]
            parts.append(fn(*blk))
        cols.append([jnp.concatenate([p[i] for p in parts], axis=0) for i in range(n_out)])
    return [jnp.concatenate([col[i] for col in cols], axis=1) for i in range(n_out)]


def _chunk_operands(p_refs, bi, d, prm, mk):
    c = CHUNK
    n_t = SUM_TERMS
    w0, a0, wa2, k_k, k_a, lb = prm
    pa_ref, lo_ref, pb_ref, f_ref = p_refs
    cols = lambda ref, i, width=D_A: ref.at[bi, :, i * width:(i + 1) * width]
    r_ref, k_ref, v_ref = cols(pa_ref, 0), cols(pa_ref, 1), cols(pa_ref, 2)
    lo = lo_ref[bi]
    lo = jnp.where(_iota((c, LORA), 1) < LORA // 2, jnp.tanh(lo), lo)
    wa = _mm(lo, wa2[d])
    yield

    def gates(wa_w, wa_a, k, fr, w0_, a0_, kk_, lb_):
        lw = -DECAY_SCALE * _sigmoid(w0_ + wa_w)
        a = _sigmoid(a0_ + wa_a)
        kk = k * kk_
        f = lb_ + (1.0 - lb_) * _sigmoid(fr)
        gf = jnp.log(f)
        return (lw, a, kk, 1.0 - f, (kk * kk).astype(BF16), *_terms(lw, n_t), gf.astype(BF16))

    res = _blockwise(gates, [wa[:, :D_A], wa[:, D_A:], k_ref, f_ref.at[bi], w0[d:d + 1], a0[d:d + 1],
                             k_k, lb[d:d + 1]], 6 + n_t)
    lw, a, kk, kf, sq = res[:5]
    lw_t, gf_t = res[5:5 + n_t], res[5 + n_t]
    seg = lambda lanes: _dot(sq[:, lanes], mk["bm64_bf"])
    n2 = jnp.concatenate([seg(slice(0, PACK)), seg(slice(PACK, D_A))], axis=1)
    g = _sum_small_first([_dot(mk["tri"][d], t) for t in lw_t])
    gh = _dot(mk["tri"][d], gf_t)
    yield
    last = c - 1 if d == 0 else 0
    gc, ghc, ghm = g[last:last + 1], gh[last:last + 1], gh[c // 2:c // 2 + 1]
    egc, eghc = jnp.exp(gc), jnp.exp(ghc)

    def rwkv_operands(r, k, kk_, n2_, a_, g_, lw_, ka_, egc_):
        kap = kk_ * lax.rsqrt(jnp.maximum(n2_, 1e-24))
        keff = k * (1.0 + (a_ - 1.0) * ka_)
        eng = _exp_neg(g_)
        kh = keff * eng
        bh = (kap * a_) * eng
        out = (r * jnp.exp(g_), kap * jnp.exp(g_ - lw_), kh, bh, kh * egc_, bh * egc_)
        return tuple(z.astype(BF16) for z in out)

    rt, kt, kh, bh, kb, bb = _blockwise(rwkv_operands, [r_ref, k_ref, kk, n2, a, g, lw, k_a, egc], 6)

    def hgrn_operands(qr, kf_, gh_, ghm_, e_m, e_cm):
        q = _silu(qr)
        ghn = gh_ - ghm_
        qn = q * jnp.exp(ghn)
        kn = kf_ * _exp_neg(ghn)
        return tuple(z.astype(BF16) for z in (qn, kn, qn * e_m, kn * e_cm))

    qn, kn, qt, kbh = _blockwise(hgrn_operands, [cols(pb_ref, 0, D_B), kf, gh, ghm, jnp.exp(ghm),
                                                 jnp.exp(ghc - ghm)], 4, width=D_B)
    v = v_ref[...].astype(BF16)
    vb = pb_ref[bi, :, D_B:2 * D_B].astype(BF16)
    return (rt, kt, kh, bh, kb, bb, v, qn, kn, qt, kbh, vb), (egc, eghc)


def _scan_dir(p_refs, bi, d, prm, sa_scr, sb_scr, mk):
    ops, (egc, eghc) = yield from _chunk_operands(p_refs, bi, d, prm, mk)
    rt, kt, kh, bh, kb, bb, v, qn, kn, qt, kbh, vb = ops
    chains = []
    for grp in range(D_A // PACK):
        sl = slice(PACK * grp, PACK * (grp + 1))
        chains.append(_rwkv_group(rt[:, sl], kt[:, sl], kh[:, sl], bh[:, sl], kb[:, sl], bb[:, sl], v[:, sl],
                                  sa_scr.at[bi, d, grp], egc[:, sl], mk, d))
    chains.append(_hgrn_dir(qn, kn, qt, kbh, vb, eghc, [sb_scr.at[bi, d, grp] for grp in range(D_B // PACK)],
                            mk, d))
    results = [None] * len(chains)
    live = list(enumerate(chains))
    while live:
        still = []
        for i, g in live:
            try:
                next(g)
                still.append((i, g))
            except StopIteration as stop:
                results[i] = stop.value
        live = still
        yield
    return jnp.concatenate(results[:-1], axis=1), results[-1]


def _state_slots():
    return [(bi, d, grp) for bi in range(SCAN_BATCH) for d in range(2) for grp in range(D_A // PACK)]


def _load_states(sa0_ref, sb0_ref, sa_scr, sb_scr):
    spread = jnp.where(_iota((HEAD_A, PACK), 0) == (_iota((HEAD_A, PACK), 1) & (HEAD_A - 1)), 1.0, 0.0)
    spread = spread.astype(BF16)
    bm64 = _head_mask_a()
    zero = jnp.zeros((HEAD_B, HEAD_B), F32)
    for bi, d, grp in _state_slots():
        sa_scr[bi, d, grp] = _mm_exact_rhs(sa0_ref[bi, d, grp], spread) * bm64
        h0 = sb0_ref[bi, d, 2 * grp].T
        h1 = sb0_ref[bi, d, 2 * grp + 1].T
        sb_scr[bi, d, grp] = jnp.concatenate([jnp.concatenate([h0, zero], axis=1),
                                              jnp.concatenate([zero, h1], axis=1)], axis=0)


def _store_states(sa_scr, sb_scr, sa_out, sb_out):
    gather = jnp.where((_iota((PACK, HEAD_A), 0) & (HEAD_A - 1)) == _iota((PACK, HEAD_A), 1), 1.0, 0.0)
    gather = gather.astype(BF16)
    for bi, d, grp in _state_slots():
        sa_out[bi, d, grp] = _mm_exact_rhs(sa_scr[bi, d, grp], gather)
        for hh in range(PACK // HEAD_B):
            blk = sb_scr[bi, d, grp, HEAD_B * hh:HEAD_B * (hh + 1), HEAD_B * hh:HEAD_B * (hh + 1)]
            sb_out[bi, d, 2 * grp + hh] = blk.T


def _scan_kernel(*refs, from_zero):
    p_refs, rest = (refs[0:4], refs[4:8]), refs[8:]
    rest, mask_refs = rest[:-len(_MASK_KEYS)], rest[-len(_MASK_KEYS):]
    if from_zero:
        (w0_ref, a0_ref, wa2_ref, kk_ref, ka_ref, lb_ref, yf_ref, yb_ref, of_ref, ob_ref, sa_out, sb_out,
         sa_scr, sb_scr) = rest
    else:
        (sa0_ref, sb0_ref, w0_ref, a0_ref, wa2_ref, kk_ref, ka_ref, lb_ref, yf_ref, yb_ref, of_ref, ob_ref,
         sa_scr, sb_scr) = rest
    ci = pl.program_id(1)

    @pl.when(ci == 0)
    def _():
        _store_masks(mask_refs)
        if from_zero:
            sa_scr[...] = jnp.zeros(sa_scr.shape, F32)
            sb_scr[...] = jnp.zeros(sb_scr.shape, F32)
        else:
            _load_states(sa0_ref, sb0_ref, sa_scr, sb_scr)

    mk = _Masks(mask_refs)
    prm = (w0_ref[...], a0_ref[...], wa2_ref, kk_ref[...], ka_ref[...], lb_ref[...])
    runs = [(bi, d) for bi in range(SCAN_BATCH) for d in range(2)]
    outs = _lockstep([_scan_dir(p_refs[d], bi, d, prm, sa_scr, sb_scr, mk) for bi, d in runs])
    for (bi, d), (y, o) in zip(runs, outs):
        (yf_ref, yb_ref)[d][bi] = y.astype(BF16)
        (of_ref, ob_ref)[d][bi] = o.astype(BF16)

    if from_zero:
        @pl.when(ci == pl.num_programs(1) - 1)
        def _():
            _store_states(sa_scr, sb_scr, sa_out, sb_out)


def _scan_call(pa, pb, states, w0, a0, wa2, k_k, k_a, lb):
    bsz, seq, _ = pa.shape
    nc = seq // CHUNK
    nb = SCAN_BATCH
    from_zero = states is None
    fwd = lambda b, c: (b, c, 0)
    bwd = lambda b, c: (b, nc - 1 - c, 0)
    st = lambda b, c: (b, 0, 0, 0, 0)
    full = lambda shape: pl.BlockSpec(shape, lambda b, c: (0,) * len(shape))
    sa_block = (nb, 2, D_A // PACK, PACK, HEAD_A)
    sb_block = (nb, 2, D_B // HEAD_B, HEAD_B, HEAD_B)
    y_shape = jax.ShapeDtypeStruct((bsz, seq, D_A), BF16)
    out_specs = [pl.BlockSpec((nb, CHUNK, D_A), fwd), pl.BlockSpec((nb, CHUNK, D_A), bwd),
                 pl.BlockSpec((nb, CHUNK, D_B), fwd), pl.BlockSpec((nb, CHUNK, D_B), bwd)]
    out_shape = [y_shape, y_shape, y_shape, y_shape]
    def chunk_windows(d):
        chunk = (lambda c: c) if d == 0 else (lambda c: nc - 1 - c)
        return [pl.BlockSpec((nb, CHUNK, 3 * D_A), lambda b, c: (b, chunk(c), 0)),
                pl.BlockSpec((nb, CHUNK, LORA), lambda b, c: (b, chunk(c), 4 * D_A // LORA + d)),
                pl.BlockSpec((nb, CHUNK, 2 * D_B), lambda b, c: (b, chunk(c), 0)),
                pl.BlockSpec((nb, CHUNK, D_B), lambda b, c: (b, chunk(c), 2 + d))]

    in_specs = chunk_windows(0) + chunk_windows(1)
    args = [pa, pa, pb, pb] * 2
    if from_zero:
        out_specs += [pl.BlockSpec(sa_block, st), pl.BlockSpec(sb_block, st)]
        out_shape += [jax.ShapeDtypeStruct((bsz,) + sa_block[1:], F32),
                      jax.ShapeDtypeStruct((bsz,) + sb_block[1:], F32)]
    else:
        in_specs += [pl.BlockSpec(sa_block, st), pl.BlockSpec(sb_block, st)]
        args += list(states)
    in_specs += [full((2, D_A)), full((2, D_A)), full((2, LORA, 2 * D_A)),
                 full((1, D_A)), full((1, D_A)), full((2, D_B))]
    args += [w0, a0, wa2, k_k, k_a, lb]
    bd_block = (nb, 2, D_A // PACK, PACK, PACK)
    return pl.pallas_call(
        functools.partial(_scan_kernel, from_zero=from_zero),
        grid=(bsz // nb, nc),
        in_specs=in_specs,
        out_specs=out_specs,
        out_shape=out_shape,
        scratch_shapes=[pltpu.VMEM(bd_block, F32), pltpu.VMEM(bd_block, F32)]
                       + [pltpu.VMEM(shape, dtype) for shape, dtype in _MASK_SCRATCH],
        compiler_params=pltpu.CompilerParams(dimension_semantics=("arbitrary", "arbitrary"),
                                             vmem_limit_bytes=VMEM_LIMIT),
        name="scan_state" if from_zero else "scan",
    )(*args)


def _out_kernel(x_ref, pa_ref, zb_ref, yf_ref, yb_ref, of_ref, ob_ref, gate_ref, a0_ref, a2p_ref, ka_ref,
                rk_ref, lnw_ref, lnb_ref, og_ref, wout_ref, fg_ref, out_ref):
    pa = pa_ref[0]
    r = pa[:, 0:D_A]
    k = pa[:, D_A:2 * D_A]
    v = pa[:, 2 * D_A:3 * D_A]
    za = pa[:, 3 * D_A:4 * D_A]
    bm64_bf = _head_mask_a().astype(BF16)

    def seg_sum(z):
        zb16 = z.astype(BF16)
        return jnp.concatenate([_dot(zb16[:, :PACK], bm64_bf), _dot(zb16[:, PACK:], bm64_bf)], axis=1)

    y = yf_ref[0].astype(F32) + yb_ref[0].astype(F32)
    mu = seg_sum(y) * (1.0 / HEAD_A)
    dlt = y - mu
    var = seg_sum(dlt * dlt) * (1.0 / HEAD_A)
    yn = dlt * lax.rsqrt(var + GN_EPS) * lnw_ref[...] + lnb_ref[...]
    a_dirs = []
    for d in range(2):
        lo = pa[:, 4 * D_A + LORA * d:4 * D_A + LORA * (d + 1)]
        a_dirs.append(_sigmoid(a0_ref[d:d + 1] + _mm(lo, a2p_ref[d])))
    kmean = k * (1.0 + (0.5 * (a_dirs[0] + a_dirs[1]) - 1.0) * ka_ref[...])
    bonus = seg_sum(r * kmean * rk_ref[...]) * v
    out_a = (yn + bonus) * _silu(za)

    o = of_ref[0].astype(F32) + ob_ref[0].astype(F32)
    og = og_ref[...]
    zb = zb_ref[0]
    outs = [out_a]
    for h in range(D_B // HEAD_B):
        sl = slice(HEAD_B * h, HEAD_B * (h + 1))
        oh = o[:, sl]
        oh = oh * lax.rsqrt(jnp.mean(oh * oh, axis=-1, keepdims=True) + EPS) * og[:, sl]
        outs.append(oh * _silu(zb[:, sl]))
    mix = jnp.concatenate(outs, axis=1)
    proj = _dot(mix.astype(BF16), wout_ref[...])
    hs = x_ref[0] + gate_ref[0] * proj
    out_ref[0] = hs * lax.rsqrt(jnp.mean(hs * hs, axis=-1, keepdims=True) + EPS) * fg_ref[...]


def _out_call(x, pa, pb, yf, yb, of, ob, gate, a0, a2p, k_a, r_k, lnx_w, lnx_b, onorm_g, w_out_bf16, final_g):
    bsz, seq, _ = x.shape
    tm = min(OUT_ROWS, seq)
    tok = lambda b, i: (b, i, 0)
    full = lambda shape: pl.BlockSpec(shape, lambda b, i: (0,) * len(shape))
    return pl.pallas_call(
        _out_kernel,
        grid=(bsz, seq // tm),
        in_specs=[pl.BlockSpec((1, tm, D_MODEL), tok),
                  pl.BlockSpec((1, tm, A_COLS), tok),
                  pl.BlockSpec((1, tm, D_B), lambda b, i: (b, i, 4)),
                  pl.BlockSpec((1, tm, D_A), tok), pl.BlockSpec((1, tm, D_A), tok),
                  pl.BlockSpec((1, tm, D_B), tok), pl.BlockSpec((1, tm, D_B), tok),
                  pl.BlockSpec((1, 1, D_MODEL), lambda b, i: (b, 0, 0)),
                  full((2, D_A)), full((2, LORA, D_A)), full((1, D_A)), full((1, D_A)),
                  full((1, D_A)), full((1, D_A)), full((1, D_B)),
                  full((D_MODEL, D_MODEL)), full((1, D_MODEL))],
        out_specs=pl.BlockSpec((1, tm, D_MODEL), tok),
        out_shape=jax.ShapeDtypeStruct((bsz, seq, D_MODEL), F32),
        compiler_params=pltpu.CompilerParams(dimension_semantics=("arbitrary", "arbitrary"),
                                             vmem_limit_bytes=VMEM_LIMIT),
        name="out",
    )(x, pa, pb, yf, yb, of, ob, gate, a0, a2p, k_a, r_k, lnx_w, lnx_b, onorm_g, w_out_bf16, final_g)


def kernel(x_prompt, x_sample, state_rwkv, state_hgrn, c, c_ctx, norm_g, w_ada, b_ada, w_in, mu_h, mu_v, w0, w2,
           a0, a2, k_k, k_a, r_k, lnx_w, lnx_b, lb_logits, onorm_g, w_out, final_g):
    l = 0
    bp = x_prompt.shape[0]
    bs = x_sample.shape[0]
    lb_all = jnp.cumsum(jax.nn.softmax(lb_logits.astype(F32), axis=0), axis=0)
    lb = lb_all[l]

    wa_bf = w_in[l, :, :A_COLS].astype(BF16)
    wb_bf = w_in[l, :, A_COLS:].astype(BF16)
    w_out_bf = w_out[l].astype(BF16)
    zeros = jnp.zeros((2, LORA // 2, D_A), F32)
    wa2 = jnp.concatenate([jnp.concatenate([w2[l], zeros], axis=2),
                           jnp.concatenate([zeros, a2[l]], axis=2)], axis=1)
    a2p = jnp.concatenate([zeros, a2[l]], axis=1)
    row = lambda z: z.reshape(1, -1)

    cc = jnp.concatenate([c_ctx[None, :], c, jnp.zeros((16 - 1 - bs, D_MODEL), F32)], axis=0)
    m = _ada_call(cc, w_ada[l], row(b_ada[l]))
    shift, scale, gate = m[:, :D_MODEL], m[:, D_MODEL:2 * D_MODEL], m[:, 2 * D_MODEL:]
    ctx = lambda z: jnp.broadcast_to(z[0:1, None, :], (bp, 1, D_MODEL))
    lat = lambda z: z[1:1 + bs, None, :]

    def path(x, sh, sc, gt, states, grid_shift):
        pa, pb = _proj_call(x, sh, sc, row(norm_g[l]), wa_bf, wb_bf, row(mu_h[l]), row(mu_v[l]), grid_shift)
        res = _scan_call(pa, pb, states, w0[l], a0[l], wa2, row(k_k[l]), row(k_a[l]), lb)
        yf, yb, of, ob = res[:4]
        y = _out_call(x, pa, pb, yf, yb, of, ob, gt, a0[l], a2p, row(k_a[l]), row(r_k[l]), row(lnx_w[l]),
                      row(lnx_b[l]), row(onorm_g[l]), w_out_bf, row(final_g))
        return y, res[4:]

    y_prompt, (s_a, s_b) = path(x_prompt, ctx(shift), ctx(scale), ctx(gate), None, False)
    groups_a = D_A // PACK
    sa0 = state_rwkv[:, l].reshape(bs, 2, groups_a, PACK, HEAD_A)
    y_sample, _ = path(x_sample, lat(shift), lat(scale), lat(gate), (sa0, state_hgrn[:, l]), True)
    s_a = s_a.reshape(bp, 1, 2, D_A // HEAD_A, HEAD_A, HEAD_A)
    return y_prompt, y_sample, s_a, s_b[:, None]
```
